```python
import jax
import jax.numpy as jnp
from jax import lax
import numpy as np

D_MODEL = 1024
BATCH = 16
SEQ = 2048
DEPTH = 1

PLE_DIM = 256
RMS_EPS = 1e-6

N_ATT_HEADS = 8
ATT_HEAD_DIM = 64
ATT_WIDTH = N_ATT_HEADS * ATT_HEAD_DIM
MOBA_BLOCK = 256
MOBA_TOPK = 3
ATT_QUERY_BLOCK = 64

N_MLSTM_HEADS = 4
MLSTM_WIDTH = D_MODEL // 2
MLSTM_V_DIM = MLSTM_WIDTH // N_MLSTM_HEADS
MLSTM_QK_DIM = MLSTM_V_DIM // 2
MLSTM_GATE_IN = N_MLSTM_HEADS * (2 * MLSTM_QK_DIM + MLSTM_V_DIM)
MLSTM_CONV = 4
MLSTM_CHUNK = 64

N_BRANCH = 2
IN_COLS = 3 * ATT_WIDTH + 2 * MLSTM_WIDTH + N_BRANCH * D_MODEL

N_EXPERTS = 32
TOP_K = 4
EXPERT_FF = D_MODEL
SWIGLU_ALPHA = 1.702
SWIGLU_LIMIT = 7.0
MOE_ROW_BLOCK = 256

kernel_name = "hybrid_moba_mlstm_moe_block"


def rms_norm(x, g):
    xf = x.astype(jnp.float32)
    y = xf * lax.rsqrt(jnp.mean(xf * xf, axis=-1, keepdims=True) + RMS_EPS)
    return (y * g.astype(jnp.float32)).astype(x.dtype)


def moba_attention(q, k, v):
    B, H, S, Dh = q.shape
    n_blk = -(-S // MOBA_BLOCK)
    pad = n_blk * MOBA_BLOCK - S
    k_p = jnp.pad(k, ((0, 0), (0, 0), (0, pad), (0, 0)))
    v_p = jnp.pad(v, ((0, 0), (0, 0), (0, pad), (0, 0)))
    k_blk = k_p.reshape(B, H, n_blk, MOBA_BLOCK, Dh)
    v_blk = v_p.reshape(B, H, n_blk, MOBA_BLOCK, Dh)
    k_mean = jnp.mean(k_blk.astype(jnp.float32), axis=3)
    k_sel = min(MOBA_TOPK, n_blk - 1)
    scale = Dh ** -0.5
    slopes = jnp.asarray(2.0 ** (-8.0 * np.arange(1, H + 1) / H), jnp.float32)
    slopes4 = slopes[None, :, None, None]
    bi = jnp.arange(B)[:, None, None]
    hi = jnp.arange(H)[None, :, None]
    blk_pos = jnp.arange(MOBA_BLOCK)
    blk_ids = jnp.arange(n_blk)

    def one_query_block(c):
        q0 = c * ATT_QUERY_BLOCK
        qc = lax.dynamic_slice_in_dim(q, q0, ATT_QUERY_BLOCK, axis=2)
        t = q0 + jnp.arange(ATT_QUERY_BLOCK)
        own = q0 // MOBA_BLOCK
        k_own = lax.dynamic_index_in_dim(k_blk, own, axis=2, keepdims=False)
        v_own = lax.dynamic_index_in_dim(v_blk, own, axis=2, keepdims=False)
        dist_own = t[:, None] - (own * MOBA_BLOCK + blk_pos)[None, :]
        lg_own = jnp.einsum('bhqd,bhkd->bhqk', qc, k_own).astype(jnp.float32) * scale
        lg_own = lg_own - slopes4 * dist_own[None, None].astype(jnp.float32)
        logits = [jnp.where(dist_own[None, None] >= 0, lg_own, -jnp.inf)]
        sel = None
        if k_sel > 0:
            gate = jnp.einsum('bhqd,bhnd->bhqn', qc.astype(jnp.float32), k_mean)
            gate = jnp.where((blk_ids < own)[None, None, None, :], gate, -jnp.inf)
            _, sel = lax.top_k(gate, k_sel)
            for j in range(k_sel):
                idx = sel[..., j]
                k_g = k_blk[bi, hi, idx]
                lg = jnp.einsum('bhqd,bhqkd->bhqk', qc, k_g).astype(jnp.float32) * scale
                pos = idx[..., None] * MOBA_BLOCK + blk_pos
                lg = lg - slopes4 * (t[None, None, :, None] - pos).astype(jnp.float32)
                logits.append(jnp.where(j < own, lg, -jnp.inf))
        probs = jax.nn.softmax(jnp.concatenate(logits, axis=-1), axis=-1)
        out = jnp.einsum('bhqk,bhkd->bhqd', probs[..., :MOBA_BLOCK].astype(v.dtype), v_own)
        for j in range(k_sel):
            v_g = v_blk[bi, hi, sel[..., j]]
            p_j = probs[..., (j + 1) * MOBA_BLOCK:(j + 2) * MOBA_BLOCK].astype(v.dtype)
            out = out + jnp.einsum('bhqk,bhqkd->bhqd', p_j, v_g)
        return out

    n_qb = S // ATT_QUERY_BLOCK
    outs = lax.map(one_query_block, jnp.arange(n_qb))
    return jnp.moveaxis(outs, 0, 2).reshape(B, H, S, Dh)


def causal_depthwise_conv(x, w, b):
    K, C = w.shape
    y = lax.conv_general_dilated(x, w[:, None, :], window_strides=(1,), padding=[(K - 1, 0)],
                                 dimension_numbers=('NWC', 'WIO', 'NWC'), feature_group_count=C)
    return y + b


def mlstm_chunkwise(q, k, v, i_pre, f_pre):
    B, H, S, Dk = q.shape
    Dv = v.shape[-1]
    L = MLSTM_CHUNK
    nc = S // L
    q = q.astype(jnp.float32)
    k = k.astype(jnp.float32) * (Dk ** -0.5)
    v = v.astype(jnp.float32)
    logf = jax.nn.log_sigmoid(f_pre.astype(jnp.float32))
    ig = i_pre.astype(jnp.float32)

    def chunks(a):
        return jnp.moveaxis(a.reshape((B, H, nc, L) + a.shape[3:]), 2, 0)

    causal = jnp.tril(jnp.ones((L, L), bool))

    def step(carry, inp):
        C, n, m = carry
        qc, kc, vc, lf, ic = inp
        b = jnp.cumsum(lf, axis=-1)
        logD = jnp.where(causal, b[..., :, None] - b[..., None, :] + ic[..., None, :], -jnp.inf)
        inter = b + m[..., None]
        m_t = jnp.maximum(inter, jnp.max(logD, axis=-1))
        w_inter = jnp.exp(inter - m_t)
        s = jnp.einsum('bhtd,bhsd->bhts', qc, kc) * jnp.exp(logD - m_t[..., None])
        num = jnp.einsum('bhts,bhsv->bhtv', s, vc) + w_inter[..., None] * jnp.einsum('bhvd,bhtd->bhtv', C, qc)
        den = jnp.sum(s, axis=-1) + w_inter * jnp.einsum('bhd,bhtd->bht', n, qc)
        h = num / jnp.maximum(jnp.abs(den), jnp.exp(-m_t))[..., None]
        bL = b[..., -1]
        g = bL[..., None] - b + ic
        m_new = jnp.maximum(bL + m, jnp.max(g, axis=-1))
        wg = jnp.exp(g - m_new[..., None])
        decay = jnp.exp(bL + m - m_new)
        C_new = decay[..., None, None] * C + jnp.einsum('bhs,bhsv,bhsd->bhvd', wg, vc, kc)
        n_new = decay[..., None] * n + jnp.einsum('bhs,bhsd->bhd', wg, kc)
        return (C_new, n_new, m_new), h

    init = (jnp.zeros((B, H, Dv, Dk), jnp.float32), jnp.zeros((B, H, Dk), jnp.float32),
            jnp.zeros((B, H), jnp.float32))
    _, hs = lax.scan(step, init, (chunks(q), chunks(k), chunks(v), chunks(logf), chunks(ig)))
    return jnp.moveaxis(hs, 0, 2).reshape(B, H, S, Dv)


def mlstm_branch(x_m, o_m, conv_w, conv_b, w_qk, w_if, b_if, g):
    B, S, _ = x_m.shape
    H = N_MLSTM_HEADS
    x_c = jax.nn.silu(causal_depthwise_conv(x_m, conv_w, conv_b))
    qk = jnp.einsum('bshc,hce->bshe', x_c.reshape(B, S, H, MLSTM_V_DIM), w_qk)
    q, k = qk[..., :MLSTM_QK_DIM], qk[..., MLSTM_QK_DIM:]
    v = x_m.reshape(B, S, H, MLSTM_V_DIM)
    gate_in = jnp.concatenate([q.reshape(B, S, -1), k.reshape(B, S, -1), x_m], axis=-1)
    if_pre = gate_in @ w_if + b_if
    i_pre, f_pre = if_pre[..., :H], if_pre[..., H:]
    h = mlstm_chunkwise(jnp.moveaxis(q, 2, 1), jnp.moveaxis(k, 2, 1), jnp.moveaxis(v, 2, 1),
                        jnp.moveaxis(i_pre, -1, 1), jnp.moveaxis(f_pre, -1, 1))
    h = jnp.moveaxis(h, 1, 2)
    h = h * lax.rsqrt(jnp.mean(h * h, axis=-1, keepdims=True) + RMS_EPS)
    h = h.reshape(B, S, MLSTM_WIDTH) * g.astype(jnp.float32)
    return (jax.nn.sigmoid(o_m.astype(jnp.float32)) * h).astype(x_m.dtype)


def moe_ffn(x, w_router, b_router, w_gu, b_gu, w_down, b_down):
    B, S, D = x.shape
    N = B * S
    xt = x.reshape(N, D)
    logits = (xt @ w_router + b_router).astype(jnp.float32)
    top_val, top_idx = lax.top_k(logits, TOP_K)
    gates = jax.nn.softmax(top_val, axis=-1)
    n_slots = N * TOP_K
    e_flat = top_idx.reshape(-1).astype(jnp.int32)
    tok_flat = jnp.arange(n_slots, dtype=jnp.int32) // TOP_K
    g_flat = gates.reshape(-1)
    order = jnp.argsort(e_flat)
    e_sorted, tok_sorted, g_sorted = e_flat[order], tok_flat[order], g_flat[order]
    counts = jnp.bincount(e_flat, length=N_EXPERTS).astype(jnp.int32)
    start = jnp.cumsum(counts) - counts
    padded = (counts + MOE_ROW_BLOCK - 1) // MOE_ROW_BLOCK * MOE_ROW_BLOCK
    pend = jnp.cumsum(padded)
    pstart = pend - padded
    dest = pstart[e_sorted] + (jnp.arange(n_slots, dtype=jnp.int32) - start[e_sorted])
    n_blocks = -(-n_slots // MOE_ROW_BLOCK) + N_EXPERTS
    P = n_blocks * MOE_ROW_BLOCK
    row_tok = jnp.full((P,), N, jnp.int32).at[dest].set(tok_sorted)
    row_gate = jnp.zeros((P,), jnp.float32).at[dest].set(g_sorted)
    blk_expert = jnp.minimum(
        jnp.searchsorted(pend, jnp.arange(n_blocks, dtype=jnp.int32) * MOE_ROW_BLOCK, side='right'),
        N_EXPERTS - 1)
    x_pad = jnp.concatenate([xt, jnp.zeros((1, D), xt.dtype)], axis=0)

    def expert_rows(args):
        toks, e = args
        xb = x_pad[toks]
        gu = xb @ w_gu[e] + b_gu[e]
        gate = jnp.minimum(gu[:, ::2], SWIGLU_LIMIT)
        up = jnp.clip(gu[:, 1::2], -SWIGLU_LIMIT, SWIGLU_LIMIT)
        glu = gate * jax.nn.sigmoid(gate * SWIGLU_ALPHA)
        return ((up + 1.0) * glu) @ w_down[e] + b_down[e]

    y_rows = lax.map(expert_rows, (row_tok.reshape(n_blocks, MOE_ROW_BLOCK), blk_expert))
    y_rows = y_rows.reshape(P, D) * row_gate[:, None].astype(y_rows.dtype)
    y = jax.ops.segment_sum(y_rows, row_tok, num_segments=N + 1)[:N]
    return y.reshape(B, S, D).astype(x.dtype)


def setup_inputs(seed: int = 0) -> dict:
    key = jax.random.key(seed)
    ks = jax.random.split(key, 24)
    L, D, H = DEPTH, D_MODEL, N_MLSTM_HEADS

    def nrm(k, shape, scale):
        return jax.random.normal(k, shape, jnp.float32) * scale

    b_i = nrm(ks[8], (L, H), 0.1)
    b_f = jnp.linspace(3.0, 6.0, H, dtype=jnp.float32)[None, :] + nrm(ks[9], (L, H), 0.1)
    return {
        "x": nrm(ks[0], (BATCH, SEQ, D), 1.0),
        "p": nrm(ks[1], (DEPTH, BATCH, SEQ, PLE_DIM), 1.0),
        "norm_mix_g": 1.0 + nrm(ks[2], (L, D), 0.01),
        "w_in": nrm(ks[3], (L, D, IN_COLS), D ** -0.5),
        "conv_w": nrm(ks[4], (L, MLSTM_CONV, MLSTM_WIDTH), MLSTM_CONV ** -0.5),
        "conv_b": nrm(ks[5], (L, MLSTM_WIDTH), 0.01),
        "w_qk_m": nrm(ks[6], (L, H, MLSTM_V_DIM, 2 * MLSTM_QK_DIM), MLSTM_V_DIM ** -0.5),
        "w_if": nrm(ks[7], (L, MLSTM_GATE_IN, 2 * H), MLSTM_GATE_IN ** -0.5),
        "b_if": jnp.concatenate([b_i, b_f], axis=-1),
        "mnorm_g": 1.0 + nrm(ks[10], (L, MLSTM_WIDTH), 0.01),
        "w_branch": nrm(ks[11], (L, N_BRANCH, ATT_WIDTH, D), ATT_WIDTH ** -0.5),
        "w_out": nrm(ks[12], (L, D, D), D ** -0.5),
        "norm_ffn_g": 1.0 + nrm(ks[13], (L, D), 0.01),
        "w_router": nrm(ks[14], (L, D, N_EXPERTS), D ** -0.5),
        "b_router": nrm(ks[15], (L, N_EXPERTS), 0.01),
        "w_gate_up": nrm(ks[16], (L, N_EXPERTS, D, 2 * EXPERT_FF), D ** -0.5),
        "b_gate_up": nrm(ks[17], (L, N_EXPERTS, 2 * EXPERT_FF), 0.01),
        "w_down": nrm(ks[18], (L, N_EXPERTS, EXPERT_FF, D), EXPERT_FF ** -0.5),
        "b_down": nrm(ks[19], (L, N_EXPERTS, D), 0.01),
        "norm_ple_g": 1.0 + nrm(ks[20], (L, D), 0.01),
        "w_ple_gate": nrm(ks[21], (L, D, D), D ** -0.5),
        "w_ple_proj": nrm(ks[22], (L, PLE_DIM, D), PLE_DIM ** -0.5),
        "final_norm_g": 1.0 + nrm(ks[23], (D,), 0.01),
    }


def reference(x, p, norm_mix_g, w_in, conv_w, conv_b, w_qk_m, w_if, b_if, mnorm_g,
              w_branch, w_out, norm_ffn_g, w_router, b_router, w_gate_up, b_gate_up,
              w_down, b_down, norm_ple_g, w_ple_gate, w_ple_proj, final_norm_g):
    B, S, _ = x.shape
    a3 = 3 * ATT_WIDTH
    split_at = [ATT_WIDTH, 2 * ATT_WIDTH, a3, a3 + MLSTM_WIDTH, a3 + 2 * MLSTM_WIDTH,
                a3 + 2 * MLSTM_WIDTH + D_MODEL]

    def heads(t):
        return t.reshape(B, S, N_ATT_HEADS, ATT_HEAD_DIM).transpose(0, 2, 1, 3)

    for i in range(DEPTH):
        h = rms_norm(x, norm_mix_g[i])
        proj = h @ w_in[i]
        q_a, k_a, v_a, x_m, o_m, g_a, g_m = jnp.split(proj, split_at, axis=-1)
        y_a = moba_attention(heads(q_a), heads(k_a), heads(v_a))
        y_a = y_a.transpose(0, 2, 1, 3).reshape(B, S, ATT_WIDTH)
        y_m = mlstm_branch(x_m, o_m, conv_w[i], conv_b[i], w_qk_m[i], w_if[i], b_if[i], mnorm_g[i])
        u = jax.nn.sigmoid(g_a) * (y_a @ w_branch[i, 0]) + jax.nn.sigmoid(g_m) * (y_m @ w_branch[i, 1])
        x = x + u @ w_out[i]
        h = rms_norm(x, norm_ffn_g[i])
        x = x + moe_ffn(h, w_router[i], b_router[i], w_gate_up[i], b_gate_up[i], w_down[i], b_down[i])
        r = rms_norm(x, norm_ple_g[i])
        x = x + jax.nn.sigmoid(r @ w_ple_gate[i]) * (p[i] @ w_ple_proj[i])
    return rms_norm(x, final_norm_g)
```

```python
import functools

import jax
import jax.numpy as jnp
import numpy as np
from jax import lax
from jax.experimental import pallas as pl
from jax.experimental.pallas import tpu as pltpu

RMS_EPS = 1e-6
LANES = 128
ROW_CHUNKS = 8
VMEM_LIMIT = 56 * 1024 * 1024

N_ATT_HEADS = 8
ATT_HEAD_DIM = 64
ATT_WIDTH = N_ATT_HEADS * ATT_HEAD_DIM
MOBA_BLOCK = 256
MOBA_TOPK = 3

N_MLSTM_HEADS = 4
MLSTM_WIDTH = 512
MLSTM_V_DIM = 128
MLSTM_QK_DIM = 64
MLSTM_CONV = 4
MLSTM_CHUNK = 128

N_EXPERTS = 32
TOP_K = 4
SWIGLU_ALPHA = 1.702
SWIGLU_LIMIT = 7.0
MOE_ROW_BLOCK = 256

NEG_BIG = -1e30

_NT = (((1,), (1,)), ((), ()))
_TN = (((0,), (0,)), ((), ()))


def _params(*sem):
    return pltpu.CompilerParams(dimension_semantics=sem, vmem_limit_bytes=VMEM_LIMIT)


def _rms(x, g):
    return x * lax.rsqrt(jnp.mean(x * x, axis=-1, keepdims=True) + RMS_EPS) * g


def _in_proj_body(x_ref, g_ref, w_ref, qkv_ref, xo_ref, gate_ref, *, col_chunk):
    h = _rms(x_ref[...], g_ref[...]).astype(jnp.bfloat16)
    col = 0
    for out_ref in (qkv_ref, xo_ref, gate_ref):
        for c in range(0, out_ref.shape[1], col_chunk):
            out_ref[:, c:c + col_chunk] = jnp.dot(
                h, w_ref[:, col + c:col + c + col_chunk],
                preferred_element_type=jnp.float32).astype(out_ref.dtype)
        col += out_ref.shape[1]


def in_proj(x2, g, w_bf16, *, tm=512, col_chunk=512):
    n, d = x2.shape
    widths = (3 * ATT_WIDTH, 2 * MLSTM_WIDTH, 2 * d)
    assert sum(widths) == w_bf16.shape[1] and n % tm == 0
    return pl.pallas_call(
        functools.partial(_in_proj_body, col_chunk=col_chunk),
        grid=(n // tm,),
        in_specs=[pl.BlockSpec((tm, d), lambda i: (i, 0)),
                  pl.BlockSpec((1, d), lambda i: (0, 0)),
                  pl.BlockSpec(w_bf16.shape, lambda i: (0, 0))],
        out_specs=[pl.BlockSpec((tm, w), lambda i: (i, 0)) for w in widths],
        out_shape=[jax.ShapeDtypeStruct((n, w), jnp.bfloat16) for w in widths],
        compiler_params=_params("parallel"),
        name="in_proj",
    )(x2, g.reshape(1, d), w_bf16)


def _moba_body(q_ref, k_ref, v_ref, o_ref, *, pair, seq):
    blk = MOBA_BLOCK
    n_blk = seq // blk
    lane = lax.broadcasted_iota(jnp.int32, (1, LANES), 1)
    row = lax.broadcasted_iota(jnp.int32, (blk, blk), 0)
    col = lax.broadcasted_iota(jnp.int32, (blk, blk), 1)
    rel = (row - col).astype(jnp.float32)
    causal = row >= col
    scale = ATT_HEAD_DIM ** -0.5

    k_all = k_ref[...]
    v_all = v_ref[...]
    k_mean = jnp.concatenate(
        [jnp.mean(k_all[j * blk:(j + 1) * blk].astype(jnp.float32), axis=0, keepdims=True)
         for j in range(n_blk)], axis=0)

    outs = []
    for hh in range(2):
        head = 2 * pair + hh
        slope = jnp.exp2(-8.0 * (head + 1).astype(jnp.float32) / N_ATT_HEADS)
        head_lanes = (lane >= hh * ATT_HEAD_DIM) & (lane < (hh + 1) * ATT_HEAD_DIM)
        km_h = jnp.where(head_lanes, k_mean, 0.0)
        head_out = []
        for qi in range(n_blk):
            q_blk = q_ref[qi * blk:(qi + 1) * blk, :]
            q_h = jnp.where(head_lanes, q_blk, jnp.zeros_like(q_blk))
            sel = None
            if qi > MOBA_TOPK:
                q32 = q_h.astype(jnp.float32)
                gates = [lax.dot_general(
                    q32, jnp.broadcast_to(km_h[n:n + 1], (LANES, LANES)), _NT,
                    precision=lax.Precision.HIGHEST, preferred_element_type=jnp.float32)
                    for n in range(qi)]
                sel = []
                for n in range(qi):
                    ahead = jnp.zeros((blk, LANES), jnp.float32)
                    for m in range(qi):
                        if m == n:
                            continue
                        beats = (gates[m] >= gates[n]) if m < n else (gates[m] > gates[n])
                        ahead = ahead + jnp.where(beats, 1.0, 0.0)
                    sel.append(ahead < MOBA_TOPK)
            logits = []
            for j in range(qi + 1):
                s = lax.dot_general(q_h, k_all[j * blk:(j + 1) * blk], _NT,
                                    preferred_element_type=jnp.float32)
                s = s * scale - slope * (rel + float((qi - j) * blk))
                if j == qi:
                    s = jnp.where(causal, s, NEG_BIG)
                elif sel is not None:
                    s = jnp.where(jnp.concatenate([sel[j]] * (blk // LANES), axis=1), s, NEG_BIG)
                logits.append(s)
            s_all = jnp.concatenate(logits, axis=1)
            m = jnp.max(s_all, axis=1, keepdims=True)
            p = jnp.exp(s_all - m)
            denom = jnp.sum(p, axis=1, keepdims=True)
            o = jnp.dot(p.astype(jnp.bfloat16), v_all[:(qi + 1) * blk],
                        preferred_element_type=jnp.float32)
            head_out.append(o / denom)
        outs.append((head_lanes, head_out))
    for qi in range(n_blk):
        o = jnp.where(outs[0][0], outs[0][1][qi], outs[1][1][qi])
        o_ref[qi * blk:(qi + 1) * blk, :] = o.astype(o_ref.dtype)


def _moba_kernel(q_ref, k_ref, v_ref, o_ref, *, seq):
    _moba_body(q_ref, k_ref, v_ref, o_ref, pair=pl.program_id(1), seq=seq)


def moba(qkv, batch, seq):
    n = batch * seq
    n_pairs = ATT_WIDTH // LANES
    assert seq % MOBA_BLOCK == 0
    spec = lambda off: pl.BlockSpec((seq, LANES), lambda b, p: (b, off + p))
    return pl.pallas_call(
        functools.partial(_moba_kernel, seq=seq),
        grid=(batch, n_pairs),
        in_specs=[spec(0), spec(n_pairs), spec(2 * n_pairs)],
        out_specs=pl.BlockSpec((seq, LANES), lambda b, p: (b, p)),
        out_shape=jax.ShapeDtypeStruct((n, ATT_WIDTH), jnp.bfloat16),
        compiler_params=_params("parallel", "parallel"),
        name="moba",
    )(qkv, qkv, qkv)


def _log_sigmoid(z):
    return jnp.minimum(z, 0.0) - jnp.log(1.0 + jnp.exp(-jnp.abs(z)))


def _mlstm_body(xm_ref, om_ref, cw_ref, cb_ref, wq_ref, wk_ref, wiq_ref, wik_ref, wix_ref,
                wiqt_ref, wikt_ref, wixt_ref, bi_ref, bit_ref, g_ref, y_ref,
                q_sc, k_sc, gcol_sc, grow_sc, ct_sc, n_sc, m_sc, *, seq):
    L = MLSTM_CHUNK
    H = N_MLSTM_HEADS
    n_chunks = seq // L
    f32, bf16 = jnp.float32, jnp.bfloat16

    x = xm_ref[...].astype(f32)
    t_idx = lax.broadcasted_iota(jnp.int32, (seq, 1), 0)
    acc = x * cw_ref[MLSTM_CONV - 1:MLSTM_CONV, :] + cb_ref[...]
    for d in range(1, MLSTM_CONV):
        shifted = jnp.where(t_idx >= d, pltpu.roll(x, d, axis=0), 0.0)
        acc = acc + shifted * cw_ref[MLSTM_CONV - 1 - d:MLSTM_CONV - d, :]
    xc = (acc * jax.nn.sigmoid(acc)).astype(bf16)

    xm = xm_ref[...]
    if_col = jnp.dot(xm, wix_ref[...], preferred_element_type=f32) + bi_ref[...]
    if_row = lax.dot_general(wixt_ref[...], xm, _NT, preferred_element_type=f32) + bit_ref[...]
    for h in range(H):
        xch = xc[:, h * LANES:(h + 1) * LANES]
        q = jnp.dot(xch, wq_ref[h], preferred_element_type=f32).astype(bf16)
        k = jnp.dot(xch, wk_ref[h], preferred_element_type=f32).astype(bf16)
        if_col = if_col + jnp.dot(q, wiq_ref[h], preferred_element_type=f32)
        if_col = if_col + jnp.dot(k, wik_ref[h], preferred_element_type=f32)
        if_row = if_row + lax.dot_general(wiqt_ref[h], q, _NT, preferred_element_type=f32)
        if_row = if_row + lax.dot_general(wikt_ref[h], k, _NT, preferred_element_type=f32)
        q_sc[h] = q
        k_sc[h] = (k.astype(f32) * (MLSTM_QK_DIM ** -0.5)).astype(bf16)
    is_f_col = lax.broadcasted_iota(jnp.int32, (1, 2 * H), 1) >= H
    is_f_row = lax.broadcasted_iota(jnp.int32, (2 * H, 1), 0) >= H
    gcol_sc[...] = jnp.where(is_f_col, _log_sigmoid(if_col), if_col)
    g_row = jnp.where(is_f_row, _log_sigmoid(if_row), if_row)
    for c in range(n_chunks):
        grow_sc[c] = g_row[:, c * L:(c + 1) * L]

    ct_sc[...] = jnp.zeros_like(ct_sc)
    n_sc[...] = jnp.zeros_like(n_sc)
    m_sc[...] = jnp.zeros_like(m_sc)

    r_i = lax.broadcasted_iota(jnp.int32, (L, L), 0)
    c_i = lax.broadcasted_iota(jnp.int32, (L, L), 1)
    causal = r_i >= c_i
    tri_lo = jnp.where(causal, 1.0, 0.0).astype(f32)
    tri_up = jnp.where(c_i >= r_i, 1.0, 0.0).astype(f32)

    def chunk(c, carry):
        r0 = pl.multiple_of(c * L, L)
        gc = gcol_sc[pl.ds(r0, L), :]
        gr = grow_sc[c]
        b_cols = jnp.dot(tri_lo, gc, precision=lax.Precision.HIGHEST, preferred_element_type=f32)
        b_rows = jnp.dot(gr, tri_up, precision=lax.Precision.HIGHEST, preferred_element_type=f32)
        for h in range(H):
            hs = slice(h * LANES, (h + 1) * LANES)
            b_col = b_cols[:, H + h:H + h + 1]
            i_col = gc[:, h:h + 1]
            b_row = b_rows[H + h:H + h + 1, :]
            i_row = gr[h:h + 1, :]
            b_last = b_col[L - 1:L, :]
            m_prev = m_sc[h][:, 0:1]
            q = q_sc[h, pl.ds(r0, L), :]
            k = k_sc[h, pl.ds(r0, L), :]
            v = xm_ref[pl.ds(r0, L), hs]

            log_d = jnp.where(causal, b_col + (i_row - b_row), NEG_BIG)
            inter = b_col + m_prev
            m_t = jnp.maximum(inter, jnp.max(log_d, axis=1, keepdims=True))
            w_inter = jnp.exp(inter - m_t)
            s = lax.dot_general(q, k, _NT, preferred_element_type=f32) * jnp.exp(log_d - m_t)
            num = jnp.dot(s.astype(bf16), v, preferred_element_type=f32)
            num = num + w_inter * jnp.dot(q, ct_sc[h].astype(bf16), preferred_element_type=f32)
            den = jnp.sum(s, axis=1, keepdims=True)
            den = den + w_inter * jnp.sum(q.astype(f32) * n_sc[h], axis=1, keepdims=True)
            hv = num / jnp.maximum(jnp.abs(den), jnp.exp(-m_t))
            hv = hv * lax.rsqrt(jnp.mean(hv * hv, axis=1, keepdims=True) + RMS_EPS) * g_ref[:, hs]
            o_gate = jax.nn.sigmoid(om_ref[pl.ds(r0, L), hs].astype(f32))
            y_ref[pl.ds(r0, L), hs] = (o_gate * hv).astype(y_ref.dtype)

            g_col = b_last - b_col + i_col
            m_new = jnp.maximum(b_last + m_prev, jnp.max(g_col, axis=0, keepdims=True))
            wg = jnp.exp(g_col - m_new)
            decay = jnp.exp(b_last + m_prev - m_new)
            wv = (wg * v.astype(f32)).astype(bf16)
            ct_sc[h] = decay * ct_sc[h] + lax.dot_general(k, wv, _TN, preferred_element_type=f32)
            n_sc[h] = decay * n_sc[h] + jnp.sum(wg * k.astype(f32), axis=0, keepdims=True)
            m_sc[h] = jnp.broadcast_to(m_new, (1, LANES))
        return carry

    lax.fori_loop(0, n_chunks, chunk, 0)


def _mlstm_kernel(*refs, seq):
    _mlstm_body(*refs, seq=seq)


def mlstm(xo, conv_w, conv_b, w_qk, w_if, b_if, g, batch, seq):
    n = batch * seq
    H, dv, dk = N_MLSTM_HEADS, MLSTM_V_DIM, MLSTM_QK_DIM
    assert seq % MLSTM_CHUNK == 0 and dv == LANES
    bf16 = jnp.bfloat16
    pad = ((0, 0), (0, 0), (0, LANES - dk))
    wq = jnp.pad(w_qk[:, :, :dk], pad).astype(bf16)
    wk = jnp.pad(w_qk[:, :, dk:], pad).astype(bf16)
    rpad = ((0, 0), (0, LANES - dk), (0, 0))
    wiq = jnp.pad(w_if[:H * dk].reshape(H, dk, 2 * H), rpad).astype(bf16)
    wik = jnp.pad(w_if[H * dk:2 * H * dk].reshape(H, dk, 2 * H), rpad).astype(bf16)
    wix = w_if[2 * H * dk:].astype(bf16)
    wiqt, wikt, wixt = wiq.transpose(0, 2, 1), wik.transpose(0, 2, 1), wix.T
    full = lambda a: pl.BlockSpec(a.shape, lambda b: (0,) * a.ndim)
    consts = [conv_w, conv_b.reshape(1, -1), wq, wk, wiq, wik, wix, wiqt, wikt, wixt,
              b_if.reshape(1, -1), b_if.reshape(-1, 1), g.reshape(1, -1)]
    return pl.pallas_call(
        functools.partial(_mlstm_kernel, seq=seq),
        grid=(batch,),
        in_specs=[pl.BlockSpec((seq, MLSTM_WIDTH), lambda b: (b, 0)),
                  pl.BlockSpec((seq, MLSTM_WIDTH), lambda b: (b, 1))] + [full(a) for a in consts],
        out_specs=pl.BlockSpec((seq, MLSTM_WIDTH), lambda b: (b, 0)),
        out_shape=jax.ShapeDtypeStruct((n, MLSTM_WIDTH), bf16),
        scratch_shapes=[pltpu.VMEM((H, seq, LANES), bf16), pltpu.VMEM((H, seq, LANES), bf16),
                        pltpu.VMEM((seq, 2 * H), jnp.float32),
                        pltpu.VMEM((seq // MLSTM_CHUNK, 2 * H, MLSTM_CHUNK), jnp.float32),
                        pltpu.VMEM((H, LANES, LANES), jnp.float32),
                        pltpu.VMEM((H, 1, LANES), jnp.float32),
                        pltpu.VMEM((H, 1, LANES), jnp.float32)],
        compiler_params=_params("parallel"),
        name="mlstm",
    )(xo, xo, *consts)


def _store_rows_as_tiles(dst_ref, val):
    rows = val.shape[0]
    for c in range(ROW_CHUNKS):
        dst_ref[pl.ds(c, rows, stride=ROW_CHUNKS), :] = val[:, c * LANES:(c + 1) * LANES]


def _load_tiles_as_rows(src_ref, rows, lead=()):
    return jnp.concatenate(
        [src_ref[lead + (pl.ds(c, rows, stride=ROW_CHUNKS), slice(None))] for c in range(ROW_CHUNKS)],
        axis=1)


def _merge_route_body(x_ref, ya_ref, ym_ref, ga_ref, gm_ref, wb0_ref, wb1_ref, wo_ref, g_ref,
                      wrt_ref, br_ref, x1_ref, h2_ref, idx_ref, gate_ref, rank_ref, cnt_ref,
                      run_sc):
    f32, bf16 = jnp.float32, jnp.bfloat16
    tm = x_ref.shape[0]

    @pl.when(pl.program_id(0) == 0)
    def _():
        run_sc[...] = jnp.zeros_like(run_sc)

    u = jax.nn.sigmoid(ga_ref[...].astype(f32)) * jnp.dot(ya_ref[...], wb0_ref[...], preferred_element_type=f32)
    u = u + jax.nn.sigmoid(gm_ref[...].astype(f32)) * jnp.dot(ym_ref[...], wb1_ref[...], preferred_element_type=f32)
    x1 = x_ref[...] + jnp.dot(u.astype(bf16), wo_ref[...], preferred_element_type=f32)
    x1_ref[...] = x1
    h2 = _rms(x1, g_ref[...])
    _store_rows_as_tiles(h2_ref, h2)

    logits = lax.dot_general(wrt_ref[...], h2, _NT, precision=lax.Precision.HIGHEST,
                             preferred_element_type=f32) + br_ref[...]
    e_id = lax.broadcasted_iota(jnp.int32, logits.shape, 0).astype(f32)
    chosen = jnp.zeros(logits.shape, f32)
    vals, ids = [], []
    for _ in range(TOP_K):
        top = jnp.max(logits, axis=0, keepdims=True)
        first = jnp.min(jnp.where(logits == top, e_id, float(N_EXPERTS)), axis=0, keepdims=True)
        hit = e_id == first
        chosen = jnp.where(hit, 1.0, chosen)
        logits = jnp.where(hit, -jnp.inf, logits)
        vals.append(top)
        ids.append(first)
    ex = [jnp.exp(v - vals[0]) for v in vals]
    total = ex[0] + ex[1] + ex[2] + ex[3]

    t_r = lax.broadcasted_iota(jnp.int32, (tm, tm), 0)
    t_c = lax.broadcasted_iota(jnp.int32, (tm, tm), 1)
    earlier = jnp.where(t_r < t_c, 1.0, 0.0).astype(bf16)
    before = jnp.dot(chosen.astype(bf16), earlier, preferred_element_type=f32) + run_sc[:, 0:1]
    for kk in range(TOP_K):
        idx_ref[kk:kk + 1, :] = ids[kk].astype(jnp.int32)
        gate_ref[kk:kk + 1, :] = ex[kk] / total
        rank_ref[kk:kk + 1, :] = jnp.sum(jnp.where(e_id == ids[kk], before, 0.0), axis=0,
                                         keepdims=True).astype(jnp.int32)
    run_sc[...] = run_sc[...] + jnp.sum(chosen, axis=1, keepdims=True)
    cnt_ref[...] = run_sc[...].astype(jnp.int32)


def merge_route(x2, ya, ym, gates, wb0, wb1, wo, g, w_router, b_router, *, tm=256):
    n, d = x2.shape
    e = w_router.shape[1]
    assert n % tm == 0 and d == ROW_CHUNKS * LANES and e == N_EXPERTS
    full = lambda a: pl.BlockSpec(a.shape, lambda i: (0,) * a.ndim)
    consts = [wb0, wb1, wo, g.reshape(1, d), w_router.T, b_router.reshape(e, 1)]
    tok = lambda dt: jax.ShapeDtypeStruct((TOP_K, n), dt)
    return pl.pallas_call(
        _merge_route_body,
        grid=(n // tm,),
        in_specs=[pl.BlockSpec((tm, d), lambda i: (i, 0)),
                  pl.BlockSpec((tm, ATT_WIDTH), lambda i: (i, 0)),
                  pl.BlockSpec((tm, MLSTM_WIDTH), lambda i: (i, 0)),
                  pl.BlockSpec((tm, d), lambda i: (i, 0)),
                  pl.BlockSpec((tm, d), lambda i: (i, 1))] + [full(a) for a in consts],
        out_specs=[pl.BlockSpec((tm, d), lambda i: (i, 0)),
                   pl.BlockSpec((tm * ROW_CHUNKS, LANES), lambda i: (i, 0)),
                   pl.BlockSpec((TOP_K, tm), lambda i: (0, i)),
                   pl.BlockSpec((TOP_K, tm), lambda i: (0, i)),
                   pl.BlockSpec((TOP_K, tm), lambda i: (0, i)),
                   pl.BlockSpec((e, LANES), lambda i: (0, 0))],
        out_shape=[jax.ShapeDtypeStruct((n, d), jnp.float32),
                   jax.ShapeDtypeStruct((n * ROW_CHUNKS, LANES), jnp.float32),
                   tok(jnp.int32), tok(jnp.float32), tok(jnp.int32),
                   jax.ShapeDtypeStruct((e, LANES), jnp.int32)],
        scratch_shapes=[pltpu.VMEM((e, LANES), jnp.float32)],
        compiler_params=_params("arbitrary"),
        name="merge_route",
    )(x2, ya, ym, gates, gates, *consts)


def _dispatch_body(dest_ref, src_ref, init_ref, dst_ref, sem, *, tc):
    del init_ref
    base = pl.program_id(0) * tc

    def copy(t, kk):
        return pltpu.make_async_copy(src_ref.at[base + t], dst_ref.at[dest_ref[kk, t]], sem)

    def start(t, carry):
        for kk in range(TOP_K):
            copy(t, kk).start()
        return carry

    def wait(t, carry):
        for kk in range(TOP_K):
            copy(t, kk).wait()
        return carry

    lax.fori_loop(0, tc, start, 0)
    lax.fori_loop(0, tc, wait, 0)


def dispatch(dest, h2_tiles, n_rows, *, tc=512):
    n = h2_tiles.shape[0]
    assert n % tc == 0
    init = jnp.zeros((n_rows,) + h2_tiles.shape[1:], h2_tiles.dtype)
    return pl.pallas_call(
        functools.partial(_dispatch_body, tc=tc),
        grid=(n // tc,),
        in_specs=[pl.BlockSpec((TOP_K, tc), lambda i: (0, i), memory_space=pltpu.SMEM),
                  pl.BlockSpec(memory_space=pl.ANY),
                  pl.BlockSpec(memory_space=pl.ANY)],
        out_specs=pl.BlockSpec(memory_space=pl.ANY),
        out_shape=jax.ShapeDtypeStruct(init.shape, init.dtype),
        scratch_shapes=[pltpu.SemaphoreType.DMA(())],
        input_output_aliases={2: 0},
        compiler_params=_params("arbitrary"),
        name="dispatch",
    )(dest, h2_tiles, init)


def _collect_body(dest_ref, src_ref, dst_ref, sem, *, tc):
    base = pl.program_id(0) * tc

    def copy(t, kk):
        return pltpu.make_async_copy(src_ref.at[dest_ref[kk, t]], dst_ref.at[kk, base + t], sem)

    def start(t, carry):
        for kk in range(TOP_K):
            copy(t, kk).start()
        return carry

    def wait(t, carry):
        for kk in range(TOP_K):
            copy(t, kk).wait()
        return carry

    lax.fori_loop(0, tc, start, 0)
    lax.fori_loop(0, tc, wait, 0)


def collect(dest, y_tiles, n, *, tc=512):
    assert n % tc == 0
    return pl.pallas_call(
        functools.partial(_collect_body, tc=tc),
        grid=(n // tc,),
        in_specs=[pl.BlockSpec((TOP_K, tc), lambda i: (0, i), memory_space=pltpu.SMEM),
                  pl.BlockSpec(memory_space=pl.ANY)],
        out_specs=pl.BlockSpec(memory_space=pl.ANY),
        out_shape=jax.ShapeDtypeStruct((TOP_K, n) + y_tiles.shape[1:], y_tiles.dtype),
        scratch_shapes=[pltpu.SemaphoreType.DMA(())],
        compiler_params=_params("arbitrary"),
        name="collect",
    )(dest, y_tiles)


def _experts_body(blk_e_ref, used_ref, x_ref, wg_ref, bg_ref, wu_ref, bu_ref, wd_ref, bd_ref, y_ref):
    del blk_e_ref
    rows = x_ref.shape[0] // ROW_CHUNKS

    @pl.when(pl.program_id(0) < used_ref[0])
    def _():
        f32, bf16 = jnp.float32, jnp.bfloat16
        xb = _load_tiles_as_rows(x_ref, rows).astype(bf16)
        gate = jnp.dot(xb, wg_ref[0], preferred_element_type=f32) + bg_ref[0]
        up = jnp.dot(xb, wu_ref[0], preferred_element_type=f32) + bu_ref[0]
        gate = jnp.minimum(gate, SWIGLU_LIMIT)
        up = jnp.clip(up, -SWIGLU_LIMIT, SWIGLU_LIMIT)
        act = (up + 1.0) * (gate * jax.nn.sigmoid(gate * SWIGLU_ALPHA))
        y = jnp.dot(act.astype(bf16), wd_ref[0], preferred_element_type=f32) + bd_ref[0]
        _store_rows_as_tiles(y_ref, y)

    @pl.when(pl.program_id(0) >= used_ref[0])
    def _():
        y_ref[...] = jnp.zeros_like(y_ref)


def experts(blk_expert, n_used, xs_tiles, w_g, b_g, w_u, b_u, w_d, b_d):
    n_blocks = blk_expert.shape[0]
    rb = MOE_ROW_BLOCK
    d, ff = w_g.shape[1], w_g.shape[2]
    e_map3 = lambda i, be, nu: (be[i], 0, 0)
    grid_spec = pltpu.PrefetchScalarGridSpec(
        num_scalar_prefetch=2,
        grid=(n_blocks,),
        in_specs=[pl.BlockSpec((rb * ROW_CHUNKS, LANES), lambda i, be, nu: (i, 0)),
                  pl.BlockSpec((1, d, ff), e_map3), pl.BlockSpec((1, 1, ff), e_map3),
                  pl.BlockSpec((1, d, ff), e_map3), pl.BlockSpec((1, 1, ff), e_map3),
                  pl.BlockSpec((1, ff, d), e_map3), pl.BlockSpec((1, 1, d), e_map3)],
        out_specs=pl.BlockSpec((rb * ROW_CHUNKS, LANES), lambda i, be, nu: (i, 0)),
    )
    return pl.pallas_call(
        _experts_body,
        grid_spec=grid_spec,
        out_shape=jax.ShapeDtypeStruct(xs_tiles.shape, jnp.float32),
        compiler_params=_params("arbitrary"),
        name="experts",
    )(blk_expert, n_used, xs_tiles, w_g, b_g, w_u, b_u, w_d, b_d)


def _combine_ple_body(x1_ref, ys_ref, gw_ref, p_ref, gp_ref, wpg_ref, wpp_ref, *rest):
    gf_ref, o_ref = rest if len(rest) == 2 else (None, rest[0])
    f32, bf16 = jnp.float32, jnp.bfloat16
    tm = x1_ref.shape[0]
    x2 = x1_ref[...]
    for kk in range(TOP_K):
        x2 = x2 + gw_ref[:, kk:kk + 1] * _load_tiles_as_rows(ys_ref, tm, lead=(kk,))
    r = _rms(x2, gp_ref[...]).astype(bf16)
    gate = jax.nn.sigmoid(jnp.dot(r, wpg_ref[...], preferred_element_type=f32))
    emb = jnp.dot(p_ref[...].astype(bf16), wpp_ref[...], preferred_element_type=f32)
    x3 = x2 + gate * emb
    o_ref[...] = x3 if gf_ref is None else _rms(x3, gf_ref[...])


def combine_ple(x1, ys_slots, gate_cols, p2, g_ple, w_pg, w_pp, g_final, *, tm=256):
    n, d = x1.shape
    full = lambda a: pl.BlockSpec(a.shape, lambda i: (0,) * a.ndim)
    consts = [g_ple.reshape(1, d), w_pg, w_pp] + ([] if g_final is None else [g_final.reshape(1, d)])
    return pl.pallas_call(
        _combine_ple_body,
        grid=(n // tm,),
        in_specs=[pl.BlockSpec((tm, d), lambda i: (i, 0)),
                  pl.BlockSpec((TOP_K, tm * ROW_CHUNKS, LANES), lambda i: (0, i, 0)),
                  pl.BlockSpec((tm, TOP_K), lambda i: (i, 0)),
                  pl.BlockSpec((tm, p2.shape[1]), lambda i: (i, 0))] + [full(a) for a in consts],
        out_specs=pl.BlockSpec((tm, d), lambda i: (i, 0)),
        out_shape=jax.ShapeDtypeStruct((n, d), jnp.float32),
        compiler_params=_params("parallel"),
        name="combine_ple",
    )(x1, ys_slots, gate_cols, p2, *consts)


def _moe(h2_tiles, top_idx, rank, counts, w_gu, b_gu, w_d, b_d):
    n = h2_tiles.shape[0] // ROW_CHUNKS
    rb = MOE_ROW_BLOCK
    n_blocks = -(-n * TOP_K // rb) + N_EXPERTS
    padded = (counts + rb - 1) // rb * rb
    pend = jnp.cumsum(padded)
    pstart = pend - padded
    dest = jnp.take(pstart, top_idx) + rank
    blk_expert = jnp.minimum(
        jnp.searchsorted(pend, jnp.arange(n_blocks, dtype=jnp.int32) * rb, side='right'),
        N_EXPERTS - 1).astype(jnp.int32)
    n_used = (pend[-1:] // rb).astype(jnp.int32)
    bf16 = jnp.bfloat16
    w_g, w_u = w_gu[..., 0::2].astype(bf16), w_gu[..., 1::2].astype(bf16)
    b_g, b_u = b_gu[:, None, 0::2], b_gu[:, None, 1::2]
    xs = dispatch(dest, h2_tiles.reshape(n, ROW_CHUNKS, LANES), n_blocks * rb)
    ys = experts(blk_expert, n_used, xs.reshape(-1, LANES), w_g, b_g, w_u, b_u,
                 w_d.astype(bf16), b_d[:, None, :])
    slots = collect(dest, ys.reshape(-1, ROW_CHUNKS, LANES), n)
    return slots.reshape(TOP_K, n * ROW_CHUNKS, LANES)


def kernel(x, p, norm_mix_g, w_in, conv_w, conv_b, w_qk_m, w_if, b_if, mnorm_g, w_branch, w_out, norm_ffn_g, w_router, b_router, w_gate_up, b_gate_up, w_down, b_down, norm_ple_g, w_ple_gate, w_ple_proj, final_norm_g):
    B, S, D = x.shape
    depth = w_in.shape[0]
    bf16 = jnp.bfloat16
    x2 = x.reshape(B * S, D)
    for i in range(depth):
        qkv, xo, gates = in_proj(x2, norm_mix_g[i], w_in[i].astype(bf16))
        y_a = moba(qkv, B, S)
        y_m = mlstm(xo, conv_w[i], conv_b[i], w_qk_m[i], w_if[i], b_if[i], mnorm_g[i], B, S)
        x1, h2_tiles, top_idx, gate_w, rank, counts = merge_route(
            x2, y_a, y_m, gates, w_branch[i, 0].astype(bf16), w_branch[i, 1].astype(bf16),
            w_out[i].astype(bf16), norm_ffn_g[i], w_router[i], b_router[i])
        slots = _moe(h2_tiles, top_idx, rank, counts[:, 0], w_gate_up[i], b_gate_up[i],
                     w_down[i], b_down[i])
        x2 = combine_ple(x1, slots, gate_w.T, p[i].reshape(B * S, -1), norm_ple_g[i],
                         w_ple_gate[i].astype(bf16), w_ple_proj[i].astype(bf16),
                         final_norm_g if i == depth - 1 else None)
    return x2.reshape(B, S, D)
```

```python
import functools

import jax
import jax.numpy as jnp
import numpy as np
from jax import lax
from jax.experimental import pallas as pl
from jax.experimental.pallas import tpu as pltpu

RMS_EPS = 1e-6
LANES = 128
ROW_CHUNKS = 8
VMEM_LIMIT = 56 * 1024 * 1024

N_ATT_HEADS = 8
ATT_HEAD_DIM = 64
ATT_WIDTH = N_ATT_HEADS * ATT_HEAD_DIM
MOBA_BLOCK = 256
MOBA_TOPK = 3

N_MLSTM_HEADS = 4
MLSTM_WIDTH = 512
MLSTM_V_DIM = 128
MLSTM_QK_DIM = 64
MLSTM_CONV = 4
MLSTM_CHUNK = 128

N_EXPERTS = 32
TOP_K = 4
SWIGLU_ALPHA = 1.702
SWIGLU_LIMIT = 7.0
MOE_ROW_BLOCK = 256

NEG_BIG = -1e30

_NT = (((1,), (1,)), ((), ()))
_TN = (((0,), (0,)), ((), ()))


def _params(*sem):
    return pltpu.CompilerParams(dimension_semantics=sem, vmem_limit_bytes=VMEM_LIMIT)


def _rms(x, g):
    return x * lax.rsqrt(jnp.mean(x * x, axis=-1, keepdims=True) + RMS_EPS) * g


def _in_proj_body(x_ref, g_ref, w_ref, qkv_ref, xo_ref, gate_ref, *, col_chunk):
    h = _rms(x_ref[...], g_ref[...]).astype(jnp.bfloat16)
    col = 0
    for out_ref in (qkv_ref, xo_ref, gate_ref):
        for c in range(0, out_ref.shape[1], col_chunk):
            out_ref[:, c:c + col_chunk] = jnp.dot(
                h, w_ref[:, col + c:col + c + col_chunk],
                preferred_element_type=jnp.float32).astype(out_ref.dtype)
        col += out_ref.shape[1]


def in_proj(x2, g, w_bf16, *, tm=512, col_chunk=512):
    n, d = x2.shape
    widths = (3 * ATT_WIDTH, 2 * MLSTM_WIDTH, 2 * d)
    assert sum(widths) == w_bf16.shape[1] and n % tm == 0
    return pl.pallas_call(
        functools.partial(_in_proj_body, col_chunk=col_chunk),
        grid=(n // tm,),
        in_specs=[pl.BlockSpec((tm, d), lambda i: (i, 0)),
                  pl.BlockSpec((1, d), lambda i: (0, 0)),
                  pl.BlockSpec(w_bf16.shape, lambda i: (0, 0))],
        out_specs=[pl.BlockSpec((tm, w), lambda i: (i, 0)) for w in widths],
        out_shape=[jax.ShapeDtypeStruct((n, w), jnp.bfloat16) for w in widths],
        compiler_params=_params("parallel"),
        name="in_proj",
    )(x2, g.reshape(1, d), w_bf16)


def _moba_body(q_ref, k_ref, v_ref, o_ref, *, pair, seq):
    blk = MOBA_BLOCK
    n_blk = seq // blk
    lane = lax.broadcasted_iota(jnp.int32, (1, LANES), 1)
    row = lax.broadcasted_iota(jnp.int32, (blk, blk), 0)
    col = lax.broadcasted_iota(jnp.int32, (blk, blk), 1)
    rel = (row - col).astype(jnp.float32)
    causal = row >= col
    scale = ATT_HEAD_DIM ** -0.5

    k_all = k_ref[...]
    v_all = v_ref[...]
    k_mean = jnp.concatenate(
        [jnp.mean(k_all[j * blk:(j + 1) * blk].astype(jnp.float32), axis=0, keepdims=True)
         for j in range(n_blk)], axis=0)

    outs = []
    for hh in range(2):
        head = 2 * pair + hh
        slope = jnp.exp2(-8.0 * (head + 1).astype(jnp.float32) / N_ATT_HEADS)
        head_lanes = (lane >= hh * ATT_HEAD_DIM) & (lane < (hh + 1) * ATT_HEAD_DIM)
        km_h = jnp.where(head_lanes, k_mean, 0.0)
        head_out = []
        for qi in range(n_blk):
            q_blk = q_ref[qi * blk:(qi + 1) * blk, :]
            q_h = jnp.where(head_lanes, q_blk, jnp.zeros_like(q_blk))
            sel = None
            if qi > MOBA_TOPK:
                q32 = q_h.astype(jnp.float32)
                gates = [lax.dot_general(
                    q32, jnp.broadcast_to(km_h[n:n + 1], (LANES, LANES)), _NT,
                    precision=lax.Precision.HIGHEST, preferred_element_type=jnp.float32)
                    for n in range(qi)]
                sel = []
                for n in range(qi):
                    ahead = jnp.zeros((blk, LANES), jnp.float32)
                    for m in range(qi):
                        if m == n:
                            continue
                        beats = (gates[m] >= gates[n]) if m < n else (gates[m] > gates[n])
                        ahead = ahead + jnp.where(beats, 1.0, 0.0)
                    sel.append(ahead < MOBA_TOPK)
            logits = []
            for j in range(qi + 1):
                s = lax.dot_general(q_h, k_all[j * blk:(j + 1) * blk], _NT,
                                    preferred_element_type=jnp.float32)
                s = s * scale - slope * (rel + float((qi - j) * blk))
                if j == qi:
                    s = jnp.where(causal, s, NEG_BIG)
                elif sel is not None:
                    s = jnp.where(jnp.concatenate([sel[j]] * (blk // LANES), axis=1), s, NEG_BIG)
                logits.append(s)
            s_all = jnp.concatenate(logits, axis=1)
            m = jnp.max(s_all, axis=1, keepdims=True)
            p = jnp.exp(s_all - m)
            denom = jnp.sum(p, axis=1, keepdims=True)
            o = jnp.dot(p.astype(jnp.bfloat16), v_all[:(qi + 1) * blk],
                        preferred_element_type=jnp.float32)
            head_out.append(o / denom)
        outs.append((head_lanes, head_out))
    for qi in range(n_blk):
        o = jnp.where(outs[0][0], outs[0][1][qi], outs[1][1][qi])
        o_ref[qi * blk:(qi + 1) * blk, :] = o.astype(o_ref.dtype)


def _moba_kernel(q_ref, k_ref, v_ref, o_ref, *, seq):
    _moba_body(q_ref, k_ref, v_ref, o_ref, pair=pl.program_id(1), seq=seq)


def moba(qkv, batch, seq):
    n = batch * seq
    n_pairs = ATT_WIDTH // LANES
    assert seq % MOBA_BLOCK == 0
    spec = lambda off: pl.BlockSpec((seq, LANES), lambda b, p: (b, off + p))
    return pl.pallas_call(
        functools.partial(_moba_kernel, seq=seq),
        grid=(batch, n_pairs),
        in_specs=[spec(0), spec(n_pairs), spec(2 * n_pairs)],
        out_specs=pl.BlockSpec((seq, LANES), lambda b, p: (b, p)),
        out_shape=jax.ShapeDtypeStruct((n, ATT_WIDTH), jnp.bfloat16),
        compiler_params=_params("parallel", "parallel"),
        name="moba",
    )(qkv, qkv, qkv)


def _log_sigmoid(z):
    return jnp.minimum(z, 0.0) - jnp.log(1.0 + jnp.exp(-jnp.abs(z)))


def _mlstm_body(xm_ref, om_ref, cw_ref, cb_ref, wq_ref, wk_ref, wiq_ref, wik_ref, wix_ref,
                wiqt_ref, wikt_ref, wixt_ref, bi_ref, bit_ref, g_ref, y_ref,
                q_sc, k_sc, gcol_sc, grow_sc, ct_sc, n_sc, m_sc, *, seq):
    L = MLSTM_CHUNK
    H = N_MLSTM_HEADS
    n_chunks = seq // L
    f32, bf16 = jnp.float32, jnp.bfloat16

    x = xm_ref[...].astype(f32)
    t_idx = lax.broadcasted_iota(jnp.int32, (seq, 1), 0)
    acc = x * cw_ref[MLSTM_CONV - 1:MLSTM_CONV, :] + cb_ref[...]
    for d in range(1, MLSTM_CONV):
        shifted = jnp.where(t_idx >= d, pltpu.roll(x, d, axis=0), 0.0)
        acc = acc + shifted * cw_ref[MLSTM_CONV - 1 - d:MLSTM_CONV - d, :]
    xc = (acc * jax.nn.sigmoid(acc)).astype(bf16)

    xm = xm_ref[...]
    if_col = jnp.dot(xm, wix_ref[...], preferred_element_type=f32) + bi_ref[...]
    if_row = lax.dot_general(wixt_ref[...], xm, _NT, preferred_element_type=f32) + bit_ref[...]
    for h in range(H):
        xch = xc[:, h * LANES:(h + 1) * LANES]
        q = jnp.dot(xch, wq_ref[h], preferred_element_type=f32).astype(bf16)
        k = jnp.dot(xch, wk_ref[h], preferred_element_type=f32).astype(bf16)
        if_col = if_col + jnp.dot(q, wiq_ref[h], preferred_element_type=f32)
        if_col = if_col + jnp.dot(k, wik_ref[h], preferred_element_type=f32)
        if_row = if_row + lax.dot_general(wiqt_ref[h], q, _NT, preferred_element_type=f32)
        if_row = if_row + lax.dot_general(wikt_ref[h], k, _NT, preferred_element_type=f32)
        q_sc[h] = q
        k_sc[h] = (k.astype(f32) * (MLSTM_QK_DIM ** -0.5)).astype(bf16)
    is_f_col = lax.broadcasted_iota(jnp.int32, (1, 2 * H), 1) >= H
    is_f_row = lax.broadcasted_iota(jnp.int32, (2 * H, 1), 0) >= H
    gcol_sc[...] = jnp.where(is_f_col, _log_sigmoid(if_col), if_col)
    g_row = jnp.where(is_f_row, _log_sigmoid(if_row), if_row)
    for c in range(n_chunks):
        grow_sc[c] = g_row[:, c * L:(c + 1) * L]

    ct_sc[...] = jnp.zeros_like(ct_sc)
    n_sc[...] = jnp.zeros_like(n_sc)
    m_sc[...] = jnp.zeros_like(m_sc)

    r_i = lax.broadcasted_iota(jnp.int32, (L, L), 0)
    c_i = lax.broadcasted_iota(jnp.int32, (L, L), 1)
    causal = r_i >= c_i
    tri_lo = jnp.where(causal, 1.0, 0.0).astype(f32)
    tri_up = jnp.where(c_i >= r_i, 1.0, 0.0).astype(f32)

    def chunk(c, carry):
        r0 = pl.multiple_of(c * L, L)
        gc = gcol_sc[pl.ds(r0, L), :]
        gr = grow_sc[c]
        b_cols = jnp.dot(tri_lo, gc, precision=lax.Precision.HIGHEST, preferred_element_type=f32)
        b_rows = jnp.dot(gr, tri_up, precision=lax.Precision.HIGHEST, preferred_element_type=f32)
        for h in range(H):
            hs = slice(h * LANES, (h + 1) * LANES)
            b_col = b_cols[:, H + h:H + h + 1]
            i_col = gc[:, h:h + 1]
            b_row = b_rows[H + h:H + h + 1, :]
            i_row = gr[h:h + 1, :]
            b_last = b_col[L - 1:L, :]
            m_prev = m_sc[h][:, 0:1]
            q = q_sc[h, pl.ds(r0, L), :]
            k = k_sc[h, pl.ds(r0, L), :]
            v = xm_ref[pl.ds(r0, L), hs]

            log_d = jnp.where(causal, b_col + (i_row - b_row), NEG_BIG)
            inter = b_col + m_prev
            m_t = jnp.maximum(inter, jnp.max(log_d, axis=1, keepdims=True))
            w_inter = jnp.exp(inter - m_t)
            s = lax.dot_general(q, k, _NT, preferred_element_type=f32) * jnp.exp(log_d - m_t)
            num = jnp.dot(s.astype(bf16), v, preferred_element_type=f32)
            num = num + w_inter * jnp.dot(q, ct_sc[h].astype(bf16), preferred_element_type=f32)
            den = jnp.sum(s, axis=1, keepdims=True)
            den = den + w_inter * jnp.sum(q.astype(f32) * n_sc[h], axis=1, keepdims=True)
            hv = num / jnp.maximum(jnp.abs(den), jnp.exp(-m_t))
            hv = hv * lax.rsqrt(jnp.mean(hv * hv, axis=1, keepdims=True) + RMS_EPS) * g_ref[:, hs]
            o_gate = jax.nn.sigmoid(om_ref[pl.ds(r0, L), hs].astype(f32))
            y_ref[pl.ds(r0, L), hs] = (o_gate * hv).astype(y_ref.dtype)

            g_col = b_last - b_col + i_col
            m_new = jnp.maximum(b_last + m_prev, jnp.max(g_col, axis=0, keepdims=True))
            wg = jnp.exp(g_col - m_new)
            decay = jnp.exp(b_last + m_prev - m_new)
            wv = (wg * v.astype(f32)).astype(bf16)
            ct_sc[h] = decay * ct_sc[h] + lax.dot_general(k, wv, _TN, preferred_element_type=f32)
            n_sc[h] = decay * n_sc[h] + jnp.sum(wg * k.astype(f32), axis=0, keepdims=True)
            m_sc[h] = jnp.broadcast_to(m_new, (1, LANES))
        return carry

    lax.fori_loop(0, n_chunks, chunk, 0)


def _mlstm_kernel(*refs, seq):
    _mlstm_body(*refs, seq=seq)


def mlstm(xo, conv_w, conv_b, w_qk, w_if, b_if, g, batch, seq):
    n = batch * seq
    H, dv, dk = N_MLSTM_HEADS, MLSTM_V_DIM, MLSTM_QK_DIM
    assert seq % MLSTM_CHUNK == 0 and dv == LANES
    bf16 = jnp.bfloat16
    pad = ((0, 0), (0, 0), (0, LANES - dk))
    wq = jnp.pad(w_qk[:, :, :dk], pad).astype(bf16)
    wk = jnp.pad(w_qk[:, :, dk:], pad).astype(bf16)
    rpad = ((0, 0), (0, LANES - dk), (0, 0))
    wiq = jnp.pad(w_if[:H * dk].reshape(H, dk, 2 * H), rpad).astype(bf16)
    wik = jnp.pad(w_if[H * dk:2 * H * dk].reshape(H, dk, 2 * H), rpad).astype(bf16)
    wix = w_if[2 * H * dk:].astype(bf16)
    wiqt, wikt, wixt = wiq.transpose(0, 2, 1), wik.transpose(0, 2, 1), wix.T
    full = lambda a: pl.BlockSpec(a.shape, lambda b: (0,) * a.ndim)
    consts = [conv_w, conv_b.reshape(1, -1), wq, wk, wiq, wik, wix, wiqt, wikt, wixt,
              b_if.reshape(1, -1), b_if.reshape(-1, 1), g.reshape(1, -1)]
    return pl.pallas_call(
        functools.partial(_mlstm_kernel, seq=seq),
        grid=(batch,),
        in_specs=[pl.BlockSpec((seq, MLSTM_WIDTH), lambda b: (b, 0)),
                  pl.BlockSpec((seq, MLSTM_WIDTH), lambda b: (b, 1))] + [full(a) for a in consts],
        out_specs=pl.BlockSpec((seq, MLSTM_WIDTH), lambda b: (b, 0)),
        out_shape=jax.ShapeDtypeStruct((n, MLSTM_WIDTH), bf16),
        scratch_shapes=[pltpu.VMEM((H, seq, LANES), bf16), pltpu.VMEM((H, seq, LANES), bf16),
                        pltpu.VMEM((seq, 2 * H), jnp.float32),
                        pltpu.VMEM((seq // MLSTM_CHUNK, 2 * H, MLSTM_CHUNK), jnp.float32),
                        pltpu.VMEM((H, LANES, LANES), jnp.float32),
                        pltpu.VMEM((H, 1, LANES), jnp.float32),
                        pltpu.VMEM((H, 1, LANES), jnp.float32)],
        compiler_params=_params("parallel"),
        name="mlstm",
    )(xo, xo, *consts)


def _store_rows_as_tiles(dst_ref, val):
    rows = val.shape[0]
    for c in range(ROW_CHUNKS):
        dst_ref[pl.ds(c, rows, stride=ROW_CHUNKS), :] = val[:, c * LANES:(c + 1) * LANES]


def _load_tiles_as_rows(src_ref, rows, lead=()):
    return jnp.concatenate(
        [src_ref[lead + (pl.ds(c, rows, stride=ROW_CHUNKS), slice(None))] for c in range(ROW_CHUNKS)],
        axis=1)


def _merge_route_body(x_ref, ya_ref, ym_ref, ga_ref, gm_ref, wb0_ref, wb1_ref, wo_ref, g_ref,
                      wrt_ref, br_ref, x1_ref, h2_ref, idx_ref, gate_ref, rank_ref, cnt_ref,
                      run_sc):
    f32, bf16 = jnp.float32, jnp.bfloat16
    tm = x_ref.shape[0]

    @pl.when(pl.program_id(0) == 0)
    def _():
        run_sc[...] = jnp.zeros_like(run_sc)

    u = jax.nn.sigmoid(ga_ref[...].astype(f32)) * jnp.dot(ya_ref[...], wb0_ref[...], preferred_element_type=f32)
    u = u + jax.nn.sigmoid(gm_ref[...].astype(f32)) * jnp.dot(ym_ref[...], wb1_ref[...], preferred_element_type=f32)
    x1 = x_ref[...] + jnp.dot(u.astype(bf16), wo_ref[...], preferred_element_type=f32)
    x1_ref[...] = x1
    h2 = _rms(x1, g_ref[...])
    _store_rows_as_tiles(h2_ref, h2)

    logits = lax.dot_general(wrt_ref[...], h2, _NT, precision=lax.Precision.HIGHEST,
                             preferred_element_type=f32) + br_ref[...]
    e_id = lax.broadcasted_iota(jnp.int32, logits.shape, 0).astype(f32)
    chosen = jnp.zeros(logits.shape, f32)
    vals, ids = [], []
    for _ in range(TOP_K):
        top = jnp.max(logits, axis=0, keepdims=True)
        first = jnp.min(jnp.where(logits == top, e_id, float(N_EXPERTS)), axis=0, keepdims=True)
        hit = e_id == first
        chosen = jnp.where(hit, 1.0, chosen)
        logits = jnp.where(hit, -jnp.inf, logits)
        vals.append(top)
        ids.append(first)
    ex = [jnp.exp(v - vals[0]) for v in vals]
    total = ex[0] + ex[1] + ex[2] + ex[3]

    t_r = lax.broadcasted_iota(jnp.int32, (tm, tm), 0)
    t_c = lax.broadcasted_iota(jnp.int32, (tm, tm), 1)
    earlier = jnp.where(t_r < t_c, 1.0, 0.0).astype(bf16)
    before = jnp.dot(chosen.astype(bf16), earlier, preferred_element_type=f32) + run_sc[:, 0:1]
    for kk in range(TOP_K):
        idx_ref[kk:kk + 1, :] = ids[kk].astype(jnp.int32)
        gate_ref[kk:kk + 1, :] = ex[kk] / total
        rank_ref[kk:kk + 1, :] = jnp.sum(jnp.where(e_id == ids[kk], before, 0.0), axis=0,
                                         keepdims=True).astype(jnp.int32)
    run_sc[...] = run_sc[...] + jnp.sum(chosen, axis=1, keepdims=True)
    cnt_ref[...] = run_sc[...].astype(jnp.int32)


def merge_route(x2, ya, ym, gates, wb0, wb1, wo, g, w_router, b_router, *, tm=256):
    n, d = x2.shape
    e = w_router.shape[1]
    assert n % tm == 0 and d == ROW_CHUNKS * LANES and e == N_EXPERTS
    full = lambda a: pl.BlockSpec(a.shape, lambda i: (0,) * a.ndim)
    consts = [wb0, wb1, wo, g.reshape(1, d), w_router.T, b_router.reshape(e, 1)]
    tok = lambda dt: jax.ShapeDtypeStruct((TOP_K, n), dt)
    return pl.pallas_call(
        _merge_route_body,
        grid=(n // tm,),
        in_specs=[pl.BlockSpec((tm, d), lambda i: (i, 0)),
                  pl.BlockSpec((tm, ATT_WIDTH), lambda i: (i, 0)),
                  pl.BlockSpec((tm, MLSTM_WIDTH), lambda i: (i, 0)),
                  pl.BlockSpec((tm, d), lambda i: (i, 0)),
                  pl.BlockSpec((tm, d), lambda i: (i, 1))] + [full(a) for a in consts],
        out_specs=[pl.BlockSpec((tm, d), lambda i: (i, 0)),
                   pl.BlockSpec((tm * ROW_CHUNKS, LANES), lambda i: (i, 0)),
                   pl.BlockSpec((TOP_K, tm), lambda i: (0, i)),
                   pl.BlockSpec((TOP_K, tm), lambda i: (0, i)),
                   pl.BlockSpec((TOP_K, tm), lambda i: (0, i)),
                   pl.BlockSpec((e, LANES), lambda i: (0, 0))],
        out_shape=[jax.ShapeDtypeStruct((n, d), jnp.float32),
                   jax.ShapeDtypeStruct((n * ROW_CHUNKS, LANES), jnp.float32),
                   tok(jnp.int32), tok(jnp.float32), tok(jnp.int32),
                   jax.ShapeDtypeStruct((e, LANES), jnp.int32)],
        scratch_shapes=[pltpu.VMEM((e, LANES), jnp.float32)],
        compiler_params=_params("arbitrary"),
        name="merge_route",
    )(x2, ya, ym, gates, gates, *consts)


def _dispatch_body(dest_ref, pad_ref, x_ref, dst_ref, zero_sc, sem, *, tc, n_pad):
    zero_sc[...] = jnp.zeros_like(zero_sc)

    def row_copy(t, kk):
        src = x_ref.at[pl.ds(pl.multiple_of(t * ROW_CHUNKS, ROW_CHUNKS), ROW_CHUNKS), :]
        off = pl.multiple_of(dest_ref[kk, t], ROW_CHUNKS)
        return pltpu.make_async_copy(src, dst_ref.at[pl.ds(off, ROW_CHUNKS), :], sem)

    def pad_copy(j):
        off = pl.multiple_of(pad_ref[0, 0, j], ROW_CHUNKS)
        return pltpu.make_async_copy(zero_sc, dst_ref.at[pl.ds(off, ROW_CHUNKS), :], sem)

    def start_rows(t, carry):
        for kk in range(TOP_K):
            row_copy(t, kk).start()
        return carry

    def wait_rows(t, carry):
        for kk in range(TOP_K):
            row_copy(t, kk).wait()
        return carry

    def start_pad(j, carry):
        pad_copy(j).start()
        return carry

    def wait_pad(j, carry):
        pad_copy(j).wait()
        return carry

    lax.fori_loop(0, tc, start_rows, 0)
    lax.fori_loop(0, n_pad, start_pad, 0)
    lax.fori_loop(0, tc, wait_rows, 0)
    lax.fori_loop(0, n_pad, wait_pad, 0)


def dispatch(dest8, pad8, h2_tiles, n_rows, *, tc=1024):
    n = h2_tiles.shape[0] // ROW_CHUNKS
    steps = n // tc
    assert n % tc == 0 and pad8.shape[0] % steps == 0
    n_pad = pad8.shape[0] // steps
    return pl.pallas_call(
        functools.partial(_dispatch_body, tc=tc, n_pad=n_pad),
        grid=(steps,),
        in_specs=[pl.BlockSpec((TOP_K, tc), lambda i: (0, i), memory_space=pltpu.SMEM),
                  pl.BlockSpec((1, 1, n_pad), lambda i: (i, 0, 0), memory_space=pltpu.SMEM),
                  pl.BlockSpec((tc * ROW_CHUNKS, LANES), lambda i: (i, 0))],
        out_specs=pl.BlockSpec(memory_space=pl.ANY),
        out_shape=jax.ShapeDtypeStruct((n_rows * ROW_CHUNKS, LANES), h2_tiles.dtype),
        scratch_shapes=[pltpu.VMEM((ROW_CHUNKS, LANES), h2_tiles.dtype), pltpu.SemaphoreType.DMA(())],
        compiler_params=_params("arbitrary"),
        name="dispatch",
    )(dest8, pad8.reshape(steps, 1, n_pad), h2_tiles)


def _split_gate_up_body(w_ref, perm_ref, wg_ref, wu_ref):
    tile = perm_ref.shape[0]
    half = tile // 2
    w = w_ref[0].astype(jnp.bfloat16)
    for t in range(w.shape[1] // tile):
        o = jnp.dot(w[:, t * tile:(t + 1) * tile], perm_ref[...], preferred_element_type=jnp.float32)
        wg_ref[0, :, t * half:(t + 1) * half] = o[:, :half].astype(jnp.bfloat16)
        wu_ref[0, :, t * half:(t + 1) * half] = o[:, half:].astype(jnp.bfloat16)


def split_gate_up(w_gu, *, cols=1024):
    e, d, ff2 = w_gu.shape
    tile = 2 * LANES
    perm = np.zeros((tile, tile), np.float32)
    perm[2 * np.arange(LANES), np.arange(LANES)] = 1.0
    perm[2 * np.arange(LANES) + 1, LANES + np.arange(LANES)] = 1.0
    out = jax.ShapeDtypeStruct((e, d, ff2 // 2), jnp.bfloat16)
    return pl.pallas_call(
        _split_gate_up_body,
        grid=(e, ff2 // cols),
        in_specs=[pl.BlockSpec((1, d, cols), lambda i, j: (i, 0, j)),
                  pl.BlockSpec((tile, tile), lambda i, j: (0, 0))],
        out_specs=[pl.BlockSpec((1, d, cols // 2), lambda i, j: (i, 0, j))] * 2,
        out_shape=[out, out],
        compiler_params=_params("parallel", "parallel"),
        name="split_gate_up",
    )(w_gu, jnp.asarray(perm, jnp.bfloat16))


def _experts_body(blk_e_ref, used_ref, x_ref, wg_ref, bg_ref, wu_ref, bu_ref, wd_ref, bd_ref, y_ref):
    del blk_e_ref
    rows = x_ref.shape[0] // ROW_CHUNKS

    @pl.when(pl.program_id(0) < used_ref[0])
    def _():
        f32, bf16 = jnp.float32, jnp.bfloat16
        xb = _load_tiles_as_rows(x_ref, rows).astype(bf16)
        gate = jnp.dot(xb, wg_ref[0], preferred_element_type=f32) + bg_ref[0]
        up = jnp.dot(xb, wu_ref[0], preferred_element_type=f32) + bu_ref[0]
        gate = jnp.minimum(gate, SWIGLU_LIMIT)
        up = jnp.clip(up, -SWIGLU_LIMIT, SWIGLU_LIMIT)
        act = (up + 1.0) * (gate * jax.nn.sigmoid(gate * SWIGLU_ALPHA))
        y = jnp.dot(act.astype(bf16), wd_ref[0], preferred_element_type=f32) + bd_ref[0]
        _store_rows_as_tiles(y_ref, y)

    @pl.when(pl.program_id(0) >= used_ref[0])
    def _():
        y_ref[...] = jnp.zeros_like(y_ref)


def experts(blk_expert, n_used, xs_tiles, w_g, b_g, w_u, b_u, w_d, b_d):
    n_blocks = blk_expert.shape[0]
    rb = MOE_ROW_BLOCK
    d, ff = w_g.shape[1], w_g.shape[2]
    e_map3 = lambda i, be, nu: (be[i], 0, 0)
    grid_spec = pltpu.PrefetchScalarGridSpec(
        num_scalar_prefetch=2,
        grid=(n_blocks,),
        in_specs=[pl.BlockSpec((rb * ROW_CHUNKS, LANES), lambda i, be, nu: (i, 0)),
                  pl.BlockSpec((1, d, ff), e_map3), pl.BlockSpec((1, 1, ff), e_map3),
                  pl.BlockSpec((1, d, ff), e_map3), pl.BlockSpec((1, 1, ff), e_map3),
                  pl.BlockSpec((1, ff, d), e_map3), pl.BlockSpec((1, 1, d), e_map3)],
        out_specs=pl.BlockSpec((rb * ROW_CHUNKS, LANES), lambda i, be, nu: (i, 0)),
    )
    return pl.pallas_call(
        _experts_body,
        grid_spec=grid_spec,
        out_shape=jax.ShapeDtypeStruct(xs_tiles.shape, jnp.float32),
        compiler_params=_params("arbitrary"),
        name="experts",
    )(blk_expert, n_used, xs_tiles, w_g, b_g, w_u, b_u, w_d, b_d)


def _combine_ple_body(dcur_ref, dnext_ref, x1_ref, ys_ref, gw_ref, p_ref, gp_ref, wpg_ref, wpp_ref, *rest):
    buf, sem = rest[-2:]
    rest = rest[:-2]
    gf_ref, o_ref = rest if len(rest) == 2 else (None, rest[0])
    f32, bf16 = jnp.float32, jnp.bfloat16
    tm = x1_ref.shape[0]
    step = pl.program_id(0)
    slot = step % 2

    def row_copy(d_ref, sl, t, kk):
        off = pl.multiple_of(d_ref[kk, t], ROW_CHUNKS)
        dst = buf.at[sl, kk, pl.ds(pl.multiple_of(t * ROW_CHUNKS, ROW_CHUNKS), ROW_CHUNKS), :]
        return pltpu.make_async_copy(ys_ref.at[pl.ds(off, ROW_CHUNKS), :], dst, sem.at[sl])

    def start_tile(d_ref, sl):
        def body(t, carry):
            for kk in range(TOP_K):
                row_copy(d_ref, sl, t, kk).start()
            return carry
        lax.fori_loop(0, tm, body, 0)

    @pl.when(step == 0)
    def _():
        start_tile(dcur_ref, 0)

    @pl.when(step + 1 < pl.num_programs(0))
    def _():
        start_tile(dnext_ref, 1 - slot)

    def wait_body(t, carry):
        for kk in range(TOP_K):
            row_copy(dcur_ref, slot, t, kk).wait()
        return carry
    lax.fori_loop(0, tm, wait_body, 0)

    x2 = x1_ref[...]
    for kk in range(TOP_K):
        x2 = x2 + gw_ref[:, kk:kk + 1] * _load_tiles_as_rows(buf, tm, lead=(slot, kk))
    r = _rms(x2, gp_ref[...]).astype(bf16)
    gate = jax.nn.sigmoid(jnp.dot(r, wpg_ref[...], preferred_element_type=f32))
    emb = jnp.dot(p_ref[...].astype(bf16), wpp_ref[...], preferred_element_type=f32)
    x3 = x2 + gate * emb
    o_ref[...] = x3 if gf_ref is None else _rms(x3, gf_ref[...])


def combine_ple(dest8, x1, ys_tiles, gate_cols, p2, g_ple, w_pg, w_pp, g_final, *, tm=256):
    n, d = x1.shape
    steps = n // tm
    full = lambda a: pl.BlockSpec(a.shape, lambda i: (0,) * a.ndim)
    consts = [g_ple.reshape(1, d), w_pg, w_pp] + ([] if g_final is None else [g_final.reshape(1, d)])
    return pl.pallas_call(
        _combine_ple_body,
        grid=(steps,),
        in_specs=[pl.BlockSpec((TOP_K, tm), lambda i: (0, i), memory_space=pltpu.SMEM),
                  pl.BlockSpec((TOP_K, tm), lambda i: (0, jnp.minimum(i + 1, steps - 1)),
                               memory_space=pltpu.SMEM),
                  pl.BlockSpec((tm, d), lambda i: (i, 0)),
                  pl.BlockSpec(memory_space=pl.ANY),
                  pl.BlockSpec((tm, TOP_K), lambda i: (i, 0)),
                  pl.BlockSpec((tm, p2.shape[1]), lambda i: (i, 0))] + [full(a) for a in consts],
        out_specs=pl.BlockSpec((tm, d), lambda i: (i, 0)),
        out_shape=jax.ShapeDtypeStruct((n, d), jnp.float32),
        scratch_shapes=[pltpu.VMEM((2, TOP_K, tm * ROW_CHUNKS, LANES), ys_tiles.dtype),
                        pltpu.SemaphoreType.DMA((2,))],
        compiler_params=_params("arbitrary"),
        name="combine_ple",
    )(dest8, dest8, x1, ys_tiles, gate_cols, p2, *consts)


def _table_lookup(table, idx):
    hit = idx[..., None] == jnp.arange(table.shape[0], dtype=idx.dtype)
    return jnp.sum(jnp.where(hit, table, 0), axis=-1)


def _moe(h2_tiles, top_idx, rank, counts, w_gu, b_gu, w_d, b_d):
    n = h2_tiles.shape[0] // ROW_CHUNKS
    rb = MOE_ROW_BLOCK
    i32 = jnp.int32
    n_blocks = -(-n * TOP_K // rb) + N_EXPERTS
    n_rows = n_blocks * rb
    padded = (counts + rb - 1) // rb * rb
    pend = jnp.cumsum(padded)
    pstart = pend - padded
    dest8 = (_table_lookup(pstart, top_idx) + rank) * ROW_CHUNKS
    blk_start = jnp.arange(n_blocks, dtype=i32) * rb
    blk_expert = jnp.minimum(jnp.sum((pend[None, :] <= blk_start[:, None]).astype(i32), axis=1),
                             N_EXPERTS - 1)
    n_used = (pend[-1:] // rb).astype(i32)
    seg_len = jnp.concatenate([padded - counts, n_rows - pend[-1:]])
    seg_end = jnp.cumsum(seg_len)
    seg_shift = jnp.concatenate([pstart + counts, pend[-1:]]) - (seg_end - seg_len)
    q = jnp.arange(n_rows - n * TOP_K, dtype=i32)
    seg = jnp.sum((seg_end[None, :] <= q[:, None]).astype(i32), axis=1)
    pad8 = (q + _table_lookup(seg_shift, seg)) * ROW_CHUNKS
    w_g, w_u = split_gate_up(w_gu)
    b_g, b_u = b_gu[:, None, 0::2], b_gu[:, None, 1::2]
    xs = dispatch(dest8, pad8, h2_tiles, n_rows)
    ys = experts(blk_expert, n_used, xs, w_g, b_g, w_u, b_u, w_d.astype(jnp.bfloat16), b_d[:, None, :])
    return dest8, ys


def kernel(x, p, norm_mix_g, w_in, conv_w, conv_b, w_qk_m, w_if, b_if, mnorm_g, w_branch, w_out, norm_ffn_g, w_router, b_router, w_gate_up, b_gate_up, w_down, b_down, norm_ple_g, w_ple_gate, w_ple_proj, final_norm_g):
    B, S, D = x.shape
    depth = w_in.shape[0]
    bf16 = jnp.bfloat16
    x2 = x.reshape(B * S, D)
    for i in range(depth):
        qkv, xo, gates = in_proj(x2, norm_mix_g[i], w_in[i].astype(bf16))
        y_a = moba(qkv, B, S)
        y_m = mlstm(xo, conv_w[i], conv_b[i], w_qk_m[i], w_if[i], b_if[i], mnorm_g[i], B, S)
        x1, h2_tiles, top_idx, gate_w, rank, counts = merge_route(
            x2, y_a, y_m, gates, w_branch[i, 0].astype(bf16), w_branch[i, 1].astype(bf16),
            w_out[i].astype(bf16), norm_ffn_g[i], w_router[i], b_router[i])
        dest8, ys = _moe(h2_tiles, top_idx, rank, counts[:, 0], w_gate_up[i], b_gate_up[i],
                         w_down[i], b_down[i])
        x2 = combine_ple(dest8, x1, ys, gate_w.T, p[i].reshape(B * S, -1), norm_ple_g[i],
                         w_ple_gate[i].astype(bf16), w_ple_proj[i].astype(bf16),
                         final_norm_g if i == depth - 1 else None)
    return x2.reshape(B, S, D)
```

```python
import functools

import jax
import jax.numpy as jnp
import numpy as np
from jax import lax
from jax.experimental import pallas as pl
from jax.experimental.pallas import tpu as pltpu

RMS_EPS = 1e-6
LANES = 128
ROW_CHUNKS = 8
VMEM_LIMIT = 56 * 1024 * 1024

N_ATT_HEADS = 8
ATT_HEAD_DIM = 64
ATT_WIDTH = N_ATT_HEADS * ATT_HEAD_DIM
MOBA_BLOCK = 256
MOBA_TOPK = 3

N_MLSTM_HEADS = 4
MLSTM_WIDTH = 512
MLSTM_V_DIM = 128
MLSTM_QK_DIM = 64
MLSTM_CONV = 4
MLSTM_CHUNK = 128

N_EXPERTS = 32
TOP_K = 4
SWIGLU_ALPHA = 1.702
SWIGLU_LIMIT = 7.0
MOE_ROW_BLOCK = 512

NEG_BIG = -1e30

_NT = (((1,), (1,)), ((), ()))
_TN = (((0,), (0,)), ((), ()))


def _params(*sem):
    return pltpu.CompilerParams(dimension_semantics=sem, vmem_limit_bytes=VMEM_LIMIT)


def _rms(x, g):
    return x * lax.rsqrt(jnp.mean(x * x, axis=-1, keepdims=True) + RMS_EPS) * g


def _in_proj_body(x_ref, g_ref, w_ref, qkv_ref, xo_ref, gate_ref, *, col_chunk):
    h = _rms(x_ref[...], g_ref[...]).astype(jnp.bfloat16)
    col = 0
    for out_ref in (qkv_ref, xo_ref, gate_ref):
        for c in range(0, out_ref.shape[1], col_chunk):
            out_ref[:, c:c + col_chunk] = jnp.dot(
                h, w_ref[:, col + c:col + c + col_chunk],
                preferred_element_type=jnp.float32).astype(out_ref.dtype)
        col += out_ref.shape[1]


def in_proj(x2, g, w_bf16, *, tm=512, col_chunk=512):
    n, d = x2.shape
    widths = (3 * ATT_WIDTH, 2 * MLSTM_WIDTH, 2 * d)
    assert sum(widths) == w_bf16.shape[1] and n % tm == 0
    return pl.pallas_call(
        functools.partial(_in_proj_body, col_chunk=col_chunk),
        grid=(n // tm,),
        in_specs=[pl.BlockSpec((tm, d), lambda i: (i, 0)),
                  pl.BlockSpec((1, d), lambda i: (0, 0)),
                  pl.BlockSpec(w_bf16.shape, lambda i: (0, 0))],
        out_specs=[pl.BlockSpec((tm, w), lambda i: (i, 0)) for w in widths],
        out_shape=[jax.ShapeDtypeStruct((n, w), jnp.bfloat16) for w in widths],
        compiler_params=_params("parallel"),
        name="in_proj",
    )(x2, g.reshape(1, d), w_bf16)


def _moba_body(q_ref, k_ref, v_ref, o_ref, *, pair, seq):
    blk = MOBA_BLOCK
    n_blk = seq // blk
    f32, bf16 = jnp.float32, jnp.bfloat16
    lane = lax.broadcasted_iota(jnp.int32, (1, LANES), 1)
    row = lax.broadcasted_iota(jnp.int32, (blk, blk), 0)
    col = lax.broadcasted_iota(jnp.int32, (blk, blk), 1)
    rel = (row - col).astype(f32)
    causal = row >= col
    log2e = float(np.log2(np.e))
    scale2 = ATT_HEAD_DIM ** -0.5 * log2e
    lane_row = lax.broadcasted_iota(jnp.int32, (LANES, LANES), 0)
    spread = [jnp.where(lane_row == j, 1.0, 0.0).astype(bf16) for j in range(n_blk)]

    k_all = k_ref[...]
    v_all = v_ref[...]
    k_mean = jnp.concatenate(
        [jnp.mean(k_all[j * blk:(j + 1) * blk].astype(f32), axis=0, keepdims=True)
         for j in range(n_blk)] + [jnp.zeros((LANES - n_blk, LANES), f32)], axis=0)

    outs = []
    for hh in range(2):
        head = 2 * pair + hh
        slope2 = jnp.exp2(-8.0 * (head + 1).astype(f32) / N_ATT_HEADS) * log2e
        bias = [slope2 * (rel + float(d * blk)) for d in range(n_blk)]
        head_lanes = (lane >= hh * ATT_HEAD_DIM) & (lane < (hh + 1) * ATT_HEAD_DIM)
        q_all = q_ref[...]
        q_all = jnp.where(head_lanes, q_all, jnp.zeros_like(q_all))
        if n_blk > MOBA_TOPK + 1:
            scores = lax.dot_general(q_all.astype(f32), jnp.where(head_lanes, k_mean, 0.0), _NT,
                                     precision=lax.Precision.HIGHEST, preferred_element_type=f32)
        head_out = []
        for qi in range(n_blk):
            q_h = q_all[qi * blk:(qi + 1) * blk]
            sel = None
            if qi > MOBA_TOPK:
                g = jnp.where(lane < qi, scores[qi * blk:(qi + 1) * blk], -jnp.inf)
                ahead = jnp.zeros((blk, LANES), f32)
                for d in range(1, qi):
                    ahead = ahead + jnp.where(pltpu.roll(g, d, axis=1) >= g, 1.0, 0.0)
                    ahead = ahead + jnp.where(pltpu.roll(g, LANES - d, axis=1) > g, 1.0, 0.0)
                keep = jnp.where(ahead < MOBA_TOPK, 1.0, 0.0).astype(bf16)
                sel = [jnp.dot(keep, spread[j], preferred_element_type=f32) > 0.5 for j in range(qi)]
            logits = []
            for j in range(qi + 1):
                s = lax.dot_general(q_h, k_all[j * blk:(j + 1) * blk], _NT, preferred_element_type=f32)
                s = s * scale2 - bias[qi - j]
                if j == qi:
                    s = jnp.where(causal, s, NEG_BIG)
                elif sel is not None:
                    s = jnp.where(jnp.concatenate([sel[j]] * (blk // LANES), axis=1), s, NEG_BIG)
                logits.append(s)
            s_all = jnp.concatenate(logits, axis=1)
            m = jnp.max(s_all, axis=1, keepdims=True)
            p = jnp.exp2(s_all - m)
            denom = jnp.sum(p, axis=1, keepdims=True)
            o = jnp.dot(p.astype(jnp.bfloat16), v_all[:(qi + 1) * blk],
                        preferred_element_type=jnp.float32)
            head_out.append(o / denom)
        outs.append((head_lanes, head_out))
    for qi in range(n_blk):
        o = jnp.where(outs[0][0], outs[0][1][qi], outs[1][1][qi])
        o_ref[qi * blk:(qi + 1) * blk, :] = o.astype(o_ref.dtype)


def _moba_kernel(q_ref, k_ref, v_ref, o_ref, *, seq):
    _moba_body(q_ref, k_ref, v_ref, o_ref, pair=pl.program_id(1), seq=seq)


def moba(qkv, batch, seq):
    n = batch * seq
    n_pairs = ATT_WIDTH // LANES
    assert seq % MOBA_BLOCK == 0
    spec = lambda off: pl.BlockSpec((seq, LANES), lambda b, p: (b, off + p))
    return pl.pallas_call(
        functools.partial(_moba_kernel, seq=seq),
        grid=(batch, n_pairs),
        in_specs=[spec(0), spec(n_pairs), spec(2 * n_pairs)],
        out_specs=pl.BlockSpec((seq, LANES), lambda b, p: (b, p)),
        out_shape=jax.ShapeDtypeStruct((n, ATT_WIDTH), jnp.bfloat16),
        compiler_params=_params("parallel", "parallel"),
        name="moba",
    )(qkv, qkv, qkv)


def _log_sigmoid(z):
    return jnp.minimum(z, 0.0) - jnp.log(1.0 + jnp.exp(-jnp.abs(z)))


def _mlstm_body(xm_ref, om_ref, cw_ref, cb_ref, wq_ref, wk_ref, wiq_ref, wik_ref, wix_ref,
                wiqt_ref, wikt_ref, wixt_ref, bi_ref, bit_ref, g_ref, y_ref,
                q_sc, k_sc, gcol_sc, grow_sc, ct_sc, n_sc, m_sc, *, seq):
    L = MLSTM_CHUNK
    H = N_MLSTM_HEADS
    n_chunks = seq // L
    f32, bf16 = jnp.float32, jnp.bfloat16

    x = xm_ref[...].astype(f32)
    t_idx = lax.broadcasted_iota(jnp.int32, (seq, 1), 0)
    acc = x * cw_ref[MLSTM_CONV - 1:MLSTM_CONV, :] + cb_ref[...]
    for d in range(1, MLSTM_CONV):
        shifted = jnp.where(t_idx >= d, pltpu.roll(x, d, axis=0), 0.0)
        acc = acc + shifted * cw_ref[MLSTM_CONV - 1 - d:MLSTM_CONV - d, :]
    xc = (acc * jax.nn.sigmoid(acc)).astype(bf16)

    xm = xm_ref[...]
    if_col = jnp.dot(xm, wix_ref[...], preferred_element_type=f32) + bi_ref[...]
    if_row = lax.dot_general(wixt_ref[...], xm, _NT, preferred_element_type=f32) + bit_ref[...]
    for h in range(H):
        xch = xc[:, h * LANES:(h + 1) * LANES]
        q = jnp.dot(xch, wq_ref[h], preferred_element_type=f32).astype(bf16)
        k = jnp.dot(xch, wk_ref[h], preferred_element_type=f32).astype(bf16)
        if_col = if_col + jnp.dot(q, wiq_ref[h], preferred_element_type=f32)
        if_col = if_col + jnp.dot(k, wik_ref[h], preferred_element_type=f32)
        if_row = if_row + lax.dot_general(wiqt_ref[h], q, _NT, preferred_element_type=f32)
        if_row = if_row + lax.dot_general(wikt_ref[h], k, _NT, preferred_element_type=f32)
        q_sc[h] = q
        k_sc[h] = (k.astype(f32) * (MLSTM_QK_DIM ** -0.5)).astype(bf16)
    is_f_col = lax.broadcasted_iota(jnp.int32, (1, 2 * H), 1) >= H
    is_f_row = lax.broadcasted_iota(jnp.int32, (2 * H, 1), 0) >= H
    gcol_sc[...] = jnp.where(is_f_col, _log_sigmoid(if_col), if_col)
    g_row = jnp.where(is_f_row, _log_sigmoid(if_row), if_row)
    for c in range(n_chunks):
        grow_sc[c] = g_row[:, c * L:(c + 1) * L]

    ct_sc[...] = jnp.zeros_like(ct_sc)
    n_sc[...] = jnp.zeros_like(n_sc)
    m_sc[...] = jnp.zeros_like(m_sc)

    r_i = lax.broadcasted_iota(jnp.int32, (L, L), 0)
    c_i = lax.broadcasted_iota(jnp.int32, (L, L), 1)
    causal = r_i >= c_i
    tri_lo = jnp.where(causal, 1.0, 0.0).astype(f32)
    tri_up = jnp.where(c_i >= r_i, 1.0, 0.0).astype(f32)

    def chunk(c, carry):
        r0 = pl.multiple_of(c * L, L)
        gc = gcol_sc[pl.ds(r0, L), :]
        gr = grow_sc[c]
        b_cols = jnp.dot(tri_lo, gc, precision=lax.Precision.HIGHEST, preferred_element_type=f32)
        b_rows = jnp.dot(gr, tri_up, precision=lax.Precision.HIGHEST, preferred_element_type=f32)
        for h in range(H):
            hs = slice(h * LANES, (h + 1) * LANES)
            b_col = b_cols[:, H + h:H + h + 1]
            i_col = gc[:, h:h + 1]
            b_row = b_rows[H + h:H + h + 1, :]
            i_row = gr[h:h + 1, :]
            b_last = b_col[L - 1:L, :]
            m_prev = m_sc[h][:, 0:1]
            q = q_sc[h, pl.ds(r0, L), :]
            k = k_sc[h, pl.ds(r0, L), :]
            v = xm_ref[pl.ds(r0, L), hs]

            log_d = jnp.where(causal, b_col + (i_row - b_row), NEG_BIG)
            inter = b_col + m_prev
            m_t = jnp.maximum(inter, jnp.max(log_d, axis=1, keepdims=True))
            w_inter = jnp.exp(inter - m_t)
            s = lax.dot_general(q, k, _NT, preferred_element_type=f32) * jnp.exp(log_d - m_t)
            num = jnp.dot(s.astype(bf16), v, preferred_element_type=f32)
            num = num + w_inter * jnp.dot(q, ct_sc[h].astype(bf16), preferred_element_type=f32)
            den = jnp.sum(s, axis=1, keepdims=True)
            den = den + w_inter * jnp.sum(q.astype(f32) * n_sc[h], axis=1, keepdims=True)
            hv = num / jnp.maximum(jnp.abs(den), jnp.exp(-m_t))
            hv = hv * lax.rsqrt(jnp.mean(hv * hv, axis=1, keepdims=True) + RMS_EPS) * g_ref[:, hs]
            o_gate = jax.nn.sigmoid(om_ref[pl.ds(r0, L), hs].astype(f32))
            y_ref[pl.ds(r0, L), hs] = (o_gate * hv).astype(y_ref.dtype)

            g_col = b_last - b_col + i_col
            m_new = jnp.maximum(b_last + m_prev, jnp.max(g_col, axis=0, keepdims=True))
            wg = jnp.exp(g_col - m_new)
            decay = jnp.exp(b_last + m_prev - m_new)
            wv = (wg * v.astype(f32)).astype(bf16)
            ct_sc[h] = decay * ct_sc[h] + lax.dot_general(k, wv, _TN, preferred_element_type=f32)
            n_sc[h] = decay * n_sc[h] + jnp.sum(wg * k.astype(f32), axis=0, keepdims=True)
            m_sc[h] = jnp.broadcast_to(m_new, (1, LANES))
        return carry

    lax.fori_loop(0, n_chunks, chunk, 0)


def _mlstm_kernel(*refs, seq):
    _mlstm_body(*refs, seq=seq)


def mlstm(xo, conv_w, conv_b, w_qk, w_if, b_if, g, batch, seq):
    n = batch * seq
    H, dv, dk = N_MLSTM_HEADS, MLSTM_V_DIM, MLSTM_QK_DIM
    assert seq % MLSTM_CHUNK == 0 and dv == LANES
    bf16 = jnp.bfloat16
    pad = ((0, 0), (0, 0), (0, LANES - dk))
    wq = jnp.pad(w_qk[:, :, :dk], pad).astype(bf16)
    wk = jnp.pad(w_qk[:, :, dk:], pad).astype(bf16)
    rpad = ((0, 0), (0, LANES - dk), (0, 0))
    wiq = jnp.pad(w_if[:H * dk].reshape(H, dk, 2 * H), rpad).astype(bf16)
    wik = jnp.pad(w_if[H * dk:2 * H * dk].reshape(H, dk, 2 * H), rpad).astype(bf16)
    wix = w_if[2 * H * dk:].astype(bf16)
    wiqt, wikt, wixt = wiq.transpose(0, 2, 1), wik.transpose(0, 2, 1), wix.T
    full = lambda a: pl.BlockSpec(a.shape, lambda b: (0,) * a.ndim)
    consts = [conv_w, conv_b.reshape(1, -1), wq, wk, wiq, wik, wix, wiqt, wikt, wixt,
              b_if.reshape(1, -1), b_if.reshape(-1, 1), g.reshape(1, -1)]
    return pl.pallas_call(
        functools.partial(_mlstm_kernel, seq=seq),
        grid=(batch,),
        in_specs=[pl.BlockSpec((seq, MLSTM_WIDTH), lambda b: (b, 0)),
                  pl.BlockSpec((seq, MLSTM_WIDTH), lambda b: (b, 1))] + [full(a) for a in consts],
        out_specs=pl.BlockSpec((seq, MLSTM_WIDTH), lambda b: (b, 0)),
        out_shape=jax.ShapeDtypeStruct((n, MLSTM_WIDTH), bf16),
        scratch_shapes=[pltpu.VMEM((H, seq, LANES), bf16), pltpu.VMEM((H, seq, LANES), bf16),
                        pltpu.VMEM((seq, 2 * H), jnp.float32),
                        pltpu.VMEM((seq // MLSTM_CHUNK, 2 * H, MLSTM_CHUNK), jnp.float32),
                        pltpu.VMEM((H, LANES, LANES), jnp.float32),
                        pltpu.VMEM((H, 1, LANES), jnp.float32),
                        pltpu.VMEM((H, 1, LANES), jnp.float32)],
        compiler_params=_params("parallel"),
        name="mlstm",
    )(xo, xo, *consts)


def _store_rows_as_tiles(dst_ref, val):
    rows = val.shape[0]
    for c in range(ROW_CHUNKS):
        dst_ref[pl.ds(c, rows, stride=ROW_CHUNKS), :] = val[:, c * LANES:(c + 1) * LANES]


def _load_tiles_as_rows(src_ref, rows, lead=()):
    return jnp.concatenate(
        [src_ref[lead + (pl.ds(c, rows, stride=ROW_CHUNKS), slice(None))] for c in range(ROW_CHUNKS)],
        axis=1)


def _merge_route_body(x_ref, ya_ref, ym_ref, ga_ref, gm_ref, wb0_ref, wb1_ref, wo_ref, g_ref,
                      wr_ref, br_ref, x1_ref, h2_ref, idx_ref, gate_ref, rank_ref, cnt_ref,
                      run_sc):
    f32, bf16 = jnp.float32, jnp.bfloat16
    tm = x_ref.shape[0]

    @pl.when(pl.program_id(0) == 0)
    def _():
        run_sc[...] = jnp.zeros_like(run_sc)

    u = jax.nn.sigmoid(ga_ref[...].astype(f32)) * jnp.dot(ya_ref[...], wb0_ref[...], preferred_element_type=f32)
    u = u + jax.nn.sigmoid(gm_ref[...].astype(f32)) * jnp.dot(ym_ref[...], wb1_ref[...], preferred_element_type=f32)
    x1 = x_ref[...] + jnp.dot(u.astype(bf16), wo_ref[...], preferred_element_type=f32)
    x1_ref[...] = x1
    h2 = _rms(x1, g_ref[...])
    _store_rows_as_tiles(h2_ref, h2)

    logits = jnp.dot(h2, wr_ref[...], precision=lax.Precision.HIGHEST,
                     preferred_element_type=f32) + br_ref[...]
    logits = jnp.transpose(logits)[:N_EXPERTS]
    e_id = lax.broadcasted_iota(jnp.int32, logits.shape, 0).astype(f32)
    chosen = jnp.zeros(logits.shape, f32)
    vals, ids = [], []
    for _ in range(TOP_K):
        top = jnp.max(logits, axis=0, keepdims=True)
        first = jnp.min(jnp.where(logits == top, e_id, float(N_EXPERTS)), axis=0, keepdims=True)
        hit = e_id == first
        chosen = jnp.where(hit, 1.0, chosen)
        logits = jnp.where(hit, -jnp.inf, logits)
        vals.append(top)
        ids.append(first)
    ex = [jnp.exp(v - vals[0]) for v in vals]
    total = ex[0] + ex[1] + ex[2] + ex[3]

    t_r = lax.broadcasted_iota(jnp.int32, (tm, tm), 0)
    t_c = lax.broadcasted_iota(jnp.int32, (tm, tm), 1)
    earlier = jnp.where(t_r < t_c, 1.0, 0.0).astype(bf16)
    before = jnp.dot(chosen.astype(bf16), earlier, preferred_element_type=f32) + run_sc[:, 0:1]
    for kk in range(TOP_K):
        idx_ref[kk:kk + 1, :] = ids[kk].astype(jnp.int32)
        gate_ref[kk:kk + 1, :] = ex[kk] / total
        rank_ref[kk:kk + 1, :] = jnp.sum(jnp.where(e_id == ids[kk], before, 0.0), axis=0,
                                         keepdims=True).astype(jnp.int32)
    run_sc[...] = run_sc[...] + jnp.sum(chosen, axis=1, keepdims=True)
    cnt_ref[...] = run_sc[...].astype(jnp.int32)


def merge_route(x2, ya, ym, gates, wb0, wb1, wo, g, w_router, b_router, *, tm=512):
    n, d = x2.shape
    e = w_router.shape[1]
    assert n % tm == 0 and d == ROW_CHUNKS * LANES and e == N_EXPERTS
    full = lambda a: pl.BlockSpec(a.shape, lambda i: (0,) * a.ndim)
    lane_pad = ((0, 0), (0, LANES - e))
    consts = [wb0, wb1, wo, g.reshape(1, d), jnp.pad(w_router, lane_pad),
              jnp.pad(b_router.reshape(1, e), lane_pad)]
    tok = lambda dt: jax.ShapeDtypeStruct((TOP_K, n), dt)
    return pl.pallas_call(
        _merge_route_body,
        grid=(n // tm,),
        in_specs=[pl.BlockSpec((tm, d), lambda i: (i, 0)),
                  pl.BlockSpec((tm, ATT_WIDTH), lambda i: (i, 0)),
                  pl.BlockSpec((tm, MLSTM_WIDTH), lambda i: (i, 0)),
                  pl.BlockSpec((tm, d), lambda i: (i, 0)),
                  pl.BlockSpec((tm, d), lambda i: (i, 1))] + [full(a) for a in consts],
        out_specs=[pl.BlockSpec((tm, d), lambda i: (i, 0)),
                   pl.BlockSpec((tm * ROW_CHUNKS, LANES), lambda i: (i, 0)),
                   pl.BlockSpec((TOP_K, tm), lambda i: (0, i)),
                   pl.BlockSpec((TOP_K, tm), lambda i: (0, i)),
                   pl.BlockSpec((TOP_K, tm), lambda i: (0, i)),
                   pl.BlockSpec((e, LANES), lambda i: (0, 0))],
        out_shape=[jax.ShapeDtypeStruct((n, d), jnp.float32),
                   jax.ShapeDtypeStruct((n * ROW_CHUNKS, LANES), jnp.float32),
                   tok(jnp.int32), tok(jnp.float32), tok(jnp.int32),
                   jax.ShapeDtypeStruct((e, LANES), jnp.int32)],
        scratch_shapes=[pltpu.VMEM((e, LANES), jnp.float32)],
        compiler_params=_params("arbitrary"),
        name="merge_route",
    )(x2, ya, ym, gates, gates, *consts)


def _dispatch_body(dest_ref, pad_ref, x_ref, dst_ref, zero_sc, sem, *, tc, n_pad):
    zero_sc[...] = jnp.zeros_like(zero_sc)

    def row_copy(t, kk):
        src = x_ref.at[pl.ds(pl.multiple_of(t * ROW_CHUNKS, ROW_CHUNKS), ROW_CHUNKS), :]
        off = pl.multiple_of(dest_ref[kk, t], ROW_CHUNKS)
        return pltpu.make_async_copy(src, dst_ref.at[pl.ds(off, ROW_CHUNKS), :], sem)

    def pad_copy(j):
        off = pl.multiple_of(pad_ref[0, 0, j], ROW_CHUNKS)
        return pltpu.make_async_copy(zero_sc, dst_ref.at[pl.ds(off, ROW_CHUNKS), :], sem)

    def start_rows(t, carry):
        for kk in range(TOP_K):
            row_copy(t, kk).start()
        return carry

    def wait_rows(t, carry):
        for kk in range(TOP_K):
            row_copy(t, kk).wait()
        return carry

    def start_pad(j, carry):
        pad_copy(j).start()
        return carry

    def wait_pad(j, carry):
        pad_copy(j).wait()
        return carry

    lax.fori_loop(0, tc, start_rows, 0)
    lax.fori_loop(0, n_pad, start_pad, 0)
    lax.fori_loop(0, tc, wait_rows, 0)
    lax.fori_loop(0, n_pad, wait_pad, 0)


def dispatch(dest8, pad8, h2_tiles, n_rows, *, tc=1024):
    n = h2_tiles.shape[0] // ROW_CHUNKS
    steps = n // tc
    assert n % tc == 0 and pad8.shape[0] % steps == 0
    n_pad = pad8.shape[0] // steps
    return pl.pallas_call(
        functools.partial(_dispatch_body, tc=tc, n_pad=n_pad),
        grid=(steps,),
        in_specs=[pl.BlockSpec((TOP_K, tc), lambda i: (0, i), memory_space=pltpu.SMEM),
                  pl.BlockSpec((1, 1, n_pad), lambda i: (i, 0, 0), memory_space=pltpu.SMEM),
                  pl.BlockSpec((tc * ROW_CHUNKS, LANES), lambda i: (i, 0))],
        out_specs=pl.BlockSpec(memory_space=pl.ANY),
        out_shape=jax.ShapeDtypeStruct((n_rows * ROW_CHUNKS, LANES), h2_tiles.dtype),
        scratch_shapes=[pltpu.VMEM((ROW_CHUNKS, LANES), h2_tiles.dtype), pltpu.SemaphoreType.DMA(())],
        compiler_params=_params("arbitrary"),
        name="dispatch",
    )(dest8, pad8.reshape(steps, 1, n_pad), h2_tiles)


def _split_gate_up_body(w_ref, perm_ref, wg_ref, wu_ref):
    tile = perm_ref.shape[0]
    half = tile // 2
    w = w_ref[0].astype(jnp.bfloat16)
    for t in range(w.shape[1] // tile):
        o = jnp.dot(w[:, t * tile:(t + 1) * tile], perm_ref[...], preferred_element_type=jnp.float32)
        wg_ref[0, :, t * half:(t + 1) * half] = o[:, :half].astype(jnp.bfloat16)
        wu_ref[0, :, t * half:(t + 1) * half] = o[:, half:].astype(jnp.bfloat16)


def split_gate_up(w_gu, *, cols=1024):
    e, d, ff2 = w_gu.shape
    tile = 2 * LANES
    perm = np.zeros((tile, tile), np.float32)
    perm[2 * np.arange(LANES), np.arange(LANES)] = 1.0
    perm[2 * np.arange(LANES) + 1, LANES + np.arange(LANES)] = 1.0
    out = jax.ShapeDtypeStruct((e, d, ff2 // 2), jnp.bfloat16)
    return pl.pallas_call(
        _split_gate_up_body,
        grid=(e, ff2 // cols),
        in_specs=[pl.BlockSpec((1, d, cols), lambda i, j: (i, 0, j)),
                  pl.BlockSpec((tile, tile), lambda i, j: (0, 0))],
        out_specs=[pl.BlockSpec((1, d, cols // 2), lambda i, j: (i, 0, j))] * 2,
        out_shape=[out, out],
        compiler_params=_params("parallel", "parallel"),
        name="split_gate_up",
    )(w_gu, jnp.asarray(perm, jnp.bfloat16))


def _experts_body(blk_e_ref, used_ref, x_ref, wg_ref, bg_ref, wu_ref, bu_ref, wd_ref, bd_ref, y_ref):
    del blk_e_ref
    rows = x_ref.shape[0] // ROW_CHUNKS

    @pl.when(pl.program_id(0) < used_ref[0])
    def _():
        f32, bf16 = jnp.float32, jnp.bfloat16
        xb = _load_tiles_as_rows(x_ref, rows).astype(bf16)
        gate = jnp.dot(xb, wg_ref[0], preferred_element_type=f32) + bg_ref[0]
        up = jnp.dot(xb, wu_ref[0], preferred_element_type=f32) + bu_ref[0]
        gate = jnp.minimum(gate, SWIGLU_LIMIT)
        up = jnp.clip(up, -SWIGLU_LIMIT, SWIGLU_LIMIT)
        act = (up + 1.0) * (gate * jax.nn.sigmoid(gate * SWIGLU_ALPHA))
        y = jnp.dot(act.astype(bf16), wd_ref[0], preferred_element_type=f32) + bd_ref[0]
        _store_rows_as_tiles(y_ref, y)

    @pl.when(pl.program_id(0) >= used_ref[0])
    def _():
        y_ref[...] = jnp.zeros_like(y_ref)


def experts(blk_expert, n_used, xs_tiles, w_g, b_g, w_u, b_u, w_d, b_d):
    n_blocks = blk_expert.shape[0]
    rb = MOE_ROW_BLOCK
    d, ff = w_g.shape[1], w_g.shape[2]
    e_map3 = lambda i, be, nu: (be[i], 0, 0)
    grid_spec = pltpu.PrefetchScalarGridSpec(
        num_scalar_prefetch=2,
        grid=(n_blocks,),
        in_specs=[pl.BlockSpec((rb * ROW_CHUNKS, LANES), lambda i, be, nu: (i, 0)),
                  pl.BlockSpec((1, d, ff), e_map3), pl.BlockSpec((1, 1, ff), e_map3),
                  pl.BlockSpec((1, d, ff), e_map3), pl.BlockSpec((1, 1, ff), e_map3),
                  pl.BlockSpec((1, ff, d), e_map3), pl.BlockSpec((1, 1, d), e_map3)],
        out_specs=pl.BlockSpec((rb * ROW_CHUNKS, LANES), lambda i, be, nu: (i, 0)),
    )
    return pl.pallas_call(
        _experts_body,
        grid_spec=grid_spec,
        out_shape=jax.ShapeDtypeStruct(xs_tiles.shape, jnp.float32),
        compiler_params=_params("arbitrary"),
        name="experts",
    )(blk_expert, n_used, xs_tiles, w_g, b_g, w_u, b_u, w_d, b_d)


def _combine_ple_body(dcur_ref, dnext_ref, x1_ref, ys_ref, gw_ref, p_ref, gp_ref, wpg_ref, wpp_ref, *rest):
    buf, sem = rest[-2:]
    rest = rest[:-2]
    gf_ref, o_ref = rest if len(rest) == 2 else (None, rest[0])
    f32, bf16 = jnp.float32, jnp.bfloat16
    tm = x1_ref.shape[0]
    step = pl.program_id(0)
    slot = step % 2

    def row_copy(d_ref, sl, t, kk):
        off = pl.multiple_of(d_ref[kk, t], ROW_CHUNKS)
        dst = buf.at[sl, kk, pl.ds(pl.multiple_of(t * ROW_CHUNKS, ROW_CHUNKS), ROW_CHUNKS), :]
        return pltpu.make_async_copy(ys_ref.at[pl.ds(off, ROW_CHUNKS), :], dst, sem.at[sl])

    def start_tile(d_ref, sl):
        def body(t, carry):
            for kk in range(TOP_K):
                row_copy(d_ref, sl, t, kk).start()
            return carry
        lax.fori_loop(0, tm, body, 0)

    @pl.when(step == 0)
    def _():
        start_tile(dcur_ref, 0)

    @pl.when(step + 1 < pl.num_programs(0))
    def _():
        start_tile(dnext_ref, 1 - slot)

    def wait_body(t, carry):
        for kk in range(TOP_K):
            row_copy(dcur_ref, slot, t, kk).wait()
        return carry
    lax.fori_loop(0, tm, wait_body, 0)

    x2 = x1_ref[...]
    for kk in range(TOP_K):
        x2 = x2 + gw_ref[:, kk:kk + 1] * _load_tiles_as_rows(buf, tm, lead=(slot, kk))
    r = _rms(x2, gp_ref[...]).astype(bf16)
    gate = jax.nn.sigmoid(jnp.dot(r, wpg_ref[...], preferred_element_type=f32))
    emb = jnp.dot(p_ref[...].astype(bf16), wpp_ref[...], preferred_element_type=f32)
    x3 = x2 + gate * emb
    o_ref[...] = x3 if gf_ref is None else _rms(x3, gf_ref[...])


def combine_ple(dest8, x1, ys_tiles, gate_cols, p2, g_ple, w_pg, w_pp, g_final, *, tm=512):
    n, d = x1.shape
    steps = n // tm
    full = lambda a: pl.BlockSpec(a.shape, lambda i: (0,) * a.ndim)
    consts = [g_ple.reshape(1, d), w_pg, w_pp] + ([] if g_final is None else [g_final.reshape(1, d)])
    return pl.pallas_call(
        _combine_ple_body,
        grid=(steps,),
        in_specs=[pl.BlockSpec((TOP_K, tm), lambda i: (0, i), memory_space=pltpu.SMEM),
                  pl.BlockSpec((TOP_K, tm), lambda i: (0, jnp.minimum(i + 1, steps - 1)),
                               memory_space=pltpu.SMEM),
                  pl.BlockSpec((tm, d), lambda i: (i, 0)),
                  pl.BlockSpec(memory_space=pl.ANY),
                  pl.BlockSpec((tm, TOP_K), lambda i: (i, 0)),
                  pl.BlockSpec((tm, p2.shape[1]), lambda i: (i, 0))] + [full(a) for a in consts],
        out_specs=pl.BlockSpec((tm, d), lambda i: (i, 0)),
        out_shape=jax.ShapeDtypeStruct((n, d), jnp.float32),
        scratch_shapes=[pltpu.VMEM((2, TOP_K, tm * ROW_CHUNKS, LANES), ys_tiles.dtype),
                        pltpu.SemaphoreType.DMA((2,))],
        compiler_params=_params("arbitrary"),
        name="combine_ple",
    )(dest8, dest8, x1, ys_tiles, gate_cols, p2, *consts)


def _table_lookup(table, idx):
    hit = idx[..., None] == jnp.arange(table.shape[0], dtype=idx.dtype)
    return jnp.sum(jnp.where(hit, table, 0), axis=-1)


def _moe(h2_tiles, top_idx, rank, counts, w_gu, b_gu, w_d, b_d):
    n = h2_tiles.shape[0] // ROW_CHUNKS
    rb = MOE_ROW_BLOCK
    i32 = jnp.int32
    n_blocks = -(-n * TOP_K // rb) + N_EXPERTS
    n_rows = n_blocks * rb
    padded = (counts + rb - 1) // rb * rb
    pend = jnp.cumsum(padded)
    pstart = pend - padded
    dest8 = (_table_lookup(pstart, top_idx) + rank) * ROW_CHUNKS
    blk_start = jnp.arange(n_blocks, dtype=i32) * rb
    blk_expert = jnp.minimum(jnp.sum((pend[None, :] <= blk_start[:, None]).astype(i32), axis=1),
                             N_EXPERTS - 1)
    n_used = (pend[-1:] // rb).astype(i32)
    seg_len = jnp.concatenate([padded - counts, n_rows - pend[-1:]])
    seg_end = jnp.cumsum(seg_len)
    seg_shift = jnp.concatenate([pstart + counts, pend[-1:]]) - (seg_end - seg_len)
    q = jnp.arange(n_rows - n * TOP_K, dtype=i32)
    seg = jnp.sum((seg_end[None, :] <= q[:, None]).astype(i32), axis=1)
    pad8 = (q + _table_lookup(seg_shift, seg)) * ROW_CHUNKS
    w_g, w_u = split_gate_up(w_gu)
    b_g, b_u = b_gu[:, None, 0::2], b_gu[:, None, 1::2]
    xs = dispatch(dest8, pad8, h2_tiles, n_rows)
    ys = experts(blk_expert, n_used, xs, w_g, b_g, w_u, b_u, w_d.astype(jnp.bfloat16), b_d[:, None, :])
    return dest8, ys


def kernel(x, p, norm_mix_g, w_in, conv_w, conv_b, w_qk_m, w_if, b_if, mnorm_g, w_branch, w_out, norm_ffn_g, w_router, b_router, w_gate_up, b_gate_up, w_down, b_down, norm_ple_g, w_ple_gate, w_ple_proj, final_norm_g):
    B, S, D = x.shape
    depth = w_in.shape[0]
    bf16 = jnp.bfloat16
    x2 = x.reshape(B * S, D)
    for i in range(depth):
        qkv, xo, gates = in_proj(x2, norm_mix_g[i], w_in[i].astype(bf16))
        y_a = moba(qkv, B, S)
        y_m = mlstm(xo, conv_w[i], conv_b[i], w_qk_m[i], w_if[i], b_if[i], mnorm_g[i], B, S)
        x1, h2_tiles, top_idx, gate_w, rank, counts = merge_route(
            x2, y_a, y_m, gates, w_branch[i, 0].astype(bf16), w_branch[i, 1].astype(bf16),
            w_out[i].astype(bf16), norm_ffn_g[i], w_router[i], b_router[i])
        dest8, ys = _moe(h2_tiles, top_idx, rank, counts[:, 0], w_gate_up[i], b_gate_up[i],
                         w_down[i], b_down[i])
        x2 = combine_ple(dest8, x1, ys, gate_w.T, p[i].reshape(B * S, -1), norm_ple_g[i],
                         w_ple_gate[i].astype(bf16), w_ple_proj[i].astype(bf16),
                         final_norm_g if i == depth - 1 else None)
    return x2.reshape(B, S, D)
```

```python
import functools

import jax
import jax.numpy as jnp
import numpy as np
from jax import lax
from jax.experimental import pallas as pl
from jax.experimental.pallas import tpu as pltpu

RMS_EPS = 1e-6
LANES = 128
ROW_CHUNKS = 8
VMEM_LIMIT = 56 * 1024 * 1024

N_ATT_HEADS = 8
ATT_HEAD_DIM = 64
ATT_WIDTH = N_ATT_HEADS * ATT_HEAD_DIM
MOBA_BLOCK = 256
MOBA_TOPK = 3

N_MLSTM_HEADS = 4
MLSTM_WIDTH = 512
MLSTM_V_DIM = 128
MLSTM_QK_DIM = 64
MLSTM_CONV = 4
MLSTM_CHUNK = 128

N_EXPERTS = 32
TOP_K = 4
SWIGLU_ALPHA = 1.702
SWIGLU_LIMIT = 7.0
MOE_ROW_BLOCK = 512
MOE_TOKEN_TILE = 512

NEG_BIG = -1e30

_NT = (((1,), (1,)), ((), ()))
_TN = (((0,), (0,)), ((), ()))


def _params(*sem):
    return pltpu.CompilerParams(dimension_semantics=sem, vmem_limit_bytes=VMEM_LIMIT)


def _rms(x, g):
    return x * lax.rsqrt(jnp.mean(x * x, axis=-1, keepdims=True) + RMS_EPS) * g


def _in_proj_body(x_ref, g_ref, w_ref, qkv_ref, xo_ref, gate_ref, *, col_chunk):
    h = _rms(x_ref[...], g_ref[...]).astype(jnp.bfloat16)
    col = 0
    for out_ref in (qkv_ref, xo_ref, gate_ref):
        for c in range(0, out_ref.shape[1], col_chunk):
            out_ref[:, c:c + col_chunk] = jnp.dot(
                h, w_ref[:, col + c:col + c + col_chunk],
                preferred_element_type=jnp.float32).astype(out_ref.dtype)
        col += out_ref.shape[1]


def in_proj(x2, g, w_bf16, *, tm=512, col_chunk=512):
    n, d = x2.shape
    widths = (3 * ATT_WIDTH, 2 * MLSTM_WIDTH, 2 * d)
    assert sum(widths) == w_bf16.shape[1] and n % tm == 0
    return pl.pallas_call(
        functools.partial(_in_proj_body, col_chunk=col_chunk),
        grid=(n // tm,),
        in_specs=[pl.BlockSpec((tm, d), lambda i: (i, 0)),
                  pl.BlockSpec((1, d), lambda i: (0, 0)),
                  pl.BlockSpec(w_bf16.shape, lambda i: (0, 0))],
        out_specs=[pl.BlockSpec((tm, w), lambda i: (i, 0)) for w in widths],
        out_shape=[jax.ShapeDtypeStruct((n, w), jnp.bfloat16) for w in widths],
        compiler_params=_params("parallel"),
        name="in_proj",
    )(x2, g.reshape(1, d), w_bf16)


def _moba_body(q_ref, k_ref, v_ref, o_ref, *, pair, seq):
    blk = MOBA_BLOCK
    n_blk = seq // blk
    f32, bf16 = jnp.float32, jnp.bfloat16
    lane = lax.broadcasted_iota(jnp.int32, (1, LANES), 1)
    row = lax.broadcasted_iota(jnp.int32, (blk, blk), 0)
    col = lax.broadcasted_iota(jnp.int32, (blk, blk), 1)
    rel = (row - col).astype(f32)
    causal = row >= col
    log2e = float(np.log2(np.e))
    scale2 = ATT_HEAD_DIM ** -0.5 * log2e
    lane_row = lax.broadcasted_iota(jnp.int32, (LANES, LANES), 0)
    spread = [jnp.where(lane_row == j, 1.0, 0.0).astype(bf16) for j in range(n_blk)]

    k_all = k_ref[...]
    v_all = v_ref[...]
    k_mean = jnp.concatenate(
        [jnp.mean(k_all[j * blk:(j + 1) * blk].astype(f32), axis=0, keepdims=True)
         for j in range(n_blk)] + [jnp.zeros((LANES - n_blk, LANES), f32)], axis=0)

    outs = []
    for hh in range(2):
        head = 2 * pair + hh
        slope2 = jnp.exp2(-8.0 * (head + 1).astype(f32) / N_ATT_HEADS) * log2e
        bias = [slope2 * (rel + float(d * blk)) for d in range(n_blk)]
        head_lanes = (lane >= hh * ATT_HEAD_DIM) & (lane < (hh + 1) * ATT_HEAD_DIM)
        q_all = q_ref[...]
        q_all = jnp.where(head_lanes, q_all, jnp.zeros_like(q_all))
        if n_blk > MOBA_TOPK + 1:
            scores = lax.dot_general(q_all.astype(f32), jnp.where(head_lanes, k_mean, 0.0), _NT,
                                     precision=lax.Precision.HIGHEST, preferred_element_type=f32)
        head_out = []
        for qi in range(n_blk):
            q_h = q_all[qi * blk:(qi + 1) * blk]
            sel = None
            if qi > MOBA_TOPK:
                g = jnp.where(lane < qi, scores[qi * blk:(qi + 1) * blk], -jnp.inf)
                ahead = jnp.zeros((blk, LANES), f32)
                for d in range(1, qi):
                    ahead = ahead + jnp.where(pltpu.roll(g, d, axis=1) >= g, 1.0, 0.0)
                    ahead = ahead + jnp.where(pltpu.roll(g, LANES - d, axis=1) > g, 1.0, 0.0)
                keep = jnp.where(ahead < MOBA_TOPK, 1.0, 0.0).astype(bf16)
                sel = [jnp.dot(keep, spread[j], preferred_element_type=f32) > 0.5 for j in range(qi)]
            logits = []
            for j in range(qi + 1):
                s = lax.dot_general(q_h, k_all[j * blk:(j + 1) * blk], _NT, preferred_element_type=f32)
                s = s * scale2 - bias[qi - j]
                if j == qi:
                    s = jnp.where(causal, s, NEG_BIG)
                elif sel is not None:
                    s = jnp.where(jnp.concatenate([sel[j]] * (blk // LANES), axis=1), s, NEG_BIG)
                logits.append(s)
            s_all = jnp.concatenate(logits, axis=1)
            m = jnp.max(s_all, axis=1, keepdims=True)
            p = jnp.exp2(s_all - m)
            denom = jnp.sum(p, axis=1, keepdims=True)
            o = jnp.dot(p.astype(jnp.bfloat16), v_all[:(qi + 1) * blk],
                        preferred_element_type=jnp.float32)
            head_out.append(o / denom)
        outs.append((head_lanes, head_out))
    for qi in range(n_blk):
        o = jnp.where(outs[0][0], outs[0][1][qi], outs[1][1][qi])
        o_ref[qi * blk:(qi + 1) * blk, :] = o.astype(o_ref.dtype)


def _moba_kernel(q_ref, k_ref, v_ref, o_ref, *, seq):
    _moba_body(q_ref, k_ref, v_ref, o_ref, pair=pl.program_id(1), seq=seq)


def moba(qkv, batch, seq):
    n = batch * seq
    n_pairs = ATT_WIDTH // LANES
    assert seq % MOBA_BLOCK == 0
    spec = lambda off: pl.BlockSpec((seq, LANES), lambda b, p: (b, off + p))
    return pl.pallas_call(
        functools.partial(_moba_kernel, seq=seq),
        grid=(batch, n_pairs),
        in_specs=[spec(0), spec(n_pairs), spec(2 * n_pairs)],
        out_specs=pl.BlockSpec((seq, LANES), lambda b, p: (b, p)),
        out_shape=jax.ShapeDtypeStruct((n, ATT_WIDTH), jnp.bfloat16),
        compiler_params=_params("parallel", "parallel"),
        name="moba",
    )(qkv, qkv, qkv)


def _log_sigmoid(z):
    return jnp.minimum(z, 0.0) - jnp.log(1.0 + jnp.exp(-jnp.abs(z)))


def _mlstm_body(xm_ref, om_ref, cw_ref, cb_ref, wq_ref, wk_ref, wiq_ref, wik_ref, wix_ref,
                wiqt_ref, wikt_ref, wixt_ref, bi_ref, bit_ref, g_ref, y_ref,
                q_sc, k_sc, gcol_sc, grow_sc, ct_sc, n_sc, m_sc, *, seq):
    L = MLSTM_CHUNK
    H = N_MLSTM_HEADS
    n_chunks = seq // L
    f32, bf16 = jnp.float32, jnp.bfloat16

    x = xm_ref[...].astype(f32)
    t_idx = lax.broadcasted_iota(jnp.int32, (seq, 1), 0)
    acc = x * cw_ref[MLSTM_CONV - 1:MLSTM_CONV, :] + cb_ref[...]
    for d in range(1, MLSTM_CONV):
        shifted = jnp.where(t_idx >= d, pltpu.roll(x, d, axis=0), 0.0)
        acc = acc + shifted * cw_ref[MLSTM_CONV - 1 - d:MLSTM_CONV - d, :]
    xc = (acc * jax.nn.sigmoid(acc)).astype(bf16)

    xm = xm_ref[...]
    if_col = jnp.dot(xm, wix_ref[...], preferred_element_type=f32) + bi_ref[...]
    if_row = lax.dot_general(wixt_ref[...], xm, _NT, preferred_element_type=f32) + bit_ref[...]
    for h in range(H):
        xch = xc[:, h * LANES:(h + 1) * LANES]
        q = jnp.dot(xch, wq_ref[h], preferred_element_type=f32).astype(bf16)
        k = jnp.dot(xch, wk_ref[h], preferred_element_type=f32).astype(bf16)
        if_col = if_col + jnp.dot(q, wiq_ref[h], preferred_element_type=f32)
        if_col = if_col + jnp.dot(k, wik_ref[h], preferred_element_type=f32)
        if_row = if_row + lax.dot_general(wiqt_ref[h], q, _NT, preferred_element_type=f32)
        if_row = if_row + lax.dot_general(wikt_ref[h], k, _NT, preferred_element_type=f32)
        q_sc[h] = q
        k_sc[h] = (k.astype(f32) * (MLSTM_QK_DIM ** -0.5)).astype(bf16)
    is_f_col = lax.broadcasted_iota(jnp.int32, (1, 2 * H), 1) >= H
    is_f_row = lax.broadcasted_iota(jnp.int32, (2 * H, 1), 0) >= H
    gcol_sc[...] = jnp.where(is_f_col, _log_sigmoid(if_col), if_col)
    g_row = jnp.where(is_f_row, _log_sigmoid(if_row), if_row)
    for c in range(n_chunks):
        grow_sc[c] = g_row[:, c * L:(c + 1) * L]

    ct_sc[...] = jnp.zeros_like(ct_sc)
    n_sc[...] = jnp.zeros_like(n_sc)
    m_sc[...] = jnp.zeros_like(m_sc)

    r_i = lax.broadcasted_iota(jnp.int32, (L, L), 0)
    c_i = lax.broadcasted_iota(jnp.int32, (L, L), 1)
    causal = r_i >= c_i
    tri_lo = jnp.where(causal, 1.0, 0.0).astype(f32)
    tri_up = jnp.where(c_i >= r_i, 1.0, 0.0).astype(f32)

    def chunk(c, carry):
        r0 = pl.multiple_of(c * L, L)
        gc = gcol_sc[pl.ds(r0, L), :]
        gr = grow_sc[c]
        b_cols = jnp.dot(tri_lo, gc, precision=lax.Precision.HIGHEST, preferred_element_type=f32)
        b_rows = jnp.dot(gr, tri_up, precision=lax.Precision.HIGHEST, preferred_element_type=f32)
        for h in range(H):
            hs = slice(h * LANES, (h + 1) * LANES)
            b_col = b_cols[:, H + h:H + h + 1]
            i_col = gc[:, h:h + 1]
            b_row = b_rows[H + h:H + h + 1, :]
            i_row = gr[h:h + 1, :]
            b_last = b_col[L - 1:L, :]
            m_prev = m_sc[h][:, 0:1]
            q = q_sc[h, pl.ds(r0, L), :]
            k = k_sc[h, pl.ds(r0, L), :]
            v = xm_ref[pl.ds(r0, L), hs]

            log_d = jnp.where(causal, b_col + (i_row - b_row), NEG_BIG)
            inter = b_col + m_prev
            m_t = jnp.maximum(inter, jnp.max(log_d, axis=1, keepdims=True))
            w_inter = jnp.exp(inter - m_t)
            s = lax.dot_general(q, k, _NT, preferred_element_type=f32) * jnp.exp(log_d - m_t)
            num = jnp.dot(s.astype(bf16), v, preferred_element_type=f32)
            num = num + w_inter * jnp.dot(q, ct_sc[h].astype(bf16), preferred_element_type=f32)
            den = jnp.sum(s, axis=1, keepdims=True)
            den = den + w_inter * jnp.sum(q.astype(f32) * n_sc[h], axis=1, keepdims=True)
            hv = num / jnp.maximum(jnp.abs(den), jnp.exp(-m_t))
            hv = hv * lax.rsqrt(jnp.mean(hv * hv, axis=1, keepdims=True) + RMS_EPS) * g_ref[:, hs]
            o_gate = jax.nn.sigmoid(om_ref[pl.ds(r0, L), hs].astype(f32))
            y_ref[pl.ds(r0, L), hs] = (o_gate * hv).astype(y_ref.dtype)

            g_col = b_last - b_col + i_col
            m_new = jnp.maximum(b_last + m_prev, jnp.max(g_col, axis=0, keepdims=True))
            wg = jnp.exp(g_col - m_new)
            decay = jnp.exp(b_last + m_prev - m_new)
            wv = (wg * v.astype(f32)).astype(bf16)
            ct_sc[h] = decay * ct_sc[h] + lax.dot_general(k, wv, _TN, preferred_element_type=f32)
            n_sc[h] = decay * n_sc[h] + jnp.sum(wg * k.astype(f32), axis=0, keepdims=True)
            m_sc[h] = jnp.broadcast_to(m_new, (1, LANES))
        return carry

    lax.fori_loop(0, n_chunks, chunk, 0)


def _mlstm_kernel(*refs, seq):
    _mlstm_body(*refs, seq=seq)


def mlstm(xo, conv_w, conv_b, w_qk, w_if, b_if, g, batch, seq):
    n = batch * seq
    H, dv, dk = N_MLSTM_HEADS, MLSTM_V_DIM, MLSTM_QK_DIM
    assert seq % MLSTM_CHUNK == 0 and dv == LANES
    bf16 = jnp.bfloat16
    pad = ((0, 0), (0, 0), (0, LANES - dk))
    wq = jnp.pad(w_qk[:, :, :dk], pad).astype(bf16)
    wk = jnp.pad(w_qk[:, :, dk:], pad).astype(bf16)
    rpad = ((0, 0), (0, LANES - dk), (0, 0))
    wiq = jnp.pad(w_if[:H * dk].reshape(H, dk, 2 * H), rpad).astype(bf16)
    wik = jnp.pad(w_if[H * dk:2 * H * dk].reshape(H, dk, 2 * H), rpad).astype(bf16)
    wix = w_if[2 * H * dk:].astype(bf16)
    wiqt, wikt, wixt = wiq.transpose(0, 2, 1), wik.transpose(0, 2, 1), wix.T
    full = lambda a: pl.BlockSpec(a.shape, lambda b: (0,) * a.ndim)
    consts = [conv_w, conv_b.reshape(1, -1), wq, wk, wiq, wik, wix, wiqt, wikt, wixt,
              b_if.reshape(1, -1), b_if.reshape(-1, 1), g.reshape(1, -1)]
    return pl.pallas_call(
        functools.partial(_mlstm_kernel, seq=seq),
        grid=(batch,),
        in_specs=[pl.BlockSpec((seq, MLSTM_WIDTH), lambda b: (b, 0)),
                  pl.BlockSpec((seq, MLSTM_WIDTH), lambda b: (b, 1))] + [full(a) for a in consts],
        out_specs=pl.BlockSpec((seq, MLSTM_WIDTH), lambda b: (b, 0)),
        out_shape=jax.ShapeDtypeStruct((n, MLSTM_WIDTH), bf16),
        scratch_shapes=[pltpu.VMEM((H, seq, LANES), bf16), pltpu.VMEM((H, seq, LANES), bf16),
                        pltpu.VMEM((seq, 2 * H), jnp.float32),
                        pltpu.VMEM((seq // MLSTM_CHUNK, 2 * H, MLSTM_CHUNK), jnp.float32),
                        pltpu.VMEM((H, LANES, LANES), jnp.float32),
                        pltpu.VMEM((H, 1, LANES), jnp.float32),
                        pltpu.VMEM((H, 1, LANES), jnp.float32)],
        compiler_params=_params("parallel"),
        name="mlstm",
    )(xo, xo, *consts)


def _store_rows_as_tiles(dst_ref, val):
    rows = val.shape[0]
    for c in range(ROW_CHUNKS):
        dst_ref[pl.ds(c, rows, stride=ROW_CHUNKS), :] = val[:, c * LANES:(c + 1) * LANES]


def _load_tiles_as_rows(src_ref, rows, lead=()):
    return jnp.concatenate(
        [src_ref[lead + (pl.ds(c, rows, stride=ROW_CHUNKS), slice(None))] for c in range(ROW_CHUNKS)],
        axis=1)


def _merge_route_body(x_ref, ya_ref, ym_ref, ga_ref, gm_ref, wb0_ref, wb1_ref, wo_ref, g_ref,
                      wr_ref, br_ref, x1_ref, h2_ref, gate_ref, lpos_ref, cnt_ref):
    f32, bf16 = jnp.float32, jnp.bfloat16
    tm = x_ref.shape[0]

    u = jax.nn.sigmoid(ga_ref[...].astype(f32)) * jnp.dot(ya_ref[...], wb0_ref[...], preferred_element_type=f32)
    u = u + jax.nn.sigmoid(gm_ref[...].astype(f32)) * jnp.dot(ym_ref[...], wb1_ref[...], preferred_element_type=f32)
    x1 = x_ref[...] + jnp.dot(u.astype(bf16), wo_ref[...], preferred_element_type=f32)
    x1_ref[...] = x1
    h2 = _rms(x1, g_ref[...])
    _store_rows_as_tiles(h2_ref, h2)

    logits = jnp.dot(h2, wr_ref[...], precision=lax.Precision.HIGHEST,
                     preferred_element_type=f32) + br_ref[...]
    logits = jnp.transpose(logits)[:N_EXPERTS]
    e_id = lax.broadcasted_iota(jnp.int32, logits.shape, 0).astype(f32)
    chosen = jnp.zeros(logits.shape, f32)
    vals, ids = [], []
    for _ in range(TOP_K):
        top = jnp.max(logits, axis=0, keepdims=True)
        first = jnp.min(jnp.where(logits == top, e_id, float(N_EXPERTS)), axis=0, keepdims=True)
        hit = e_id == first
        chosen = jnp.where(hit, 1.0, chosen)
        logits = jnp.where(hit, -jnp.inf, logits)
        vals.append(top)
        ids.append(first)
    ex = [jnp.exp(v - vals[0]) for v in vals]
    total = ex[0] + ex[1] + ex[2] + ex[3]

    chosen_b = chosen.astype(bf16)
    t_r = lax.broadcasted_iota(jnp.int32, (tm, tm), 0)
    t_c = lax.broadcasted_iota(jnp.int32, (tm, tm), 1)
    earlier = jnp.where(t_r < t_c, 1.0, 0.0).astype(bf16)
    before = jnp.dot(chosen_b, earlier, preferred_element_type=f32)
    e_r = lax.broadcasted_iota(jnp.int32, (N_EXPERTS, N_EXPERTS), 0)
    e_c = lax.broadcasted_iota(jnp.int32, (N_EXPERTS, N_EXPERTS), 1)
    lower = jnp.where(e_c < e_r, 1.0, 0.0).astype(bf16)
    seg_off = jnp.sum(jnp.dot(lower, chosen_b, preferred_element_type=f32), axis=1, keepdims=True)
    pos = before + seg_off
    for kk in range(TOP_K):
        gate_ref[kk:kk + 1, :] = ex[kk] / total
        lpos_ref[kk:kk + 1, :] = jnp.sum(jnp.where(e_id == ids[kk], pos, 0.0), axis=0,
                                         keepdims=True).astype(jnp.int32)
    counts = jnp.sum(chosen, axis=1, keepdims=True)
    cnt_ref[0] = jnp.broadcast_to(counts, (N_EXPERTS, LANES)).astype(jnp.int32)


def merge_route(x2, ya, ym, gates, wb0, wb1, wo, g, w_router, b_router, *, tm):
    n, d = x2.shape
    e = w_router.shape[1]
    assert n % tm == 0 and d == ROW_CHUNKS * LANES and e == N_EXPERTS
    full = lambda a: pl.BlockSpec(a.shape, lambda i: (0,) * a.ndim)
    lane_pad = ((0, 0), (0, LANES - e))
    consts = [wb0, wb1, wo, g.reshape(1, d), jnp.pad(w_router, lane_pad),
              jnp.pad(b_router.reshape(1, e), lane_pad)]
    tok = lambda dt: jax.ShapeDtypeStruct((TOP_K, n), dt)
    return pl.pallas_call(
        _merge_route_body,
        grid=(n // tm,),
        in_specs=[pl.BlockSpec((tm, d), lambda i: (i, 0)),
                  pl.BlockSpec((tm, ATT_WIDTH), lambda i: (i, 0)),
                  pl.BlockSpec((tm, MLSTM_WIDTH), lambda i: (i, 0)),
                  pl.BlockSpec((tm, d), lambda i: (i, 0)),
                  pl.BlockSpec((tm, d), lambda i: (i, 1))] + [full(a) for a in consts],
        out_specs=[pl.BlockSpec((tm, d), lambda i: (i, 0)),
                   pl.BlockSpec((tm * ROW_CHUNKS, LANES), lambda i: (i, 0)),
                   pl.BlockSpec((TOP_K, tm), lambda i: (0, i)),
                   pl.BlockSpec((TOP_K, tm), lambda i: (0, i)),
                   pl.BlockSpec((1, e, LANES), lambda i: (i, 0, 0))],
        out_shape=[jax.ShapeDtypeStruct((n, d), jnp.float32),
                   jax.ShapeDtypeStruct((n * ROW_CHUNKS, LANES), jnp.float32),
                   tok(jnp.float32), tok(jnp.int32),
                   jax.ShapeDtypeStruct((n // tm, e, LANES), jnp.int32)],
        compiler_params=_params("parallel"),
        name="merge_route",
    )(x2, ya, ym, gates, gates, *consts)


def _segment_pieces(cnt, max_rows, fn):
    done = 0
    size = 1 << (max_rows.bit_length() - 1)
    while size:
        bit = cnt & size

        def piece(done=done, size=size):
            fn(done, size)
        pl.when(bit != 0)(piece)
        done = done + bit
        size >>= 1


def _for_tile_segments(meta, tile_rows, make_copy, action):
    cnt_ref, off_ref, dst_ref = meta

    def per_expert(e, carry):
        off, dst = off_ref[0, 0, e], dst_ref[0, 0, e]
        _segment_pieces(cnt_ref[0, 0, e], tile_rows,
                        lambda done, size: getattr(make_copy(off + done, dst + done, size), action)())
        return carry
    lax.fori_loop(0, N_EXPERTS, per_expert, 0)


def _dispatch_body(lpos_ref, cnt_ref, off_ref, dst_ref, pcnt_ref, poff_ref, pdst_ref,
                   pad_len_ref, pad_row_ref, x_ref, out_ref, cbuf, zbuf, sem, zsem, *, tc, rb):
    step = pl.program_id(0)
    last = pl.num_programs(0) - 1
    slot = step % 2

    def compact(t, carry):
        row = x_ref[t]
        for kk in range(TOP_K):
            cbuf[slot, lpos_ref[kk, t]] = row
        return carry
    lax.fori_loop(0, tc, compact, 0)

    def seg_copy(sl):
        return lambda local, glob, size: pltpu.make_async_copy(
            cbuf.at[sl, pl.ds(local, size)], out_ref.at[pl.ds(glob, size)], sem.at[sl])

    cur, prev = (cnt_ref, off_ref, dst_ref), (pcnt_ref, poff_ref, pdst_ref)
    _for_tile_segments(cur, tc, seg_copy(slot), "start")

    @pl.when(step > 0)
    def _():
        _for_tile_segments(prev, tc, seg_copy(1 - slot), "wait")

    @pl.when(step == last)
    def _():
        _for_tile_segments(cur, tc, seg_copy(slot), "wait")
        zbuf[...] = jnp.zeros_like(zbuf)

        def pads(action):
            def per_pad(j, carry):
                row = pad_row_ref[0, j]
                _segment_pieces(pad_len_ref[0, j], rb, lambda done, size: getattr(
                    pltpu.make_async_copy(zbuf.at[pl.ds(0, size)], out_ref.at[pl.ds(row + done, size)], zsem),
                    action)())
                return carry
            lax.fori_loop(0, pad_len_ref.shape[1], per_pad, 0)
        pads("start")
        pads("wait")


def dispatch(lpos, seg_cnt, seg_off, seg_dst, pad_len, pad_row, h2_tiles, n_rows, *, tc, rb):
    n = h2_tiles.shape[0]
    steps = n // tc
    assert n % tc == 0
    smem = pltpu.SMEM
    seg = lambda shift: pl.BlockSpec((1, 1, N_EXPERTS), lambda i: (jnp.maximum(i - shift, 0), 0, 0),
                                     memory_space=smem)
    whole = lambda a: pl.BlockSpec(a.shape, lambda i: (0,) * a.ndim, memory_space=smem)
    return pl.pallas_call(
        functools.partial(_dispatch_body, tc=tc, rb=rb),
        grid=(steps,),
        in_specs=[pl.BlockSpec((TOP_K, tc), lambda i: (0, i), memory_space=smem),
                  seg(0), seg(0), seg(0), seg(1), seg(1), seg(1), whole(pad_len), whole(pad_row),
                  pl.BlockSpec((tc, ROW_CHUNKS, LANES), lambda i: (i, 0, 0))],
        out_specs=pl.BlockSpec(memory_space=pl.ANY),
        out_shape=jax.ShapeDtypeStruct((n_rows, ROW_CHUNKS, LANES), h2_tiles.dtype),
        scratch_shapes=[pltpu.VMEM((2, TOP_K * tc, ROW_CHUNKS, LANES), h2_tiles.dtype),
                        pltpu.VMEM((rb, ROW_CHUNKS, LANES), h2_tiles.dtype),
                        pltpu.SemaphoreType.DMA((2,)), pltpu.SemaphoreType.DMA(())],
        compiler_params=_params("arbitrary"),
        name="dispatch",
    )(lpos, seg_cnt, seg_off, seg_dst, seg_cnt, seg_off, seg_dst, pad_len, pad_row, h2_tiles)


def _split_gate_up_body(w_ref, perm_ref, wg_ref, wu_ref):
    tile = perm_ref.shape[0]
    half = tile // 2
    w = w_ref[0].astype(jnp.bfloat16)
    for t in range(w.shape[1] // tile):
        o = jnp.dot(w[:, t * tile:(t + 1) * tile], perm_ref[...], preferred_element_type=jnp.float32)
        wg_ref[0, :, t * half:(t + 1) * half] = o[:, :half].astype(jnp.bfloat16)
        wu_ref[0, :, t * half:(t + 1) * half] = o[:, half:].astype(jnp.bfloat16)


def split_gate_up(w_gu, *, cols=1024):
    e, d, ff2 = w_gu.shape
    tile = 2 * LANES
    perm = np.zeros((tile, tile), np.float32)
    perm[2 * np.arange(LANES), np.arange(LANES)] = 1.0
    perm[2 * np.arange(LANES) + 1, LANES + np.arange(LANES)] = 1.0
    out = jax.ShapeDtypeStruct((e, d, ff2 // 2), jnp.bfloat16)
    return pl.pallas_call(
        _split_gate_up_body,
        grid=(e, ff2 // cols),
        in_specs=[pl.BlockSpec((1, d, cols), lambda i, j: (i, 0, j)),
                  pl.BlockSpec((tile, tile), lambda i, j: (0, 0))],
        out_specs=[pl.BlockSpec((1, d, cols // 2), lambda i, j: (i, 0, j))] * 2,
        out_shape=[out, out],
        compiler_params=_params("parallel", "parallel"),
        name="split_gate_up",
    )(w_gu, jnp.asarray(perm, jnp.bfloat16))


def _experts_body(blk_e_ref, used_ref, x_ref, wg_ref, bg_ref, wu_ref, bu_ref, wd_ref, bd_ref, y_ref):
    del blk_e_ref
    rows = x_ref.shape[0] // ROW_CHUNKS

    @pl.when(pl.program_id(0) < used_ref[0])
    def _():
        f32, bf16 = jnp.float32, jnp.bfloat16
        xb = _load_tiles_as_rows(x_ref, rows).astype(bf16)
        gate = jnp.dot(xb, wg_ref[0], preferred_element_type=f32) + bg_ref[0]
        up = jnp.dot(xb, wu_ref[0], preferred_element_type=f32) + bu_ref[0]
        gate = jnp.minimum(gate, SWIGLU_LIMIT)
        up = jnp.clip(up, -SWIGLU_LIMIT, SWIGLU_LIMIT)
        act = (up + 1.0) * (gate * jax.nn.sigmoid(gate * SWIGLU_ALPHA))
        y = jnp.dot(act.astype(bf16), wd_ref[0], preferred_element_type=f32) + bd_ref[0]
        _store_rows_as_tiles(y_ref, y)

    @pl.when(pl.program_id(0) >= used_ref[0])
    def _():
        y_ref[...] = jnp.zeros_like(y_ref)


def experts(blk_expert, n_used, xs_tiles, w_g, b_g, w_u, b_u, w_d, b_d):
    n_blocks = blk_expert.shape[0]
    rb = MOE_ROW_BLOCK
    d, ff = w_g.shape[1], w_g.shape[2]
    e_map3 = lambda i, be, nu: (be[i], 0, 0)
    grid_spec = pltpu.PrefetchScalarGridSpec(
        num_scalar_prefetch=2,
        grid=(n_blocks,),
        in_specs=[pl.BlockSpec((rb * ROW_CHUNKS, LANES), lambda i, be, nu: (i, 0)),
                  pl.BlockSpec((1, d, ff), e_map3), pl.BlockSpec((1, 1, ff), e_map3),
                  pl.BlockSpec((1, d, ff), e_map3), pl.BlockSpec((1, 1, ff), e_map3),
                  pl.BlockSpec((1, ff, d), e_map3), pl.BlockSpec((1, 1, d), e_map3)],
        out_specs=pl.BlockSpec((rb * ROW_CHUNKS, LANES), lambda i, be, nu: (i, 0)),
    )
    return pl.pallas_call(
        _experts_body,
        grid_spec=grid_spec,
        out_shape=jax.ShapeDtypeStruct(xs_tiles.shape, jnp.float32),
        compiler_params=_params("arbitrary"),
        name="experts",
    )(blk_expert, n_used, xs_tiles, w_g, b_g, w_u, b_u, w_d, b_d)


def _combine_ple_body(lpos_ref, gw_ref, cnt_ref, off_ref, dst_ref, ncnt_ref, noff_ref, ndst_ref,
                      x1_ref, ys_ref, p_ref, gp_ref, wpg_ref, wpp_ref, *rest):
    cbuf, msum, sem = rest[-3:]
    rest = rest[:-3]
    gf_ref, o_ref = rest if len(rest) == 2 else (None, rest[0])
    f32, bf16 = jnp.float32, jnp.bfloat16
    tm = x1_ref.shape[0]
    step = pl.program_id(0)
    slot = step % 2

    def seg_copy(sl):
        return lambda local, glob, size: pltpu.make_async_copy(
            ys_ref.at[pl.ds(glob, size)], cbuf.at[sl, pl.ds(local, size)], sem.at[sl])

    cur, nxt = (cnt_ref, off_ref, dst_ref), (ncnt_ref, noff_ref, ndst_ref)

    @pl.when(step == 0)
    def _():
        _for_tile_segments(cur, tm, seg_copy(0), "start")

    @pl.when(step + 1 < pl.num_programs(0))
    def _():
        _for_tile_segments(nxt, tm, seg_copy(1 - slot), "start")

    _for_tile_segments(cur, tm, seg_copy(slot), "wait")

    def weighted_sum(t, carry):
        acc = gw_ref[0, t] * cbuf[slot, lpos_ref[0, t]]
        for kk in range(1, TOP_K):
            acc = acc + gw_ref[kk, t] * cbuf[slot, lpos_ref[kk, t]]
        msum[t] = acc
        return carry
    lax.fori_loop(0, tm, weighted_sum, 0)

    moe = jnp.concatenate([msum[:, c, :] for c in range(ROW_CHUNKS)], axis=1)
    x2 = x1_ref[...] + moe
    r = _rms(x2, gp_ref[...]).astype(bf16)
    gate = jax.nn.sigmoid(jnp.dot(r, wpg_ref[...], preferred_element_type=f32))
    emb = jnp.dot(p_ref[...].astype(bf16), wpp_ref[...], preferred_element_type=f32)
    x3 = x2 + gate * emb
    o_ref[...] = x3 if gf_ref is None else _rms(x3, gf_ref[...])


def combine_ple(lpos, gate_w, seg_cnt, seg_off, seg_dst, x1, ys_tiles, p2, g_ple, w_pg, w_pp, g_final,
                *, tm):
    n, d = x1.shape
    steps = n // tm
    smem = pltpu.SMEM
    full = lambda a: pl.BlockSpec(a.shape, lambda i: (0,) * a.ndim)
    seg = lambda shift: pl.BlockSpec((1, 1, N_EXPERTS),
                                     lambda i: (jnp.minimum(i + shift, steps - 1), 0, 0), memory_space=smem)
    tok = pl.BlockSpec((TOP_K, tm), lambda i: (0, i), memory_space=smem)
    consts = [g_ple.reshape(1, d), w_pg, w_pp] + ([] if g_final is None else [g_final.reshape(1, d)])
    return pl.pallas_call(
        _combine_ple_body,
        grid=(steps,),
        in_specs=[tok, tok, seg(0), seg(0), seg(0), seg(1), seg(1), seg(1),
                  pl.BlockSpec((tm, d), lambda i: (i, 0)),
                  pl.BlockSpec(memory_space=pl.ANY),
                  pl.BlockSpec((tm, p2.shape[1]), lambda i: (i, 0))] + [full(a) for a in consts],
        out_specs=pl.BlockSpec((tm, d), lambda i: (i, 0)),
        out_shape=jax.ShapeDtypeStruct((n, d), jnp.float32),
        scratch_shapes=[pltpu.VMEM((2, TOP_K * tm, ROW_CHUNKS, LANES), ys_tiles.dtype),
                        pltpu.VMEM((tm, ROW_CHUNKS, LANES), jnp.float32),
                        pltpu.SemaphoreType.DMA((2,))],
        compiler_params=_params("arbitrary"),
        name="combine_ple",
    )(lpos, gate_w, seg_cnt, seg_off, seg_dst, seg_cnt, seg_off, seg_dst, x1, ys_tiles, p2, *consts)


def _moe(h2_tiles, lpos, tile_cnt, w_gu, b_gu, w_d, b_d, *, tm):
    n = h2_tiles.shape[0] // ROW_CHUNKS
    rb = MOE_ROW_BLOCK
    i32 = jnp.int32
    n_blocks = -(-n * TOP_K // rb) + N_EXPERTS
    n_rows = n_blocks * rb
    counts = jnp.sum(tile_cnt, axis=0)
    padded = (counts + rb - 1) // rb * rb
    pend = jnp.cumsum(padded)
    pstart = pend - padded
    blk_start = jnp.arange(n_blocks, dtype=i32) * rb
    blk_expert = jnp.minimum(jnp.sum((pend[None, :] <= blk_start[:, None]).astype(i32), axis=1),
                             N_EXPERTS - 1)
    n_used = pend[-1:] // rb
    seg_cnt = tile_cnt[:, None, :]
    seg_off = (jnp.cumsum(tile_cnt, axis=1) - tile_cnt)[:, None, :]
    seg_dst = (pstart[None, :] + jnp.cumsum(tile_cnt, axis=0) - tile_cnt)[:, None, :]
    tail = n_used + jnp.arange(N_EXPERTS, dtype=i32)
    pad_len = jnp.concatenate([padded - counts, jnp.where(tail < n_blocks, rb, 0)])[None, :]
    pad_row = jnp.concatenate([pstart + counts, tail * rb])[None, :]
    w_g, w_u = split_gate_up(w_gu)
    b_g, b_u = b_gu[:, None, 0::2], b_gu[:, None, 1::2]
    xs = dispatch(lpos, seg_cnt, seg_off, seg_dst, pad_len.astype(i32), pad_row.astype(i32),
                  h2_tiles.reshape(n, ROW_CHUNKS, LANES), n_rows, tc=tm, rb=rb)
    ys = experts(blk_expert, n_used.astype(i32), xs.reshape(n_rows * ROW_CHUNKS, LANES),
                 w_g, b_g, w_u, b_u, w_d.astype(jnp.bfloat16), b_d[:, None, :])
    return (seg_cnt, seg_off, seg_dst), ys.reshape(n_rows, ROW_CHUNKS, LANES)


def kernel(x, p, norm_mix_g, w_in, conv_w, conv_b, w_qk_m, w_if, b_if, mnorm_g, w_branch, w_out, norm_ffn_g, w_router, b_router, w_gate_up, b_gate_up, w_down, b_down, norm_ple_g, w_ple_gate, w_ple_proj, final_norm_g):
    B, S, D = x.shape
    depth = w_in.shape[0]
    bf16 = jnp.bfloat16
    x2 = x.reshape(B * S, D)
    for i in range(depth):
        qkv, xo, gates = in_proj(x2, norm_mix_g[i], w_in[i].astype(bf16))
        y_a = moba(qkv, B, S)
        y_m = mlstm(xo, conv_w[i], conv_b[i], w_qk_m[i], w_if[i], b_if[i], mnorm_g[i], B, S)
        x1, h2_tiles, gate_w, lpos, tile_cnt = merge_route(
            x2, y_a, y_m, gates, w_branch[i, 0].astype(bf16), w_branch[i, 1].astype(bf16),
            w_out[i].astype(bf16), norm_ffn_g[i], w_router[i], b_router[i], tm=MOE_TOKEN_TILE)
        segs, ys = _moe(h2_tiles, lpos, tile_cnt[:, :, 0], w_gate_up[i], b_gate_up[i],
                        w_down[i], b_down[i], tm=MOE_TOKEN_TILE)
        x2 = combine_ple(lpos, gate_w, *segs, x1, ys, p[i].reshape(B * S, -1), norm_ple_g[i],
                         w_ple_gate[i].astype(bf16), w_ple_proj[i].astype(bf16),
                         final_norm_g if i == depth - 1 else None, tm=MOE_TOKEN_TILE)
    return x2.reshape(B, S, D)
```

```python
import functools

import jax
import jax.numpy as jnp
import numpy as np
from jax import lax
from jax.experimental import pallas as pl
from jax.experimental.pallas import tpu as pltpu

RMS_EPS = 1e-6
LANES = 128
ROW_CHUNKS = 8
VMEM_LIMIT = 56 * 1024 * 1024

N_ATT_HEADS = 8
ATT_HEAD_DIM = 64
ATT_WIDTH = N_ATT_HEADS * ATT_HEAD_DIM
MOBA_BLOCK = 256
MOBA_TOPK = 3

N_MLSTM_HEADS = 4
MLSTM_WIDTH = 512
MLSTM_V_DIM = 128
MLSTM_QK_DIM = 64
MLSTM_CONV = 4
MLSTM_CHUNK = 128

N_EXPERTS = 32
TOP_K = 4
SWIGLU_ALPHA = 1.702
SWIGLU_LIMIT = 7.0
MOE_ROW_BLOCK = 512
MOE_TOKEN_TILE = 512

NEG_BIG = -1e30

_NT = (((1,), (1,)), ((), ()))
_TN = (((0,), (0,)), ((), ()))


def _params(*sem):
    return pltpu.CompilerParams(dimension_semantics=sem, vmem_limit_bytes=VMEM_LIMIT)


def _rms(x, g):
    return x * lax.rsqrt(jnp.mean(x * x, axis=-1, keepdims=True) + RMS_EPS) * g


def _in_proj_body(x_ref, g_ref, w_ref, qkv_ref, xo_ref, gate_ref, *, col_chunk):
    h = _rms(x_ref[...], g_ref[...]).astype(jnp.bfloat16)
    col = 0
    for out_ref in (qkv_ref, xo_ref, gate_ref):
        for c in range(0, out_ref.shape[1], col_chunk):
            out_ref[:, c:c + col_chunk] = jnp.dot(
                h, w_ref[:, col + c:col + c + col_chunk],
                preferred_element_type=jnp.float32).astype(out_ref.dtype)
        col += out_ref.shape[1]


def in_proj(x2, g, w_bf16, *, tm=512, col_chunk=512):
    n, d = x2.shape
    widths = (3 * ATT_WIDTH, 2 * MLSTM_WIDTH, 2 * d)
    assert sum(widths) == w_bf16.shape[1] and n % tm == 0
    return pl.pallas_call(
        functools.partial(_in_proj_body, col_chunk=col_chunk),
        grid=(n // tm,),
        in_specs=[pl.BlockSpec((tm, d), lambda i: (i, 0)),
                  pl.BlockSpec((1, d), lambda i: (0, 0)),
                  pl.BlockSpec(w_bf16.shape, lambda i: (0, 0))],
        out_specs=[pl.BlockSpec((tm, w), lambda i: (i, 0)) for w in widths],
        out_shape=[jax.ShapeDtypeStruct((n, w), jnp.bfloat16) for w in widths],
        compiler_params=_params("parallel"),
        name="in_proj",
    )(x2, g.reshape(1, d), w_bf16)


def _moba_body(q_ref, k_ref, v_ref, o_ref, *, pair, seq):
    blk = MOBA_BLOCK
    n_blk = seq // blk
    f32, bf16 = jnp.float32, jnp.bfloat16
    lane = lax.broadcasted_iota(jnp.int32, (1, LANES), 1)
    row = lax.broadcasted_iota(jnp.int32, (blk, blk), 0)
    col = lax.broadcasted_iota(jnp.int32, (blk, blk), 1)
    rel = (row - col).astype(f32)
    causal = row >= col
    log2e = float(np.log2(np.e))
    scale2 = ATT_HEAD_DIM ** -0.5 * log2e
    lane_row = lax.broadcasted_iota(jnp.int32, (LANES, LANES), 0)
    spread = [jnp.where(lane_row == j, 1.0, 0.0).astype(bf16) for j in range(n_blk)]

    k_all = k_ref[...]
    v_all = v_ref[...]
    k_mean = jnp.concatenate(
        [jnp.mean(k_all[j * blk:(j + 1) * blk].astype(f32), axis=0, keepdims=True)
         for j in range(n_blk)] + [jnp.zeros((LANES - n_blk, LANES), f32)], axis=0)

    outs = []
    for hh in range(2):
        head = 2 * pair + hh
        slope2 = jnp.exp2(-8.0 * (head + 1).astype(f32) / N_ATT_HEADS) * log2e
        bias = [slope2 * (rel + float(d * blk)) for d in range(n_blk)]
        head_lanes = (lane >= hh * ATT_HEAD_DIM) & (lane < (hh + 1) * ATT_HEAD_DIM)
        q_all = q_ref[...]
        q_all = jnp.where(head_lanes, q_all, jnp.zeros_like(q_all))
        if n_blk > MOBA_TOPK + 1:
            scores = lax.dot_general(q_all.astype(f32), jnp.where(head_lanes, k_mean, 0.0), _NT,
                                     precision=lax.Precision.HIGHEST, preferred_element_type=f32)
        head_out = []
        for qi in range(n_blk):
            q_h = q_all[qi * blk:(qi + 1) * blk]
            sel = None
            if qi > MOBA_TOPK:
                g = jnp.where(lane < qi, scores[qi * blk:(qi + 1) * blk], -jnp.inf)
                ahead = jnp.zeros((blk, LANES), f32)
                for d in range(1, qi):
                    ahead = ahead + jnp.where(pltpu.roll(g, d, axis=1) >= g, 1.0, 0.0)
                    ahead = ahead + jnp.where(pltpu.roll(g, LANES - d, axis=1) > g, 1.0, 0.0)
                keep = jnp.where(ahead < MOBA_TOPK, 1.0, 0.0).astype(bf16)
                sel = [jnp.dot(keep, spread[j], preferred_element_type=f32) > 0.5 for j in range(qi)]
            logits = []
            for j in range(qi + 1):
                s = lax.dot_general(q_h, k_all[j * blk:(j + 1) * blk], _NT, preferred_element_type=f32)
                s = s * scale2 - bias[qi - j]
                if j == qi:
                    s = jnp.where(causal, s, NEG_BIG)
                elif sel is not None:
                    s = jnp.where(jnp.concatenate([sel[j]] * (blk // LANES), axis=1), s, NEG_BIG)
                logits.append(s)
            s_all = jnp.concatenate(logits, axis=1)
            m = jnp.max(s_all, axis=1, keepdims=True)
            p = jnp.exp2(s_all - m)
            denom = jnp.sum(p, axis=1, keepdims=True)
            o = jnp.dot(p.astype(jnp.bfloat16), v_all[:(qi + 1) * blk],
                        preferred_element_type=jnp.float32)
            head_out.append(o / denom)
        outs.append((head_lanes, head_out))
    for qi in range(n_blk):
        o = jnp.where(outs[0][0], outs[0][1][qi], outs[1][1][qi])
        o_ref[qi * blk:(qi + 1) * blk, :] = o.astype(o_ref.dtype)


def _moba_kernel(q_ref, k_ref, v_ref, o_ref, *, seq):
    _moba_body(q_ref, k_ref, v_ref, o_ref, pair=pl.program_id(1), seq=seq)


def moba(qkv, batch, seq):
    n = batch * seq
    n_pairs = ATT_WIDTH // LANES
    assert seq % MOBA_BLOCK == 0
    spec = lambda off: pl.BlockSpec((seq, LANES), lambda b, p: (b, off + p))
    return pl.pallas_call(
        functools.partial(_moba_kernel, seq=seq),
        grid=(batch, n_pairs),
        in_specs=[spec(0), spec(n_pairs), spec(2 * n_pairs)],
        out_specs=pl.BlockSpec((seq, LANES), lambda b, p: (b, p)),
        out_shape=jax.ShapeDtypeStruct((n, ATT_WIDTH), jnp.bfloat16),
        compiler_params=_params("parallel", "parallel"),
        name="moba",
    )(qkv, qkv, qkv)


def _log_sigmoid(z):
    return jnp.minimum(z, 0.0) - jnp.log(1.0 + jnp.exp(-jnp.abs(z)))


def _mlstm_body(xm_ref, om_ref, cw_ref, cb_ref, wq_ref, wk_ref, wiq_ref, wik_ref, wix_ref,
                wiqt_ref, wikt_ref, wixt_ref, bi_ref, bit_ref, g_ref, y_ref,
                q_sc, k_sc, gcol_sc, grow_sc, ct_sc, n_sc, m_sc, *, seq):
    L = MLSTM_CHUNK
    H = N_MLSTM_HEADS
    n_chunks = seq // L
    f32, bf16 = jnp.float32, jnp.bfloat16

    x = xm_ref[...].astype(f32)
    t_idx = lax.broadcasted_iota(jnp.int32, (seq, 1), 0)
    acc = x * cw_ref[MLSTM_CONV - 1:MLSTM_CONV, :] + cb_ref[...]
    for d in range(1, MLSTM_CONV):
        shifted = jnp.where(t_idx >= d, pltpu.roll(x, d, axis=0), 0.0)
        acc = acc + shifted * cw_ref[MLSTM_CONV - 1 - d:MLSTM_CONV - d, :]
    xc = (acc * jax.nn.sigmoid(acc)).astype(bf16)

    xm = xm_ref[...]
    if_col = jnp.dot(xm, wix_ref[...], preferred_element_type=f32) + bi_ref[...]
    if_row = lax.dot_general(wixt_ref[...], xm, _NT, preferred_element_type=f32) + bit_ref[...]
    for h in range(H):
        xch = xc[:, h * LANES:(h + 1) * LANES]
        q = jnp.dot(xch, wq_ref[h], preferred_element_type=f32).astype(bf16)
        k = jnp.dot(xch, wk_ref[h], preferred_element_type=f32).astype(bf16)
        if_col = if_col + jnp.dot(q, wiq_ref[h], preferred_element_type=f32)
        if_col = if_col + jnp.dot(k, wik_ref[h], preferred_element_type=f32)
        if_row = if_row + lax.dot_general(wiqt_ref[h], q, _NT, preferred_element_type=f32)
        if_row = if_row + lax.dot_general(wikt_ref[h], k, _NT, preferred_element_type=f32)
        q_sc[h] = q
        k_sc[h] = (k.astype(f32) * (MLSTM_QK_DIM ** -0.5)).astype(bf16)
    is_f_col = lax.broadcasted_iota(jnp.int32, (1, 2 * H), 1) >= H
    is_f_row = lax.broadcasted_iota(jnp.int32, (2 * H, 1), 0) >= H
    gcol_sc[...] = jnp.where(is_f_col, _log_sigmoid(if_col), if_col)
    g_row = jnp.where(is_f_row, _log_sigmoid(if_row), if_row)
    for c in range(n_chunks):
        grow_sc[c] = g_row[:, c * L:(c + 1) * L]

    ct_sc[...] = jnp.zeros_like(ct_sc)
    n_sc[...] = jnp.zeros_like(n_sc)
    m_sc[...] = jnp.zeros_like(m_sc)

    r_i = lax.broadcasted_iota(jnp.int32, (L, L), 0)
    c_i = lax.broadcasted_iota(jnp.int32, (L, L), 1)
    causal = r_i >= c_i
    tri_lo = jnp.where(causal, 1.0, 0.0).astype(f32)
    tri_up = jnp.where(c_i >= r_i, 1.0, 0.0).astype(f32)

    def chunk(c, carry):
        r0 = pl.multiple_of(c * L, L)
        gc = gcol_sc[pl.ds(r0, L), :]
        gr = grow_sc[c]
        b_cols = jnp.dot(tri_lo, gc, precision=lax.Precision.HIGHEST, preferred_element_type=f32)
        b_rows = jnp.dot(gr, tri_up, precision=lax.Precision.HIGHEST, preferred_element_type=f32)
        for h in range(H):
            hs = slice(h * LANES, (h + 1) * LANES)
            b_col = b_cols[:, H + h:H + h + 1]
            i_col = gc[:, h:h + 1]
            b_row = b_rows[H + h:H + h + 1, :]
            i_row = gr[h:h + 1, :]
            b_last = b_col[L - 1:L, :]
            m_prev = m_sc[h][:, 0:1]
            q = q_sc[h, pl.ds(r0, L), :]
            k = k_sc[h, pl.ds(r0, L), :]
            v = xm_ref[pl.ds(r0, L), hs]

            log_d = jnp.where(causal, b_col + (i_row - b_row), NEG_BIG)
            inter = b_col + m_prev
            m_t = jnp.maximum(inter, jnp.max(log_d, axis=1, keepdims=True))
            w_inter = jnp.exp(inter - m_t)
            s = lax.dot_general(q, k, _NT, preferred_element_type=f32) * jnp.exp(log_d - m_t)
            num = jnp.dot(s.astype(bf16), v, preferred_element_type=f32)
            num = num + w_inter * jnp.dot(q, ct_sc[h].astype(bf16), preferred_element_type=f32)
            den = jnp.sum(s, axis=1, keepdims=True)
            den = den + w_inter * jnp.sum(q.astype(f32) * n_sc[h], axis=1, keepdims=True)
            hv = num / jnp.maximum(jnp.abs(den), jnp.exp(-m_t))
            hv = hv * lax.rsqrt(jnp.mean(hv * hv, axis=1, keepdims=True) + RMS_EPS) * g_ref[:, hs]
            o_gate = jax.nn.sigmoid(om_ref[pl.ds(r0, L), hs].astype(f32))
            y_ref[pl.ds(r0, L), hs] = (o_gate * hv).astype(y_ref.dtype)

            g_col = b_last - b_col + i_col
            m_new = jnp.maximum(b_last + m_prev, jnp.max(g_col, axis=0, keepdims=True))
            wg = jnp.exp(g_col - m_new)
            decay = jnp.exp(b_last + m_prev - m_new)
            wv = (wg * v.astype(f32)).astype(bf16)
            ct_sc[h] = decay * ct_sc[h] + lax.dot_general(k, wv, _TN, preferred_element_type=f32)
            n_sc[h] = decay * n_sc[h] + jnp.sum(wg * k.astype(f32), axis=0, keepdims=True)
            m_sc[h] = jnp.broadcast_to(m_new, (1, LANES))
        return carry

    lax.fori_loop(0, n_chunks, chunk, 0)


def _mlstm_kernel(*refs, seq):
    _mlstm_body(*refs, seq=seq)


def mlstm(xo, conv_w, conv_b, w_qk, w_if, b_if, g, batch, seq):
    n = batch * seq
    H, dv, dk = N_MLSTM_HEADS, MLSTM_V_DIM, MLSTM_QK_DIM
    assert seq % MLSTM_CHUNK == 0 and dv == LANES
    bf16 = jnp.bfloat16
    pad = ((0, 0), (0, 0), (0, LANES - dk))
    wq = jnp.pad(w_qk[:, :, :dk], pad).astype(bf16)
    wk = jnp.pad(w_qk[:, :, dk:], pad).astype(bf16)
    rpad = ((0, 0), (0, LANES - dk), (0, 0))
    wiq = jnp.pad(w_if[:H * dk].reshape(H, dk, 2 * H), rpad).astype(bf16)
    wik = jnp.pad(w_if[H * dk:2 * H * dk].reshape(H, dk, 2 * H), rpad).astype(bf16)
    wix = w_if[2 * H * dk:].astype(bf16)
    wiqt, wikt, wixt = wiq.transpose(0, 2, 1), wik.transpose(0, 2, 1), wix.T
    full = lambda a: pl.BlockSpec(a.shape, lambda b: (0,) * a.ndim)
    consts = [conv_w, conv_b.reshape(1, -1), wq, wk, wiq, wik, wix, wiqt, wikt, wixt,
              b_if.reshape(1, -1), b_if.reshape(-1, 1), g.reshape(1, -1)]
    return pl.pallas_call(
        functools.partial(_mlstm_kernel, seq=seq),
        grid=(batch,),
        in_specs=[pl.BlockSpec((seq, MLSTM_WIDTH), lambda b: (b, 0)),
                  pl.BlockSpec((seq, MLSTM_WIDTH), lambda b: (b, 1))] + [full(a) for a in consts],
        out_specs=pl.BlockSpec((seq, MLSTM_WIDTH), lambda b: (b, 0)),
        out_shape=jax.ShapeDtypeStruct((n, MLSTM_WIDTH), bf16),
        scratch_shapes=[pltpu.VMEM((H, seq, LANES), bf16), pltpu.VMEM((H, seq, LANES), bf16),
                        pltpu.VMEM((seq, 2 * H), jnp.float32),
                        pltpu.VMEM((seq // MLSTM_CHUNK, 2 * H, MLSTM_CHUNK), jnp.float32),
                        pltpu.VMEM((H, LANES, LANES), jnp.float32),
                        pltpu.VMEM((H, 1, LANES), jnp.float32),
                        pltpu.VMEM((H, 1, LANES), jnp.float32)],
        compiler_params=_params("parallel"),
        name="mlstm",
    )(xo, xo, *consts)


def _store_rows_as_tiles(dst_ref, val):
    rows = val.shape[0]
    for c in range(ROW_CHUNKS):
        dst_ref[pl.ds(c, rows, stride=ROW_CHUNKS), :] = val[:, c * LANES:(c + 1) * LANES]


def _load_tiles_as_rows(src_ref, rows, lead=()):
    return jnp.concatenate(
        [src_ref[lead + (pl.ds(c, rows, stride=ROW_CHUNKS), slice(None))] for c in range(ROW_CHUNKS)],
        axis=1)


def _merge_route_body(x_ref, ya_ref, ym_ref, ga_ref, gm_ref, wb0_ref, wb1_ref, wo_ref, g_ref,
                      wr_ref, br_ref, x1_ref, h2_ref, gate_ref, lpos_ref, cnt_ref):
    f32, bf16 = jnp.float32, jnp.bfloat16
    tm = x_ref.shape[0]

    u = jax.nn.sigmoid(ga_ref[...].astype(f32)) * jnp.dot(ya_ref[...], wb0_ref[...], preferred_element_type=f32)
    u = u + jax.nn.sigmoid(gm_ref[...].astype(f32)) * jnp.dot(ym_ref[...], wb1_ref[...], preferred_element_type=f32)
    x1 = x_ref[...] + jnp.dot(u.astype(bf16), wo_ref[...], preferred_element_type=f32)
    x1_ref[...] = x1
    h2 = _rms(x1, g_ref[...])
    _store_rows_as_tiles(h2_ref, h2)

    logits = jnp.dot(h2, wr_ref[...], precision=lax.Precision.HIGHEST,
                     preferred_element_type=f32) + br_ref[...]
    logits = jnp.transpose(logits)[:N_EXPERTS]
    e_id = lax.broadcasted_iota(jnp.int32, logits.shape, 0).astype(f32)
    chosen = jnp.zeros(logits.shape, f32)
    vals, ids = [], []
    for _ in range(TOP_K):
        top = jnp.max(logits, axis=0, keepdims=True)
        first = jnp.min(jnp.where(logits == top, e_id, float(N_EXPERTS)), axis=0, keepdims=True)
        hit = e_id == first
        chosen = jnp.where(hit, 1.0, chosen)
        logits = jnp.where(hit, -jnp.inf, logits)
        vals.append(top)
        ids.append(first)
    ex = [jnp.exp(v - vals[0]) for v in vals]
    total = ex[0] + ex[1] + ex[2] + ex[3]

    chosen_b = chosen.astype(bf16)
    t_r = lax.broadcasted_iota(jnp.int32, (tm, tm), 0)
    t_c = lax.broadcasted_iota(jnp.int32, (tm, tm), 1)
    earlier = jnp.where(t_r < t_c, 1.0, 0.0).astype(bf16)
    before = jnp.dot(chosen_b, earlier, preferred_element_type=f32)
    e_r = lax.broadcasted_iota(jnp.int32, (N_EXPERTS, N_EXPERTS), 0)
    e_c = lax.broadcasted_iota(jnp.int32, (N_EXPERTS, N_EXPERTS), 1)
    lower = jnp.where(e_c < e_r, 1.0, 0.0).astype(bf16)
    seg_off = jnp.sum(jnp.dot(lower, chosen_b, preferred_element_type=f32), axis=1, keepdims=True)
    pos = before + seg_off
    for kk in range(TOP_K):
        gate_ref[kk:kk + 1, :] = ex[kk] / total
        lpos_ref[kk:kk + 1, :] = jnp.sum(jnp.where(e_id == ids[kk], pos, 0.0), axis=0,
                                         keepdims=True).astype(jnp.int32)
    counts = jnp.sum(chosen, axis=1, keepdims=True)
    cnt_ref[0] = jnp.broadcast_to(counts, (N_EXPERTS, LANES)).astype(jnp.int32)


def merge_route(x2, ya, ym, gates, wb0, wb1, wo, g, w_router, b_router, *, tm):
    n, d = x2.shape
    e = w_router.shape[1]
    assert n % tm == 0 and d == ROW_CHUNKS * LANES and e == N_EXPERTS
    full = lambda a: pl.BlockSpec(a.shape, lambda i: (0,) * a.ndim)
    lane_pad = ((0, 0), (0, LANES - e))
    consts = [wb0, wb1, wo, g.reshape(1, d), jnp.pad(w_router, lane_pad),
              jnp.pad(b_router.reshape(1, e), lane_pad)]
    tok = lambda dt: jax.ShapeDtypeStruct((TOP_K, n), dt)
    return pl.pallas_call(
        _merge_route_body,
        grid=(n // tm,),
        in_specs=[pl.BlockSpec((tm, d), lambda i: (i, 0)),
                  pl.BlockSpec((tm, ATT_WIDTH), lambda i: (i, 0)),
                  pl.BlockSpec((tm, MLSTM_WIDTH), lambda i: (i, 0)),
                  pl.BlockSpec((tm, d), lambda i: (i, 0)),
                  pl.BlockSpec((tm, d), lambda i: (i, 1))] + [full(a) for a in consts],
        out_specs=[pl.BlockSpec((tm, d), lambda i: (i, 0)),
                   pl.BlockSpec((tm * ROW_CHUNKS, LANES), lambda i: (i, 0)),
                   pl.BlockSpec((TOP_K, tm), lambda i: (0, i)),
                   pl.BlockSpec((TOP_K, tm), lambda i: (0, i)),
                   pl.BlockSpec((1, e, LANES), lambda i: (i, 0, 0))],
        out_shape=[jax.ShapeDtypeStruct((n, d), jnp.float32),
                   jax.ShapeDtypeStruct((n * ROW_CHUNKS, LANES), jnp.float32),
                   tok(jnp.float32), tok(jnp.int32),
                   jax.ShapeDtypeStruct((n // tm, e, LANES), jnp.int32)],
        compiler_params=_params("parallel"),
        name="merge_route",
    )(x2, ya, ym, gates, gates, *consts)


def _segment_pieces(cnt, max_rows, fn):
    done = 0
    size = 1 << (max_rows.bit_length() - 1)
    while size:
        bit = cnt & size

        def piece(done=done, size=size):
            fn(done, size)
        pl.when(bit != 0)(piece)
        done = done + bit
        size >>= 1


def _for_tile_segments(meta, tile_rows, make_copy, action):
    cnt_ref, off_ref, dst_ref = meta

    def per_expert(e, carry):
        off, dst = off_ref[0, 0, e], dst_ref[0, 0, e]
        _segment_pieces(cnt_ref[0, 0, e], tile_rows,
                        lambda done, size: getattr(make_copy(off + done, dst + done, size), action)())
        return carry
    lax.fori_loop(0, N_EXPERTS, per_expert, 0)


def _dispatch_body(lpos_ref, cnt_ref, off_ref, dst_ref, pcnt_ref, poff_ref, pdst_ref,
                   pad_len_ref, pad_row_ref, x_ref, out_ref, cbuf, zbuf, sem, zsem, *, tc, rb):
    step = pl.program_id(0)
    last = pl.num_programs(0) - 1
    slot = step % 2

    def compact(t, carry):
        row = x_ref[t]
        for kk in range(TOP_K):
            cbuf[slot, lpos_ref[kk, t]] = row
        return carry
    lax.fori_loop(0, tc, compact, 0, unroll=8)

    def seg_copy(sl):
        return lambda local, glob, size: pltpu.make_async_copy(
            cbuf.at[sl, pl.ds(local, size)], out_ref.at[pl.ds(glob, size)], sem.at[sl])

    cur, prev = (cnt_ref, off_ref, dst_ref), (pcnt_ref, poff_ref, pdst_ref)
    _for_tile_segments(cur, tc, seg_copy(slot), "start")

    @pl.when(step > 0)
    def _():
        _for_tile_segments(prev, tc, seg_copy(1 - slot), "wait")

    @pl.when(step == last)
    def _():
        _for_tile_segments(cur, tc, seg_copy(slot), "wait")
        zbuf[...] = jnp.zeros_like(zbuf)

        def pads(action):
            def per_pad(j, carry):
                row = pad_row_ref[0, j]
                _segment_pieces(pad_len_ref[0, j], rb, lambda done, size: getattr(
                    pltpu.make_async_copy(zbuf.at[pl.ds(0, size)], out_ref.at[pl.ds(row + done, size)], zsem),
                    action)())
                return carry
            lax.fori_loop(0, pad_len_ref.shape[1], per_pad, 0)
        pads("start")
        pads("wait")


def dispatch(lpos, seg_cnt, seg_off, seg_dst, pad_len, pad_row, h2_tiles, n_rows, *, tc, rb):
    n = h2_tiles.shape[0]
    steps = n // tc
    assert n % tc == 0
    smem = pltpu.SMEM
    seg = lambda shift: pl.BlockSpec((1, 1, N_EXPERTS), lambda i: (jnp.maximum(i - shift, 0), 0, 0),
                                     memory_space=smem)
    whole = lambda a: pl.BlockSpec(a.shape, lambda i: (0,) * a.ndim, memory_space=smem)
    return pl.pallas_call(
        functools.partial(_dispatch_body, tc=tc, rb=rb),
        grid=(steps,),
        in_specs=[pl.BlockSpec((TOP_K, tc), lambda i: (0, i), memory_space=smem),
                  seg(0), seg(0), seg(0), seg(1), seg(1), seg(1), whole(pad_len), whole(pad_row),
                  pl.BlockSpec((tc, ROW_CHUNKS, LANES), lambda i: (i, 0, 0))],
        out_specs=pl.BlockSpec(memory_space=pl.ANY),
        out_shape=jax.ShapeDtypeStruct((n_rows, ROW_CHUNKS, LANES), h2_tiles.dtype),
        scratch_shapes=[pltpu.VMEM((2, TOP_K * tc, ROW_CHUNKS, LANES), h2_tiles.dtype),
                        pltpu.VMEM((rb, ROW_CHUNKS, LANES), h2_tiles.dtype),
                        pltpu.SemaphoreType.DMA((2,)), pltpu.SemaphoreType.DMA(())],
        compiler_params=_params("arbitrary"),
        name="dispatch",
    )(lpos, seg_cnt, seg_off, seg_dst, seg_cnt, seg_off, seg_dst, pad_len, pad_row, h2_tiles)


def _experts_body(blk_e_ref, used_ref, x_ref, wgu_ref, bg_ref, bu_ref, wd_ref, bd_ref, perm_ref, y_ref,
                  wg_sc, wu_sc, wd_sc, *, halves):
    f32, bf16 = jnp.float32, jnp.bfloat16
    step = pl.program_id(0)
    live = step < used_ref[0]
    rows = x_ref.shape[0] // ROW_CHUNKS

    @pl.when(live & ((step == 0) | (blk_e_ref[step] != blk_e_ref[jnp.maximum(step - 1, 0)])))
    def _():
        tile = perm_ref.shape[0]
        half = tile // 2
        for t in range(wgu_ref.shape[2] // tile):
            w = wgu_ref[0, :, t * tile:(t + 1) * tile].astype(bf16)
            o = jnp.dot(w, perm_ref[...], preferred_element_type=f32)
            wg_sc[:, t * half:(t + 1) * half] = o[:, :half].astype(bf16)
            wu_sc[:, t * half:(t + 1) * half] = o[:, half:].astype(bf16)
        wd_sc[...] = wd_ref[0].astype(bf16)

    @pl.when(live)
    def _():
        part = rows // halves
        for h in range(halves):
            xb = jnp.concatenate(
                [x_ref[pl.ds(h * part * ROW_CHUNKS + c, part, stride=ROW_CHUNKS), :]
                 for c in range(ROW_CHUNKS)], axis=1).astype(bf16)
            gate = jnp.dot(xb, wg_sc[...], preferred_element_type=f32) + bg_ref[0]
            up = jnp.dot(xb, wu_sc[...], preferred_element_type=f32) + bu_ref[0]
            gate = jnp.minimum(gate, SWIGLU_LIMIT)
            up = jnp.clip(up, -SWIGLU_LIMIT, SWIGLU_LIMIT)
            act = (up + 1.0) * (gate * jax.nn.sigmoid(gate * SWIGLU_ALPHA))
            y = jnp.dot(act.astype(bf16), wd_sc[...], preferred_element_type=f32) + bd_ref[0]
            for c in range(ROW_CHUNKS):
                y_ref[pl.ds(h * part * ROW_CHUNKS + c, part, stride=ROW_CHUNKS), :] = \
                    y[:, c * LANES:(c + 1) * LANES]

    @pl.when(jnp.logical_not(live))
    def _():
        y_ref[...] = jnp.zeros_like(y_ref)


def experts(blk_expert, n_used, xs_tiles, w_gu, b_g, b_u, w_d, b_d, *, halves=2):
    n_blocks = blk_expert.shape[0]
    rb = MOE_ROW_BLOCK
    d, ff = w_d.shape[2], w_d.shape[1]
    tile = 2 * LANES
    perm = np.zeros((tile, tile), np.float32)
    perm[2 * np.arange(LANES), np.arange(LANES)] = 1.0
    perm[2 * np.arange(LANES) + 1, LANES + np.arange(LANES)] = 1.0
    e_map3 = lambda i, be, nu: (be[i], 0, 0)
    grid_spec = pltpu.PrefetchScalarGridSpec(
        num_scalar_prefetch=2,
        grid=(n_blocks,),
        in_specs=[pl.BlockSpec((rb * ROW_CHUNKS, LANES), lambda i, be, nu: (i, 0)),
                  pl.BlockSpec((1, d, 2 * ff), e_map3),
                  pl.BlockSpec((1, 1, ff), e_map3), pl.BlockSpec((1, 1, ff), e_map3),
                  pl.BlockSpec((1, ff, d), e_map3), pl.BlockSpec((1, 1, d), e_map3),
                  pl.BlockSpec((tile, tile), lambda i, be, nu: (0, 0))],
        out_specs=pl.BlockSpec((rb * ROW_CHUNKS, LANES), lambda i, be, nu: (i, 0)),
        scratch_shapes=[pltpu.VMEM((d, ff), jnp.bfloat16), pltpu.VMEM((d, ff), jnp.bfloat16),
                        pltpu.VMEM((ff, d), jnp.bfloat16)],
    )
    return pl.pallas_call(
        functools.partial(_experts_body, halves=halves),
        grid_spec=grid_spec,
        out_shape=jax.ShapeDtypeStruct(xs_tiles.shape, jnp.float32),
        compiler_params=_params("arbitrary"),
        name="experts",
    )(blk_expert, n_used, xs_tiles, w_gu, b_g, b_u, w_d, b_d, jnp.asarray(perm, jnp.bfloat16))


def _combine_ple_body(lpos_ref, gw_ref, cnt_ref, off_ref, dst_ref, ncnt_ref, noff_ref, ndst_ref,
                      x1_ref, ys_ref, p_ref, gp_ref, wpg_ref, wpp_ref, *rest):
    cbuf, msum, sem = rest[-3:]
    rest = rest[:-3]
    gf_ref, o_ref = rest if len(rest) == 2 else (None, rest[0])
    f32, bf16 = jnp.float32, jnp.bfloat16
    tm = x1_ref.shape[0]
    step = pl.program_id(0)
    slot = step % 2

    def seg_copy(sl):
        return lambda local, glob, size: pltpu.make_async_copy(
            ys_ref.at[pl.ds(glob, size)], cbuf.at[sl, pl.ds(local, size)], sem.at[sl])

    cur, nxt = (cnt_ref, off_ref, dst_ref), (ncnt_ref, noff_ref, ndst_ref)

    @pl.when(step == 0)
    def _():
        _for_tile_segments(cur, tm, seg_copy(0), "start")

    @pl.when(step + 1 < pl.num_programs(0))
    def _():
        _for_tile_segments(nxt, tm, seg_copy(1 - slot), "start")

    _for_tile_segments(cur, tm, seg_copy(slot), "wait")

    def weighted_sum(t, carry):
        acc = gw_ref[0, t] * cbuf[slot, lpos_ref[0, t]]
        for kk in range(1, TOP_K):
            acc = acc + gw_ref[kk, t] * cbuf[slot, lpos_ref[kk, t]]
        msum[t] = acc
        return carry
    lax.fori_loop(0, tm, weighted_sum, 0, unroll=8)

    moe = jnp.concatenate([msum[:, c, :] for c in range(ROW_CHUNKS)], axis=1)
    x2 = x1_ref[...] + moe
    r = _rms(x2, gp_ref[...]).astype(bf16)
    gate = jax.nn.sigmoid(jnp.dot(r, wpg_ref[...], preferred_element_type=f32))
    emb = jnp.dot(p_ref[...].astype(bf16), wpp_ref[...], preferred_element_type=f32)
    x3 = x2 + gate * emb
    o_ref[...] = x3 if gf_ref is None else _rms(x3, gf_ref[...])


def combine_ple(lpos, gate_w, seg_cnt, seg_off, seg_dst, x1, ys_tiles, p2, g_ple, w_pg, w_pp, g_final,
                *, tm):
    n, d = x1.shape
    steps = n // tm
    smem = pltpu.SMEM
    full = lambda a: pl.BlockSpec(a.shape, lambda i: (0,) * a.ndim)
    seg = lambda shift: pl.BlockSpec((1, 1, N_EXPERTS),
                                     lambda i: (jnp.minimum(i + shift, steps - 1), 0, 0), memory_space=smem)
    tok = pl.BlockSpec((TOP_K, tm), lambda i: (0, i), memory_space=smem)
    consts = [g_ple.reshape(1, d), w_pg, w_pp] + ([] if g_final is None else [g_final.reshape(1, d)])
    return pl.pallas_call(
        _combine_ple_body,
        grid=(steps,),
        in_specs=[tok, tok, seg(0), seg(0), seg(0), seg(1), seg(1), seg(1),
                  pl.BlockSpec((tm, d), lambda i: (i, 0)),
                  pl.BlockSpec(memory_space=pl.ANY),
                  pl.BlockSpec((tm, p2.shape[1]), lambda i: (i, 0))] + [full(a) for a in consts],
        out_specs=pl.BlockSpec((tm, d), lambda i: (i, 0)),
        out_shape=jax.ShapeDtypeStruct((n, d), jnp.float32),
        scratch_shapes=[pltpu.VMEM((2, TOP_K * tm, ROW_CHUNKS, LANES), ys_tiles.dtype),
                        pltpu.VMEM((tm, ROW_CHUNKS, LANES), jnp.float32),
                        pltpu.SemaphoreType.DMA((2,))],
        compiler_params=_params("arbitrary"),
        name="combine_ple",
    )(lpos, gate_w, seg_cnt, seg_off, seg_dst, seg_cnt, seg_off, seg_dst, x1, ys_tiles, p2, *consts)


def _moe(h2_tiles, lpos, tile_cnt, w_gu, b_gu, w_d, b_d, *, tm):
    n = h2_tiles.shape[0] // ROW_CHUNKS
    rb = MOE_ROW_BLOCK
    i32 = jnp.int32
    n_blocks = -(-n * TOP_K // rb) + N_EXPERTS
    n_rows = n_blocks * rb
    counts = jnp.sum(tile_cnt, axis=0)
    padded = (counts + rb - 1) // rb * rb
    pend = jnp.cumsum(padded)
    pstart = pend - padded
    blk_start = jnp.arange(n_blocks, dtype=i32) * rb
    blk_expert = jnp.minimum(jnp.sum((pend[None, :] <= blk_start[:, None]).astype(i32), axis=1),
                             N_EXPERTS - 1)
    n_used = pend[-1:] // rb
    seg_cnt = tile_cnt[:, None, :]
    seg_off = (jnp.cumsum(tile_cnt, axis=1) - tile_cnt)[:, None, :]
    seg_dst = (pstart[None, :] + jnp.cumsum(tile_cnt, axis=0) - tile_cnt)[:, None, :]
    tail = n_used + jnp.arange(N_EXPERTS, dtype=i32)
    pad_len = jnp.concatenate([padded - counts, jnp.where(tail < n_blocks, rb, 0)])[None, :]
    pad_row = jnp.concatenate([pstart + counts, tail * rb])[None, :]
    b_g, b_u = b_gu[:, None, 0::2], b_gu[:, None, 1::2]
    xs = dispatch(lpos, seg_cnt, seg_off, seg_dst, pad_len.astype(i32), pad_row.astype(i32),
                  h2_tiles.reshape(n, ROW_CHUNKS, LANES), n_rows, tc=tm, rb=rb)
    ys = experts(blk_expert, n_used.astype(i32), xs.reshape(n_rows * ROW_CHUNKS, LANES),
                 w_gu, b_g, b_u, w_d, b_d[:, None, :])
    return (seg_cnt, seg_off, seg_dst), ys.reshape(n_rows, ROW_CHUNKS, LANES)


def kernel(x, p, norm_mix_g, w_in, conv_w, conv_b, w_qk_m, w_if, b_if, mnorm_g, w_branch, w_out, norm_ffn_g, w_router, b_router, w_gate_up, b_gate_up, w_down, b_down, norm_ple_g, w_ple_gate, w_ple_proj, final_norm_g):
    B, S, D = x.shape
    depth = w_in.shape[0]
    bf16 = jnp.bfloat16
    x2 = x.reshape(B * S, D)
    for i in range(depth):
        qkv, xo, gates = in_proj(x2, norm_mix_g[i], w_in[i].astype(bf16))
        y_a = moba(qkv, B, S)
        y_m = mlstm(xo, conv_w[i], conv_b[i], w_qk_m[i], w_if[i], b_if[i], mnorm_g[i], B, S)
        x1, h2_tiles, gate_w, lpos, tile_cnt = merge_route(
            x2, y_a, y_m, gates, w_branch[i, 0].astype(bf16), w_branch[i, 1].astype(bf16),
            w_out[i].astype(bf16), norm_ffn_g[i], w_router[i], b_router[i], tm=MOE_TOKEN_TILE)
        segs, ys = _moe(h2_tiles, lpos, tile_cnt[:, :, 0], w_gate_up[i], b_gate_up[i],
                        w_down[i], b_down[i], tm=MOE_TOKEN_TILE)
        x2 = combine_ple(lpos, gate_w, *segs, x1, ys, p[i].reshape(B * S, -1), norm_ple_g[i],
                         w_ple_gate[i].astype(bf16), w_ple_proj[i].astype(bf16),
                         final_norm_g if i == depth - 1 else None, tm=MOE_TOKEN_TILE)
    return x2.reshape(B, S, D)
```

```python
import functools

import jax
import jax.numpy as jnp
import numpy as np
from jax import lax
from jax.experimental import pallas as pl
from jax.experimental.pallas import tpu as pltpu

RMS_EPS = 1e-6
LANES = 128
ROW_CHUNKS = 8
VMEM_LIMIT = 56 * 1024 * 1024

N_ATT_HEADS = 8
ATT_HEAD_DIM = 64
ATT_WIDTH = N_ATT_HEADS * ATT_HEAD_DIM
MOBA_BLOCK = 256
MOBA_TOPK = 3

N_MLSTM_HEADS = 4
MLSTM_WIDTH = 512
MLSTM_V_DIM = 128
MLSTM_QK_DIM = 64
MLSTM_CONV = 4
MLSTM_CHUNK = 128

N_EXPERTS = 32
TOP_K = 4
SWIGLU_ALPHA = 1.702
SWIGLU_LIMIT = 7.0
MOE_ROW_BLOCK = 512
MOE_TOKEN_TILE = 512

NEG_BIG = -1e30

_NT = (((1,), (1,)), ((), ()))
_TN = (((0,), (0,)), ((), ()))


def _params(*sem):
    return pltpu.CompilerParams(dimension_semantics=sem, vmem_limit_bytes=VMEM_LIMIT)


def _rms(x, g):
    return x * lax.rsqrt(jnp.mean(x * x, axis=-1, keepdims=True) + RMS_EPS) * g


def _in_proj_body(x_ref, g_ref, w_ref, qkv_ref, xo_ref, gate_ref, *, col_chunk):
    h = _rms(x_ref[...], g_ref[...]).astype(jnp.bfloat16)
    col = 0
    for out_ref in (qkv_ref, xo_ref, gate_ref):
        for c in range(0, out_ref.shape[1], col_chunk):
            out_ref[:, c:c + col_chunk] = jnp.dot(
                h, w_ref[:, col + c:col + c + col_chunk],
                preferred_element_type=jnp.float32).astype(out_ref.dtype)
        col += out_ref.shape[1]


def in_proj(x2, g, w_bf16, *, tm=512, col_chunk=512):
    n, d = x2.shape
    widths = (3 * ATT_WIDTH, 2 * MLSTM_WIDTH, 2 * d)
    assert sum(widths) == w_bf16.shape[1] and n % tm == 0
    return pl.pallas_call(
        functools.partial(_in_proj_body, col_chunk=col_chunk),
        grid=(n // tm,),
        in_specs=[pl.BlockSpec((tm, d), lambda i: (i, 0)),
                  pl.BlockSpec((1, d), lambda i: (0, 0)),
                  pl.BlockSpec(w_bf16.shape, lambda i: (0, 0))],
        out_specs=[pl.BlockSpec((tm, w), lambda i: (i, 0)) for w in widths],
        out_shape=[jax.ShapeDtypeStruct((n, w), jnp.bfloat16) for w in widths],
        compiler_params=_params("parallel"),
        name="in_proj",
    )(x2, g.reshape(1, d), w_bf16)


def _moba_body(q_ref, k_ref, v_ref, o_ref, *, pair, seq):
    blk = MOBA_BLOCK
    n_blk = seq // blk
    f32, bf16 = jnp.float32, jnp.bfloat16
    lane = lax.broadcasted_iota(jnp.int32, (1, LANES), 1)
    row = lax.broadcasted_iota(jnp.int32, (blk, blk), 0)
    col = lax.broadcasted_iota(jnp.int32, (blk, blk), 1)
    rel = (row - col).astype(f32)
    causal = row >= col
    log2e = float(np.log2(np.e))
    scale2 = ATT_HEAD_DIM ** -0.5 * log2e
    lane_row = lax.broadcasted_iota(jnp.int32, (LANES, LANES), 0)
    spread = [jnp.where(lane_row == j, 1.0, 0.0).astype(bf16) for j in range(n_blk)]

    k_all = k_ref[...]
    v_all = v_ref[...]
    k_mean = jnp.concatenate(
        [jnp.mean(k_all[j * blk:(j + 1) * blk].astype(f32), axis=0, keepdims=True)
         for j in range(n_blk)] + [jnp.zeros((LANES - n_blk, LANES), f32)], axis=0)

    outs = []
    for hh in range(2):
        head = 2 * pair + hh
        slope2 = jnp.exp2(-8.0 * (head + 1).astype(f32) / N_ATT_HEADS) * log2e
        bias = [slope2 * (rel + float(d * blk)) for d in range(n_blk)]
        head_lanes = (lane >= hh * ATT_HEAD_DIM) & (lane < (hh + 1) * ATT_HEAD_DIM)
        q_all = q_ref[...]
        q_all = jnp.where(head_lanes, q_all, jnp.zeros_like(q_all))
        if n_blk > MOBA_TOPK + 1:
            scores = lax.dot_general(q_all.astype(f32), jnp.where(head_lanes, k_mean, 0.0), _NT,
                                     precision=lax.Precision.HIGHEST, preferred_element_type=f32)
        head_out = []
        for qi in range(n_blk):
            q_h = q_all[qi * blk:(qi + 1) * blk]
            sel = None
            if qi > MOBA_TOPK:
                g = jnp.where(lane < qi, scores[qi * blk:(qi + 1) * blk], -jnp.inf)
                ahead = jnp.zeros((blk, LANES), f32)
                for d in range(1, qi):
                    ahead = ahead + jnp.where(pltpu.roll(g, d, axis=1) >= g, 1.0, 0.0)
                    ahead = ahead + jnp.where(pltpu.roll(g, LANES - d, axis=1) > g, 1.0, 0.0)
                keep = jnp.where(ahead < MOBA_TOPK, 1.0, 0.0).astype(bf16)
                sel = [jnp.dot(keep, spread[j], preferred_element_type=f32) > 0.5 for j in range(qi)]
            logits = []
            for j in range(qi + 1):
                s = lax.dot_general(q_h, k_all[j * blk:(j + 1) * blk], _NT, preferred_element_type=f32)
                s = s * scale2 - bias[qi - j]
                if j == qi:
                    s = jnp.where(causal, s, NEG_BIG)
                elif sel is not None:
                    s = jnp.where(jnp.concatenate([sel[j]] * (blk // LANES), axis=1), s, NEG_BIG)
                logits.append(s)
            s_all = jnp.concatenate(logits, axis=1)
            m = jnp.max(s_all, axis=1, keepdims=True)
            p = jnp.exp2(s_all - m)
            denom = jnp.sum(p, axis=1, keepdims=True)
            o = jnp.dot(p.astype(jnp.bfloat16), v_all[:(qi + 1) * blk],
                        preferred_element_type=jnp.float32)
            head_out.append(o / denom)
        outs.append((head_lanes, head_out))
    for qi in range(n_blk):
        o = jnp.where(outs[0][0], outs[0][1][qi], outs[1][1][qi])
        o_ref[qi * blk:(qi + 1) * blk, :] = o.astype(o_ref.dtype)


def _moba_kernel(q_ref, k_ref, v_ref, o_ref, *, seq):
    _moba_body(q_ref, k_ref, v_ref, o_ref, pair=pl.program_id(1), seq=seq)


def moba(qkv, batch, seq):
    n = batch * seq
    n_pairs = ATT_WIDTH // LANES
    assert seq % MOBA_BLOCK == 0
    spec = lambda off: pl.BlockSpec((seq, LANES), lambda b, p: (b, off + p))
    return pl.pallas_call(
        functools.partial(_moba_kernel, seq=seq),
        grid=(batch, n_pairs),
        in_specs=[spec(0), spec(n_pairs), spec(2 * n_pairs)],
        out_specs=pl.BlockSpec((seq, LANES), lambda b, p: (b, p)),
        out_shape=jax.ShapeDtypeStruct((n, ATT_WIDTH), jnp.bfloat16),
        compiler_params=_params("parallel", "parallel"),
        name="moba",
    )(qkv, qkv, qkv)


def _log_sigmoid(z):
    return jnp.minimum(z, 0.0) - jnp.log(1.0 + jnp.exp(-jnp.abs(z)))


def _mlstm_body(xm_ref, om_ref, cw_ref, cb_ref, wq_ref, wk_ref, wiq_ref, wik_ref, wix_ref,
                wiqt_ref, wikt_ref, wixt_ref, bi_ref, bit_ref, g_ref, y_ref,
                q_sc, k_sc, gcol_sc, grow_sc, ct_sc, n_sc, m_sc, *, seq):
    L = MLSTM_CHUNK
    H = N_MLSTM_HEADS
    n_chunks = seq // L
    f32, bf16 = jnp.float32, jnp.bfloat16

    x = xm_ref[...].astype(f32)
    t_idx = lax.broadcasted_iota(jnp.int32, (seq, 1), 0)
    acc = x * cw_ref[MLSTM_CONV - 1:MLSTM_CONV, :] + cb_ref[...]
    for d in range(1, MLSTM_CONV):
        shifted = jnp.where(t_idx >= d, pltpu.roll(x, d, axis=0), 0.0)
        acc = acc + shifted * cw_ref[MLSTM_CONV - 1 - d:MLSTM_CONV - d, :]
    xc = (acc * jax.nn.sigmoid(acc)).astype(bf16)

    xm = xm_ref[...]
    if_col = jnp.dot(xm, wix_ref[...], preferred_element_type=f32) + bi_ref[...]
    if_row = lax.dot_general(wixt_ref[...], xm, _NT, preferred_element_type=f32) + bit_ref[...]
    for h in range(H):
        xch = xc[:, h * LANES:(h + 1) * LANES]
        q = jnp.dot(xch, wq_ref[h], preferred_element_type=f32).astype(bf16)
        k = jnp.dot(xch, wk_ref[h], preferred_element_type=f32).astype(bf16)
        if_col = if_col + jnp.dot(q, wiq_ref[h], preferred_element_type=f32)
        if_col = if_col + jnp.dot(k, wik_ref[h], preferred_element_type=f32)
        if_row = if_row + lax.dot_general(wiqt_ref[h], q, _NT, preferred_element_type=f32)
        if_row = if_row + lax.dot_general(wikt_ref[h], k, _NT, preferred_element_type=f32)
        q_sc[h] = q
        k_sc[h] = (k.astype(f32) * (MLSTM_QK_DIM ** -0.5)).astype(bf16)
    is_f_col = lax.broadcasted_iota(jnp.int32, (1, 2 * H), 1) >= H
    is_f_row = lax.broadcasted_iota(jnp.int32, (2 * H, 1), 0) >= H
    gcol_sc[...] = jnp.where(is_f_col, _log_sigmoid(if_col), if_col)
    g_row = jnp.where(is_f_row, _log_sigmoid(if_row), if_row)
    for c in range(n_chunks):
        grow_sc[c] = g_row[:, c * L:(c + 1) * L]

    ct_sc[...] = jnp.zeros_like(ct_sc)
    n_sc[...] = jnp.zeros_like(n_sc)
    m_sc[...] = jnp.zeros_like(m_sc)

    r_i = lax.broadcasted_iota(jnp.int32, (L, L), 0)
    c_i = lax.broadcasted_iota(jnp.int32, (L, L), 1)
    causal = r_i >= c_i
    tri_lo = jnp.where(causal, 1.0, 0.0).astype(f32)
    tri_up = jnp.where(c_i >= r_i, 1.0, 0.0).astype(f32)

    def chunk(c, carry):
        r0 = pl.multiple_of(c * L, L)
        gc = gcol_sc[pl.ds(r0, L), :]
        gr = grow_sc[c]
        b_cols = jnp.dot(tri_lo, gc, precision=lax.Precision.HIGHEST, preferred_element_type=f32)
        b_rows = jnp.dot(gr, tri_up, precision=lax.Precision.HIGHEST, preferred_element_type=f32)
        for h in range(H):
            hs = slice(h * LANES, (h + 1) * LANES)
            b_col = b_cols[:, H + h:H + h + 1]
            i_col = gc[:, h:h + 1]
            b_row = b_rows[H + h:H + h + 1, :]
            i_row = gr[h:h + 1, :]
            b_last = b_col[L - 1:L, :]
            m_prev = m_sc[h][:, 0:1]
            q = q_sc[h, pl.ds(r0, L), :]
            k = k_sc[h, pl.ds(r0, L), :]
            v = xm_ref[pl.ds(r0, L), hs]

            log_d = jnp.where(causal, b_col + (i_row - b_row), NEG_BIG)
            inter = b_col + m_prev
            m_t = jnp.maximum(inter, jnp.max(log_d, axis=1, keepdims=True))
            w_inter = jnp.exp(inter - m_t)
            s = lax.dot_general(q, k, _NT, preferred_element_type=f32) * jnp.exp(log_d - m_t)
            num = jnp.dot(s.astype(bf16), v, preferred_element_type=f32)
            num = num + w_inter * jnp.dot(q, ct_sc[h].astype(bf16), preferred_element_type=f32)
            den = jnp.sum(s, axis=1, keepdims=True)
            den = den + w_inter * jnp.sum(q.astype(f32) * n_sc[h], axis=1, keepdims=True)
            hv = num / jnp.maximum(jnp.abs(den), jnp.exp(-m_t))
            hv = hv * lax.rsqrt(jnp.mean(hv * hv, axis=1, keepdims=True) + RMS_EPS) * g_ref[:, hs]
            o_gate = jax.nn.sigmoid(om_ref[pl.ds(r0, L), hs].astype(f32))
            y_ref[pl.ds(r0, L), hs] = (o_gate * hv).astype(y_ref.dtype)

            g_col = b_last - b_col + i_col
            m_new = jnp.maximum(b_last + m_prev, jnp.max(g_col, axis=0, keepdims=True))
            wg = jnp.exp(g_col - m_new)
            decay = jnp.exp(b_last + m_prev - m_new)
            wv = (wg * v.astype(f32)).astype(bf16)
            ct_sc[h] = decay * ct_sc[h] + lax.dot_general(k, wv, _TN, preferred_element_type=f32)
            n_sc[h] = decay * n_sc[h] + jnp.sum(wg * k.astype(f32), axis=0, keepdims=True)
            m_sc[h] = jnp.broadcast_to(m_new, (1, LANES))
        return carry

    lax.fori_loop(0, n_chunks, chunk, 0)


def _mlstm_kernel(*refs, seq):
    _mlstm_body(*refs, seq=seq)


def mlstm(xo, conv_w, conv_b, w_qk, w_if, b_if, g, batch, seq):
    n = batch * seq
    H, dv, dk = N_MLSTM_HEADS, MLSTM_V_DIM, MLSTM_QK_DIM
    assert seq % MLSTM_CHUNK == 0 and dv == LANES
    bf16 = jnp.bfloat16
    pad = ((0, 0), (0, 0), (0, LANES - dk))
    wq = jnp.pad(w_qk[:, :, :dk], pad).astype(bf16)
    wk = jnp.pad(w_qk[:, :, dk:], pad).astype(bf16)
    rpad = ((0, 0), (0, LANES - dk), (0, 0))
    wiq = jnp.pad(w_if[:H * dk].reshape(H, dk, 2 * H), rpad).astype(bf16)
    wik = jnp.pad(w_if[H * dk:2 * H * dk].reshape(H, dk, 2 * H), rpad).astype(bf16)
    wix = w_if[2 * H * dk:].astype(bf16)
    wiqt, wikt, wixt = wiq.transpose(0, 2, 1), wik.transpose(0, 2, 1), wix.T
    full = lambda a: pl.BlockSpec(a.shape, lambda b: (0,) * a.ndim)
    consts = [conv_w, conv_b.reshape(1, -1), wq, wk, wiq, wik, wix, wiqt, wikt, wixt,
              b_if.reshape(1, -1), b_if.reshape(-1, 1), g.reshape(1, -1)]
    return pl.pallas_call(
        functools.partial(_mlstm_kernel, seq=seq),
        grid=(batch,),
        in_specs=[pl.BlockSpec((seq, MLSTM_WIDTH), lambda b: (b, 0)),
                  pl.BlockSpec((seq, MLSTM_WIDTH), lambda b: (b, 1))] + [full(a) for a in consts],
        out_specs=pl.BlockSpec((seq, MLSTM_WIDTH), lambda b: (b, 0)),
        out_shape=jax.ShapeDtypeStruct((n, MLSTM_WIDTH), bf16),
        scratch_shapes=[pltpu.VMEM((H, seq, LANES), bf16), pltpu.VMEM((H, seq, LANES), bf16),
                        pltpu.VMEM((seq, 2 * H), jnp.float32),
                        pltpu.VMEM((seq // MLSTM_CHUNK, 2 * H, MLSTM_CHUNK), jnp.float32),
                        pltpu.VMEM((H, LANES, LANES), jnp.float32),
                        pltpu.VMEM((H, 1, LANES), jnp.float32),
                        pltpu.VMEM((H, 1, LANES), jnp.float32)],
        compiler_params=_params("parallel"),
        name="mlstm",
    )(xo, xo, *consts)


def _store_rows_as_tiles(dst_ref, val):
    rows = val.shape[0]
    for c in range(ROW_CHUNKS):
        dst_ref[pl.ds(c, rows, stride=ROW_CHUNKS), :] = val[:, c * LANES:(c + 1) * LANES]


def _load_tiles_as_rows(src_ref, rows, lead=()):
    return jnp.concatenate(
        [src_ref[lead + (pl.ds(c, rows, stride=ROW_CHUNKS), slice(None))] for c in range(ROW_CHUNKS)],
        axis=1)


def _merge_route_body(x_ref, ya_ref, ym_ref, ga_ref, gm_ref, wb0_ref, wb1_ref, wo_ref, g_ref,
                      wr_ref, br_ref, x1_ref, h2_ref, gate_ref, lpos_ref, cnt_ref):
    f32, bf16 = jnp.float32, jnp.bfloat16
    tm = x_ref.shape[0]

    u = jax.nn.sigmoid(ga_ref[...].astype(f32)) * jnp.dot(ya_ref[...], wb0_ref[...], preferred_element_type=f32)
    u = u + jax.nn.sigmoid(gm_ref[...].astype(f32)) * jnp.dot(ym_ref[...], wb1_ref[...], preferred_element_type=f32)
    x1 = x_ref[...] + jnp.dot(u.astype(bf16), wo_ref[...], preferred_element_type=f32)
    x1_ref[...] = x1
    h2 = _rms(x1, g_ref[...])
    _store_rows_as_tiles(h2_ref, h2)

    logits = jnp.dot(h2, wr_ref[...], precision=lax.Precision.HIGHEST,
                     preferred_element_type=f32) + br_ref[...]
    logits = jnp.transpose(logits)[:N_EXPERTS]
    e_id = lax.broadcasted_iota(jnp.int32, logits.shape, 0).astype(f32)
    chosen = jnp.zeros(logits.shape, f32)
    vals, ids = [], []
    for _ in range(TOP_K):
        top = jnp.max(logits, axis=0, keepdims=True)
        first = jnp.min(jnp.where(logits == top, e_id, float(N_EXPERTS)), axis=0, keepdims=True)
        hit = e_id == first
        chosen = jnp.where(hit, 1.0, chosen)
        logits = jnp.where(hit, -jnp.inf, logits)
        vals.append(top)
        ids.append(first)
    ex = [jnp.exp(v - vals[0]) for v in vals]
    total = ex[0] + ex[1] + ex[2] + ex[3]

    chosen_b = chosen.astype(bf16)
    t_r = lax.broadcasted_iota(jnp.int32, (tm, tm), 0)
    t_c = lax.broadcasted_iota(jnp.int32, (tm, tm), 1)
    earlier = jnp.where(t_r < t_c, 1.0, 0.0).astype(bf16)
    before = jnp.dot(chosen_b, earlier, preferred_element_type=f32)
    e_r = lax.broadcasted_iota(jnp.int32, (N_EXPERTS, N_EXPERTS), 0)
    e_c = lax.broadcasted_iota(jnp.int32, (N_EXPERTS, N_EXPERTS), 1)
    lower = jnp.where(e_c < e_r, 1.0, 0.0).astype(bf16)
    seg_off = jnp.sum(jnp.dot(lower, chosen_b, preferred_element_type=f32), axis=1, keepdims=True)
    pos = before + seg_off
    for kk in range(TOP_K):
        gate_ref[kk:kk + 1, :] = ex[kk] / total
        lpos_ref[kk:kk + 1, :] = jnp.sum(jnp.where(e_id == ids[kk], pos, 0.0), axis=0,
                                         keepdims=True).astype(jnp.int32)
    counts = jnp.sum(chosen, axis=1, keepdims=True)
    cnt_ref[0] = jnp.broadcast_to(counts, (N_EXPERTS, LANES)).astype(jnp.int32)


def merge_route(x2, ya, ym, gates, wb0, wb1, wo, g, w_router, b_router, *, tm):
    n, d = x2.shape
    e = w_router.shape[1]
    assert n % tm == 0 and d == ROW_CHUNKS * LANES and e == N_EXPERTS
    full = lambda a: pl.BlockSpec(a.shape, lambda i: (0,) * a.ndim)
    lane_pad = ((0, 0), (0, LANES - e))
    consts = [wb0, wb1, wo, g.reshape(1, d), jnp.pad(w_router, lane_pad),
              jnp.pad(b_router.reshape(1, e), lane_pad)]
    tok = lambda dt: jax.ShapeDtypeStruct((TOP_K, n), dt)
    return pl.pallas_call(
        _merge_route_body,
        grid=(n // tm,),
        in_specs=[pl.BlockSpec((tm, d), lambda i: (i, 0)),
                  pl.BlockSpec((tm, ATT_WIDTH), lambda i: (i, 0)),
                  pl.BlockSpec((tm, MLSTM_WIDTH), lambda i: (i, 0)),
                  pl.BlockSpec((tm, d), lambda i: (i, 0)),
                  pl.BlockSpec((tm, d), lambda i: (i, 1))] + [full(a) for a in consts],
        out_specs=[pl.BlockSpec((tm, d), lambda i: (i, 0)),
                   pl.BlockSpec((tm * ROW_CHUNKS, LANES), lambda i: (i, 0)),
                   pl.BlockSpec((TOP_K, tm), lambda i: (0, i)),
                   pl.BlockSpec((TOP_K, tm), lambda i: (0, i)),
                   pl.BlockSpec((1, e, LANES), lambda i: (i, 0, 0))],
        out_shape=[jax.ShapeDtypeStruct((n, d), jnp.float32),
                   jax.ShapeDtypeStruct((n * ROW_CHUNKS, LANES), jnp.float32),
                   tok(jnp.float32), tok(jnp.int32),
                   jax.ShapeDtypeStruct((n // tm, e, LANES), jnp.int32)],
        compiler_params=_params("parallel"),
        name="merge_route",
    )(x2, ya, ym, gates, gates, *consts)


def _segment_pieces(cnt, max_rows, fn):
    done = 0
    size = 1 << (max_rows.bit_length() - 1)
    while size:
        bit = cnt & size

        def piece(done=done, size=size):
            fn(done, size)
        pl.when(bit != 0)(piece)
        done = done + bit
        size >>= 1


def _for_tile_segments(meta, tile_rows, make_copy, action):
    cnt_ref, off_ref, dst_ref = meta

    def per_expert(e, carry):
        off, dst = off_ref[0, 0, e], dst_ref[0, 0, e]
        _segment_pieces(cnt_ref[0, 0, e], tile_rows,
                        lambda done, size: getattr(make_copy(off + done, dst + done, size), action)())
        return carry
    lax.fori_loop(0, N_EXPERTS, per_expert, 0)


def _dispatch_body(lpos_ref, cnt_ref, off_ref, dst_ref, pcnt_ref, poff_ref, pdst_ref,
                   pad_len_ref, pad_row_ref, x_ref, out_ref, cbuf, zbuf, sem, zsem, *, tc, rb):
    step = pl.program_id(0)
    last = pl.num_programs(0) - 1
    slot = step % 2

    def compact(t, carry):
        row = x_ref[t]
        for kk in range(TOP_K):
            cbuf[slot, lpos_ref[kk, t]] = row
        return carry
    lax.fori_loop(0, tc, compact, 0, unroll=8)

    def seg_copy(sl):
        return lambda local, glob, size: pltpu.make_async_copy(
            cbuf.at[sl, pl.ds(local, size)], out_ref.at[pl.ds(glob, size)], sem.at[sl])

    cur, prev = (cnt_ref, off_ref, dst_ref), (pcnt_ref, poff_ref, pdst_ref)
    _for_tile_segments(cur, tc, seg_copy(slot), "start")

    @pl.when(step > 0)
    def _():
        _for_tile_segments(prev, tc, seg_copy(1 - slot), "wait")

    @pl.when(step == last)
    def _():
        _for_tile_segments(cur, tc, seg_copy(slot), "wait")
        zbuf[...] = jnp.zeros_like(zbuf)

        def pads(action):
            def per_pad(j, carry):
                row = pad_row_ref[0, j]
                _segment_pieces(pad_len_ref[0, j], rb, lambda done, size: getattr(
                    pltpu.make_async_copy(zbuf.at[pl.ds(0, size)], out_ref.at[pl.ds(row + done, size)], zsem),
                    action)())
                return carry
            lax.fori_loop(0, pad_len_ref.shape[1], per_pad, 0)
        pads("start")
        pads("wait")


def dispatch(lpos, seg_cnt, seg_off, seg_dst, pad_len, pad_row, h2_tiles, n_rows, *, tc, rb):
    n = h2_tiles.shape[0]
    steps = n // tc
    assert n % tc == 0
    smem = pltpu.SMEM
    seg = lambda shift: pl.BlockSpec((1, 1, N_EXPERTS), lambda i: (jnp.maximum(i - shift, 0), 0, 0),
                                     memory_space=smem)
    whole = lambda a: pl.BlockSpec(a.shape, lambda i: (0,) * a.ndim, memory_space=smem)
    return pl.pallas_call(
        functools.partial(_dispatch_body, tc=tc, rb=rb),
        grid=(steps,),
        in_specs=[pl.BlockSpec((TOP_K, tc), lambda i: (0, i), memory_space=smem),
                  seg(0), seg(0), seg(0), seg(1), seg(1), seg(1), whole(pad_len), whole(pad_row),
                  pl.BlockSpec((tc, ROW_CHUNKS, LANES), lambda i: (i, 0, 0))],
        out_specs=pl.BlockSpec(memory_space=pl.ANY),
        out_shape=jax.ShapeDtypeStruct((n_rows, ROW_CHUNKS, LANES), h2_tiles.dtype),
        scratch_shapes=[pltpu.VMEM((2, TOP_K * tc, ROW_CHUNKS, LANES), h2_tiles.dtype),
                        pltpu.VMEM((rb, ROW_CHUNKS, LANES), h2_tiles.dtype),
                        pltpu.SemaphoreType.DMA((2,)), pltpu.SemaphoreType.DMA(())],
        compiler_params=_params("arbitrary"),
        name="dispatch",
    )(lpos, seg_cnt, seg_off, seg_dst, seg_cnt, seg_off, seg_dst, pad_len, pad_row, h2_tiles)


def _experts_body(blk_e_ref, nxt_e_ref, used_ref, x_ref, wgu_hbm, bg_ref, bu_ref, wd_hbm, bd_ref, perm_ref,
                  y_ref, wgu_st, wd_st, wg_sc, wu_sc, wd_sc, sem, *, halves):
    f32, bf16 = jnp.float32, jnp.bfloat16
    step = pl.program_id(0)
    live = step < used_ref[0]
    rows = x_ref.shape[0] // ROW_CHUNKS
    expert = blk_e_ref[step]

    def fetch(e, action):
        getattr(pltpu.make_async_copy(wgu_hbm.at[e], wgu_st, sem.at[0]), action)()
        getattr(pltpu.make_async_copy(wd_hbm.at[e], wd_st, sem.at[1]), action)()

    @pl.when(live & ((step == 0) | (expert != blk_e_ref[jnp.maximum(step - 1, 0)])))
    def _():
        @pl.when(step == 0)
        def _():
            fetch(expert, "start")
        fetch(expert, "wait")
        tile = perm_ref.shape[0]
        half = tile // 2
        for t in range(wgu_st.shape[1] // tile):
            w = wgu_st[:, t * tile:(t + 1) * tile].astype(bf16)
            o = jnp.dot(w, perm_ref[...], preferred_element_type=f32)
            wg_sc[:, t * half:(t + 1) * half] = o[:, :half].astype(bf16)
            wu_sc[:, t * half:(t + 1) * half] = o[:, half:].astype(bf16)
        wd_sc[...] = wd_st[...].astype(bf16)
        nxt = nxt_e_ref[step]

        @pl.when(nxt >= 0)
        def _():
            fetch(nxt, "start")

    @pl.when(live)
    def _():
        part = rows // halves
        for h in range(halves):
            xb = jnp.concatenate(
                [x_ref[pl.ds(h * part * ROW_CHUNKS + c, part, stride=ROW_CHUNKS), :]
                 for c in range(ROW_CHUNKS)], axis=1).astype(bf16)
            gate = jnp.dot(xb, wg_sc[...], preferred_element_type=f32) + bg_ref[0]
            up = jnp.dot(xb, wu_sc[...], preferred_element_type=f32) + bu_ref[0]
            gate = jnp.minimum(gate, SWIGLU_LIMIT)
            up = jnp.clip(up, -SWIGLU_LIMIT, SWIGLU_LIMIT)
            act = (up + 1.0) * (gate * jax.nn.sigmoid(gate * SWIGLU_ALPHA))
            y = jnp.dot(act.astype(bf16), wd_sc[...], preferred_element_type=f32) + bd_ref[0]
            for c in range(ROW_CHUNKS):
                y_ref[pl.ds(h * part * ROW_CHUNKS + c, part, stride=ROW_CHUNKS), :] = \
                    y[:, c * LANES:(c + 1) * LANES]

    @pl.when(jnp.logical_not(live))
    def _():
        y_ref[...] = jnp.zeros_like(y_ref)


def experts(blk_expert, nxt_expert, n_used, xs_tiles, w_gu, b_g, b_u, w_d, b_d, *, halves=2):
    n_blocks = blk_expert.shape[0]
    rb = MOE_ROW_BLOCK
    d, ff = w_d.shape[2], w_d.shape[1]
    tile = 2 * LANES
    perm = np.zeros((tile, tile), np.float32)
    perm[2 * np.arange(LANES), np.arange(LANES)] = 1.0
    perm[2 * np.arange(LANES) + 1, LANES + np.arange(LANES)] = 1.0
    e_map3 = lambda i, be, nx, nu: (be[i], 0, 0)
    rows_map = lambda i, be, nx, nu: (i, 0)
    grid_spec = pltpu.PrefetchScalarGridSpec(
        num_scalar_prefetch=3,
        grid=(n_blocks,),
        in_specs=[pl.BlockSpec((rb * ROW_CHUNKS, LANES), rows_map),
                  pl.BlockSpec(memory_space=pl.ANY),
                  pl.BlockSpec((1, 1, ff), e_map3), pl.BlockSpec((1, 1, ff), e_map3),
                  pl.BlockSpec(memory_space=pl.ANY), pl.BlockSpec((1, 1, d), e_map3),
                  pl.BlockSpec((tile, tile), lambda i, be, nx, nu: (0, 0))],
        out_specs=pl.BlockSpec((rb * ROW_CHUNKS, LANES), rows_map),
        scratch_shapes=[pltpu.VMEM((d, 2 * ff), w_gu.dtype), pltpu.VMEM((ff, d), w_d.dtype),
                        pltpu.VMEM((d, ff), jnp.bfloat16), pltpu.VMEM((d, ff), jnp.bfloat16),
                        pltpu.VMEM((ff, d), jnp.bfloat16), pltpu.SemaphoreType.DMA((2,))],
    )
    return pl.pallas_call(
        functools.partial(_experts_body, halves=halves),
        grid_spec=grid_spec,
        out_shape=jax.ShapeDtypeStruct(xs_tiles.shape, jnp.float32),
        compiler_params=_params("arbitrary"),
        name="experts",
    )(blk_expert, nxt_expert, n_used, xs_tiles, w_gu, b_g, b_u, w_d, b_d, jnp.asarray(perm, jnp.bfloat16))


def _combine_ple_body(lpos_ref, gw_ref, cnt_ref, off_ref, dst_ref, ncnt_ref, noff_ref, ndst_ref,
                      x1_ref, ys_ref, p_ref, gp_ref, wpg_ref, wpp_ref, *rest):
    cbuf, msum, sem = rest[-3:]
    rest = rest[:-3]
    gf_ref, o_ref = rest if len(rest) == 2 else (None, rest[0])
    f32, bf16 = jnp.float32, jnp.bfloat16
    tm = x1_ref.shape[0]
    step = pl.program_id(0)
    slot = step % 2

    def seg_copy(sl):
        return lambda local, glob, size: pltpu.make_async_copy(
            ys_ref.at[pl.ds(glob, size)], cbuf.at[sl, pl.ds(local, size)], sem.at[sl])

    cur, nxt = (cnt_ref, off_ref, dst_ref), (ncnt_ref, noff_ref, ndst_ref)

    @pl.when(step == 0)
    def _():
        _for_tile_segments(cur, tm, seg_copy(0), "start")

    @pl.when(step + 1 < pl.num_programs(0))
    def _():
        _for_tile_segments(nxt, tm, seg_copy(1 - slot), "start")

    _for_tile_segments(cur, tm, seg_copy(slot), "wait")

    def weighted_sum(t, carry):
        acc = gw_ref[0, t] * cbuf[slot, lpos_ref[0, t]]
        for kk in range(1, TOP_K):
            acc = acc + gw_ref[kk, t] * cbuf[slot, lpos_ref[kk, t]]
        msum[t] = acc
        return carry
    lax.fori_loop(0, tm, weighted_sum, 0, unroll=8)

    moe = jnp.concatenate([msum[:, c, :] for c in range(ROW_CHUNKS)], axis=1)
    x2 = x1_ref[...] + moe
    r = _rms(x2, gp_ref[...]).astype(bf16)
    gate = jax.nn.sigmoid(jnp.dot(r, wpg_ref[...], preferred_element_type=f32))
    emb = jnp.dot(p_ref[...].astype(bf16), wpp_ref[...], preferred_element_type=f32)
    x3 = x2 + gate * emb
    o_ref[...] = x3 if gf_ref is None else _rms(x3, gf_ref[...])


def combine_ple(lpos, gate_w, seg_cnt, seg_off, seg_dst, x1, ys_tiles, p2, g_ple, w_pg, w_pp, g_final,
                *, tm):
    n, d = x1.shape
    steps = n // tm
    smem = pltpu.SMEM
    full = lambda a: pl.BlockSpec(a.shape, lambda i: (0,) * a.ndim)
    seg = lambda shift: pl.BlockSpec((1, 1, N_EXPERTS),
                                     lambda i: (jnp.minimum(i + shift, steps - 1), 0, 0), memory_space=smem)
    tok = pl.BlockSpec((TOP_K, tm), lambda i: (0, i), memory_space=smem)
    consts = [g_ple.reshape(1, d), w_pg, w_pp] + ([] if g_final is None else [g_final.reshape(1, d)])
    return pl.pallas_call(
        _combine_ple_body,
        grid=(steps,),
        in_specs=[tok, tok, seg(0), seg(0), seg(0), seg(1), seg(1), seg(1),
                  pl.BlockSpec((tm, d), lambda i: (i, 0)),
                  pl.BlockSpec(memory_space=pl.ANY),
                  pl.BlockSpec((tm, p2.shape[1]), lambda i: (i, 0))] + [full(a) for a in consts],
        out_specs=pl.BlockSpec((tm, d), lambda i: (i, 0)),
        out_shape=jax.ShapeDtypeStruct((n, d), jnp.float32),
        scratch_shapes=[pltpu.VMEM((2, TOP_K * tm, ROW_CHUNKS, LANES), ys_tiles.dtype),
                        pltpu.VMEM((tm, ROW_CHUNKS, LANES), jnp.float32),
                        pltpu.SemaphoreType.DMA((2,))],
        compiler_params=_params("arbitrary"),
        name="combine_ple",
    )(lpos, gate_w, seg_cnt, seg_off, seg_dst, seg_cnt, seg_off, seg_dst, x1, ys_tiles, p2, *consts)


def _moe(h2_tiles, lpos, tile_cnt, w_gu, b_gu, w_d, b_d, *, tm):
    n = h2_tiles.shape[0] // ROW_CHUNKS
    rb = MOE_ROW_BLOCK
    i32 = jnp.int32
    n_blocks = -(-n * TOP_K // rb) + N_EXPERTS
    n_rows = n_blocks * rb
    counts = jnp.sum(tile_cnt, axis=0)
    padded = (counts + rb - 1) // rb * rb
    pend = jnp.cumsum(padded)
    pstart = pend - padded
    blk_start = jnp.arange(n_blocks, dtype=i32) * rb
    blk_expert = jnp.minimum(jnp.sum((pend[None, :] <= blk_start[:, None]).astype(i32), axis=1),
                             N_EXPERTS - 1)
    n_used = pend[-1:] // rb
    seg_cnt = tile_cnt[:, None, :]
    seg_off = (jnp.cumsum(tile_cnt, axis=1) - tile_cnt)[:, None, :]
    seg_dst = (pstart[None, :] + jnp.cumsum(tile_cnt, axis=0) - tile_cnt)[:, None, :]
    tail = n_used + jnp.arange(N_EXPERTS, dtype=i32)
    pad_len = jnp.concatenate([padded - counts, jnp.where(tail < n_blocks, rb, 0)])[None, :]
    pad_row = jnp.concatenate([pstart + counts, tail * rb])[None, :]
    b_g, b_u = b_gu[:, None, 0::2], b_gu[:, None, 1::2]
    xs = dispatch(lpos, seg_cnt, seg_off, seg_dst, pad_len.astype(i32), pad_row.astype(i32),
                  h2_tiles.reshape(n, ROW_CHUNKS, LANES), n_rows, tc=tm, rb=rb)
    e_ids = jnp.arange(N_EXPERTS, dtype=i32)
    later = (padded > 0)[None, :] & (e_ids[None, :] > e_ids[:, None])
    nxt_tab = jnp.min(jnp.where(later, e_ids[None, :], N_EXPERTS), axis=1)
    nxt_tab = jnp.where(nxt_tab == N_EXPERTS, -1, nxt_tab)
    nxt_expert = jnp.sum(jnp.where(blk_expert[:, None] == e_ids[None, :], nxt_tab[None, :], 0), axis=1)
    ys = experts(blk_expert, nxt_expert.astype(i32), n_used.astype(i32),
                 xs.reshape(n_rows * ROW_CHUNKS, LANES), w_gu, b_g, b_u, w_d, b_d[:, None, :])
    return (seg_cnt, seg_off, seg_dst), ys.reshape(n_rows, ROW_CHUNKS, LANES)


def kernel(x, p, norm_mix_g, w_in, conv_w, conv_b, w_qk_m, w_if, b_if, mnorm_g, w_branch, w_out, norm_ffn_g, w_router, b_router, w_gate_up, b_gate_up, w_down, b_down, norm_ple_g, w_ple_gate, w_ple_proj, final_norm_g):
    B, S, D = x.shape
    depth = w_in.shape[0]
    bf16 = jnp.bfloat16
    x2 = x.reshape(B * S, D)
    for i in range(depth):
        qkv, xo, gates = in_proj(x2, norm_mix_g[i], w_in[i].astype(bf16))
        y_a = moba(qkv, B, S)
        y_m = mlstm(xo, conv_w[i], conv_b[i], w_qk_m[i], w_if[i], b_if[i], mnorm_g[i], B, S)
        x1, h2_tiles, gate_w, lpos, tile_cnt = merge_route(
            x2, y_a, y_m, gates, w_branch[i, 0].astype(bf16), w_branch[i, 1].astype(bf16),
            w_out[i].astype(bf16), norm_ffn_g[i], w_router[i], b_router[i], tm=MOE_TOKEN_TILE)
        segs, ys = _moe(h2_tiles, lpos, tile_cnt[:, :, 0], w_gate_up[i], b_gate_up[i],
                        w_down[i], b_down[i], tm=MOE_TOKEN_TILE)
        x2 = combine_ple(lpos, gate_w, *segs, x1, ys, p[i].reshape(B * S, -1), norm_ple_g[i],
                         w_ple_gate[i].astype(bf16), w_ple_proj[i].astype(bf16),
                         final_norm_g if i == depth - 1 else None, tm=MOE_TOKEN_TILE)
    return x2.reshape(B, S, D)
```

```python
import functools

import jax
import jax.numpy as jnp
import numpy as np
from jax import lax
from jax.experimental import pallas as pl
from jax.experimental.pallas import tpu as pltpu

RMS_EPS = 1e-6
LANES = 128
ROW_CHUNKS = 8
VMEM_LIMIT = 56 * 1024 * 1024

N_ATT_HEADS = 8
ATT_HEAD_DIM = 64
ATT_WIDTH = N_ATT_HEADS * ATT_HEAD_DIM
MOBA_BLOCK = 256
MOBA_TOPK = 3

N_MLSTM_HEADS = 4
MLSTM_WIDTH = 512
MLSTM_V_DIM = 128
MLSTM_QK_DIM = 64
MLSTM_CONV = 4
MLSTM_CHUNK = 256

N_EXPERTS = 32
TOP_K = 4
SWIGLU_ALPHA = 1.702
SWIGLU_LIMIT = 7.0
MOE_ROW_BLOCK = 512
MOE_TOKEN_TILE = 512

NEG_BIG = -1e30

_NT = (((1,), (1,)), ((), ()))
_TN = (((0,), (0,)), ((), ()))


def _params(*sem):
    return pltpu.CompilerParams(dimension_semantics=sem, vmem_limit_bytes=VMEM_LIMIT)


def _rms(x, g):
    return x * lax.rsqrt(jnp.mean(x * x, axis=-1, keepdims=True) + RMS_EPS) * g


def _bf16_parts(x, n):
    parts = []
    for _ in range(n):
        p = x.astype(jnp.bfloat16)
        parts.append(p)
        x = x - p.astype(jnp.float32)
    return parts


def _in_proj_body(x_ref, g_ref, w_ref, qkv_ref, xo_ref, gate_ref, *, col_chunk):
    h = _rms(x_ref[...], g_ref[...]).astype(jnp.bfloat16)
    col = 0
    for out_ref in (qkv_ref, xo_ref, gate_ref):
        for c in range(0, out_ref.shape[1], col_chunk):
            out_ref[:, c:c + col_chunk] = jnp.dot(
                h, w_ref[:, col + c:col + c + col_chunk],
                preferred_element_type=jnp.float32).astype(out_ref.dtype)
        col += out_ref.shape[1]


def in_proj(x2, g, w_bf16, *, tm=512, col_chunk=512):
    n, d = x2.shape
    widths = (3 * ATT_WIDTH, 2 * MLSTM_WIDTH, 2 * d)
    assert sum(widths) == w_bf16.shape[1] and n % tm == 0
    return pl.pallas_call(
        functools.partial(_in_proj_body, col_chunk=col_chunk),
        grid=(n // tm,),
        in_specs=[pl.BlockSpec((tm, d), lambda i: (i, 0)),
                  pl.BlockSpec((1, d), lambda i: (0, 0)),
                  pl.BlockSpec(w_bf16.shape, lambda i: (0, 0))],
        out_specs=[pl.BlockSpec((tm, w), lambda i: (i, 0)) for w in widths],
        out_shape=[jax.ShapeDtypeStruct((n, w), jnp.bfloat16) for w in widths],
        compiler_params=_params("parallel"),
        name="in_proj",
    )(x2, g.reshape(1, d), w_bf16)


def _moba_body(q_ref, k_ref, v_ref, o_ref, *, pair, seq):
    blk = MOBA_BLOCK
    n_blk = seq // blk
    f32, bf16 = jnp.float32, jnp.bfloat16
    lane = lax.broadcasted_iota(jnp.int32, (1, LANES), 1)
    first_head = lane < ATT_HEAD_DIM
    row = lax.broadcasted_iota(jnp.int32, (2 * blk, blk), 0)
    col = lax.broadcasted_iota(jnp.int32, (2 * blk, blk), 1)
    t_in_blk = jnp.where(row >= blk, row - blk, row)
    rel = (t_in_blk - col).astype(f32)
    causal = t_in_blk >= col
    log2e = float(np.log2(np.e))
    scale2 = ATT_HEAD_DIM ** -0.5 * log2e
    head = (2 * pair + (row[:, 0:1] >= blk).astype(jnp.int32)).astype(f32)
    slope2 = jnp.exp2(-8.0 * (head + 1.0) / N_ATT_HEADS) * log2e
    bias = [slope2 * (rel + float(d * blk)) for d in range(n_blk)]
    lane_row = lax.broadcasted_iota(jnp.int32, (LANES, LANES), 0)
    spread = [jnp.where(lane_row == j, 1.0, 0.0).astype(bf16) for j in range(n_blk)]

    k_all = k_ref[...]
    v_all = v_ref[...]
    k_mean = jnp.concatenate(
        [jnp.mean(k_all[j * blk:(j + 1) * blk].astype(f32), axis=0, keepdims=True)
         for j in range(n_blk)] + [jnp.zeros((LANES - n_blk, LANES), f32)], axis=0)
    k_mean_parts = _bf16_parts(k_mean, 3)

    for qi in range(n_blk):
        q_blk = q_ref[qi * blk:(qi + 1) * blk, :]
        zero = jnp.zeros_like(q_blk)
        q2 = jnp.concatenate([jnp.where(first_head, q_blk, zero), jnp.where(first_head, zero, q_blk)], axis=0)
        sel = None
        if qi > MOBA_TOPK:
            g = sum(lax.dot_general(q2, part, _NT, preferred_element_type=f32) for part in k_mean_parts)
            g = jnp.where(lane < qi, g, -jnp.inf)
            ahead = jnp.zeros((2 * blk, LANES), f32)
            for d in range(1, qi):
                ahead = ahead + jnp.where(pltpu.roll(g, d, axis=1) >= g, 1.0, 0.0)
                ahead = ahead + jnp.where(pltpu.roll(g, LANES - d, axis=1) > g, 1.0, 0.0)
            keep = jnp.where(ahead < MOBA_TOPK, 1.0, 0.0).astype(bf16)
            sel = [jnp.dot(keep, spread[j], preferred_element_type=f32) > 0.5 for j in range(qi)]
        logits = []
        for j in range(qi + 1):
            s = lax.dot_general(q2, k_all[j * blk:(j + 1) * blk], _NT, preferred_element_type=f32)
            s = s * scale2 - bias[qi - j]
            if j == qi:
                s = jnp.where(causal, s, NEG_BIG)
            elif sel is not None:
                s = jnp.where(jnp.concatenate([sel[j]] * (blk // LANES), axis=1), s, NEG_BIG)
            logits.append(s)
        s_all = jnp.concatenate(logits, axis=1)
        m = jnp.max(s_all, axis=1, keepdims=True)
        p = jnp.exp2(s_all - m)
        denom = jnp.sum(p, axis=1, keepdims=True)
        o2 = jnp.dot(p.astype(bf16), v_all[:(qi + 1) * blk], preferred_element_type=f32) / denom
        o = jnp.where(first_head, o2[:blk], o2[blk:])
        o_ref[qi * blk:(qi + 1) * blk, :] = o.astype(o_ref.dtype)


def _moba_kernel(q_ref, k_ref, v_ref, o_ref, *, seq):
    _moba_body(q_ref, k_ref, v_ref, o_ref, pair=pl.program_id(1), seq=seq)


def moba(qkv, batch, seq):
    n = batch * seq
    n_pairs = ATT_WIDTH // LANES
    assert seq % MOBA_BLOCK == 0
    spec = lambda off: pl.BlockSpec((seq, LANES), lambda b, p: (b, off + p))
    return pl.pallas_call(
        functools.partial(_moba_kernel, seq=seq),
        grid=(batch, n_pairs),
        in_specs=[spec(0), spec(n_pairs), spec(2 * n_pairs)],
        out_specs=pl.BlockSpec((seq, LANES), lambda b, p: (b, p)),
        out_shape=jax.ShapeDtypeStruct((n, ATT_WIDTH), jnp.bfloat16),
        compiler_params=_params("parallel", "parallel"),
        name="moba",
    )(qkv, qkv, qkv)


def _log_sigmoid(z):
    return jnp.minimum(z, 0.0) - jnp.log(1.0 + jnp.exp(-jnp.abs(z)))


def _mlstm_body(xm_ref, om_ref, cw_ref, cb_ref, wq_ref, wk_ref, wiq_ref, wik_ref, wix_ref,
                wiqt_ref, wikt_ref, wixt_ref, bi_ref, bit_ref, g_ref, y_ref,
                q_sc, k_sc, gcol_sc, grow_sc, ct_sc, n_sc, m_sc, *, seq):
    L = MLSTM_CHUNK
    H = N_MLSTM_HEADS
    n_chunks = seq // L
    f32, bf16 = jnp.float32, jnp.bfloat16

    x = xm_ref[...].astype(f32)
    t_idx = lax.broadcasted_iota(jnp.int32, (seq, 1), 0)
    acc = x * cw_ref[MLSTM_CONV - 1:MLSTM_CONV, :] + cb_ref[...]
    for d in range(1, MLSTM_CONV):
        shifted = jnp.where(t_idx >= d, pltpu.roll(x, d, axis=0), 0.0)
        acc = acc + shifted * cw_ref[MLSTM_CONV - 1 - d:MLSTM_CONV - d, :]
    xc = (acc * jax.nn.sigmoid(acc)).astype(bf16)

    xm = xm_ref[...]
    if_col = jnp.dot(xm, wix_ref[...], preferred_element_type=f32) + bi_ref[...]
    if_row = lax.dot_general(wixt_ref[...], xm, _NT, preferred_element_type=f32) + bit_ref[...]
    for h in range(H):
        xch = xc[:, h * LANES:(h + 1) * LANES]
        q = jnp.dot(xch, wq_ref[h], preferred_element_type=f32).astype(bf16)
        k = jnp.dot(xch, wk_ref[h], preferred_element_type=f32).astype(bf16)
        if_col = if_col + jnp.dot(q, wiq_ref[h], preferred_element_type=f32)
        if_col = if_col + jnp.dot(k, wik_ref[h], preferred_element_type=f32)
        if_row = if_row + lax.dot_general(wiqt_ref[h], q, _NT, preferred_element_type=f32)
        if_row = if_row + lax.dot_general(wikt_ref[h], k, _NT, preferred_element_type=f32)
        q_sc[h] = q
        k_sc[h] = (k.astype(f32) * (MLSTM_QK_DIM ** -0.5)).astype(bf16)
    is_f_col = lax.broadcasted_iota(jnp.int32, (1, 2 * H), 1) >= H
    is_f_row = lax.broadcasted_iota(jnp.int32, (2 * H, 1), 0) >= H
    gcol_sc[...] = jnp.where(is_f_col, _log_sigmoid(if_col), if_col)
    g_row = jnp.where(is_f_row, _log_sigmoid(if_row), if_row)
    for c in range(n_chunks):
        grow_sc[c] = g_row[:, c * L:(c + 1) * L]

    ct_sc[...] = jnp.zeros_like(ct_sc)
    n_sc[...] = jnp.zeros_like(n_sc)
    m_sc[...] = jnp.zeros_like(m_sc)

    r_i = lax.broadcasted_iota(jnp.int32, (L, L), 0)
    c_i = lax.broadcasted_iota(jnp.int32, (L, L), 1)
    causal = r_i >= c_i
    tri_lo = jnp.where(causal, 1.0, 0.0).astype(f32)
    tri_up = jnp.where(c_i >= r_i, 1.0, 0.0).astype(f32)

    def chunk(c, carry):
        r0 = pl.multiple_of(c * L, L)
        gc = gcol_sc[pl.ds(r0, L), :]
        gr = grow_sc[c]
        b_cols = jnp.dot(tri_lo, gc, precision=lax.Precision.HIGHEST, preferred_element_type=f32)
        b_rows = jnp.dot(gr, tri_up, precision=lax.Precision.HIGHEST, preferred_element_type=f32)
        for h in range(H):
            hs = slice(h * LANES, (h + 1) * LANES)
            b_col = b_cols[:, H + h:H + h + 1]
            i_col = gc[:, h:h + 1]
            b_row = b_rows[H + h:H + h + 1, :]
            i_row = gr[h:h + 1, :]
            b_last = b_col[L - 1:L, :]
            m_prev = m_sc[h][:, 0:1]
            q = q_sc[h, pl.ds(r0, L), :]
            k = k_sc[h, pl.ds(r0, L), :]
            v = xm_ref[pl.ds(r0, L), hs]

            log_d = jnp.where(causal, b_col + (i_row - b_row), NEG_BIG)
            inter = b_col + m_prev
            m_t = jnp.maximum(inter, jnp.max(log_d, axis=1, keepdims=True))
            w_inter = jnp.exp(inter - m_t)
            s = lax.dot_general(q, k, _NT, preferred_element_type=f32) * jnp.exp(log_d - m_t)
            num = jnp.dot(s.astype(bf16), v, preferred_element_type=f32)
            num = num + w_inter * jnp.dot(q, ct_sc[h].astype(bf16), preferred_element_type=f32)
            den = jnp.sum(s, axis=1, keepdims=True)
            den = den + w_inter * jnp.sum(q.astype(f32) * n_sc[h], axis=1, keepdims=True)
            hv = num / jnp.maximum(jnp.abs(den), jnp.exp(-m_t))
            hv = hv * lax.rsqrt(jnp.mean(hv * hv, axis=1, keepdims=True) + RMS_EPS) * g_ref[:, hs]
            o_gate = jax.nn.sigmoid(om_ref[pl.ds(r0, L), hs].astype(f32))
            y_ref[pl.ds(r0, L), hs] = (o_gate * hv).astype(y_ref.dtype)

            g_col = b_last - b_col + i_col
            m_new = jnp.maximum(b_last + m_prev, jnp.max(g_col, axis=0, keepdims=True))
            wg = jnp.exp(g_col - m_new)
            decay = jnp.exp(b_last + m_prev - m_new)
            wv = (wg * v.astype(f32)).astype(bf16)
            ct_sc[h] = decay * ct_sc[h] + lax.dot_general(k, wv, _TN, preferred_element_type=f32)
            n_sc[h] = decay * n_sc[h] + jnp.sum(wg * k.astype(f32), axis=0, keepdims=True)
            m_sc[h] = jnp.broadcast_to(m_new, (1, LANES))
        return carry

    lax.fori_loop(0, n_chunks, chunk, 0)


def _mlstm_kernel(*refs, seq):
    _mlstm_body(*refs, seq=seq)


def mlstm(xo, conv_w, conv_b, w_qk, w_if, b_if, g, batch, seq):
    n = batch * seq
    H, dv, dk = N_MLSTM_HEADS, MLSTM_V_DIM, MLSTM_QK_DIM
    assert seq % MLSTM_CHUNK == 0 and dv == LANES
    bf16 = jnp.bfloat16
    pad = ((0, 0), (0, 0), (0, LANES - dk))
    wq = jnp.pad(w_qk[:, :, :dk], pad).astype(bf16)
    wk = jnp.pad(w_qk[:, :, dk:], pad).astype(bf16)
    rpad = ((0, 0), (0, LANES - dk), (0, 0))
    wiq = jnp.pad(w_if[:H * dk].reshape(H, dk, 2 * H), rpad).astype(bf16)
    wik = jnp.pad(w_if[H * dk:2 * H * dk].reshape(H, dk, 2 * H), rpad).astype(bf16)
    wix = w_if[2 * H * dk:].astype(bf16)
    wiqt, wikt, wixt = wiq.transpose(0, 2, 1), wik.transpose(0, 2, 1), wix.T
    full = lambda a: pl.BlockSpec(a.shape, lambda b: (0,) * a.ndim)
    consts = [conv_w, conv_b.reshape(1, -1), wq, wk, wiq, wik, wix, wiqt, wikt, wixt,
              b_if.reshape(1, -1), b_if.reshape(-1, 1), g.reshape(1, -1)]
    return pl.pallas_call(
        functools.partial(_mlstm_kernel, seq=seq),
        grid=(batch,),
        in_specs=[pl.BlockSpec((seq, MLSTM_WIDTH), lambda b: (b, 0)),
                  pl.BlockSpec((seq, MLSTM_WIDTH), lambda b: (b, 1))] + [full(a) for a in consts],
        out_specs=pl.BlockSpec((seq, MLSTM_WIDTH), lambda b: (b, 0)),
        out_shape=jax.ShapeDtypeStruct((n, MLSTM_WIDTH), bf16),
        scratch_shapes=[pltpu.VMEM((H, seq, LANES), bf16), pltpu.VMEM((H, seq, LANES), bf16),
                        pltpu.VMEM((seq, 2 * H), jnp.float32),
                        pltpu.VMEM((seq // MLSTM_CHUNK, 2 * H, MLSTM_CHUNK), jnp.float32),
                        pltpu.VMEM((H, LANES, LANES), jnp.float32),
                        pltpu.VMEM((H, 1, LANES), jnp.float32),
                        pltpu.VMEM((H, 1, LANES), jnp.float32)],
        compiler_params=_params("parallel"),
        name="mlstm",
    )(xo, xo, *consts)


def _store_rows_as_tiles(dst_ref, val):
    rows = val.shape[0]
    for c in range(ROW_CHUNKS):
        dst_ref[pl.ds(c, rows, stride=ROW_CHUNKS), :] = val[:, c * LANES:(c + 1) * LANES]


def _load_tiles_as_rows(src_ref, rows, lead=()):
    return jnp.concatenate(
        [src_ref[lead + (pl.ds(c, rows, stride=ROW_CHUNKS), slice(None))] for c in range(ROW_CHUNKS)],
        axis=1)


def _merge_route_body(x_ref, ya_ref, ym_ref, ga_ref, gm_ref, wb0_ref, wb1_ref, wo_ref, g_ref,
                      wr_ref, br_ref, x1_ref, h2_ref, gate_ref, lpos_ref, cnt_ref):
    f32, bf16 = jnp.float32, jnp.bfloat16
    tm = x_ref.shape[0]

    u = jax.nn.sigmoid(ga_ref[...].astype(f32)) * jnp.dot(ya_ref[...], wb0_ref[...], preferred_element_type=f32)
    u = u + jax.nn.sigmoid(gm_ref[...].astype(f32)) * jnp.dot(ym_ref[...], wb1_ref[...], preferred_element_type=f32)
    x1 = x_ref[...] + jnp.dot(u.astype(bf16), wo_ref[...], preferred_element_type=f32)
    x1_ref[...] = x1
    h2 = _rms(x1, g_ref[...])
    _store_rows_as_tiles(h2_ref, h2)

    (h_hi, h_lo), (w_hi, w_lo) = _bf16_parts(h2, 2), _bf16_parts(wr_ref[...], 2)
    logits = (jnp.dot(h_hi, w_hi, preferred_element_type=f32) + jnp.dot(h_lo, w_hi, preferred_element_type=f32)
              + jnp.dot(h_hi, w_lo, preferred_element_type=f32)) + br_ref[...]
    logits = jnp.transpose(logits)[:N_EXPERTS]
    e_id = lax.broadcasted_iota(jnp.int32, logits.shape, 0).astype(f32)
    chosen = jnp.zeros(logits.shape, f32)
    vals, ids = [], []
    for _ in range(TOP_K):
        top = jnp.max(logits, axis=0, keepdims=True)
        first = jnp.min(jnp.where(logits == top, e_id, float(N_EXPERTS)), axis=0, keepdims=True)
        hit = e_id == first
        chosen = jnp.where(hit, 1.0, chosen)
        logits = jnp.where(hit, -jnp.inf, logits)
        vals.append(top)
        ids.append(first)
    ex = [jnp.exp(v - vals[0]) for v in vals]
    total = ex[0] + ex[1] + ex[2] + ex[3]

    chosen_b = chosen.astype(bf16)
    t_r = lax.broadcasted_iota(jnp.int32, (tm, tm), 0)
    t_c = lax.broadcasted_iota(jnp.int32, (tm, tm), 1)
    earlier = jnp.where(t_r < t_c, 1.0, 0.0).astype(bf16)
    before = jnp.dot(chosen_b, earlier, preferred_element_type=f32)
    e_r = lax.broadcasted_iota(jnp.int32, (N_EXPERTS, N_EXPERTS), 0)
    e_c = lax.broadcasted_iota(jnp.int32, (N_EXPERTS, N_EXPERTS), 1)
    lower = jnp.where(e_c < e_r, 1.0, 0.0).astype(bf16)
    seg_off = jnp.sum(jnp.dot(lower, chosen_b, preferred_element_type=f32), axis=1, keepdims=True)
    pos = before + seg_off
    for kk in range(TOP_K):
        gate_ref[kk:kk + 1, :] = ex[kk] / total
        lpos_ref[kk:kk + 1, :] = jnp.sum(jnp.where(e_id == ids[kk], pos, 0.0), axis=0,
                                         keepdims=True).astype(jnp.int32)
    counts = jnp.sum(chosen, axis=1, keepdims=True)
    cnt_ref[0] = jnp.broadcast_to(counts, (N_EXPERTS, LANES)).astype(jnp.int32)


def merge_route(x2, ya, ym, gates, wb0, wb1, wo, g, w_router, b_router, *, tm):
    n, d = x2.shape
    e = w_router.shape[1]
    assert n % tm == 0 and d == ROW_CHUNKS * LANES and e == N_EXPERTS
    full = lambda a: pl.BlockSpec(a.shape, lambda i: (0,) * a.ndim)
    lane_pad = ((0, 0), (0, LANES - e))
    consts = [wb0, wb1, wo, g.reshape(1, d), jnp.pad(w_router, lane_pad),
              jnp.pad(b_router.reshape(1, e), lane_pad)]
    tok = lambda dt: jax.ShapeDtypeStruct((TOP_K, n), dt)
    return pl.pallas_call(
        _merge_route_body,
        grid=(n // tm,),
        in_specs=[pl.BlockSpec((tm, d), lambda i: (i, 0)),
                  pl.BlockSpec((tm, ATT_WIDTH), lambda i: (i, 0)),
                  pl.BlockSpec((tm, MLSTM_WIDTH), lambda i: (i, 0)),
                  pl.BlockSpec((tm, d), lambda i: (i, 0)),
                  pl.BlockSpec((tm, d), lambda i: (i, 1))] + [full(a) for a in consts],
        out_specs=[pl.BlockSpec((tm, d), lambda i: (i, 0)),
                   pl.BlockSpec((tm * ROW_CHUNKS, LANES), lambda i: (i, 0)),
                   pl.BlockSpec((TOP_K, tm), lambda i: (0, i)),
                   pl.BlockSpec((TOP_K, tm), lambda i: (0, i)),
                   pl.BlockSpec((1, e, LANES), lambda i: (i, 0, 0))],
        out_shape=[jax.ShapeDtypeStruct((n, d), jnp.float32),
                   jax.ShapeDtypeStruct((n * ROW_CHUNKS, LANES), jnp.float32),
                   tok(jnp.float32), tok(jnp.int32),
                   jax.ShapeDtypeStruct((n // tm, e, LANES), jnp.int32)],
        compiler_params=_params("parallel"),
        name="merge_route",
    )(x2, ya, ym, gates, gates, *consts)


def _segment_pieces(cnt, max_rows, fn):
    done = 0
    size = 1 << (max_rows.bit_length() - 1)
    while size:
        bit = cnt & size

        def piece(done=done, size=size):
            fn(done, size)
        pl.when(bit != 0)(piece)
        done = done + bit
        size >>= 1


def _for_tile_segments(meta, tile_rows, make_copy, action):
    cnt_ref, off_ref, dst_ref = meta

    def per_expert(e, carry):
        off, dst = off_ref[0, 0, e], dst_ref[0, 0, e]
        _segment_pieces(cnt_ref[0, 0, e], tile_rows,
                        lambda done, size: getattr(make_copy(off + done, dst + done, size), action)())
        return carry
    lax.fori_loop(0, N_EXPERTS, per_expert, 0)


def _dispatch_body(lpos_ref, cnt_ref, off_ref, dst_ref, pcnt_ref, poff_ref, pdst_ref,
                   pad_len_ref, pad_row_ref, x_ref, out_ref, cbuf, zbuf, sem, zsem, *, tc, rb):
    step = pl.program_id(0)
    last = pl.num_programs(0) - 1
    slot = step % 2

    def compact(t, carry):
        row = x_ref[t]
        for kk in range(TOP_K):
            cbuf[slot, lpos_ref[kk, t]] = row
        return carry
    lax.fori_loop(0, tc, compact, 0, unroll=8)

    def seg_copy(sl):
        return lambda local, glob, size: pltpu.make_async_copy(
            cbuf.at[sl, pl.ds(local, size)], out_ref.at[pl.ds(glob, size)], sem.at[sl])

    cur, prev = (cnt_ref, off_ref, dst_ref), (pcnt_ref, poff_ref, pdst_ref)
    _for_tile_segments(cur, tc, seg_copy(slot), "start")

    @pl.when(step > 0)
    def _():
        _for_tile_segments(prev, tc, seg_copy(1 - slot), "wait")

    @pl.when(step == last)
    def _():
        _for_tile_segments(cur, tc, seg_copy(slot), "wait")
        zbuf[...] = jnp.zeros_like(zbuf)

        def pads(action):
            def per_pad(j, carry):
                row = pad_row_ref[0, j]
                _segment_pieces(pad_len_ref[0, j], rb, lambda done, size: getattr(
                    pltpu.make_async_copy(zbuf.at[pl.ds(0, size)], out_ref.at[pl.ds(row + done, size)], zsem),
                    action)())
                return carry
            lax.fori_loop(0, pad_len_ref.shape[1], per_pad, 0)
        pads("start")
        pads("wait")


def dispatch(lpos, seg_cnt, seg_off, seg_dst, pad_len, pad_row, h2_tiles, n_rows, *, tc, rb):
    n = h2_tiles.shape[0]
    steps = n // tc
    assert n % tc == 0
    smem = pltpu.SMEM
    seg = lambda shift: pl.BlockSpec((1, 1, N_EXPERTS), lambda i: (jnp.maximum(i - shift, 0), 0, 0),
                                     memory_space=smem)
    whole = lambda a: pl.BlockSpec(a.shape, lambda i: (0,) * a.ndim, memory_space=smem)
    return pl.pallas_call(
        functools.partial(_dispatch_body, tc=tc, rb=rb),
        grid=(steps,),
        in_specs=[pl.BlockSpec((TOP_K, tc), lambda i: (0, i), memory_space=smem),
                  seg(0), seg(0), seg(0), seg(1), seg(1), seg(1), whole(pad_len), whole(pad_row),
                  pl.BlockSpec((tc, ROW_CHUNKS, LANES), lambda i: (i, 0, 0))],
        out_specs=pl.BlockSpec(memory_space=pl.ANY),
        out_shape=jax.ShapeDtypeStruct((n_rows, ROW_CHUNKS, LANES), h2_tiles.dtype),
        scratch_shapes=[pltpu.VMEM((2, TOP_K * tc, ROW_CHUNKS, LANES), h2_tiles.dtype),
                        pltpu.VMEM((rb, ROW_CHUNKS, LANES), h2_tiles.dtype),
                        pltpu.SemaphoreType.DMA((2,)), pltpu.SemaphoreType.DMA(())],
        compiler_params=_params("arbitrary"),
        name="dispatch",
    )(lpos, seg_cnt, seg_off, seg_dst, seg_cnt, seg_off, seg_dst, pad_len, pad_row, h2_tiles)


def _experts_body(blk_e_ref, nxt_e_ref, used_ref, x_ref, wgu_hbm, bg_ref, bu_ref, wd_hbm, bd_ref, perm_ref,
                  y_ref, wgu_st, wd_st, wg_sc, wu_sc, wd_sc, sem, *, halves):
    f32, bf16 = jnp.float32, jnp.bfloat16
    step = pl.program_id(0)
    live = step < used_ref[0]
    rows = x_ref.shape[0] // ROW_CHUNKS
    expert = blk_e_ref[step]

    def fetch(e, action):
        getattr(pltpu.make_async_copy(wgu_hbm.at[e], wgu_st, sem.at[0]), action)()
        getattr(pltpu.make_async_copy(wd_hbm.at[e], wd_st, sem.at[1]), action)()

    @pl.when(live & ((step == 0) | (expert != blk_e_ref[jnp.maximum(step - 1, 0)])))
    def _():
        @pl.when(step == 0)
        def _():
            fetch(expert, "start")
        fetch(expert, "wait")
        tile = perm_ref.shape[0]
        half = tile // 2
        for t in range(wgu_st.shape[1] // tile):
            w = wgu_st[:, t * tile:(t + 1) * tile].astype(bf16)
            o = jnp.dot(w, perm_ref[...], preferred_element_type=f32)
            wg_sc[:, t * half:(t + 1) * half] = o[:, :half].astype(bf16)
            wu_sc[:, t * half:(t + 1) * half] = o[:, half:].astype(bf16)
        wd_sc[...] = wd_st[...].astype(bf16)
        nxt = nxt_e_ref[step]

        @pl.when(nxt >= 0)
        def _():
            fetch(nxt, "start")

    @pl.when(live)
    def _():
        part = rows // halves
        for h in range(halves):
            xb = jnp.concatenate(
                [x_ref[pl.ds(h * part * ROW_CHUNKS + c, part, stride=ROW_CHUNKS), :]
                 for c in range(ROW_CHUNKS)], axis=1).astype(bf16)
            gate = jnp.dot(xb, wg_sc[...], preferred_element_type=f32) + bg_ref[0]
            up = jnp.dot(xb, wu_sc[...], preferred_element_type=f32) + bu_ref[0]
            gate = jnp.minimum(gate, SWIGLU_LIMIT)
            up = jnp.clip(up, -SWIGLU_LIMIT, SWIGLU_LIMIT)
            act = (up + 1.0) * (gate * jax.nn.sigmoid(gate * SWIGLU_ALPHA))
            y = jnp.dot(act.astype(bf16), wd_sc[...], preferred_element_type=f32) + bd_ref[0]
            for c in range(ROW_CHUNKS):
                y_ref[pl.ds(h * part * ROW_CHUNKS + c, part, stride=ROW_CHUNKS), :] = \
                    y[:, c * LANES:(c + 1) * LANES]

    @pl.when(jnp.logical_not(live))
    def _():
        y_ref[...] = jnp.zeros_like(y_ref)


def experts(blk_expert, nxt_expert, n_used, xs_tiles, w_gu, b_g, b_u, w_d, b_d, *, halves=2):
    n_blocks = blk_expert.shape[0]
    rb = MOE_ROW_BLOCK
    d, ff = w_d.shape[2], w_d.shape[1]
    tile = 2 * LANES
    perm = np.zeros((tile, tile), np.float32)
    perm[2 * np.arange(LANES), np.arange(LANES)] = 1.0
    perm[2 * np.arange(LANES) + 1, LANES + np.arange(LANES)] = 1.0
    e_map3 = lambda i, be, nx, nu: (be[i], 0, 0)
    rows_map = lambda i, be, nx, nu: (i, 0)
    grid_spec = pltpu.PrefetchScalarGridSpec(
        num_scalar_prefetch=3,
        grid=(n_blocks,),
        in_specs=[pl.BlockSpec((rb * ROW_CHUNKS, LANES), rows_map),
                  pl.BlockSpec(memory_space=pl.ANY),
                  pl.BlockSpec((1, 1, ff), e_map3), pl.BlockSpec((1, 1, ff), e_map3),
                  pl.BlockSpec(memory_space=pl.ANY), pl.BlockSpec((1, 1, d), e_map3),
                  pl.BlockSpec((tile, tile), lambda i, be, nx, nu: (0, 0))],
        out_specs=pl.BlockSpec((rb * ROW_CHUNKS, LANES), rows_map),
        scratch_shapes=[pltpu.VMEM((d, 2 * ff), w_gu.dtype), pltpu.VMEM((ff, d), w_d.dtype),
                        pltpu.VMEM((d, ff), jnp.bfloat16), pltpu.VMEM((d, ff), jnp.bfloat16),
                        pltpu.VMEM((ff, d), jnp.bfloat16), pltpu.SemaphoreType.DMA((2,))],
    )
    return pl.pallas_call(
        functools.partial(_experts_body, halves=halves),
        grid_spec=grid_spec,
        out_shape=jax.ShapeDtypeStruct(xs_tiles.shape, jnp.float32),
        compiler_params=_params("arbitrary"),
        name="experts",
    )(blk_expert, nxt_expert, n_used, xs_tiles, w_gu, b_g, b_u, w_d, b_d, jnp.asarray(perm, jnp.bfloat16))


def _combine_ple_body(lpos_ref, gw_ref, cnt_ref, off_ref, dst_ref, ncnt_ref, noff_ref, ndst_ref,
                      x1_ref, ys_ref, p_ref, gp_ref, wpg_ref, wpp_ref, *rest):
    cbuf, msum, sem = rest[-3:]
    rest = rest[:-3]
    gf_ref, o_ref = rest if len(rest) == 2 else (None, rest[0])
    f32, bf16 = jnp.float32, jnp.bfloat16
    tm = x1_ref.shape[0]
    step = pl.program_id(0)
    slot = step % 2

    def seg_copy(sl):
        return lambda local, glob, size: pltpu.make_async_copy(
            ys_ref.at[pl.ds(glob, size)], cbuf.at[sl, pl.ds(local, size)], sem.at[sl])

    cur, nxt = (cnt_ref, off_ref, dst_ref), (ncnt_ref, noff_ref, ndst_ref)

    @pl.when(step == 0)
    def _():
        _for_tile_segments(cur, tm, seg_copy(0), "start")

    @pl.when(step + 1 < pl.num_programs(0))
    def _():
        _for_tile_segments(nxt, tm, seg_copy(1 - slot), "start")

    _for_tile_segments(cur, tm, seg_copy(slot), "wait")

    def weighted_sum(t, carry):
        acc = gw_ref[0, t] * cbuf[slot, lpos_ref[0, t]]
        for kk in range(1, TOP_K):
            acc = acc + gw_ref[kk, t] * cbuf[slot, lpos_ref[kk, t]]
        msum[t] = acc
        return carry
    lax.fori_loop(0, tm, weighted_sum, 0, unroll=8)

    moe = jnp.concatenate([msum[:, c, :] for c in range(ROW_CHUNKS)], axis=1)
    x2 = x1_ref[...] + moe
    r = _rms(x2, gp_ref[...]).astype(bf16)
    gate = jax.nn.sigmoid(jnp.dot(r, wpg_ref[...], preferred_element_type=f32))
    emb = jnp.dot(p_ref[...].astype(bf16), wpp_ref[...], preferred_element_type=f32)
    x3 = x2 + gate * emb
    o_ref[...] = x3 if gf_ref is None else _rms(x3, gf_ref[...])


def combine_ple(lpos, gate_w, seg_cnt, seg_off, seg_dst, x1, ys_tiles, p2, g_ple, w_pg, w_pp, g_final,
                *, tm):
    n, d = x1.shape
    steps = n // tm
    smem = pltpu.SMEM
    full = lambda a: pl.BlockSpec(a.shape, lambda i: (0,) * a.ndim)
    seg = lambda shift: pl.BlockSpec((1, 1, N_EXPERTS),
                                     lambda i: (jnp.minimum(i + shift, steps - 1), 0, 0), memory_space=smem)
    tok = pl.BlockSpec((TOP_K, tm), lambda i: (0, i), memory_space=smem)
    consts = [g_ple.reshape(1, d), w_pg, w_pp] + ([] if g_final is None else [g_final.reshape(1, d)])
    return pl.pallas_call(
        _combine_ple_body,
        grid=(steps,),
        in_specs=[tok, tok, seg(0), seg(0), seg(0), seg(1), seg(1), seg(1),
                  pl.BlockSpec((tm, d), lambda i: (i, 0)),
                  pl.BlockSpec(memory_space=pl.ANY),
                  pl.BlockSpec((tm, p2.shape[1]), lambda i: (i, 0))] + [full(a) for a in consts],
        out_specs=pl.BlockSpec((tm, d), lambda i: (i, 0)),
        out_shape=jax.ShapeDtypeStruct((n, d), jnp.float32),
        scratch_shapes=[pltpu.VMEM((2, TOP_K * tm, ROW_CHUNKS, LANES), ys_tiles.dtype),
                        pltpu.VMEM((tm, ROW_CHUNKS, LANES), jnp.float32),
                        pltpu.SemaphoreType.DMA((2,))],
        compiler_params=_params("arbitrary"),
        name="combine_ple",
    )(lpos, gate_w, seg_cnt, seg_off, seg_dst, seg_cnt, seg_off, seg_dst, x1, ys_tiles, p2, *consts)


def _moe(h2_tiles, lpos, tile_cnt, w_gu, b_gu, w_d, b_d, *, tm):
    n = h2_tiles.shape[0] // ROW_CHUNKS
    rb = MOE_ROW_BLOCK
    i32 = jnp.int32
    n_blocks = -(-n * TOP_K // rb) + N_EXPERTS
    n_rows = n_blocks * rb
    counts = jnp.sum(tile_cnt, axis=0)
    padded = (counts + rb - 1) // rb * rb
    pend = jnp.cumsum(padded)
    pstart = pend - padded
    blk_start = jnp.arange(n_blocks, dtype=i32) * rb
    blk_expert = jnp.minimum(jnp.sum((pend[None, :] <= blk_start[:, None]).astype(i32), axis=1),
                             N_EXPERTS - 1)
    n_used = pend[-1:] // rb
    seg_cnt = tile_cnt[:, None, :]
    seg_off = (jnp.cumsum(tile_cnt, axis=1) - tile_cnt)[:, None, :]
    seg_dst = (pstart[None, :] + jnp.cumsum(tile_cnt, axis=0) - tile_cnt)[:, None, :]
    tail = n_used + jnp.arange(N_EXPERTS, dtype=i32)
    pad_len = jnp.concatenate([padded - counts, jnp.where(tail < n_blocks, rb, 0)])[None, :]
    pad_row = jnp.concatenate([pstart + counts, tail * rb])[None, :]
    b_g, b_u = b_gu[:, None, 0::2], b_gu[:, None, 1::2]
    xs = dispatch(lpos, seg_cnt, seg_off, seg_dst, pad_len.astype(i32), pad_row.astype(i32),
                  h2_tiles.reshape(n, ROW_CHUNKS, LANES), n_rows, tc=tm, rb=rb)
    e_ids = jnp.arange(N_EXPERTS, dtype=i32)
    later = (padded > 0)[None, :] & (e_ids[None, :] > e_ids[:, None])
    nxt_tab = jnp.min(jnp.where(later, e_ids[None, :], N_EXPERTS), axis=1)
    nxt_tab = jnp.where(nxt_tab == N_EXPERTS, -1, nxt_tab)
    nxt_expert = jnp.sum(jnp.where(blk_expert[:, None] == e_ids[None, :], nxt_tab[None, :], 0), axis=1)
    ys = experts(blk_expert, nxt_expert.astype(i32), n_used.astype(i32),
                 xs.reshape(n_rows * ROW_CHUNKS, LANES), w_gu, b_g, b_u, w_d, b_d[:, None, :])
    return (seg_cnt, seg_off, seg_dst), ys.reshape(n_rows, ROW_CHUNKS, LANES)


def kernel(x, p, norm_mix_g, w_in, conv_w, conv_b, w_qk_m, w_if, b_if, mnorm_g, w_branch, w_out, norm_ffn_g, w_router, b_router, w_gate_up, b_gate_up, w_down, b_down, norm_ple_g, w_ple_gate, w_ple_proj, final_norm_g):
    B, S, D = x.shape
    depth = w_in.shape[0]
    bf16 = jnp.bfloat16
    x2 = x.reshape(B * S, D)
    for i in range(depth):
        qkv, xo, gates = in_proj(x2, norm_mix_g[i], w_in[i].astype(bf16))
        y_a = moba(qkv, B, S)
        y_m = mlstm(xo, conv_w[i], conv_b[i], w_qk_m[i], w_if[i], b_if[i], mnorm_g[i], B, S)
        x1, h2_tiles, gate_w, lpos, tile_cnt = merge_route(
            x2, y_a, y_m, gates, w_branch[i, 0].astype(bf16), w_branch[i, 1].astype(bf16),
            w_out[i].astype(bf16), norm_ffn_g[i], w_router[i], b_router[i], tm=MOE_TOKEN_TILE)
        segs, ys = _moe(h2_tiles, lpos, tile_cnt[:, :, 0], w_gate_up[i], b_gate_up[i],
                        w_down[i], b_down[i], tm=MOE_TOKEN_TILE)
        x2 = combine_ple(lpos, gate_w, *segs, x1, ys, p[i].reshape(B * S, -1), norm_ple_g[i],
                         w_ple_gate[i].astype(bf16), w_ple_proj[i].astype(bf16),
                         final_norm_g if i == depth - 1 else None, tm=MOE_TOKEN_TILE)
    return x2.reshape(B, S, D)
```

```python
import functools

import jax
import jax.numpy as jnp
import numpy as np
from jax import lax
from jax.experimental import pallas as pl
from jax.experimental.pallas import tpu as pltpu

RMS_EPS = 1e-6
LANES = 128
ROW_CHUNKS = 8
VMEM_LIMIT = 56 * 1024 * 1024

N_ATT_HEADS = 8
ATT_HEAD_DIM = 64
ATT_WIDTH = N_ATT_HEADS * ATT_HEAD_DIM
MOBA_BLOCK = 256
MOBA_TOPK = 3

N_MLSTM_HEADS = 4
MLSTM_WIDTH = 512
MLSTM_V_DIM = 128
MLSTM_QK_DIM = 64
MLSTM_CONV = 4
MLSTM_CHUNK = 256

N_EXPERTS = 32
TOP_K = 4
SWIGLU_ALPHA = 1.702
SWIGLU_LIMIT = 7.0
MOE_ROW_BLOCK = 512
MOE_TOKEN_TILE = 512

NEG_BIG = -1e30

_NT = (((1,), (1,)), ((), ()))
_TN = (((0,), (0,)), ((), ()))


def _params(*sem):
    return pltpu.CompilerParams(dimension_semantics=sem, vmem_limit_bytes=VMEM_LIMIT)


def _rms(x, g):
    return x * lax.rsqrt(jnp.mean(x * x, axis=-1, keepdims=True) + RMS_EPS) * g


def _bf16_parts(x, n):
    parts = []
    for _ in range(n):
        p = x.astype(jnp.bfloat16)
        parts.append(p)
        x = x - p.astype(jnp.float32)
    return parts


def _in_proj_body(x_ref, g_ref, w_ref, qkv_ref, xo_ref, gate_ref, *, col_chunk):
    h = _rms(x_ref[...], g_ref[...]).astype(jnp.bfloat16)
    col = 0
    for out_ref in (qkv_ref, xo_ref, gate_ref):
        for c in range(0, out_ref.shape[1], col_chunk):
            out_ref[:, c:c + col_chunk] = jnp.dot(
                h, w_ref[:, col + c:col + c + col_chunk],
                preferred_element_type=jnp.float32).astype(out_ref.dtype)
        col += out_ref.shape[1]


def in_proj(x2, g, w_bf16, *, tm=512, col_chunk=512):
    n, d = x2.shape
    widths = (3 * ATT_WIDTH, 2 * MLSTM_WIDTH, 2 * d)
    assert sum(widths) == w_bf16.shape[1] and n % tm == 0
    return pl.pallas_call(
        functools.partial(_in_proj_body, col_chunk=col_chunk),
        grid=(n // tm,),
        in_specs=[pl.BlockSpec((tm, d), lambda i: (i, 0)),
                  pl.BlockSpec((1, d), lambda i: (0, 0)),
                  pl.BlockSpec(w_bf16.shape, lambda i: (0, 0))],
        out_specs=[pl.BlockSpec((tm, w), lambda i: (i, 0)) for w in widths],
        out_shape=[jax.ShapeDtypeStruct((n, w), jnp.bfloat16) for w in widths],
        compiler_params=_params("parallel"),
        name="in_proj",
    )(x2, g.reshape(1, d), w_bf16)


def _moba_body(q_ref, k_ref, v_ref, o_ref, *, pair, seq):
    blk = MOBA_BLOCK
    n_blk = seq // blk
    f32, bf16 = jnp.float32, jnp.bfloat16
    lane = lax.broadcasted_iota(jnp.int32, (1, LANES), 1)
    first_head = lane < ATT_HEAD_DIM
    row = lax.broadcasted_iota(jnp.int32, (2 * blk, blk), 0)
    col = lax.broadcasted_iota(jnp.int32, (2 * blk, blk), 1)
    t_in_blk = jnp.where(row >= blk, row - blk, row)
    rel = (t_in_blk - col).astype(f32)
    causal = t_in_blk >= col
    log2e = float(np.log2(np.e))
    scale2 = ATT_HEAD_DIM ** -0.5 * log2e
    head = (2 * pair + (row[:, 0:1] >= blk).astype(jnp.int32)).astype(f32)
    slope2 = jnp.exp2(-8.0 * (head + 1.0) / N_ATT_HEADS) * log2e
    bias = [slope2 * (rel + float(d * blk)) for d in range(n_blk)]
    lane_row = lax.broadcasted_iota(jnp.int32, (LANES, LANES), 0)
    spread = [jnp.where(lane_row == j, 1.0, 0.0).astype(bf16) for j in range(n_blk)]

    k_all = k_ref[...]
    v_all = v_ref[...]
    k_mean = jnp.concatenate(
        [jnp.mean(k_all[j * blk:(j + 1) * blk].astype(f32), axis=0, keepdims=True)
         for j in range(n_blk)] + [jnp.zeros((LANES - n_blk, LANES), f32)], axis=0)
    k_mean_parts = _bf16_parts(k_mean, 3)

    for qi in range(n_blk):
        q_blk = q_ref[qi * blk:(qi + 1) * blk, :]
        zero = jnp.zeros_like(q_blk)
        q2 = jnp.concatenate([jnp.where(first_head, q_blk, zero), jnp.where(first_head, zero, q_blk)], axis=0)
        sel = None
        if qi > MOBA_TOPK:
            g = sum(lax.dot_general(q2, part, _NT, preferred_element_type=f32) for part in k_mean_parts)
            g = jnp.where(lane < qi, g, -jnp.inf)
            ahead = jnp.zeros((2 * blk, LANES), f32)
            for d in range(1, qi):
                ahead = ahead + jnp.where(pltpu.roll(g, d, axis=1) >= g, 1.0, 0.0)
                ahead = ahead + jnp.where(pltpu.roll(g, LANES - d, axis=1) > g, 1.0, 0.0)
            keep = jnp.where(ahead < MOBA_TOPK, 1.0, 0.0).astype(bf16)
            sel = [jnp.dot(keep, spread[j], preferred_element_type=f32) > 0.5 for j in range(qi)]
        logits = []
        for j in range(qi + 1):
            s = lax.dot_general(q2, k_all[j * blk:(j + 1) * blk], _NT, preferred_element_type=f32)
            s = s * scale2 - bias[qi - j]
            if j == qi:
                s = jnp.where(causal, s, NEG_BIG)
            elif sel is not None:
                s = jnp.where(jnp.concatenate([sel[j]] * (blk // LANES), axis=1), s, NEG_BIG)
            logits.append(s)
        s_all = jnp.concatenate(logits, axis=1)
        m = jnp.max(s_all, axis=1, keepdims=True)
        p = jnp.exp2(s_all - m)
        denom = jnp.sum(p, axis=1, keepdims=True)
        o2 = jnp.dot(p.astype(bf16), v_all[:(qi + 1) * blk], preferred_element_type=f32) / denom
        o = jnp.where(first_head, o2[:blk], o2[blk:])
        o_ref[qi * blk:(qi + 1) * blk, :] = o.astype(o_ref.dtype)


def _moba_kernel(q_ref, k_ref, v_ref, o_ref, *, seq):
    _moba_body(q_ref, k_ref, v_ref, o_ref, pair=pl.program_id(1), seq=seq)


def moba(qkv, batch, seq):
    n = batch * seq
    n_pairs = ATT_WIDTH // LANES
    assert seq % MOBA_BLOCK == 0
    spec = lambda off: pl.BlockSpec((seq, LANES), lambda b, p: (b, off + p))
    return pl.pallas_call(
        functools.partial(_moba_kernel, seq=seq),
        grid=(batch, n_pairs),
        in_specs=[spec(0), spec(n_pairs), spec(2 * n_pairs)],
        out_specs=pl.BlockSpec((seq, LANES), lambda b, p: (b, p)),
        out_shape=jax.ShapeDtypeStruct((n, ATT_WIDTH), jnp.bfloat16),
        compiler_params=_params("parallel", "parallel"),
        name="moba",
    )(qkv, qkv, qkv)


def _log_sigmoid(z):
    return jnp.minimum(z, 0.0) - jnp.log(1.0 + jnp.exp(-jnp.abs(z)))


def _mlstm_body(xm_ref, om_ref, cw_ref, cb_ref, wq_ref, wk_ref, wiq_ref, wik_ref, wix_ref,
                wiqt_ref, wikt_ref, wixt_ref, bi_ref, bit_ref, g_ref, y_ref,
                q_sc, k_sc, gcol_sc, grow_sc, ct_sc, n_sc, m_sc, *, seq):
    L = MLSTM_CHUNK
    H = N_MLSTM_HEADS
    n_chunks = seq // L
    f32, bf16 = jnp.float32, jnp.bfloat16

    x = xm_ref[...].astype(f32)
    t_idx = lax.broadcasted_iota(jnp.int32, (seq, 1), 0)
    acc = x * cw_ref[MLSTM_CONV - 1:MLSTM_CONV, :] + cb_ref[...]
    for d in range(1, MLSTM_CONV):
        shifted = jnp.where(t_idx >= d, pltpu.roll(x, d, axis=0), 0.0)
        acc = acc + shifted * cw_ref[MLSTM_CONV - 1 - d:MLSTM_CONV - d, :]
    xc = (acc * jax.nn.sigmoid(acc)).astype(bf16)

    xm = xm_ref[...]
    if_col = jnp.dot(xm, wix_ref[...], preferred_element_type=f32) + bi_ref[...]
    if_row = lax.dot_general(wixt_ref[...], xm, _NT, preferred_element_type=f32) + bit_ref[...]
    for h in range(H):
        xch = xc[:, h * LANES:(h + 1) * LANES]
        q = jnp.dot(xch, wq_ref[h], preferred_element_type=f32).astype(bf16)
        k = jnp.dot(xch, wk_ref[h], preferred_element_type=f32).astype(bf16)
        if_col = if_col + jnp.dot(q, wiq_ref[h], preferred_element_type=f32)
        if_col = if_col + jnp.dot(k, wik_ref[h], preferred_element_type=f32)
        if_row = if_row + lax.dot_general(wiqt_ref[h], q, _NT, preferred_element_type=f32)
        if_row = if_row + lax.dot_general(wikt_ref[h], k, _NT, preferred_element_type=f32)
        q_sc[h] = q
        k_sc[h] = (k.astype(f32) * (MLSTM_QK_DIM ** -0.5)).astype(bf16)
    is_f_col = lax.broadcasted_iota(jnp.int32, (1, 2 * H), 1) >= H
    is_f_row = lax.broadcasted_iota(jnp.int32, (2 * H, 1), 0) >= H
    gcol_sc[...] = jnp.where(is_f_col, _log_sigmoid(if_col), if_col)
    g_row = jnp.where(is_f_row, _log_sigmoid(if_row), if_row)
    for c in range(n_chunks):
        grow_sc[c] = g_row[:, c * L:(c + 1) * L]

    ct_sc[...] = jnp.zeros_like(ct_sc)
    n_sc[...] = jnp.zeros_like(n_sc)
    m_sc[...] = jnp.zeros_like(m_sc)

    r_i = lax.broadcasted_iota(jnp.int32, (L, L), 0)
    c_i = lax.broadcasted_iota(jnp.int32, (L, L), 1)
    causal = r_i >= c_i
    tri_lo = jnp.where(causal, 1.0, 0.0).astype(f32)
    tri_up = jnp.where(c_i >= r_i, 1.0, 0.0).astype(f32)

    def chunk(c, carry):
        r0 = pl.multiple_of(c * L, L)
        gc = gcol_sc[pl.ds(r0, L), :]
        gr = grow_sc[c]
        b_cols = jnp.dot(tri_lo, gc, precision=lax.Precision.HIGHEST, preferred_element_type=f32)
        b_rows = jnp.dot(gr, tri_up, precision=lax.Precision.HIGHEST, preferred_element_type=f32)
        for h in range(H):
            hs = slice(h * LANES, (h + 1) * LANES)
            b_col = b_cols[:, H + h:H + h + 1]
            i_col = gc[:, h:h + 1]
            b_row = b_rows[H + h:H + h + 1, :]
            i_row = gr[h:h + 1, :]
            b_last = b_col[L - 1:L, :]
            m_prev = m_sc[h][:, 0:1]
            q = q_sc[h, pl.ds(r0, L), :]
            k = k_sc[h, pl.ds(r0, L), :]
            v = xm_ref[pl.ds(r0, L), hs]

            log_d = jnp.where(causal, b_col + (i_row - b_row), NEG_BIG)
            inter = b_col + m_prev
            m_t = jnp.maximum(inter, jnp.max(log_d, axis=1, keepdims=True))
            w_inter = jnp.exp(inter - m_t)
            s = lax.dot_general(q, k, _NT, preferred_element_type=f32) * jnp.exp(log_d - m_t)
            num = jnp.dot(s.astype(bf16), v, preferred_element_type=f32)
            num = num + w_inter * jnp.dot(q, ct_sc[h].astype(bf16), preferred_element_type=f32)
            den = jnp.sum(s, axis=1, keepdims=True)
            den = den + w_inter * jnp.sum(q.astype(f32) * n_sc[h], axis=1, keepdims=True)
            hv = num / jnp.maximum(jnp.abs(den), jnp.exp(-m_t))
            hv = hv * lax.rsqrt(jnp.mean(hv * hv, axis=1, keepdims=True) + RMS_EPS) * g_ref[:, hs]
            o_gate = jax.nn.sigmoid(om_ref[pl.ds(r0, L), hs].astype(f32))
            y_ref[pl.ds(r0, L), hs] = (o_gate * hv).astype(y_ref.dtype)

            g_col = b_last - b_col + i_col
            m_new = jnp.maximum(b_last + m_prev, jnp.max(g_col, axis=0, keepdims=True))
            wg = jnp.exp(g_col - m_new)
            decay = jnp.exp(b_last + m_prev - m_new)
            wv = (wg * v.astype(f32)).astype(bf16)
            ct_sc[h] = decay * ct_sc[h] + lax.dot_general(k, wv, _TN, preferred_element_type=f32)
            n_sc[h] = decay * n_sc[h] + jnp.sum(wg * k.astype(f32), axis=0, keepdims=True)
            m_sc[h] = jnp.broadcast_to(m_new, (1, LANES))
        return carry

    lax.fori_loop(0, n_chunks, chunk, 0)


def _mlstm_kernel(*refs, seq):
    _mlstm_body(*refs, seq=seq)


def mlstm(xo, conv_w, conv_b, w_qk, w_if, b_if, g, batch, seq):
    n = batch * seq
    H, dv, dk = N_MLSTM_HEADS, MLSTM_V_DIM, MLSTM_QK_DIM
    assert seq % MLSTM_CHUNK == 0 and dv == LANES
    bf16 = jnp.bfloat16
    pad = ((0, 0), (0, 0), (0, LANES - dk))
    wq = jnp.pad(w_qk[:, :, :dk], pad).astype(bf16)
    wk = jnp.pad(w_qk[:, :, dk:], pad).astype(bf16)
    rpad = ((0, 0), (0, LANES - dk), (0, 0))
    wiq = jnp.pad(w_if[:H * dk].reshape(H, dk, 2 * H), rpad).astype(bf16)
    wik = jnp.pad(w_if[H * dk:2 * H * dk].reshape(H, dk, 2 * H), rpad).astype(bf16)
    wix = w_if[2 * H * dk:].astype(bf16)
    wiqt, wikt, wixt = wiq.transpose(0, 2, 1), wik.transpose(0, 2, 1), wix.T
    full = lambda a: pl.BlockSpec(a.shape, lambda b: (0,) * a.ndim)
    consts = [conv_w, conv_b.reshape(1, -1), wq, wk, wiq, wik, wix, wiqt, wikt, wixt,
              b_if.reshape(1, -1), b_if.reshape(-1, 1), g.reshape(1, -1)]
    return pl.pallas_call(
        functools.partial(_mlstm_kernel, seq=seq),
        grid=(batch,),
        in_specs=[pl.BlockSpec((seq, MLSTM_WIDTH), lambda b: (b, 0)),
                  pl.BlockSpec((seq, MLSTM_WIDTH), lambda b: (b, 1))] + [full(a) for a in consts],
        out_specs=pl.BlockSpec((seq, MLSTM_WIDTH), lambda b: (b, 0)),
        out_shape=jax.ShapeDtypeStruct((n, MLSTM_WIDTH), bf16),
        scratch_shapes=[pltpu.VMEM((H, seq, LANES), bf16), pltpu.VMEM((H, seq, LANES), bf16),
                        pltpu.VMEM((seq, 2 * H), jnp.float32),
                        pltpu.VMEM((seq // MLSTM_CHUNK, 2 * H, MLSTM_CHUNK), jnp.float32),
                        pltpu.VMEM((H, LANES, LANES), jnp.float32),
                        pltpu.VMEM((H, 1, LANES), jnp.float32),
                        pltpu.VMEM((H, 1, LANES), jnp.float32)],
        compiler_params=_params("parallel"),
        name="mlstm",
    )(xo, xo, *consts)


def _store_rows_as_tiles(dst_ref, val):
    rows = val.shape[0]
    for c in range(ROW_CHUNKS):
        dst_ref[pl.ds(c, rows, stride=ROW_CHUNKS), :] = val[:, c * LANES:(c + 1) * LANES]


def _load_tiles_as_rows(src_ref, rows, lead=()):
    return jnp.concatenate(
        [src_ref[lead + (pl.ds(c, rows, stride=ROW_CHUNKS), slice(None))] for c in range(ROW_CHUNKS)],
        axis=1)


def _merge_route_body(x_ref, ya_ref, ym_ref, ga_ref, gm_ref, wb0_ref, wb1_ref, wo_ref, g_ref,
                      wr_ref, br_ref, x1_ref, h2_ref, gate_ref, lpos_ref, cnt_ref):
    f32, bf16 = jnp.float32, jnp.bfloat16
    tm = x_ref.shape[0]

    u = jax.nn.sigmoid(ga_ref[...].astype(f32)) * jnp.dot(ya_ref[...], wb0_ref[...], preferred_element_type=f32)
    u = u + jax.nn.sigmoid(gm_ref[...].astype(f32)) * jnp.dot(ym_ref[...], wb1_ref[...], preferred_element_type=f32)
    x1 = x_ref[...] + jnp.dot(u.astype(bf16), wo_ref[...], preferred_element_type=f32)
    x1_ref[...] = x1
    h2 = _rms(x1, g_ref[...])
    _store_rows_as_tiles(h2_ref, h2)

    (h_hi, h_lo), (w_hi, w_lo) = _bf16_parts(h2, 2), _bf16_parts(wr_ref[...], 2)
    logits = (jnp.dot(h_hi, w_hi, preferred_element_type=f32) + jnp.dot(h_lo, w_hi, preferred_element_type=f32)
              + jnp.dot(h_hi, w_lo, preferred_element_type=f32)) + br_ref[...]
    logits = jnp.transpose(logits)[:N_EXPERTS]
    e_id = lax.broadcasted_iota(jnp.int32, logits.shape, 0).astype(f32)
    chosen = jnp.zeros(logits.shape, f32)
    vals, ids = [], []
    for _ in range(TOP_K):
        top = jnp.max(logits, axis=0, keepdims=True)
        first = jnp.min(jnp.where(logits == top, e_id, float(N_EXPERTS)), axis=0, keepdims=True)
        hit = e_id == first
        chosen = jnp.where(hit, 1.0, chosen)
        logits = jnp.where(hit, -jnp.inf, logits)
        vals.append(top)
        ids.append(first)
    ex = [jnp.exp(v - vals[0]) for v in vals]
    total = ex[0] + ex[1] + ex[2] + ex[3]

    chosen_b = chosen.astype(bf16)
    t_r = lax.broadcasted_iota(jnp.int32, (tm, tm), 0)
    t_c = lax.broadcasted_iota(jnp.int32, (tm, tm), 1)
    earlier = jnp.where(t_r < t_c, 1.0, 0.0).astype(bf16)
    before = jnp.dot(chosen_b, earlier, preferred_element_type=f32)
    e_r = lax.broadcasted_iota(jnp.int32, (N_EXPERTS, N_EXPERTS), 0)
    e_c = lax.broadcasted_iota(jnp.int32, (N_EXPERTS, N_EXPERTS), 1)
    lower = jnp.where(e_c < e_r, 1.0, 0.0).astype(bf16)
    seg_off = jnp.sum(jnp.dot(lower, chosen_b, preferred_element_type=f32), axis=1, keepdims=True)
    pos = before + seg_off
    for kk in range(TOP_K):
        gate_ref[kk:kk + 1, :] = ex[kk] / total
        lpos_ref[kk:kk + 1, :] = jnp.sum(jnp.where(e_id == ids[kk], pos, 0.0), axis=0,
                                         keepdims=True).astype(jnp.int32)
    counts = jnp.sum(chosen, axis=1, keepdims=True)
    cnt_ref[0] = jnp.broadcast_to(counts, (N_EXPERTS, LANES)).astype(jnp.int32)


def merge_route(x2, ya, ym, gates, wb0, wb1, wo, g, w_router, b_router, *, tm):
    n, d = x2.shape
    e = w_router.shape[1]
    assert n % tm == 0 and d == ROW_CHUNKS * LANES and e == N_EXPERTS
    full = lambda a: pl.BlockSpec(a.shape, lambda i: (0,) * a.ndim)
    lane_pad = ((0, 0), (0, LANES - e))
    consts = [wb0, wb1, wo, g.reshape(1, d), jnp.pad(w_router, lane_pad),
              jnp.pad(b_router.reshape(1, e), lane_pad)]
    tok = lambda dt: jax.ShapeDtypeStruct((TOP_K, n), dt)
    return pl.pallas_call(
        _merge_route_body,
        grid=(n // tm,),
        in_specs=[pl.BlockSpec((tm, d), lambda i: (i, 0)),
                  pl.BlockSpec((tm, ATT_WIDTH), lambda i: (i, 0)),
                  pl.BlockSpec((tm, MLSTM_WIDTH), lambda i: (i, 0)),
                  pl.BlockSpec((tm, d), lambda i: (i, 0)),
                  pl.BlockSpec((tm, d), lambda i: (i, 1))] + [full(a) for a in consts],
        out_specs=[pl.BlockSpec((tm, d), lambda i: (i, 0)),
                   pl.BlockSpec((tm * ROW_CHUNKS, LANES), lambda i: (i, 0)),
                   pl.BlockSpec((TOP_K, tm), lambda i: (0, i)),
                   pl.BlockSpec((TOP_K, tm), lambda i: (0, i)),
                   pl.BlockSpec((1, e, LANES), lambda i: (i, 0, 0))],
        out_shape=[jax.ShapeDtypeStruct((n, d), jnp.float32),
                   jax.ShapeDtypeStruct((n * ROW_CHUNKS, LANES), jnp.float32),
                   tok(jnp.float32), tok(jnp.int32),
                   jax.ShapeDtypeStruct((n // tm, e, LANES), jnp.int32)],
        compiler_params=_params("parallel"),
        name="merge_route",
    )(x2, ya, ym, gates, gates, *consts)


def _segment_pieces(cnt, max_rows, fn):
    done = 0
    size = 1 << (max_rows.bit_length() - 1)
    while size:
        bit = cnt & size

        def piece(done=done, size=size):
            fn(done, size)
        pl.when(bit != 0)(piece)
        done = done + bit
        size >>= 1


def _for_tile_segments(meta, tile_rows, make_copy, action):
    cnt_ref, off_ref, dst_ref = meta

    def per_expert(e, carry):
        off, dst = off_ref[0, 0, e], dst_ref[0, 0, e]
        _segment_pieces(cnt_ref[0, 0, e], tile_rows,
                        lambda done, size: getattr(make_copy(off + done, dst + done, size), action)())
        return carry
    lax.fori_loop(0, N_EXPERTS, per_expert, 0)


def _dispatch_body(lpos_ref, cnt_ref, off_ref, dst_ref, pad_len_ref, pad_row_ref, x_ref, out_ref,
                   cbuf, zbuf, sem, zsem, *, tc, rb):
    step = pl.program_id(0)
    last = pl.num_programs(0) - 1
    slot = step % 2

    def compact(t, carry):
        row = x_ref[t]
        for kk in range(TOP_K):
            cbuf[slot, lpos_ref[kk, t]] = row
        return carry
    lax.fori_loop(0, tc, compact, 0, unroll=8)

    def seg_copy(sl):
        return lambda local, glob, size: pltpu.make_async_copy(
            cbuf.at[sl, pl.ds(local, size)], out_ref.at[pl.ds(glob, size)], sem.at[sl])

    def wait_tile(sl):
        pltpu.make_async_copy(cbuf.at[sl], out_ref.at[pl.ds(0, TOP_K * tc)], sem.at[sl]).wait()

    _for_tile_segments((cnt_ref, off_ref, dst_ref), tc, seg_copy(slot), "start")

    @pl.when(step > 0)
    def _():
        wait_tile(1 - slot)

    @pl.when(step == last)
    def _():
        wait_tile(slot)
        zbuf[...] = jnp.zeros_like(zbuf)

        def pads(action):
            def per_pad(j, carry):
                row = pad_row_ref[0, j]
                _segment_pieces(pad_len_ref[0, j], rb, lambda done, size: getattr(
                    pltpu.make_async_copy(zbuf.at[pl.ds(0, size)], out_ref.at[pl.ds(row + done, size)], zsem),
                    action)())
                return carry
            lax.fori_loop(0, pad_len_ref.shape[1], per_pad, 0)
        pads("start")
        pads("wait")


def dispatch(lpos, seg_cnt, seg_off, seg_dst, pad_len, pad_row, h2_tiles, n_rows, *, tc, rb):
    n = h2_tiles.shape[0]
    steps = n // tc
    assert n % tc == 0
    smem = pltpu.SMEM
    seg = pl.BlockSpec((1, 1, N_EXPERTS), lambda i: (i, 0, 0), memory_space=smem)
    whole = lambda a: pl.BlockSpec(a.shape, lambda i: (0,) * a.ndim, memory_space=smem)
    return pl.pallas_call(
        functools.partial(_dispatch_body, tc=tc, rb=rb),
        grid=(steps,),
        in_specs=[pl.BlockSpec((TOP_K, tc), lambda i: (0, i), memory_space=smem),
                  seg, seg, seg, whole(pad_len), whole(pad_row),
                  pl.BlockSpec((tc, ROW_CHUNKS, LANES), lambda i: (i, 0, 0))],
        out_specs=pl.BlockSpec(memory_space=pl.ANY),
        out_shape=jax.ShapeDtypeStruct((n_rows, ROW_CHUNKS, LANES), h2_tiles.dtype),
        scratch_shapes=[pltpu.VMEM((2, TOP_K * tc, ROW_CHUNKS, LANES), h2_tiles.dtype),
                        pltpu.VMEM((rb, ROW_CHUNKS, LANES), h2_tiles.dtype),
                        pltpu.SemaphoreType.DMA((2,)), pltpu.SemaphoreType.DMA(())],
        compiler_params=_params("arbitrary"),
        name="dispatch",
    )(lpos, seg_cnt, seg_off, seg_dst, pad_len, pad_row, h2_tiles)


def _experts_body(blk_e_ref, nxt_e_ref, used_ref, x_ref, wgu_hbm, bg_ref, bu_ref, wd_hbm, bd_ref, perm_ref,
                  y_ref, wgu_st, wd_st, wg_sc, wu_sc, wd_sc, sem, *, halves):
    f32, bf16 = jnp.float32, jnp.bfloat16
    step = pl.program_id(0)
    live = step < used_ref[0]
    rows = x_ref.shape[0] // ROW_CHUNKS
    expert = blk_e_ref[step]

    def fetch(e, action):
        getattr(pltpu.make_async_copy(wgu_hbm.at[e], wgu_st, sem.at[0]), action)()
        getattr(pltpu.make_async_copy(wd_hbm.at[e], wd_st, sem.at[1]), action)()

    @pl.when(live & ((step == 0) | (expert != blk_e_ref[jnp.maximum(step - 1, 0)])))
    def _():
        @pl.when(step == 0)
        def _():
            fetch(expert, "start")
        fetch(expert, "wait")
        tile = perm_ref.shape[0]
        half = tile // 2
        for t in range(wgu_st.shape[1] // tile):
            w = wgu_st[:, t * tile:(t + 1) * tile].astype(bf16)
            o = jnp.dot(w, perm_ref[...], preferred_element_type=f32)
            wg_sc[:, t * half:(t + 1) * half] = o[:, :half].astype(bf16)
            wu_sc[:, t * half:(t + 1) * half] = o[:, half:].astype(bf16)
        wd_sc[...] = wd_st[...].astype(bf16)
        nxt = nxt_e_ref[step]

        @pl.when(nxt >= 0)
        def _():
            fetch(nxt, "start")

    @pl.when(live)
    def _():
        part = rows // halves
        for h in range(halves):
            xb = jnp.concatenate(
                [x_ref[pl.ds(h * part * ROW_CHUNKS + c, part, stride=ROW_CHUNKS), :]
                 for c in range(ROW_CHUNKS)], axis=1).astype(bf16)
            gate = jnp.dot(xb, wg_sc[...], preferred_element_type=f32) + bg_ref[0]
            up = jnp.dot(xb, wu_sc[...], preferred_element_type=f32) + bu_ref[0]
            gate = jnp.minimum(gate, SWIGLU_LIMIT)
            up = jnp.clip(up, -SWIGLU_LIMIT, SWIGLU_LIMIT)
            act = (up + 1.0) * (gate * jax.nn.sigmoid(gate * SWIGLU_ALPHA))
            y = jnp.dot(act.astype(bf16), wd_sc[...], preferred_element_type=f32) + bd_ref[0]
            for c in range(ROW_CHUNKS):
                y_ref[pl.ds(h * part * ROW_CHUNKS + c, part, stride=ROW_CHUNKS), :] = \
                    y[:, c * LANES:(c + 1) * LANES]

    @pl.when(jnp.logical_not(live))
    def _():
        y_ref[...] = jnp.zeros_like(y_ref)


def experts(blk_expert, nxt_expert, n_used, xs_tiles, w_gu, b_g, b_u, w_d, b_d, *, halves=2):
    n_blocks = blk_expert.shape[0]
    rb = MOE_ROW_BLOCK
    d, ff = w_d.shape[2], w_d.shape[1]
    tile = 2 * LANES
    perm = np.zeros((tile, tile), np.float32)
    perm[2 * np.arange(LANES), np.arange(LANES)] = 1.0
    perm[2 * np.arange(LANES) + 1, LANES + np.arange(LANES)] = 1.0
    e_map3 = lambda i, be, nx, nu: (be[i], 0, 0)
    rows_map = lambda i, be, nx, nu: (i, 0)
    grid_spec = pltpu.PrefetchScalarGridSpec(
        num_scalar_prefetch=3,
        grid=(n_blocks,),
        in_specs=[pl.BlockSpec((rb * ROW_CHUNKS, LANES), rows_map),
                  pl.BlockSpec(memory_space=pl.ANY),
                  pl.BlockSpec((1, 1, ff), e_map3), pl.BlockSpec((1, 1, ff), e_map3),
                  pl.BlockSpec(memory_space=pl.ANY), pl.BlockSpec((1, 1, d), e_map3),
                  pl.BlockSpec((tile, tile), lambda i, be, nx, nu: (0, 0))],
        out_specs=pl.BlockSpec((rb * ROW_CHUNKS, LANES), rows_map),
        scratch_shapes=[pltpu.VMEM((d, 2 * ff), w_gu.dtype), pltpu.VMEM((ff, d), w_d.dtype),
                        pltpu.VMEM((d, ff), jnp.bfloat16), pltpu.VMEM((d, ff), jnp.bfloat16),
                        pltpu.VMEM((ff, d), jnp.bfloat16), pltpu.SemaphoreType.DMA((2,))],
    )
    return pl.pallas_call(
        functools.partial(_experts_body, halves=halves),
        grid_spec=grid_spec,
        out_shape=jax.ShapeDtypeStruct(xs_tiles.shape, jnp.float32),
        compiler_params=_params("arbitrary"),
        name="experts",
    )(blk_expert, nxt_expert, n_used, xs_tiles, w_gu, b_g, b_u, w_d, b_d, jnp.asarray(perm, jnp.bfloat16))


def _combine_ple_body(lpos_ref, gw_ref, cnt_ref, off_ref, dst_ref, ncnt_ref, noff_ref, ndst_ref,
                      x1_ref, ys_ref, p_ref, gp_ref, wpg_ref, wpp_ref, *rest):
    cbuf, msum, sem = rest[-3:]
    rest = rest[:-3]
    gf_ref, o_ref = rest if len(rest) == 2 else (None, rest[0])
    f32, bf16 = jnp.float32, jnp.bfloat16
    tm = x1_ref.shape[0]
    step = pl.program_id(0)
    slot = step % 2

    def seg_copy(sl):
        return lambda local, glob, size: pltpu.make_async_copy(
            ys_ref.at[pl.ds(glob, size)], cbuf.at[sl, pl.ds(local, size)], sem.at[sl])

    cur, nxt = (cnt_ref, off_ref, dst_ref), (ncnt_ref, noff_ref, ndst_ref)

    @pl.when(step == 0)
    def _():
        _for_tile_segments(cur, tm, seg_copy(0), "start")

    @pl.when(step + 1 < pl.num_programs(0))
    def _():
        _for_tile_segments(nxt, tm, seg_copy(1 - slot), "start")

    pltpu.make_async_copy(ys_ref.at[pl.ds(0, TOP_K * tm)], cbuf.at[slot], sem.at[slot]).wait()

    def weighted_sum(t, carry):
        acc = gw_ref[0, t] * cbuf[slot, lpos_ref[0, t]]
        for kk in range(1, TOP_K):
            acc = acc + gw_ref[kk, t] * cbuf[slot, lpos_ref[kk, t]]
        msum[t] = acc
        return carry
    lax.fori_loop(0, tm, weighted_sum, 0, unroll=8)

    moe = jnp.concatenate([msum[:, c, :] for c in range(ROW_CHUNKS)], axis=1)
    x2 = x1_ref[...] + moe
    r = _rms(x2, gp_ref[...]).astype(bf16)
    gate = jax.nn.sigmoid(jnp.dot(r, wpg_ref[...], preferred_element_type=f32))
    emb = jnp.dot(p_ref[...].astype(bf16), wpp_ref[...], preferred_element_type=f32)
    x3 = x2 + gate * emb
    o_ref[...] = x3 if gf_ref is None else _rms(x3, gf_ref[...])


def combine_ple(lpos, gate_w, seg_cnt, seg_off, seg_dst, x1, ys_tiles, p2, g_ple, w_pg, w_pp, g_final,
                *, tm):
    n, d = x1.shape
    steps = n // tm
    smem = pltpu.SMEM
    full = lambda a: pl.BlockSpec(a.shape, lambda i: (0,) * a.ndim)
    seg = lambda shift: pl.BlockSpec((1, 1, N_EXPERTS),
                                     lambda i: (jnp.minimum(i + shift, steps - 1), 0, 0), memory_space=smem)
    tok = pl.BlockSpec((TOP_K, tm), lambda i: (0, i), memory_space=smem)
    consts = [g_ple.reshape(1, d), w_pg, w_pp] + ([] if g_final is None else [g_final.reshape(1, d)])
    return pl.pallas_call(
        _combine_ple_body,
        grid=(steps,),
        in_specs=[tok, tok, seg(0), seg(0), seg(0), seg(1), seg(1), seg(1),
                  pl.BlockSpec((tm, d), lambda i: (i, 0)),
                  pl.BlockSpec(memory_space=pl.ANY),
                  pl.BlockSpec((tm, p2.shape[1]), lambda i: (i, 0))] + [full(a) for a in consts],
        out_specs=pl.BlockSpec((tm, d), lambda i: (i, 0)),
        out_shape=jax.ShapeDtypeStruct((n, d), jnp.float32),
        scratch_shapes=[pltpu.VMEM((2, TOP_K * tm, ROW_CHUNKS, LANES), ys_tiles.dtype),
                        pltpu.VMEM((tm, ROW_CHUNKS, LANES), jnp.float32),
                        pltpu.SemaphoreType.DMA((2,))],
        compiler_params=_params("arbitrary"),
        name="combine_ple",
    )(lpos, gate_w, seg_cnt, seg_off, seg_dst, seg_cnt, seg_off, seg_dst, x1, ys_tiles, p2, *consts)


def _moe(h2_tiles, lpos, tile_cnt, w_gu, b_gu, w_d, b_d, *, tm):
    n = h2_tiles.shape[0] // ROW_CHUNKS
    rb = MOE_ROW_BLOCK
    i32 = jnp.int32
    n_blocks = -(-n * TOP_K // rb) + N_EXPERTS
    n_rows = n_blocks * rb
    counts = jnp.sum(tile_cnt, axis=0)
    padded = (counts + rb - 1) // rb * rb
    pend = jnp.cumsum(padded)
    pstart = pend - padded
    blk_start = jnp.arange(n_blocks, dtype=i32) * rb
    blk_expert = jnp.minimum(jnp.sum((pend[None, :] <= blk_start[:, None]).astype(i32), axis=1),
                             N_EXPERTS - 1)
    n_used = pend[-1:] // rb
    seg_cnt = tile_cnt[:, None, :]
    seg_off = (jnp.cumsum(tile_cnt, axis=1) - tile_cnt)[:, None, :]
    seg_dst = (pstart[None, :] + jnp.cumsum(tile_cnt, axis=0) - tile_cnt)[:, None, :]
    tail = n_used + jnp.arange(N_EXPERTS, dtype=i32)
    pad_len = jnp.concatenate([padded - counts, jnp.where(tail < n_blocks, rb, 0)])[None, :]
    pad_row = jnp.concatenate([pstart + counts, tail * rb])[None, :]
    b_g, b_u = b_gu[:, None, 0::2], b_gu[:, None, 1::2]
    xs = dispatch(lpos, seg_cnt, seg_off, seg_dst, pad_len.astype(i32), pad_row.astype(i32),
                  h2_tiles.reshape(n, ROW_CHUNKS, LANES), n_rows, tc=tm, rb=rb)
    e_ids = jnp.arange(N_EXPERTS, dtype=i32)
    later = (padded > 0)[None, :] & (e_ids[None, :] > e_ids[:, None])
    nxt_tab = jnp.min(jnp.where(later, e_ids[None, :], N_EXPERTS), axis=1)
    nxt_tab = jnp.where(nxt_tab == N_EXPERTS, -1, nxt_tab)
    nxt_expert = jnp.sum(jnp.where(blk_expert[:, None] == e_ids[None, :], nxt_tab[None, :], 0), axis=1)
    ys = experts(blk_expert, nxt_expert.astype(i32), n_used.astype(i32),
                 xs.reshape(n_rows * ROW_CHUNKS, LANES), w_gu, b_g, b_u, w_d, b_d[:, None, :])
    return (seg_cnt, seg_off, seg_dst), ys.reshape(n_rows, ROW_CHUNKS, LANES)


def kernel(x, p, norm_mix_g, w_in, conv_w, conv_b, w_qk_m, w_if, b_if, mnorm_g, w_branch, w_out, norm_ffn_g, w_router, b_router, w_gate_up, b_gate_up, w_down, b_down, norm_ple_g, w_ple_gate, w_ple_proj, final_norm_g):
    B, S, D = x.shape
    depth = w_in.shape[0]
    bf16 = jnp.bfloat16
    x2 = x.reshape(B * S, D)
    for i in range(depth):
        qkv, xo, gates = in_proj(x2, norm_mix_g[i], w_in[i].astype(bf16))
        y_a = moba(qkv, B, S)
        y_m = mlstm(xo, conv_w[i], conv_b[i], w_qk_m[i], w_if[i], b_if[i], mnorm_g[i], B, S)
        x1, h2_tiles, gate_w, lpos, tile_cnt = merge_route(
            x2, y_a, y_m, gates, w_branch[i, 0].astype(bf16), w_branch[i, 1].astype(bf16),
            w_out[i].astype(bf16), norm_ffn_g[i], w_router[i], b_router[i], tm=MOE_TOKEN_TILE)
        segs, ys = _moe(h2_tiles, lpos, tile_cnt[:, :, 0], w_gate_up[i], b_gate_up[i],
                        w_down[i], b_down[i], tm=MOE_TOKEN_TILE)
        x2 = combine_ple(lpos, gate_w, *segs, x1, ys, p[i].reshape(B * S, -1), norm_ple_g[i],
                         w_ple_gate[i].astype(bf16), w_ple_proj[i].astype(bf16),
                         final_norm_g if i == depth - 1 else None, tm=MOE_TOKEN_TILE)
    return x2.reshape(B, S, D)
```

```python
import functools

import jax
import jax.numpy as jnp
import numpy as np
from jax import lax
from jax.experimental import pallas as pl
from jax.experimental.pallas import tpu as pltpu

RMS_EPS = 1e-6
LANES = 128
ROW_CHUNKS = 8
VMEM_LIMIT = 56 * 1024 * 1024

N_ATT_HEADS = 8
ATT_HEAD_DIM = 64
ATT_WIDTH = N_ATT_HEADS * ATT_HEAD_DIM
MOBA_BLOCK = 256
MOBA_TOPK = 3

N_MLSTM_HEADS = 4
MLSTM_WIDTH = 512
MLSTM_V_DIM = 128
MLSTM_QK_DIM = 64
MLSTM_CONV = 4
MLSTM_CHUNK = 256

N_EXPERTS = 32
TOP_K = 4
SWIGLU_ALPHA = 1.702
SWIGLU_LIMIT = 7.0
MOE_ROW_BLOCK = 512
MOE_TOKEN_TILE = 512

NEG_BIG = -1e30

_NT = (((1,), (1,)), ((), ()))
_TN = (((0,), (0,)), ((), ()))


def _params(*sem):
    return pltpu.CompilerParams(dimension_semantics=sem, vmem_limit_bytes=VMEM_LIMIT)


def _rms(x, g):
    return x * lax.rsqrt(jnp.mean(x * x, axis=-1, keepdims=True) + RMS_EPS) * g


def _bf16_parts(x, n):
    parts = []
    for _ in range(n):
        p = x.astype(jnp.bfloat16)
        parts.append(p)
        x = x - p.astype(jnp.float32)
    return parts


def _in_proj_body(x_ref, g_ref, w_ref, qkv_ref, xo_ref, gate_ref, *, col_chunk):
    h = _rms(x_ref[...], g_ref[...]).astype(jnp.bfloat16)
    col = 0
    for out_ref in (qkv_ref, xo_ref, gate_ref):
        for c in range(0, out_ref.shape[1], col_chunk):
            out_ref[:, c:c + col_chunk] = jnp.dot(
                h, w_ref[:, col + c:col + c + col_chunk],
                preferred_element_type=jnp.float32).astype(out_ref.dtype)
        col += out_ref.shape[1]


def in_proj(x2, g, w_bf16, *, tm=512, col_chunk=512):
    n, d = x2.shape
    widths = (3 * ATT_WIDTH, 2 * MLSTM_WIDTH, 2 * d)
    assert sum(widths) == w_bf16.shape[1] and n % tm == 0
    return pl.pallas_call(
        functools.partial(_in_proj_body, col_chunk=col_chunk),
        grid=(n // tm,),
        in_specs=[pl.BlockSpec((tm, d), lambda i: (i, 0)),
                  pl.BlockSpec((1, d), lambda i: (0, 0)),
                  pl.BlockSpec(w_bf16.shape, lambda i: (0, 0))],
        out_specs=[pl.BlockSpec((tm, w), lambda i: (i, 0)) for w in widths],
        out_shape=[jax.ShapeDtypeStruct((n, w), jnp.bfloat16) for w in widths],
        compiler_params=_params("parallel"),
        name="in_proj",
    )(x2, g.reshape(1, d), w_bf16)


def _moba_body(q_ref, k_ref, v_ref, o_ref, *, pair, seq):
    blk = MOBA_BLOCK
    n_blk = seq // blk
    f32, bf16 = jnp.float32, jnp.bfloat16
    lane = lax.broadcasted_iota(jnp.int32, (1, LANES), 1)
    first_head = lane < ATT_HEAD_DIM
    row = lax.broadcasted_iota(jnp.int32, (2 * blk, blk), 0)
    col = lax.broadcasted_iota(jnp.int32, (2 * blk, blk), 1)
    t_in_blk = jnp.where(row >= blk, row - blk, row)
    rel = (t_in_blk - col).astype(f32)
    causal = t_in_blk >= col
    log2e = float(np.log2(np.e))
    scale2 = ATT_HEAD_DIM ** -0.5 * log2e
    head = (2 * pair + (row[:, 0:1] >= blk).astype(jnp.int32)).astype(f32)
    slope2 = jnp.exp2(-8.0 * (head + 1.0) / N_ATT_HEADS) * log2e
    bias = [slope2 * (rel + float(d * blk)) for d in range(n_blk)]
    lane_row = lax.broadcasted_iota(jnp.int32, (LANES, LANES), 0)
    spread = [jnp.where(lane_row == j, 1.0, 0.0).astype(bf16) for j in range(n_blk)]

    k_all = k_ref[...]
    v_all = v_ref[...]
    k_mean = jnp.concatenate(
        [jnp.mean(k_all[j * blk:(j + 1) * blk].astype(f32), axis=0, keepdims=True)
         for j in range(n_blk)] + [jnp.zeros((LANES - n_blk, LANES), f32)], axis=0)
    k_mean_parts = _bf16_parts(k_mean, 3)

    for qi in range(n_blk):
        q_blk = q_ref[qi * blk:(qi + 1) * blk, :]
        zero = jnp.zeros_like(q_blk)
        q2 = jnp.concatenate([jnp.where(first_head, q_blk, zero), jnp.where(first_head, zero, q_blk)], axis=0)
        sel = None
        if qi > MOBA_TOPK:
            g = sum(lax.dot_general(q2, part, _NT, preferred_element_type=f32) for part in k_mean_parts)
            g = jnp.where(lane < qi, g, -jnp.inf)
            ahead = jnp.zeros((2 * blk, LANES), f32)
            for d in range(1, qi):
                ahead = ahead + jnp.where(pltpu.roll(g, d, axis=1) >= g, 1.0, 0.0)
                ahead = ahead + jnp.where(pltpu.roll(g, LANES - d, axis=1) > g, 1.0, 0.0)
            keep = jnp.where(ahead < MOBA_TOPK, 1.0, 0.0).astype(bf16)
            sel = [jnp.dot(keep, spread[j], preferred_element_type=f32) > 0.5 for j in range(qi)]
        logits = []
        for j in range(qi + 1):
            s = lax.dot_general(q2, k_all[j * blk:(j + 1) * blk], _NT, preferred_element_type=f32)
            s = s * scale2 - bias[qi - j]
            if j == qi:
                s = jnp.where(causal, s, NEG_BIG)
            elif sel is not None:
                s = jnp.where(jnp.concatenate([sel[j]] * (blk // LANES), axis=1), s, NEG_BIG)
            logits.append(s)
        s_all = jnp.concatenate(logits, axis=1)
        m = jnp.max(s_all, axis=1, keepdims=True)
        p = jnp.exp2(s_all - m)
        denom = jnp.sum(p, axis=1, keepdims=True)
        o2 = jnp.dot(p.astype(bf16), v_all[:(qi + 1) * blk], preferred_element_type=f32) / denom
        o = jnp.where(first_head, o2[:blk], o2[blk:])
        o_ref[qi * blk:(qi + 1) * blk, :] = o.astype(o_ref.dtype)


def _moba_kernel(q_ref, k_ref, v_ref, o_ref, *, seq):
    _moba_body(q_ref, k_ref, v_ref, o_ref, pair=pl.program_id(1), seq=seq)


def moba(qkv, batch, seq):
    n = batch * seq
    n_pairs = ATT_WIDTH // LANES
    assert seq % MOBA_BLOCK == 0
    spec = lambda off: pl.BlockSpec((seq, LANES), lambda b, p: (b, off + p))
    return pl.pallas_call(
        functools.partial(_moba_kernel, seq=seq),
        grid=(batch, n_pairs),
        in_specs=[spec(0), spec(n_pairs), spec(2 * n_pairs)],
        out_specs=pl.BlockSpec((seq, LANES), lambda b, p: (b, p)),
        out_shape=jax.ShapeDtypeStruct((n, ATT_WIDTH), jnp.bfloat16),
        compiler_params=_params("parallel", "parallel"),
        name="moba",
    )(qkv, qkv, qkv)


def _log_sigmoid(z):
    return jnp.minimum(z, 0.0) - jnp.log(1.0 + jnp.exp(-jnp.abs(z)))


def _mlstm_body(xm_ref, om_ref, cw_ref, cb_ref, wq_ref, wk_ref, wiq_ref, wik_ref, wix_ref,
                wiqt_ref, wikt_ref, wixt_ref, bi_ref, bit_ref, g_ref, y_ref,
                q_sc, k_sc, gcol_sc, grow_sc, ct_sc, n_sc, m_sc, *, seq):
    L = MLSTM_CHUNK
    H = N_MLSTM_HEADS
    n_chunks = seq // L
    f32, bf16 = jnp.float32, jnp.bfloat16

    x = xm_ref[...].astype(f32)
    t_idx = lax.broadcasted_iota(jnp.int32, (seq, 1), 0)
    acc = x * cw_ref[MLSTM_CONV - 1:MLSTM_CONV, :] + cb_ref[...]
    for d in range(1, MLSTM_CONV):
        shifted = jnp.where(t_idx >= d, pltpu.roll(x, d, axis=0), 0.0)
        acc = acc + shifted * cw_ref[MLSTM_CONV - 1 - d:MLSTM_CONV - d, :]
    xc = (acc * jax.nn.sigmoid(acc)).astype(bf16)

    xm = xm_ref[...]
    if_col = jnp.dot(xm, wix_ref[...], preferred_element_type=f32) + bi_ref[...]
    if_row = lax.dot_general(wixt_ref[...], xm, _NT, preferred_element_type=f32) + bit_ref[...]
    for h in range(H):
        xch = xc[:, h * LANES:(h + 1) * LANES]
        q = jnp.dot(xch, wq_ref[h], preferred_element_type=f32).astype(bf16)
        k = jnp.dot(xch, wk_ref[h], preferred_element_type=f32).astype(bf16)
        if_col = if_col + jnp.dot(q, wiq_ref[h], preferred_element_type=f32)
        if_col = if_col + jnp.dot(k, wik_ref[h], preferred_element_type=f32)
        if_row = if_row + lax.dot_general(wiqt_ref[h], q, _NT, preferred_element_type=f32)
        if_row = if_row + lax.dot_general(wikt_ref[h], k, _NT, preferred_element_type=f32)
        q_sc[h] = q
        k_sc[h] = (k.astype(f32) * (MLSTM_QK_DIM ** -0.5)).astype(bf16)
    is_f_col = lax.broadcasted_iota(jnp.int32, (1, 2 * H), 1) >= H
    is_f_row = lax.broadcasted_iota(jnp.int32, (2 * H, 1), 0) >= H
    gcol_sc[...] = jnp.where(is_f_col, _log_sigmoid(if_col), if_col)
    g_row = jnp.where(is_f_row, _log_sigmoid(if_row), if_row)
    for c in range(n_chunks):
        grow_sc[c] = g_row[:, c * L:(c + 1) * L]

    ct_sc[...] = jnp.zeros_like(ct_sc)
    n_sc[...] = jnp.zeros_like(n_sc)
    m_sc[...] = jnp.zeros_like(m_sc)

    r_i = lax.broadcasted_iota(jnp.int32, (L, L), 0)
    c_i = lax.broadcasted_iota(jnp.int32, (L, L), 1)
    causal = r_i >= c_i
    tri_lo = jnp.where(causal, 1.0, 0.0).astype(f32)
    tri_up = jnp.where(c_i >= r_i, 1.0, 0.0).astype(f32)

    def chunk(c, carry):
        r0 = pl.multiple_of(c * L, L)
        gc = gcol_sc[pl.ds(r0, L), :]
        gr = grow_sc[c]
        b_cols = jnp.dot(tri_lo, gc, precision=lax.Precision.HIGHEST, preferred_element_type=f32)
        b_rows = jnp.dot(gr, tri_up, precision=lax.Precision.HIGHEST, preferred_element_type=f32)
        for h in range(H):
            hs = slice(h * LANES, (h + 1) * LANES)
            b_col = b_cols[:, H + h:H + h + 1]
            i_col = gc[:, h:h + 1]
            b_row = b_rows[H + h:H + h + 1, :]
            i_row = gr[h:h + 1, :]
            b_last = b_col[L - 1:L, :]
            m_prev = m_sc[h][:, 0:1]
            q = q_sc[h, pl.ds(r0, L), :]
            k = k_sc[h, pl.ds(r0, L), :]
            v = xm_ref[pl.ds(r0, L), hs]

            log_d = jnp.where(causal, b_col + (i_row - b_row), NEG_BIG)
            inter = b_col + m_prev
            m_t = jnp.maximum(inter, jnp.max(log_d, axis=1, keepdims=True))
            w_inter = jnp.exp(inter - m_t)
            s = lax.dot_general(q, k, _NT, preferred_element_type=f32) * jnp.exp(log_d - m_t)
            num = jnp.dot(s.astype(bf16), v, preferred_element_type=f32)
            num = num + w_inter * jnp.dot(q, ct_sc[h].astype(bf16), preferred_element_type=f32)
            den = jnp.sum(s, axis=1, keepdims=True)
            den = den + w_inter * jnp.sum(q.astype(f32) * n_sc[h], axis=1, keepdims=True)
            hv = num / jnp.maximum(jnp.abs(den), jnp.exp(-m_t))
            hv = hv * lax.rsqrt(jnp.mean(hv * hv, axis=1, keepdims=True) + RMS_EPS) * g_ref[:, hs]
            o_gate = jax.nn.sigmoid(om_ref[pl.ds(r0, L), hs].astype(f32))
            y_ref[pl.ds(r0, L), hs] = (o_gate * hv).astype(y_ref.dtype)

            g_col = b_last - b_col + i_col
            m_new = jnp.maximum(b_last + m_prev, jnp.max(g_col, axis=0, keepdims=True))
            wg = jnp.exp(g_col - m_new)
            decay = jnp.exp(b_last + m_prev - m_new)
            wv = (wg * v.astype(f32)).astype(bf16)
            ct_sc[h] = decay * ct_sc[h] + lax.dot_general(k, wv, _TN, preferred_element_type=f32)
            n_sc[h] = decay * n_sc[h] + jnp.sum(wg * k.astype(f32), axis=0, keepdims=True)
            m_sc[h] = jnp.broadcast_to(m_new, (1, LANES))
        return carry

    lax.fori_loop(0, n_chunks, chunk, 0)


def _mlstm_kernel(*refs, seq):
    _mlstm_body(*refs, seq=seq)


def mlstm(xo, conv_w, conv_b, w_qk, w_if, b_if, g, batch, seq):
    n = batch * seq
    H, dv, dk = N_MLSTM_HEADS, MLSTM_V_DIM, MLSTM_QK_DIM
    assert seq % MLSTM_CHUNK == 0 and dv == LANES
    bf16 = jnp.bfloat16
    pad = ((0, 0), (0, 0), (0, LANES - dk))
    wq = jnp.pad(w_qk[:, :, :dk], pad).astype(bf16)
    wk = jnp.pad(w_qk[:, :, dk:], pad).astype(bf16)
    rpad = ((0, 0), (0, LANES - dk), (0, 0))
    wiq = jnp.pad(w_if[:H * dk].reshape(H, dk, 2 * H), rpad).astype(bf16)
    wik = jnp.pad(w_if[H * dk:2 * H * dk].reshape(H, dk, 2 * H), rpad).astype(bf16)
    wix = w_if[2 * H * dk:].astype(bf16)
    wiqt, wikt, wixt = wiq.transpose(0, 2, 1), wik.transpose(0, 2, 1), wix.T
    full = lambda a: pl.BlockSpec(a.shape, lambda b: (0,) * a.ndim)
    consts = [conv_w, conv_b.reshape(1, -1), wq, wk, wiq, wik, wix, wiqt, wikt, wixt,
              b_if.reshape(1, -1), b_if.reshape(-1, 1), g.reshape(1, -1)]
    return pl.pallas_call(
        functools.partial(_mlstm_kernel, seq=seq),
        grid=(batch,),
        in_specs=[pl.BlockSpec((seq, MLSTM_WIDTH), lambda b: (b, 0)),
                  pl.BlockSpec((seq, MLSTM_WIDTH), lambda b: (b, 1))] + [full(a) for a in consts],
        out_specs=pl.BlockSpec((seq, MLSTM_WIDTH), lambda b: (b, 0)),
        out_shape=jax.ShapeDtypeStruct((n, MLSTM_WIDTH), bf16),
        scratch_shapes=[pltpu.VMEM((H, seq, LANES), bf16), pltpu.VMEM((H, seq, LANES), bf16),
                        pltpu.VMEM((seq, 2 * H), jnp.float32),
                        pltpu.VMEM((seq // MLSTM_CHUNK, 2 * H, MLSTM_CHUNK), jnp.float32),
                        pltpu.VMEM((H, LANES, LANES), jnp.float32),
                        pltpu.VMEM((H, 1, LANES), jnp.float32),
                        pltpu.VMEM((H, 1, LANES), jnp.float32)],
        compiler_params=_params("parallel"),
        name="mlstm",
    )(xo, xo, *consts)


def _store_rows_as_tiles(dst_ref, val):
    rows = val.shape[0]
    for c in range(ROW_CHUNKS):
        dst_ref[pl.ds(c, rows, stride=ROW_CHUNKS), :] = val[:, c * LANES:(c + 1) * LANES]


def _load_tiles_as_rows(src_ref, rows, lead=()):
    return jnp.concatenate(
        [src_ref[lead + (pl.ds(c, rows, stride=ROW_CHUNKS), slice(None))] for c in range(ROW_CHUNKS)],
        axis=1)


def _merge_route_body(x_ref, ya_ref, ym_ref, ga_ref, gm_ref, wb0_ref, wb1_ref, wo_ref, g_ref,
                      wr_ref, br_ref, x1_ref, h2_ref, gate_ref, lpos_ref, cnt_ref):
    f32, bf16 = jnp.float32, jnp.bfloat16
    tm = x_ref.shape[0]

    u = jax.nn.sigmoid(ga_ref[...].astype(f32)) * jnp.dot(ya_ref[...], wb0_ref[...], preferred_element_type=f32)
    u = u + jax.nn.sigmoid(gm_ref[...].astype(f32)) * jnp.dot(ym_ref[...], wb1_ref[...], preferred_element_type=f32)
    x1 = x_ref[...] + jnp.dot(u.astype(bf16), wo_ref[...], preferred_element_type=f32)
    x1_ref[...] = x1
    h2 = _rms(x1, g_ref[...])
    _store_rows_as_tiles(h2_ref, h2)

    (h_hi, h_lo), (w_hi, w_lo) = _bf16_parts(h2, 2), _bf16_parts(wr_ref[...], 2)
    logits = (jnp.dot(h_hi, w_hi, preferred_element_type=f32) + jnp.dot(h_lo, w_hi, preferred_element_type=f32)
              + jnp.dot(h_hi, w_lo, preferred_element_type=f32)) + br_ref[...]
    logits = jnp.transpose(logits)[:N_EXPERTS]
    e_id = lax.broadcasted_iota(jnp.int32, logits.shape, 0).astype(f32)
    chosen = jnp.zeros(logits.shape, f32)
    vals, ids = [], []
    for _ in range(TOP_K):
        top = jnp.max(logits, axis=0, keepdims=True)
        first = jnp.min(jnp.where(logits == top, e_id, float(N_EXPERTS)), axis=0, keepdims=True)
        hit = e_id == first
        chosen = jnp.where(hit, 1.0, chosen)
        logits = jnp.where(hit, -jnp.inf, logits)
        vals.append(top)
        ids.append(first)
    ex = [jnp.exp(v - vals[0]) for v in vals]
    total = ex[0] + ex[1] + ex[2] + ex[3]

    chosen_b = chosen.astype(bf16)
    t_r = lax.broadcasted_iota(jnp.int32, (tm, tm), 0)
    t_c = lax.broadcasted_iota(jnp.int32, (tm, tm), 1)
    earlier = jnp.where(t_r < t_c, 1.0, 0.0).astype(bf16)
    before = jnp.dot(chosen_b, earlier, preferred_element_type=f32)
    e_r = lax.broadcasted_iota(jnp.int32, (N_EXPERTS, N_EXPERTS), 0)
    e_c = lax.broadcasted_iota(jnp.int32, (N_EXPERTS, N_EXPERTS), 1)
    lower = jnp.where(e_c < e_r, 1.0, 0.0).astype(bf16)
    seg_off = jnp.sum(jnp.dot(lower, chosen_b, preferred_element_type=f32), axis=1, keepdims=True)
    pos = before + seg_off
    for kk in range(TOP_K):
        gate_ref[kk:kk + 1, :] = ex[kk] / total
        lpos_ref[kk:kk + 1, :] = jnp.sum(jnp.where(e_id == ids[kk], pos, 0.0), axis=0,
                                         keepdims=True).astype(jnp.int32)
    counts = jnp.sum(chosen, axis=1, keepdims=True)
    cnt_ref[0] = jnp.broadcast_to(counts, (N_EXPERTS, LANES)).astype(jnp.int32)


def merge_route(x2, ya, ym, gates, wb0, wb1, wo, g, w_router, b_router, *, tm):
    n, d = x2.shape
    e = w_router.shape[1]
    assert n % tm == 0 and d == ROW_CHUNKS * LANES and e == N_EXPERTS
    full = lambda a: pl.BlockSpec(a.shape, lambda i: (0,) * a.ndim)
    lane_pad = ((0, 0), (0, LANES - e))
    consts = [wb0, wb1, wo, g.reshape(1, d), jnp.pad(w_router, lane_pad),
              jnp.pad(b_router.reshape(1, e), lane_pad)]
    tok = lambda dt: jax.ShapeDtypeStruct((TOP_K, n), dt)
    return pl.pallas_call(
        _merge_route_body,
        grid=(n // tm,),
        in_specs=[pl.BlockSpec((tm, d), lambda i: (i, 0)),
                  pl.BlockSpec((tm, ATT_WIDTH), lambda i: (i, 0)),
                  pl.BlockSpec((tm, MLSTM_WIDTH), lambda i: (i, 0)),
                  pl.BlockSpec((tm, d), lambda i: (i, 0)),
                  pl.BlockSpec((tm, d), lambda i: (i, 1))] + [full(a) for a in consts],
        out_specs=[pl.BlockSpec((tm, d), lambda i: (i, 0)),
                   pl.BlockSpec((tm * ROW_CHUNKS, LANES), lambda i: (i, 0)),
                   pl.BlockSpec((TOP_K, tm), lambda i: (0, i)),
                   pl.BlockSpec((TOP_K, tm), lambda i: (0, i)),
                   pl.BlockSpec((1, e, LANES), lambda i: (i, 0, 0))],
        out_shape=[jax.ShapeDtypeStruct((n, d), jnp.float32),
                   jax.ShapeDtypeStruct((n * ROW_CHUNKS, LANES), jnp.float32),
                   tok(jnp.float32), tok(jnp.int32),
                   jax.ShapeDtypeStruct((n // tm, e, LANES), jnp.int32)],
        compiler_params=_params("parallel"),
        name="merge_route",
    )(x2, ya, ym, gates, gates, *consts)


def _segment_pieces(cnt, max_rows, fn):
    done = 0
    size = 1 << (max_rows.bit_length() - 1)
    while size:
        bit = cnt & size

        def piece(done=done, size=size):
            fn(done, size)
        pl.when(bit != 0)(piece)
        done = done + bit
        size >>= 1


def _for_tile_segments(meta, tile_rows, make_copy, action):
    cnt_ref, off_ref, dst_ref = meta

    def per_expert(e, carry):
        off, dst = off_ref[0, 0, e], dst_ref[0, 0, e]
        _segment_pieces(cnt_ref[0, 0, e], tile_rows,
                        lambda done, size: getattr(make_copy(off + done, dst + done, size), action)())
        return carry
    lax.fori_loop(0, N_EXPERTS, per_expert, 0)


def _dispatch_body(lpos_ref, cnt_ref, off_ref, dst_ref, pad_len_ref, pad_row_ref, x_ref, out_ref,
                   cbuf, zbuf, sem, zsem, *, tc, rb):
    step = pl.program_id(0)
    last = pl.num_programs(0) - 1
    slot = step % 2

    def compact(t, carry):
        row = x_ref[t]
        for kk in range(TOP_K):
            cbuf[slot, lpos_ref[0, 0, t * TOP_K + kk]] = row
        return carry
    lax.fori_loop(0, tc, compact, 0, unroll=8)

    def seg_copy(sl):
        return lambda local, glob, size: pltpu.make_async_copy(
            cbuf.at[sl, pl.ds(local, size)], out_ref.at[pl.ds(glob, size)], sem.at[sl])

    def wait_tile(sl):
        pltpu.make_async_copy(cbuf.at[sl], out_ref.at[pl.ds(0, TOP_K * tc)], sem.at[sl]).wait()

    _for_tile_segments((cnt_ref, off_ref, dst_ref), tc, seg_copy(slot), "start")

    @pl.when(step > 0)
    def _():
        wait_tile(1 - slot)

    @pl.when(step == last)
    def _():
        wait_tile(slot)
        zbuf[...] = jnp.zeros_like(zbuf)

        def pads(action):
            def per_pad(j, carry):
                row = pad_row_ref[0, j]
                _segment_pieces(pad_len_ref[0, j], rb, lambda done, size: getattr(
                    pltpu.make_async_copy(zbuf.at[pl.ds(0, size)], out_ref.at[pl.ds(row + done, size)], zsem),
                    action)())
                return carry
            lax.fori_loop(0, pad_len_ref.shape[1], per_pad, 0)
        pads("start")
        pads("wait")


def dispatch(lpos, seg_cnt, seg_off, seg_dst, pad_len, pad_row, h2_tiles, n_rows, *, tc, rb):
    n = h2_tiles.shape[0]
    steps = n // tc
    assert n % tc == 0
    smem = pltpu.SMEM
    seg = pl.BlockSpec((1, 1, N_EXPERTS), lambda i: (i, 0, 0), memory_space=smem)
    whole = lambda a: pl.BlockSpec(a.shape, lambda i: (0,) * a.ndim, memory_space=smem)
    return pl.pallas_call(
        functools.partial(_dispatch_body, tc=tc, rb=rb),
        grid=(steps,),
        in_specs=[pl.BlockSpec((1, 1, TOP_K * tc), lambda i: (i, 0, 0), memory_space=smem),
                  seg, seg, seg, whole(pad_len), whole(pad_row),
                  pl.BlockSpec((tc, ROW_CHUNKS, LANES), lambda i: (i, 0, 0))],
        out_specs=pl.BlockSpec(memory_space=pl.ANY),
        out_shape=jax.ShapeDtypeStruct((n_rows, ROW_CHUNKS, LANES), h2_tiles.dtype),
        scratch_shapes=[pltpu.VMEM((2, TOP_K * tc, ROW_CHUNKS, LANES), h2_tiles.dtype),
                        pltpu.VMEM((rb, ROW_CHUNKS, LANES), h2_tiles.dtype),
                        pltpu.SemaphoreType.DMA((2,)), pltpu.SemaphoreType.DMA(())],
        compiler_params=_params("arbitrary"),
        name="dispatch",
    )(lpos, seg_cnt, seg_off, seg_dst, pad_len, pad_row, h2_tiles)


def _experts_body(blk_e_ref, nxt_e_ref, used_ref, x_ref, wgu_hbm, bg_ref, bu_ref, wd_hbm, bd_ref, perm_ref,
                  y_ref, wgu_st, wd_st, wg_sc, wu_sc, wd_sc, sem, *, halves):
    f32, bf16 = jnp.float32, jnp.bfloat16
    step = pl.program_id(0)
    live = step < used_ref[0]
    rows = x_ref.shape[0] // ROW_CHUNKS
    expert = blk_e_ref[step]

    def fetch(e, action):
        getattr(pltpu.make_async_copy(wgu_hbm.at[e], wgu_st, sem.at[0]), action)()
        getattr(pltpu.make_async_copy(wd_hbm.at[e], wd_st, sem.at[1]), action)()

    @pl.when(live & ((step == 0) | (expert != blk_e_ref[jnp.maximum(step - 1, 0)])))
    def _():
        @pl.when(step == 0)
        def _():
            fetch(expert, "start")
        fetch(expert, "wait")
        tile = perm_ref.shape[0]
        half = tile // 2
        for t in range(wgu_st.shape[1] // tile):
            w = wgu_st[:, t * tile:(t + 1) * tile].astype(bf16)
            o = jnp.dot(w, perm_ref[...], preferred_element_type=f32)
            wg_sc[:, t * half:(t + 1) * half] = o[:, :half].astype(bf16)
            wu_sc[:, t * half:(t + 1) * half] = o[:, half:].astype(bf16)
        wd_sc[...] = wd_st[...].astype(bf16)
        nxt = nxt_e_ref[step]

        @pl.when(nxt >= 0)
        def _():
            fetch(nxt, "start")

    @pl.when(live)
    def _():
        part = rows // halves
        for h in range(halves):
            xb = jnp.concatenate(
                [x_ref[pl.ds(h * part * ROW_CHUNKS + c, part, stride=ROW_CHUNKS), :]
                 for c in range(ROW_CHUNKS)], axis=1).astype(bf16)
            gate = jnp.dot(xb, wg_sc[...], preferred_element_type=f32) + bg_ref[0]
            up = jnp.dot(xb, wu_sc[...], preferred_element_type=f32) + bu_ref[0]
            gate = jnp.minimum(gate, SWIGLU_LIMIT)
            up = jnp.clip(up, -SWIGLU_LIMIT, SWIGLU_LIMIT)
            act = (up + 1.0) * (gate * jax.nn.sigmoid(gate * SWIGLU_ALPHA))
            y = jnp.dot(act.astype(bf16), wd_sc[...], preferred_element_type=f32) + bd_ref[0]
            for c in range(ROW_CHUNKS):
                y_ref[pl.ds(h * part * ROW_CHUNKS + c, part, stride=ROW_CHUNKS), :] = \
                    y[:, c * LANES:(c + 1) * LANES]

    @pl.when(jnp.logical_not(live))
    def _():
        y_ref[...] = jnp.zeros_like(y_ref)


def experts(blk_expert, nxt_expert, n_used, xs_tiles, w_gu, b_g, b_u, w_d, b_d, *, halves=2):
    n_blocks = blk_expert.shape[0]
    rb = MOE_ROW_BLOCK
    d, ff = w_d.shape[2], w_d.shape[1]
    tile = 2 * LANES
    perm = np.zeros((tile, tile), np.float32)
    perm[2 * np.arange(LANES), np.arange(LANES)] = 1.0
    perm[2 * np.arange(LANES) + 1, LANES + np.arange(LANES)] = 1.0
    e_map3 = lambda i, be, nx, nu: (be[i], 0, 0)
    rows_map = lambda i, be, nx, nu: (i, 0)
    grid_spec = pltpu.PrefetchScalarGridSpec(
        num_scalar_prefetch=3,
        grid=(n_blocks,),
        in_specs=[pl.BlockSpec((rb * ROW_CHUNKS, LANES), rows_map),
                  pl.BlockSpec(memory_space=pl.ANY),
                  pl.BlockSpec((1, 1, ff), e_map3), pl.BlockSpec((1, 1, ff), e_map3),
                  pl.BlockSpec(memory_space=pl.ANY), pl.BlockSpec((1, 1, d), e_map3),
                  pl.BlockSpec((tile, tile), lambda i, be, nx, nu: (0, 0))],
        out_specs=pl.BlockSpec((rb * ROW_CHUNKS, LANES), rows_map),
        scratch_shapes=[pltpu.VMEM((d, 2 * ff), w_gu.dtype), pltpu.VMEM((ff, d), w_d.dtype),
                        pltpu.VMEM((d, ff), jnp.bfloat16), pltpu.VMEM((d, ff), jnp.bfloat16),
                        pltpu.VMEM((ff, d), jnp.bfloat16), pltpu.SemaphoreType.DMA((2,))],
    )
    return pl.pallas_call(
        functools.partial(_experts_body, halves=halves),
        grid_spec=grid_spec,
        out_shape=jax.ShapeDtypeStruct(xs_tiles.shape, jnp.float32),
        compiler_params=_params("arbitrary"),
        name="experts",
    )(blk_expert, nxt_expert, n_used, xs_tiles, w_gu, b_g, b_u, w_d, b_d, jnp.asarray(perm, jnp.bfloat16))


def _combine_ple_body(lpos_ref, gw_ref, cnt_ref, off_ref, dst_ref, ncnt_ref, noff_ref, ndst_ref,
                      x1_ref, ys_ref, p_ref, gp_ref, wpg_ref, wpp_ref, *rest):
    cbuf, msum, sem = rest[-3:]
    rest = rest[:-3]
    gf_ref, o_ref = rest if len(rest) == 2 else (None, rest[0])
    f32, bf16 = jnp.float32, jnp.bfloat16
    tm = x1_ref.shape[0]
    step = pl.program_id(0)
    slot = step % 2

    def seg_copy(sl):
        return lambda local, glob, size: pltpu.make_async_copy(
            ys_ref.at[pl.ds(glob, size)], cbuf.at[sl, pl.ds(local, size)], sem.at[sl])

    cur, nxt = (cnt_ref, off_ref, dst_ref), (ncnt_ref, noff_ref, ndst_ref)

    @pl.when(step == 0)
    def _():
        _for_tile_segments(cur, tm, seg_copy(0), "start")

    @pl.when(step + 1 < pl.num_programs(0))
    def _():
        _for_tile_segments(nxt, tm, seg_copy(1 - slot), "start")

    pltpu.make_async_copy(ys_ref.at[pl.ds(0, TOP_K * tm)], cbuf.at[slot], sem.at[slot]).wait()

    def weighted_sum(t, carry):
        acc = gw_ref[0, 0, t * TOP_K] * cbuf[slot, lpos_ref[0, 0, t * TOP_K]]
        for kk in range(1, TOP_K):
            acc = acc + gw_ref[0, 0, t * TOP_K + kk] * cbuf[slot, lpos_ref[0, 0, t * TOP_K + kk]]
        msum[t] = acc
        return carry
    lax.fori_loop(0, tm, weighted_sum, 0, unroll=8)

    moe = jnp.concatenate([msum[:, c, :] for c in range(ROW_CHUNKS)], axis=1)
    x2 = x1_ref[...] + moe
    r = _rms(x2, gp_ref[...]).astype(bf16)
    gate = jax.nn.sigmoid(jnp.dot(r, wpg_ref[...], preferred_element_type=f32))
    emb = jnp.dot(p_ref[...].astype(bf16), wpp_ref[...], preferred_element_type=f32)
    x3 = x2 + gate * emb
    o_ref[...] = x3 if gf_ref is None else _rms(x3, gf_ref[...])


def combine_ple(lpos, gate_w, seg_cnt, seg_off, seg_dst, x1, ys_tiles, p2, g_ple, w_pg, w_pp, g_final,
                *, tm):
    n, d = x1.shape
    steps = n // tm
    smem = pltpu.SMEM
    full = lambda a: pl.BlockSpec(a.shape, lambda i: (0,) * a.ndim)
    seg = lambda shift: pl.BlockSpec((1, 1, N_EXPERTS),
                                     lambda i: (jnp.minimum(i + shift, steps - 1), 0, 0), memory_space=smem)
    tok = pl.BlockSpec((1, 1, TOP_K * tm), lambda i: (i, 0, 0), memory_space=smem)
    consts = [g_ple.reshape(1, d), w_pg, w_pp] + ([] if g_final is None else [g_final.reshape(1, d)])
    return pl.pallas_call(
        _combine_ple_body,
        grid=(steps,),
        in_specs=[tok, tok, seg(0), seg(0), seg(0), seg(1), seg(1), seg(1),
                  pl.BlockSpec((tm, d), lambda i: (i, 0)),
                  pl.BlockSpec(memory_space=pl.ANY),
                  pl.BlockSpec((tm, p2.shape[1]), lambda i: (i, 0))] + [full(a) for a in consts],
        out_specs=pl.BlockSpec((tm, d), lambda i: (i, 0)),
        out_shape=jax.ShapeDtypeStruct((n, d), jnp.float32),
        scratch_shapes=[pltpu.VMEM((2, TOP_K * tm, ROW_CHUNKS, LANES), ys_tiles.dtype),
                        pltpu.VMEM((tm, ROW_CHUNKS, LANES), jnp.float32),
                        pltpu.SemaphoreType.DMA((2,))],
        compiler_params=_params("arbitrary"),
        name="combine_ple",
    )(lpos, gate_w, seg_cnt, seg_off, seg_dst, seg_cnt, seg_off, seg_dst, x1, ys_tiles, p2, *consts)


def _moe(h2_tiles, lpos, tile_cnt, w_gu, b_gu, w_d, b_d, *, tm):
    n = h2_tiles.shape[0] // ROW_CHUNKS
    rb = MOE_ROW_BLOCK
    i32 = jnp.int32
    n_blocks = -(-n * TOP_K // rb) + N_EXPERTS
    n_rows = n_blocks * rb
    counts = jnp.sum(tile_cnt, axis=0)
    padded = (counts + rb - 1) // rb * rb
    pend = jnp.cumsum(padded)
    pstart = pend - padded
    blk_start = jnp.arange(n_blocks, dtype=i32) * rb
    blk_expert = jnp.minimum(jnp.sum((pend[None, :] <= blk_start[:, None]).astype(i32), axis=1),
                             N_EXPERTS - 1)
    n_used = pend[-1:] // rb
    seg_cnt = tile_cnt[:, None, :]
    seg_off = (jnp.cumsum(tile_cnt, axis=1) - tile_cnt)[:, None, :]
    seg_dst = (pstart[None, :] + jnp.cumsum(tile_cnt, axis=0) - tile_cnt)[:, None, :]
    tail = n_used + jnp.arange(N_EXPERTS, dtype=i32)
    pad_len = jnp.concatenate([padded - counts, jnp.where(tail < n_blocks, rb, 0)])[None, :]
    pad_row = jnp.concatenate([pstart + counts, tail * rb])[None, :]
    b_g, b_u = b_gu[:, None, 0::2], b_gu[:, None, 1::2]
    xs = dispatch(lpos, seg_cnt, seg_off, seg_dst, pad_len.astype(i32), pad_row.astype(i32),
                  h2_tiles.reshape(n, ROW_CHUNKS, LANES), n_rows, tc=tm, rb=rb)
    e_ids = jnp.arange(N_EXPERTS, dtype=i32)
    later = (padded > 0)[None, :] & (e_ids[None, :] > e_ids[:, None])
    nxt_tab = jnp.min(jnp.where(later, e_ids[None, :], N_EXPERTS), axis=1)
    nxt_tab = jnp.where(nxt_tab == N_EXPERTS, -1, nxt_tab)
    nxt_expert = jnp.sum(jnp.where(blk_expert[:, None] == e_ids[None, :], nxt_tab[None, :], 0), axis=1)
    ys = experts(blk_expert, nxt_expert.astype(i32), n_used.astype(i32),
                 xs.reshape(n_rows * ROW_CHUNKS, LANES), w_gu, b_g, b_u, w_d, b_d[:, None, :])
    return (seg_cnt, seg_off, seg_dst), ys.reshape(n_rows, ROW_CHUNKS, LANES)


def kernel(x, p, norm_mix_g, w_in, conv_w, conv_b, w_qk_m, w_if, b_if, mnorm_g, w_branch, w_out, norm_ffn_g, w_router, b_router, w_gate_up, b_gate_up, w_down, b_down, norm_ple_g, w_ple_gate, w_ple_proj, final_norm_g):
    B, S, D = x.shape
    depth = w_in.shape[0]
    bf16 = jnp.bfloat16
    x2 = x.reshape(B * S, D)
    for i in range(depth):
        qkv, xo, gates = in_proj(x2, norm_mix_g[i], w_in[i].astype(bf16))
        y_a = moba(qkv, B, S)
        y_m = mlstm(xo, conv_w[i], conv_b[i], w_qk_m[i], w_if[i], b_if[i], mnorm_g[i], B, S)
        x1, h2_tiles, gate_w, lpos, tile_cnt = merge_route(
            x2, y_a, y_m, gates, w_branch[i, 0].astype(bf16), w_branch[i, 1].astype(bf16),
            w_out[i].astype(bf16), norm_ffn_g[i], w_router[i], b_router[i], tm=MOE_TOKEN_TILE)
        by_tile = lambda a: a.T.reshape(-1, 1, MOE_TOKEN_TILE * TOP_K)
        lpos, gate_w = by_tile(lpos), by_tile(gate_w)
        segs, ys = _moe(h2_tiles, lpos, tile_cnt[:, :, 0], w_gate_up[i], b_gate_up[i],
                        w_down[i], b_down[i], tm=MOE_TOKEN_TILE)
        x2 = combine_ple(lpos, gate_w, *segs, x1, ys, p[i].reshape(B * S, -1), norm_ple_g[i],
                         w_ple_gate[i].astype(bf16), w_ple_proj[i].astype(bf16),
                         final_norm_g if i == depth - 1 else None, tm=MOE_TOKEN_TILE)
    return x2.reshape(B, S, D)
```

```python
import functools

import jax
import jax.numpy as jnp
import numpy as np
from jax import lax
from jax.experimental import pallas as pl
from jax.experimental.pallas import tpu as pltpu

RMS_EPS = 1e-6
LANES = 128
ROW_CHUNKS = 8
VMEM_LIMIT = 56 * 1024 * 1024

N_ATT_HEADS = 8
ATT_HEAD_DIM = 64
ATT_WIDTH = N_ATT_HEADS * ATT_HEAD_DIM
MOBA_BLOCK = 256
MOBA_TOPK = 3

N_MLSTM_HEADS = 4
MLSTM_WIDTH = 512
MLSTM_V_DIM = 128
MLSTM_QK_DIM = 64
MLSTM_CONV = 4
MLSTM_CHUNK = 256

N_EXPERTS = 32
TOP_K = 4
SWIGLU_ALPHA = 1.702
SWIGLU_LIMIT = 7.0
MOE_ROW_BLOCK = 512
MOE_TOKEN_TILE = 512

NEG_BIG = -1e30

_NT = (((1,), (1,)), ((), ()))
_TN = (((0,), (0,)), ((), ()))


def _params(*sem):
    return pltpu.CompilerParams(dimension_semantics=sem, vmem_limit_bytes=VMEM_LIMIT)


def _rms(x, g):
    return x * lax.rsqrt(jnp.mean(x * x, axis=-1, keepdims=True) + RMS_EPS) * g


def _bf16_parts(x, n):
    parts = []
    for _ in range(n):
        p = x.astype(jnp.bfloat16)
        parts.append(p)
        x = x - p.astype(jnp.float32)
    return parts


def _in_proj_body(x_ref, g_ref, w_ref, qkv_ref, xo_ref, gate_ref, *, col_chunk):
    h = _rms(x_ref[...], g_ref[...]).astype(jnp.bfloat16)
    col = 0
    for out_ref in (qkv_ref, xo_ref, gate_ref):
        for c in range(0, out_ref.shape[1], col_chunk):
            out_ref[:, c:c + col_chunk] = jnp.dot(
                h, w_ref[:, col + c:col + c + col_chunk],
                preferred_element_type=jnp.float32).astype(out_ref.dtype)
        col += out_ref.shape[1]


def in_proj(x2, g, w_bf16, *, tm=512, col_chunk=512):
    n, d = x2.shape
    widths = (3 * ATT_WIDTH, 2 * MLSTM_WIDTH, 2 * d)
    assert sum(widths) == w_bf16.shape[1] and n % tm == 0
    return pl.pallas_call(
        functools.partial(_in_proj_body, col_chunk=col_chunk),
        grid=(n // tm,),
        in_specs=[pl.BlockSpec((tm, d), lambda i: (i, 0)),
                  pl.BlockSpec((1, d), lambda i: (0, 0)),
                  pl.BlockSpec(w_bf16.shape, lambda i: (0, 0))],
        out_specs=[pl.BlockSpec((tm, w), lambda i: (i, 0)) for w in widths],
        out_shape=[jax.ShapeDtypeStruct((n, w), jnp.bfloat16) for w in widths],
        compiler_params=_params("parallel"),
        name="in_proj",
    )(x2, g.reshape(1, d), w_bf16)


def _moba_body(q_ref, k_ref, v_ref, o_ref, *, pair, seq):
    blk = MOBA_BLOCK
    n_blk = seq // blk
    f32, bf16 = jnp.float32, jnp.bfloat16
    lane = lax.broadcasted_iota(jnp.int32, (1, LANES), 1)
    first_head = lane < ATT_HEAD_DIM
    row = lax.broadcasted_iota(jnp.int32, (2 * blk, blk), 0)
    col = lax.broadcasted_iota(jnp.int32, (2 * blk, blk), 1)
    t_in_blk = jnp.where(row >= blk, row - blk, row)
    rel = (t_in_blk - col).astype(f32)
    causal = t_in_blk >= col
    log2e = float(np.log2(np.e))
    scale2 = ATT_HEAD_DIM ** -0.5 * log2e
    head = (2 * pair + (row[:, 0:1] >= blk).astype(jnp.int32)).astype(f32)
    slope2 = jnp.exp2(-8.0 * (head + 1.0) / N_ATT_HEADS) * log2e
    bias = [slope2 * (rel + float(d * blk)) for d in range(n_blk)]
    lane_row = lax.broadcasted_iota(jnp.int32, (LANES, LANES), 0)
    spread = [jnp.where(lane_row == j, 1.0, 0.0).astype(bf16) for j in range(n_blk)]

    k_all = k_ref[...]
    v_ones = jnp.concatenate([v_ref[...], jnp.ones((seq, LANES), bf16)], axis=1)
    k_mean = jnp.concatenate(
        [jnp.mean(k_all[j * blk:(j + 1) * blk].astype(f32), axis=0, keepdims=True)
         for j in range(n_blk)] + [jnp.zeros((LANES - n_blk, LANES), f32)], axis=0)
    k_mean_parts = _bf16_parts(k_mean, 3)

    for qi in range(n_blk):
        q_blk = q_ref[qi * blk:(qi + 1) * blk, :]
        zero = jnp.zeros_like(q_blk)
        q2 = jnp.concatenate([jnp.where(first_head, q_blk, zero), jnp.where(first_head, zero, q_blk)], axis=0)
        sel = None
        if qi > MOBA_TOPK:
            g = sum(lax.dot_general(q2, part, _NT, preferred_element_type=f32) for part in k_mean_parts)
            g = jnp.where(lane < qi, g, -jnp.inf)
            ahead = jnp.zeros((2 * blk, LANES), f32)
            for d in range(1, qi):
                ahead = ahead + jnp.where(pltpu.roll(g, d, axis=1) >= g, 1.0, 0.0)
                ahead = ahead + jnp.where(pltpu.roll(g, LANES - d, axis=1) > g, 1.0, 0.0)
            keep = jnp.where(ahead < MOBA_TOPK, 1.0, 0.0).astype(bf16)
            sel = [jnp.dot(keep, spread[j], preferred_element_type=f32) > 0.5 for j in range(qi)]
        logits = []
        for j in range(qi + 1):
            s = lax.dot_general(q2, k_all[j * blk:(j + 1) * blk], _NT, preferred_element_type=f32)
            s = s * scale2 - bias[qi - j]
            if j == qi:
                s = jnp.where(causal, s, NEG_BIG)
            elif sel is not None:
                s = jnp.where(jnp.concatenate([sel[j]] * (blk // LANES), axis=1), s, NEG_BIG)
            logits.append(s)
        s_all = jnp.concatenate(logits, axis=1)
        m = jnp.max(s_all, axis=1, keepdims=True)
        p = jnp.exp2(s_all - m)
        o2 = jnp.dot(p.astype(bf16), v_ones[:(qi + 1) * blk], preferred_element_type=f32)
        o2 = o2[:, :LANES] / o2[:, LANES:LANES + 1]
        o = jnp.where(first_head, o2[:blk], o2[blk:])
        o_ref[qi * blk:(qi + 1) * blk, :] = o.astype(o_ref.dtype)


def _moba_kernel(q_ref, k_ref, v_ref, o_ref, *, seq):
    _moba_body(q_ref, k_ref, v_ref, o_ref, pair=pl.program_id(1), seq=seq)


def moba(qkv, batch, seq):
    n = batch * seq
    n_pairs = ATT_WIDTH // LANES
    assert seq % MOBA_BLOCK == 0
    spec = lambda off: pl.BlockSpec((seq, LANES), lambda b, p: (b, off + p))
    return pl.pallas_call(
        functools.partial(_moba_kernel, seq=seq),
        grid=(batch, n_pairs),
        in_specs=[spec(0), spec(n_pairs), spec(2 * n_pairs)],
        out_specs=pl.BlockSpec((seq, LANES), lambda b, p: (b, p)),
        out_shape=jax.ShapeDtypeStruct((n, ATT_WIDTH), jnp.bfloat16),
        compiler_params=_params("parallel", "parallel"),
        name="moba",
    )(qkv, qkv, qkv)


def _log_sigmoid(z):
    return jnp.minimum(z, 0.0) - jnp.log(1.0 + jnp.exp(-jnp.abs(z)))


def _mlstm_body(xm_ref, om_ref, cw_ref, cb_ref, wq_ref, wk_ref, wiq_ref, wik_ref, wix_ref,
                wiqt_ref, wikt_ref, wixt_ref, bi_ref, bit_ref, g_ref, y_ref,
                q_sc, k_sc, gcol_sc, grow_sc, ct_sc, n_sc, m_sc, *, seq):
    L = MLSTM_CHUNK
    H = N_MLSTM_HEADS
    n_chunks = seq // L
    f32, bf16 = jnp.float32, jnp.bfloat16

    x = xm_ref[...].astype(f32)
    t_idx = lax.broadcasted_iota(jnp.int32, (seq, 1), 0)
    acc = x * cw_ref[MLSTM_CONV - 1:MLSTM_CONV, :] + cb_ref[...]
    for d in range(1, MLSTM_CONV):
        shifted = jnp.where(t_idx >= d, pltpu.roll(x, d, axis=0), 0.0)
        acc = acc + shifted * cw_ref[MLSTM_CONV - 1 - d:MLSTM_CONV - d, :]
    xc = (acc * jax.nn.sigmoid(acc)).astype(bf16)

    xm = xm_ref[...]
    if_col = jnp.dot(xm, wix_ref[...], preferred_element_type=f32) + bi_ref[...]
    if_row = lax.dot_general(wixt_ref[...], xm, _NT, preferred_element_type=f32) + bit_ref[...]
    for h in range(H):
        xch = xc[:, h * LANES:(h + 1) * LANES]
        q = jnp.dot(xch, wq_ref[h], preferred_element_type=f32).astype(bf16)
        k = jnp.dot(xch, wk_ref[h], preferred_element_type=f32).astype(bf16)
        if_col = if_col + jnp.dot(q, wiq_ref[h], preferred_element_type=f32)
        if_col = if_col + jnp.dot(k, wik_ref[h], preferred_element_type=f32)
        if_row = if_row + lax.dot_general(wiqt_ref[h], q, _NT, preferred_element_type=f32)
        if_row = if_row + lax.dot_general(wikt_ref[h], k, _NT, preferred_element_type=f32)
        q_sc[h] = q
        k_sc[h] = (k.astype(f32) * (MLSTM_QK_DIM ** -0.5)).astype(bf16)
    is_f_col = lax.broadcasted_iota(jnp.int32, (1, 2 * H), 1) >= H
    is_f_row = lax.broadcasted_iota(jnp.int32, (2 * H, 1), 0) >= H
    gcol_sc[...] = jnp.where(is_f_col, _log_sigmoid(if_col), if_col)
    g_row = jnp.where(is_f_row, _log_sigmoid(if_row), if_row)
    for c in range(n_chunks):
        grow_sc[c] = g_row[:, c * L:(c + 1) * L]

    ct_sc[...] = jnp.zeros_like(ct_sc)
    n_sc[...] = jnp.zeros_like(n_sc)
    m_sc[...] = jnp.zeros_like(m_sc)

    r_i = lax.broadcasted_iota(jnp.int32, (L, L), 0)
    c_i = lax.broadcasted_iota(jnp.int32, (L, L), 1)
    causal = r_i >= c_i
    tri_lo = jnp.where(causal, 1.0, 0.0).astype(f32)
    tri_up = jnp.where(c_i >= r_i, 1.0, 0.0).astype(f32)

    def chunk(c, carry):
        r0 = pl.multiple_of(c * L, L)
        gc = gcol_sc[pl.ds(r0, L), :]
        gr = grow_sc[c]
        b_cols = jnp.dot(tri_lo, gc, precision=lax.Precision.HIGHEST, preferred_element_type=f32)
        b_rows = jnp.dot(gr, tri_up, precision=lax.Precision.HIGHEST, preferred_element_type=f32)
        for h in range(H):
            hs = slice(h * LANES, (h + 1) * LANES)
            b_col = b_cols[:, H + h:H + h + 1]
            i_col = gc[:, h:h + 1]
            b_row = b_rows[H + h:H + h + 1, :]
            i_row = gr[h:h + 1, :]
            b_last = b_col[L - 1:L, :]
            m_prev = m_sc[h][:, 0:1]
            q = q_sc[h, pl.ds(r0, L), :]
            k = k_sc[h, pl.ds(r0, L), :]
            v = xm_ref[pl.ds(r0, L), hs]

            log_d = jnp.where(causal, b_col + (i_row - b_row), NEG_BIG)
            inter = b_col + m_prev
            m_t = jnp.maximum(inter, jnp.max(log_d, axis=1, keepdims=True))
            w_inter = jnp.exp(inter - m_t)
            s = lax.dot_general(q, k, _NT, preferred_element_type=f32) * jnp.exp(log_d - m_t)
            num = jnp.dot(s.astype(bf16), v, preferred_element_type=f32)
            num = num + w_inter * jnp.dot(q, ct_sc[h].astype(bf16), preferred_element_type=f32)
            den = jnp.sum(s, axis=1, keepdims=True)
            den = den + w_inter * jnp.sum(q.astype(f32) * n_sc[h], axis=1, keepdims=True)
            hv = num / jnp.maximum(jnp.abs(den), jnp.exp(-m_t))
            hv = hv * lax.rsqrt(jnp.mean(hv * hv, axis=1, keepdims=True) + RMS_EPS) * g_ref[:, hs]
            o_gate = jax.nn.sigmoid(om_ref[pl.ds(r0, L), hs].astype(f32))
            y_ref[pl.ds(r0, L), hs] = (o_gate * hv).astype(y_ref.dtype)

            g_col = b_last - b_col + i_col
            m_new = jnp.maximum(b_last + m_prev, jnp.max(g_col, axis=0, keepdims=True))
            wg = jnp.exp(g_col - m_new)
            decay = jnp.exp(b_last + m_prev - m_new)
            wv = (wg * v.astype(f32)).astype(bf16)
            ct_sc[h] = decay * ct_sc[h] + lax.dot_general(k, wv, _TN, preferred_element_type=f32)
            n_sc[h] = decay * n_sc[h] + jnp.sum(wg * k.astype(f32), axis=0, keepdims=True)
            m_sc[h] = jnp.broadcast_to(m_new, (1, LANES))
        return carry

    lax.fori_loop(0, n_chunks, chunk, 0)


def _mlstm_kernel(*refs, seq):
    _mlstm_body(*refs, seq=seq)


def mlstm(xo, conv_w, conv_b, w_qk, w_if, b_if, g, batch, seq):
    n = batch * seq
    H, dv, dk = N_MLSTM_HEADS, MLSTM_V_DIM, MLSTM_QK_DIM
    assert seq % MLSTM_CHUNK == 0 and dv == LANES
    bf16 = jnp.bfloat16
    pad = ((0, 0), (0, 0), (0, LANES - dk))
    wq = jnp.pad(w_qk[:, :, :dk], pad).astype(bf16)
    wk = jnp.pad(w_qk[:, :, dk:], pad).astype(bf16)
    rpad = ((0, 0), (0, LANES - dk), (0, 0))
    wiq = jnp.pad(w_if[:H * dk].reshape(H, dk, 2 * H), rpad).astype(bf16)
    wik = jnp.pad(w_if[H * dk:2 * H * dk].reshape(H, dk, 2 * H), rpad).astype(bf16)
    wix = w_if[2 * H * dk:].astype(bf16)
    wiqt, wikt, wixt = wiq.transpose(0, 2, 1), wik.transpose(0, 2, 1), wix.T
    full = lambda a: pl.BlockSpec(a.shape, lambda b: (0,) * a.ndim)
    consts = [conv_w, conv_b.reshape(1, -1), wq, wk, wiq, wik, wix, wiqt, wikt, wixt,
              b_if.reshape(1, -1), b_if.reshape(-1, 1), g.reshape(1, -1)]
    return pl.pallas_call(
        functools.partial(_mlstm_kernel, seq=seq),
        grid=(batch,),
        in_specs=[pl.BlockSpec((seq, MLSTM_WIDTH), lambda b: (b, 0)),
                  pl.BlockSpec((seq, MLSTM_WIDTH), lambda b: (b, 1))] + [full(a) for a in consts],
        out_specs=pl.BlockSpec((seq, MLSTM_WIDTH), lambda b: (b, 0)),
        out_shape=jax.ShapeDtypeStruct((n, MLSTM_WIDTH), bf16),
        scratch_shapes=[pltpu.VMEM((H, seq, LANES), bf16), pltpu.VMEM((H, seq, LANES), bf16),
                        pltpu.VMEM((seq, 2 * H), jnp.float32),
                        pltpu.VMEM((seq // MLSTM_CHUNK, 2 * H, MLSTM_CHUNK), jnp.float32),
                        pltpu.VMEM((H, LANES, LANES), jnp.float32),
                        pltpu.VMEM((H, 1, LANES), jnp.float32),
                        pltpu.VMEM((H, 1, LANES), jnp.float32)],
        compiler_params=_params("parallel"),
        name="mlstm",
    )(xo, xo, *consts)


def _store_rows_as_tiles(dst_ref, val):
    rows = val.shape[0]
    for c in range(ROW_CHUNKS):
        dst_ref[pl.ds(c, rows, stride=ROW_CHUNKS), :] = val[:, c * LANES:(c + 1) * LANES]


def _load_tiles_as_rows(src_ref, rows, lead=()):
    return jnp.concatenate(
        [src_ref[lead + (pl.ds(c, rows, stride=ROW_CHUNKS), slice(None))] for c in range(ROW_CHUNKS)],
        axis=1)


def _merge_route_body(x_ref, ya_ref, ym_ref, ga_ref, gm_ref, wb0_ref, wb1_ref, wo_ref, g_ref,
                      wr_ref, br_ref, x1_ref, h2_ref, gate_ref, lpos_ref, cnt_ref):
    f32, bf16 = jnp.float32, jnp.bfloat16
    tm = x_ref.shape[0]

    u = jax.nn.sigmoid(ga_ref[...].astype(f32)) * jnp.dot(ya_ref[...], wb0_ref[...], preferred_element_type=f32)
    u = u + jax.nn.sigmoid(gm_ref[...].astype(f32)) * jnp.dot(ym_ref[...], wb1_ref[...], preferred_element_type=f32)
    x1 = x_ref[...] + jnp.dot(u.astype(bf16), wo_ref[...], preferred_element_type=f32)
    x1_ref[...] = x1
    h2 = _rms(x1, g_ref[...])
    _store_rows_as_tiles(h2_ref, h2)

    (h_hi, h_lo), (w_hi, w_lo) = _bf16_parts(h2, 2), _bf16_parts(wr_ref[...], 2)
    logits = (jnp.dot(h_hi, w_hi, preferred_element_type=f32) + jnp.dot(h_lo, w_hi, preferred_element_type=f32)
              + jnp.dot(h_hi, w_lo, preferred_element_type=f32)) + br_ref[...]
    logits = jnp.transpose(logits)[:N_EXPERTS]
    e_id = lax.broadcasted_iota(jnp.int32, logits.shape, 0).astype(f32)
    chosen = jnp.zeros(logits.shape, f32)
    vals, ids = [], []
    for _ in range(TOP_K):
        top = jnp.max(logits, axis=0, keepdims=True)
        first = jnp.min(jnp.where(logits == top, e_id, float(N_EXPERTS)), axis=0, keepdims=True)
        hit = e_id == first
        chosen = jnp.where(hit, 1.0, chosen)
        logits = jnp.where(hit, -jnp.inf, logits)
        vals.append(top)
        ids.append(first)
    ex = [jnp.exp(v - vals[0]) for v in vals]
    total = ex[0] + ex[1] + ex[2] + ex[3]

    chosen_b = chosen.astype(bf16)
    t_r = lax.broadcasted_iota(jnp.int32, (tm, tm), 0)
    t_c = lax.broadcasted_iota(jnp.int32, (tm, tm), 1)
    earlier = jnp.where(t_r < t_c, 1.0, 0.0).astype(bf16)
    before = jnp.dot(chosen_b, earlier, preferred_element_type=f32)
    e_r = lax.broadcasted_iota(jnp.int32, (N_EXPERTS, N_EXPERTS), 0)
    e_c = lax.broadcasted_iota(jnp.int32, (N_EXPERTS, N_EXPERTS), 1)
    lower = jnp.where(e_c < e_r, 1.0, 0.0).astype(bf16)
    seg_off = jnp.sum(jnp.dot(lower, chosen_b, preferred_element_type=f32), axis=1, keepdims=True)
    pos = before + seg_off
    for kk in range(TOP_K):
        gate_ref[kk:kk + 1, :] = ex[kk] / total
        lpos_ref[kk:kk + 1, :] = jnp.sum(jnp.where(e_id == ids[kk], pos, 0.0), axis=0,
                                         keepdims=True).astype(jnp.int32)
    counts = jnp.sum(chosen, axis=1, keepdims=True)
    cnt_ref[0] = jnp.broadcast_to(counts, (N_EXPERTS, LANES)).astype(jnp.int32)


def merge_route(x2, ya, ym, gates, wb0, wb1, wo, g, w_router, b_router, *, tm):
    n, d = x2.shape
    e = w_router.shape[1]
    assert n % tm == 0 and d == ROW_CHUNKS * LANES and e == N_EXPERTS
    full = lambda a: pl.BlockSpec(a.shape, lambda i: (0,) * a.ndim)
    lane_pad = ((0, 0), (0, LANES - e))
    consts = [wb0, wb1, wo, g.reshape(1, d), jnp.pad(w_router, lane_pad),
              jnp.pad(b_router.reshape(1, e), lane_pad)]
    tok = lambda dt: jax.ShapeDtypeStruct((TOP_K, n), dt)
    return pl.pallas_call(
        _merge_route_body,
        grid=(n // tm,),
        in_specs=[pl.BlockSpec((tm, d), lambda i: (i, 0)),
                  pl.BlockSpec((tm, ATT_WIDTH), lambda i: (i, 0)),
                  pl.BlockSpec((tm, MLSTM_WIDTH), lambda i: (i, 0)),
                  pl.BlockSpec((tm, d), lambda i: (i, 0)),
                  pl.BlockSpec((tm, d), lambda i: (i, 1))] + [full(a) for a in consts],
        out_specs=[pl.BlockSpec((tm, d), lambda i: (i, 0)),
                   pl.BlockSpec((tm * ROW_CHUNKS, LANES), lambda i: (i, 0)),
                   pl.BlockSpec((TOP_K, tm), lambda i: (0, i)),
                   pl.BlockSpec((TOP_K, tm), lambda i: (0, i)),
                   pl.BlockSpec((1, e, LANES), lambda i: (i, 0, 0))],
        out_shape=[jax.ShapeDtypeStruct((n, d), jnp.float32),
                   jax.ShapeDtypeStruct((n * ROW_CHUNKS, LANES), jnp.float32),
                   tok(jnp.float32), tok(jnp.int32),
                   jax.ShapeDtypeStruct((n // tm, e, LANES), jnp.int32)],
        compiler_params=_params("parallel"),
        name="merge_route",
    )(x2, ya, ym, gates, gates, *consts)


def _segment_pieces(cnt, max_rows, fn):
    top = 1 << (max_rows.bit_length() - 1)
    sizes = [top >> i for i in range(top.bit_length())]
    rare = [s for s in sizes if s * 8 > top]

    def pieces(group, done):
        for size in group:
            bit = cnt & size

            def piece(done=done, size=size):
                fn(done, size)
            pl.when(bit != 0)(piece)
            done = done + bit

    pl.when(cnt >= rare[-1])(lambda: pieces(rare, 0))
    pieces([s for s in sizes if s not in rare], cnt & -rare[-1])


def _for_tile_segments(meta, tile_rows, make_copy, action):
    cnt_ref, off_ref, dst_ref = meta

    def per_expert(e, carry):
        off, dst = off_ref[0, 0, e], dst_ref[0, 0, e]
        _segment_pieces(cnt_ref[0, 0, e], tile_rows,
                        lambda done, size: getattr(make_copy(off + done, dst + done, size), action)())
        return carry
    lax.fori_loop(0, N_EXPERTS, per_expert, 0)


def _dispatch_body(lpos_ref, cnt_ref, off_ref, dst_ref, pad_len_ref, pad_row_ref, x_ref, out_ref,
                   cbuf, zbuf, sem, zsem, *, tc, rb):
    step = pl.program_id(0)
    last = pl.num_programs(0) - 1
    slot = step % 2

    def compact(t, carry):
        row = x_ref[t]
        for kk in range(TOP_K):
            cbuf[slot, lpos_ref[0, 0, t * TOP_K + kk]] = row
        return carry
    lax.fori_loop(0, tc, compact, 0, unroll=8)

    def seg_copy(sl):
        return lambda local, glob, size: pltpu.make_async_copy(
            cbuf.at[sl, pl.ds(local, size)], out_ref.at[pl.ds(glob, size)], sem.at[sl])

    def wait_tile(sl):
        pltpu.make_async_copy(cbuf.at[sl], out_ref.at[pl.ds(0, TOP_K * tc)], sem.at[sl]).wait()

    _for_tile_segments((cnt_ref, off_ref, dst_ref), tc, seg_copy(slot), "start")

    @pl.when(step > 0)
    def _():
        wait_tile(1 - slot)

    @pl.when(step == last)
    def _():
        wait_tile(slot)
        zbuf[...] = jnp.zeros_like(zbuf)

        def pads(action):
            def per_pad(j, carry):
                row = pad_row_ref[0, j]
                _segment_pieces(pad_len_ref[0, j], rb, lambda done, size: getattr(
                    pltpu.make_async_copy(zbuf.at[pl.ds(0, size)], out_ref.at[pl.ds(row + done, size)], zsem),
                    action)())
                return carry
            lax.fori_loop(0, pad_len_ref.shape[1], per_pad, 0)
        pads("start")
        pads("wait")


def dispatch(lpos, seg_cnt, seg_off, seg_dst, pad_len, pad_row, h2_tiles, n_rows, *, tc, rb):
    n = h2_tiles.shape[0]
    steps = n // tc
    assert n % tc == 0
    smem = pltpu.SMEM
    seg = pl.BlockSpec((1, 1, N_EXPERTS), lambda i: (i, 0, 0), memory_space=smem)
    whole = lambda a: pl.BlockSpec(a.shape, lambda i: (0,) * a.ndim, memory_space=smem)
    return pl.pallas_call(
        functools.partial(_dispatch_body, tc=tc, rb=rb),
        grid=(steps,),
        in_specs=[pl.BlockSpec((1, 1, TOP_K * tc), lambda i: (i, 0, 0), memory_space=smem),
                  seg, seg, seg, whole(pad_len), whole(pad_row),
                  pl.BlockSpec((tc, ROW_CHUNKS, LANES), lambda i: (i, 0, 0))],
        out_specs=pl.BlockSpec(memory_space=pl.ANY),
        out_shape=jax.ShapeDtypeStruct((n_rows, ROW_CHUNKS, LANES), h2_tiles.dtype),
        scratch_shapes=[pltpu.VMEM((2, TOP_K * tc, ROW_CHUNKS, LANES), h2_tiles.dtype),
                        pltpu.VMEM((rb, ROW_CHUNKS, LANES), h2_tiles.dtype),
                        pltpu.SemaphoreType.DMA((2,)), pltpu.SemaphoreType.DMA(())],
        compiler_params=_params("arbitrary"),
        name="dispatch",
    )(lpos, seg_cnt, seg_off, seg_dst, pad_len, pad_row, h2_tiles)


def _experts_body(blk_e_ref, nxt_e_ref, used_ref, x_ref, wgu_hbm, bg_ref, bu_ref, wd_hbm, bd_ref, perm_ref,
                  y_ref, wgu_st, wd_st, wg_sc, wu_sc, wd_sc, sem, *, halves):
    f32, bf16 = jnp.float32, jnp.bfloat16
    step = pl.program_id(0)
    live = step < used_ref[0]
    rows = x_ref.shape[0] // ROW_CHUNKS
    expert = blk_e_ref[step]

    def fetch(e, action):
        getattr(pltpu.make_async_copy(wgu_hbm.at[e], wgu_st, sem.at[0]), action)()
        getattr(pltpu.make_async_copy(wd_hbm.at[e], wd_st, sem.at[1]), action)()

    @pl.when(live & ((step == 0) | (expert != blk_e_ref[jnp.maximum(step - 1, 0)])))
    def _():
        @pl.when(step == 0)
        def _():
            fetch(expert, "start")
        fetch(expert, "wait")
        tile = perm_ref.shape[0]
        half = tile // 2
        for t in range(wgu_st.shape[1] // tile):
            w = wgu_st[:, t * tile:(t + 1) * tile].astype(bf16)
            o = jnp.dot(w, perm_ref[...], preferred_element_type=f32)
            wg_sc[:, t * half:(t + 1) * half] = o[:, :half].astype(bf16)
            wu_sc[:, t * half:(t + 1) * half] = o[:, half:].astype(bf16)
        wd_sc[...] = wd_st[...].astype(bf16)
        nxt = nxt_e_ref[step]

        @pl.when(nxt >= 0)
        def _():
            fetch(nxt, "start")

    @pl.when(live)
    def _():
        part = rows // halves
        for h in range(halves):
            xb = jnp.concatenate(
                [x_ref[pl.ds(h * part * ROW_CHUNKS + c, part, stride=ROW_CHUNKS), :]
                 for c in range(ROW_CHUNKS)], axis=1).astype(bf16)
            gate = jnp.dot(xb, wg_sc[...], preferred_element_type=f32) + bg_ref[0]
            up = jnp.dot(xb, wu_sc[...], preferred_element_type=f32) + bu_ref[0]
            gate = jnp.minimum(gate, SWIGLU_LIMIT)
            up = jnp.clip(up, -SWIGLU_LIMIT, SWIGLU_LIMIT)
            act = (up + 1.0) * (gate * jax.nn.sigmoid(gate * SWIGLU_ALPHA))
            y = jnp.dot(act.astype(bf16), wd_sc[...], preferred_element_type=f32) + bd_ref[0]
            for c in range(ROW_CHUNKS):
                y_ref[pl.ds(h * part * ROW_CHUNKS + c, part, stride=ROW_CHUNKS), :] = \
                    y[:, c * LANES:(c + 1) * LANES]

    @pl.when(jnp.logical_not(live))
    def _():
        y_ref[...] = jnp.zeros_like(y_ref)


def experts(blk_expert, nxt_expert, n_used, xs_tiles, w_gu, b_g, b_u, w_d, b_d, *, halves=2):
    n_blocks = blk_expert.shape[0]
    rb = MOE_ROW_BLOCK
    d, ff = w_d.shape[2], w_d.shape[1]
    tile = 2 * LANES
    perm = np.zeros((tile, tile), np.float32)
    perm[2 * np.arange(LANES), np.arange(LANES)] = 1.0
    perm[2 * np.arange(LANES) + 1, LANES + np.arange(LANES)] = 1.0
    e_map3 = lambda i, be, nx, nu: (be[i], 0, 0)
    rows_map = lambda i, be, nx, nu: (i, 0)
    grid_spec = pltpu.PrefetchScalarGridSpec(
        num_scalar_prefetch=3,
        grid=(n_blocks,),
        in_specs=[pl.BlockSpec((rb * ROW_CHUNKS, LANES), rows_map),
                  pl.BlockSpec(memory_space=pl.ANY),
                  pl.BlockSpec((1, 1, ff), e_map3), pl.BlockSpec((1, 1, ff), e_map3),
                  pl.BlockSpec(memory_space=pl.ANY), pl.BlockSpec((1, 1, d), e_map3),
                  pl.BlockSpec((tile, tile), lambda i, be, nx, nu: (0, 0))],
        out_specs=pl.BlockSpec((rb * ROW_CHUNKS, LANES), rows_map),
        scratch_shapes=[pltpu.VMEM((d, 2 * ff), w_gu.dtype), pltpu.VMEM((ff, d), w_d.dtype),
                        pltpu.VMEM((d, ff), jnp.bfloat16), pltpu.VMEM((d, ff), jnp.bfloat16),
                        pltpu.VMEM((ff, d), jnp.bfloat16), pltpu.SemaphoreType.DMA((2,))],
    )
    return pl.pallas_call(
        functools.partial(_experts_body, halves=halves),
        grid_spec=grid_spec,
        out_shape=jax.ShapeDtypeStruct(xs_tiles.shape, jnp.float32),
        compiler_params=_params("arbitrary"),
        name="experts",
    )(blk_expert, nxt_expert, n_used, xs_tiles, w_gu, b_g, b_u, w_d, b_d, jnp.asarray(perm, jnp.bfloat16))


def _combine_ple_body(lpos_ref, gw_ref, cnt_ref, off_ref, dst_ref, ncnt_ref, noff_ref, ndst_ref,
                      x1_ref, ys_ref, p_ref, gp_ref, wpg_ref, wpp_ref, *rest):
    cbuf, msum, sem = rest[-3:]
    rest = rest[:-3]
    gf_ref, o_ref = rest if len(rest) == 2 else (None, rest[0])
    f32, bf16 = jnp.float32, jnp.bfloat16
    tm = x1_ref.shape[0]
    step = pl.program_id(0)
    slot = step % 2

    def seg_copy(sl):
        return lambda local, glob, size: pltpu.make_async_copy(
            ys_ref.at[pl.ds(glob, size)], cbuf.at[sl, pl.ds(local, size)], sem.at[sl])

    cur, nxt = (cnt_ref, off_ref, dst_ref), (ncnt_ref, noff_ref, ndst_ref)

    @pl.when(step == 0)
    def _():
        _for_tile_segments(cur, tm, seg_copy(0), "start")

    @pl.when(step + 1 < pl.num_programs(0))
    def _():
        _for_tile_segments(nxt, tm, seg_copy(1 - slot), "start")

    pltpu.make_async_copy(ys_ref.at[pl.ds(0, TOP_K * tm)], cbuf.at[slot], sem.at[slot]).wait()

    def weighted_sum(t, carry):
        acc = gw_ref[0, 0, t * TOP_K] * cbuf[slot, lpos_ref[0, 0, t * TOP_K]]
        for kk in range(1, TOP_K):
            acc = acc + gw_ref[0, 0, t * TOP_K + kk] * cbuf[slot, lpos_ref[0, 0, t * TOP_K + kk]]
        msum[t] = acc
        return carry
    lax.fori_loop(0, tm, weighted_sum, 0, unroll=8)

    moe = jnp.concatenate([msum[:, c, :] for c in range(ROW_CHUNKS)], axis=1)
    x2 = x1_ref[...] + moe
    r = _rms(x2, gp_ref[...]).astype(bf16)
    gate = jax.nn.sigmoid(jnp.dot(r, wpg_ref[...], preferred_element_type=f32))
    emb = jnp.dot(p_ref[...].astype(bf16), wpp_ref[...], preferred_element_type=f32)
    x3 = x2 + gate * emb
    o_ref[...] = x3 if gf_ref is None else _rms(x3, gf_ref[...])


def combine_ple(lpos, gate_w, seg_cnt, seg_off, seg_dst, x1, ys_tiles, p2, g_ple, w_pg, w_pp, g_final,
                *, tm):
    n, d = x1.shape
    steps = n // tm
    smem = pltpu.SMEM
    full = lambda a: pl.BlockSpec(a.shape, lambda i: (0,) * a.ndim)
    seg = lambda shift: pl.BlockSpec((1, 1, N_EXPERTS),
                                     lambda i: (jnp.minimum(i + shift, steps - 1), 0, 0), memory_space=smem)
    tok = pl.BlockSpec((1, 1, TOP_K * tm), lambda i: (i, 0, 0), memory_space=smem)
    consts = [g_ple.reshape(1, d), w_pg, w_pp] + ([] if g_final is None else [g_final.reshape(1, d)])
    return pl.pallas_call(
        _combine_ple_body,
        grid=(steps,),
        in_specs=[tok, tok, seg(0), seg(0), seg(0), seg(1), seg(1), seg(1),
                  pl.BlockSpec((tm, d), lambda i: (i, 0)),
                  pl.BlockSpec(memory_space=pl.ANY),
                  pl.BlockSpec((tm, p2.shape[1]), lambda i: (i, 0))] + [full(a) for a in consts],
        out_specs=pl.BlockSpec((tm, d), lambda i: (i, 0)),
        out_shape=jax.ShapeDtypeStruct((n, d), jnp.float32),
        scratch_shapes=[pltpu.VMEM((2, TOP_K * tm, ROW_CHUNKS, LANES), ys_tiles.dtype),
                        pltpu.VMEM((tm, ROW_CHUNKS, LANES), jnp.float32),
                        pltpu.SemaphoreType.DMA((2,))],
        compiler_params=_params("arbitrary"),
        name="combine_ple",
    )(lpos, gate_w, seg_cnt, seg_off, seg_dst, seg_cnt, seg_off, seg_dst, x1, ys_tiles, p2, *consts)


def _moe(h2_tiles, lpos, tile_cnt, w_gu, b_gu, w_d, b_d, *, tm):
    n = h2_tiles.shape[0] // ROW_CHUNKS
    rb = MOE_ROW_BLOCK
    i32 = jnp.int32
    n_blocks = -(-n * TOP_K // rb) + N_EXPERTS
    n_rows = n_blocks * rb
    counts = jnp.sum(tile_cnt, axis=0)
    padded = (counts + rb - 1) // rb * rb
    pend = jnp.cumsum(padded)
    pstart = pend - padded
    blk_start = jnp.arange(n_blocks, dtype=i32) * rb
    blk_expert = jnp.minimum(jnp.sum((pend[None, :] <= blk_start[:, None]).astype(i32), axis=1),
                             N_EXPERTS - 1)
    n_used = pend[-1:] // rb
    seg_cnt = tile_cnt[:, None, :]
    seg_off = (jnp.cumsum(tile_cnt, axis=1) - tile_cnt)[:, None, :]
    seg_dst = (pstart[None, :] + jnp.cumsum(tile_cnt, axis=0) - tile_cnt)[:, None, :]
    tail = n_used + jnp.arange(N_EXPERTS, dtype=i32)
    pad_len = jnp.concatenate([padded - counts, jnp.where(tail < n_blocks, rb, 0)])[None, :]
    pad_row = jnp.concatenate([pstart + counts, tail * rb])[None, :]
    b_g, b_u = b_gu[:, None, 0::2], b_gu[:, None, 1::2]
    xs = dispatch(lpos, seg_cnt, seg_off, seg_dst, pad_len.astype(i32), pad_row.astype(i32),
                  h2_tiles.reshape(n, ROW_CHUNKS, LANES), n_rows, tc=tm, rb=rb)
    e_ids = jnp.arange(N_EXPERTS, dtype=i32)
    later = (padded > 0)[None, :] & (e_ids[None, :] > e_ids[:, None])
    nxt_tab = jnp.min(jnp.where(later, e_ids[None, :], N_EXPERTS), axis=1)
    nxt_tab = jnp.where(nxt_tab == N_EXPERTS, -1, nxt_tab)
    nxt_expert = jnp.sum(jnp.where(blk_expert[:, None] == e_ids[None, :], nxt_tab[None, :], 0), axis=1)
    ys = experts(blk_expert, nxt_expert.astype(i32), n_used.astype(i32),
                 xs.reshape(n_rows * ROW_CHUNKS, LANES), w_gu, b_g, b_u, w_d, b_d[:, None, :])
    return (seg_cnt, seg_off, seg_dst), ys.reshape(n_rows, ROW_CHUNKS, LANES)


def kernel(x, p, norm_mix_g, w_in, conv_w, conv_b, w_qk_m, w_if, b_if, mnorm_g, w_branch, w_out, norm_ffn_g, w_router, b_router, w_gate_up, b_gate_up, w_down, b_down, norm_ple_g, w_ple_gate, w_ple_proj, final_norm_g):
    B, S, D = x.shape
    depth = w_in.shape[0]
    bf16 = jnp.bfloat16
    x2 = x.reshape(B * S, D)
    for i in range(depth):
        qkv, xo, gates = in_proj(x2, norm_mix_g[i], w_in[i].astype(bf16))
        y_a = moba(qkv, B, S)
        y_m = mlstm(xo, conv_w[i], conv_b[i], w_qk_m[i], w_if[i], b_if[i], mnorm_g[i], B, S)
        x1, h2_tiles, gate_w, lpos, tile_cnt = merge_route(
            x2, y_a, y_m, gates, w_branch[i, 0].astype(bf16), w_branch[i, 1].astype(bf16),
            w_out[i].astype(bf16), norm_ffn_g[i], w_router[i], b_router[i], tm=MOE_TOKEN_TILE)
        by_tile = lambda a: a.T.reshape(-1, 1, MOE_TOKEN_TILE * TOP_K)
        lpos, gate_w = by_tile(lpos), by_tile(gate_w)
        segs, ys = _moe(h2_tiles, lpos, tile_cnt[:, :, 0], w_gate_up[i], b_gate_up[i],
                        w_down[i], b_down[i], tm=MOE_TOKEN_TILE)
        x2 = combine_ple(lpos, gate_w, *segs, x1, ys, p[i].reshape(B * S, -1), norm_ple_g[i],
                         w_ple_gate[i].astype(bf16), w_ple_proj[i].astype(bf16),
                         final_norm_g if i == depth - 1 else None, tm=MOE_TOKEN_TILE)
    return x2.reshape(B, S, D)
```

```python
import functools

import jax
import jax.numpy as jnp
import numpy as np
from jax import lax
from jax.experimental import pallas as pl
from jax.experimental.pallas import tpu as pltpu

RMS_EPS = 1e-6
LANES = 128
ROW_CHUNKS = 8
VMEM_LIMIT = 56 * 1024 * 1024

N_ATT_HEADS = 8
ATT_HEAD_DIM = 64
ATT_WIDTH = N_ATT_HEADS * ATT_HEAD_DIM
MOBA_BLOCK = 256
MOBA_TOPK = 3

N_MLSTM_HEADS = 4
MLSTM_WIDTH = 512
MLSTM_V_DIM = 128
MLSTM_QK_DIM = 64
MLSTM_CONV = 4
MLSTM_CHUNK = 256

N_EXPERTS = 32
TOP_K = 4
SWIGLU_ALPHA = 1.702
SWIGLU_LIMIT = 7.0
MOE_ROW_BLOCK = 512
MOE_TOKEN_TILE = 512

NEG_BIG = -1e30

_NT = (((1,), (1,)), ((), ()))
_TN = (((0,), (0,)), ((), ()))


def _params(*sem):
    return pltpu.CompilerParams(dimension_semantics=sem, vmem_limit_bytes=VMEM_LIMIT)


def _rms(x, g):
    return x * lax.rsqrt(jnp.mean(x * x, axis=-1, keepdims=True) + RMS_EPS) * g


def _bf16_parts(x, n):
    parts = []
    for _ in range(n):
        p = x.astype(jnp.bfloat16)
        parts.append(p)
        x = x - p.astype(jnp.float32)
    return parts


def _in_proj_body(x_ref, g_ref, w_ref, qkv_ref, xo_ref, gate_ref, *, col_chunk):
    h = _rms(x_ref[...], g_ref[...]).astype(jnp.bfloat16)
    col = 0
    for out_ref in (qkv_ref, xo_ref, gate_ref):
        for c in range(0, out_ref.shape[1], col_chunk):
            out_ref[:, c:c + col_chunk] = jnp.dot(
                h, w_ref[:, col + c:col + c + col_chunk],
                preferred_element_type=jnp.float32).astype(out_ref.dtype)
        col += out_ref.shape[1]


def in_proj(x2, g, w_bf16, *, tm=512, col_chunk=512):
    n, d = x2.shape
    widths = (3 * ATT_WIDTH, 2 * MLSTM_WIDTH, 2 * d)
    assert sum(widths) == w_bf16.shape[1] and n % tm == 0
    return pl.pallas_call(
        functools.partial(_in_proj_body, col_chunk=col_chunk),
        grid=(n // tm,),
        in_specs=[pl.BlockSpec((tm, d), lambda i: (i, 0)),
                  pl.BlockSpec((1, d), lambda i: (0, 0)),
                  pl.BlockSpec(w_bf16.shape, lambda i: (0, 0))],
        out_specs=[pl.BlockSpec((tm, w), lambda i: (i, 0)) for w in widths],
        out_shape=[jax.ShapeDtypeStruct((n, w), jnp.bfloat16) for w in widths],
        compiler_params=_params("parallel"),
        name="in_proj",
    )(x2, g.reshape(1, d), w_bf16)


def _moba_body(q_ref, k_ref, v_ref, o_ref, *, pair, seq):
    blk = MOBA_BLOCK
    n_blk = seq // blk
    f32, bf16 = jnp.float32, jnp.bfloat16
    lane = lax.broadcasted_iota(jnp.int32, (1, LANES), 1)
    first_head = lane < ATT_HEAD_DIM
    row = lax.broadcasted_iota(jnp.int32, (2 * blk, blk), 0)
    col = lax.broadcasted_iota(jnp.int32, (2 * blk, blk), 1)
    t_in_blk = jnp.where(row >= blk, row - blk, row)
    rel = (t_in_blk - col).astype(f32)
    causal = t_in_blk >= col
    log2e = float(np.log2(np.e))
    scale2 = ATT_HEAD_DIM ** -0.5 * log2e
    head = (2 * pair + (row[:, 0:1] >= blk).astype(jnp.int32)).astype(f32)
    slope2 = jnp.exp2(-8.0 * (head + 1.0) / N_ATT_HEADS) * log2e
    bias = [slope2 * (rel + float(d * blk)) for d in range(n_blk)]
    lane_row = lax.broadcasted_iota(jnp.int32, (LANES, LANES), 0)
    spread = [jnp.where(lane_row == j, 1.0, 0.0).astype(bf16) for j in range(n_blk)]

    k_all = k_ref[...]
    v_ones = jnp.concatenate([v_ref[...], jnp.ones((seq, LANES), bf16)], axis=1)
    k_mean = jnp.concatenate(
        [jnp.mean(k_all[j * blk:(j + 1) * blk].astype(f32), axis=0, keepdims=True)
         for j in range(n_blk)] + [jnp.zeros((LANES - n_blk, LANES), f32)], axis=0)
    k_mean_parts = _bf16_parts(k_mean, 3)

    for qi in range(n_blk):
        q_blk = q_ref[qi * blk:(qi + 1) * blk, :]
        zero = jnp.zeros_like(q_blk)
        q2 = jnp.concatenate([jnp.where(first_head, q_blk, zero), jnp.where(first_head, zero, q_blk)], axis=0)
        sel = None
        if qi > MOBA_TOPK:
            g = sum(lax.dot_general(q2, part, _NT, preferred_element_type=f32) for part in k_mean_parts)
            g = jnp.where(lane < qi, g, -jnp.inf)
            ahead = jnp.zeros((2 * blk, LANES), f32)
            for d in range(1, qi):
                ahead = ahead + jnp.where(pltpu.roll(g, d, axis=1) >= g, 1.0, 0.0)
                ahead = ahead + jnp.where(pltpu.roll(g, LANES - d, axis=1) > g, 1.0, 0.0)
            keep = jnp.where(ahead < MOBA_TOPK, 1.0, 0.0).astype(bf16)
            sel = [jnp.dot(keep, spread[j], preferred_element_type=f32) > 0.5 for j in range(qi)]
        logits = []
        for j in range(qi + 1):
            s = lax.dot_general(q2, k_all[j * blk:(j + 1) * blk], _NT, preferred_element_type=f32)
            s = s * scale2 - bias[qi - j]
            if j == qi:
                s = jnp.where(causal, s, NEG_BIG)
            elif sel is not None:
                s = jnp.where(jnp.concatenate([sel[j]] * (blk // LANES), axis=1), s, NEG_BIG)
            logits.append(s)
        s_all = jnp.concatenate(logits, axis=1)
        m = jnp.max(s_all, axis=1, keepdims=True)
        p = jnp.exp2(s_all - m)
        o2 = jnp.dot(p.astype(bf16), v_ones[:(qi + 1) * blk], preferred_element_type=f32)
        o2 = o2[:, :LANES] / o2[:, LANES:LANES + 1]
        o = jnp.where(first_head, o2[:blk], o2[blk:])
        o_ref[qi * blk:(qi + 1) * blk, :] = o.astype(o_ref.dtype)


def _moba_kernel(q_ref, k_ref, v_ref, o_ref, *, seq):
    _moba_body(q_ref, k_ref, v_ref, o_ref, pair=pl.program_id(1), seq=seq)


def moba(qkv, batch, seq):
    n = batch * seq
    n_pairs = ATT_WIDTH // LANES
    assert seq % MOBA_BLOCK == 0
    spec = lambda off: pl.BlockSpec((seq, LANES), lambda b, p: (b, off + p))
    return pl.pallas_call(
        functools.partial(_moba_kernel, seq=seq),
        grid=(batch, n_pairs),
        in_specs=[spec(0), spec(n_pairs), spec(2 * n_pairs)],
        out_specs=pl.BlockSpec((seq, LANES), lambda b, p: (b, p)),
        out_shape=jax.ShapeDtypeStruct((n, ATT_WIDTH), jnp.bfloat16),
        compiler_params=_params("parallel", "parallel"),
        name="moba",
    )(qkv, qkv, qkv)


def _log_sigmoid(z):
    return jnp.minimum(z, 0.0) - jnp.log(1.0 + jnp.exp(-jnp.abs(z)))


def _mlstm_body(xm_ref, om_ref, cw_ref, cb_ref, wq_ref, wk_ref, wiq_ref, wik_ref, wix_ref,
                wiqt_ref, wikt_ref, wixt_ref, bi_ref, bit_ref, g_ref, y_ref,
                q_sc, k_sc, gcol_sc, grow_sc, ct_sc, n_sc, m_sc, *, seq):
    L = MLSTM_CHUNK
    H = N_MLSTM_HEADS
    n_chunks = seq // L
    f32, bf16 = jnp.float32, jnp.bfloat16

    x = xm_ref[...].astype(f32)
    t_idx = lax.broadcasted_iota(jnp.int32, (seq, 1), 0)
    acc = x * cw_ref[MLSTM_CONV - 1:MLSTM_CONV, :] + cb_ref[...]
    for d in range(1, MLSTM_CONV):
        shifted = jnp.where(t_idx >= d, pltpu.roll(x, d, axis=0), 0.0)
        acc = acc + shifted * cw_ref[MLSTM_CONV - 1 - d:MLSTM_CONV - d, :]
    xc = (acc * jax.nn.sigmoid(acc)).astype(bf16)

    xm = xm_ref[...]
    if_col = jnp.dot(xm, wix_ref[...], preferred_element_type=f32) + bi_ref[...]
    if_row = lax.dot_general(wixt_ref[...], xm, _NT, preferred_element_type=f32) + bit_ref[...]
    for h in range(H):
        xch = xc[:, h * LANES:(h + 1) * LANES]
        q = jnp.dot(xch, wq_ref[h], preferred_element_type=f32).astype(bf16)
        k = jnp.dot(xch, wk_ref[h], preferred_element_type=f32).astype(bf16)
        if_col = if_col + jnp.dot(q, wiq_ref[h], preferred_element_type=f32)
        if_col = if_col + jnp.dot(k, wik_ref[h], preferred_element_type=f32)
        if_row = if_row + lax.dot_general(wiqt_ref[h], q, _NT, preferred_element_type=f32)
        if_row = if_row + lax.dot_general(wikt_ref[h], k, _NT, preferred_element_type=f32)
        q_sc[h] = q
        k_sc[h] = (k.astype(f32) * (MLSTM_QK_DIM ** -0.5)).astype(bf16)
    is_f_col = lax.broadcasted_iota(jnp.int32, (1, 2 * H), 1) >= H
    is_f_row = lax.broadcasted_iota(jnp.int32, (2 * H, 1), 0) >= H
    gcol_sc[...] = jnp.where(is_f_col, _log_sigmoid(if_col), if_col)
    g_row = jnp.where(is_f_row, _log_sigmoid(if_row), if_row)
    for c in range(n_chunks):
        grow_sc[c] = g_row[:, c * L:(c + 1) * L]

    ct_sc[...] = jnp.zeros_like(ct_sc)
    n_sc[...] = jnp.zeros_like(n_sc)
    m_sc[...] = jnp.zeros_like(m_sc)

    r_i = lax.broadcasted_iota(jnp.int32, (L, L), 0)
    c_i = lax.broadcasted_iota(jnp.int32, (L, L), 1)
    causal = r_i >= c_i
    tri_lo = jnp.where(causal, 1.0, 0.0).astype(f32)
    tri_up = jnp.where(c_i >= r_i, 1.0, 0.0).astype(f32)

    def chunk(c, carry):
        r0 = pl.multiple_of(c * L, L)
        gc = gcol_sc[pl.ds(r0, L), :]
        gr = grow_sc[c]
        b_cols = jnp.dot(tri_lo, gc, precision=lax.Precision.HIGHEST, preferred_element_type=f32)
        b_rows = jnp.dot(gr, tri_up, precision=lax.Precision.HIGHEST, preferred_element_type=f32)
        for h in range(H):
            hs = slice(h * LANES, (h + 1) * LANES)
            b_col = b_cols[:, H + h:H + h + 1]
            i_col = gc[:, h:h + 1]
            b_row = b_rows[H + h:H + h + 1, :]
            i_row = gr[h:h + 1, :]
            b_last = b_col[L - 1:L, :]
            m_prev = m_sc[h][:, 0:1]
            q = q_sc[h, pl.ds(r0, L), :]
            k = k_sc[h, pl.ds(r0, L), :]
            v = xm_ref[pl.ds(r0, L), hs]

            log_d = jnp.where(causal, b_col + (i_row - b_row), NEG_BIG)
            inter = b_col + m_prev
            m_t = jnp.maximum(inter, jnp.max(log_d, axis=1, keepdims=True))
            w_inter = jnp.exp(inter - m_t)
            s = lax.dot_general(q, k, _NT, preferred_element_type=f32) * jnp.exp(log_d - m_t)
            num = jnp.dot(s.astype(bf16), v, preferred_element_type=f32)
            num = num + w_inter * jnp.dot(q, ct_sc[h].astype(bf16), preferred_element_type=f32)
            den = jnp.sum(s, axis=1, keepdims=True)
            den = den + w_inter * jnp.sum(q.astype(f32) * n_sc[h], axis=1, keepdims=True)
            hv = num / jnp.maximum(jnp.abs(den), jnp.exp(-m_t))
            hv = hv * lax.rsqrt(jnp.mean(hv * hv, axis=1, keepdims=True) + RMS_EPS) * g_ref[:, hs]
            o_gate = jax.nn.sigmoid(om_ref[pl.ds(r0, L), hs].astype(f32))
            y_ref[pl.ds(r0, L), hs] = (o_gate * hv).astype(y_ref.dtype)

            g_col = b_last - b_col + i_col
            m_new = jnp.maximum(b_last + m_prev, jnp.max(g_col, axis=0, keepdims=True))
            wg = jnp.exp(g_col - m_new)
            decay = jnp.exp(b_last + m_prev - m_new)
            wv = (wg * v.astype(f32)).astype(bf16)
            ct_sc[h] = decay * ct_sc[h] + lax.dot_general(k, wv, _TN, preferred_element_type=f32)
            n_sc[h] = decay * n_sc[h] + jnp.sum(wg * k.astype(f32), axis=0, keepdims=True)
            m_sc[h] = jnp.broadcast_to(m_new, (1, LANES))
        return carry

    lax.fori_loop(0, n_chunks, chunk, 0)


def _mlstm_kernel(*refs, seq):
    _mlstm_body(*refs, seq=seq)


def mlstm(xo, conv_w, conv_b, w_qk, w_if, b_if, g, batch, seq):
    n = batch * seq
    H, dv, dk = N_MLSTM_HEADS, MLSTM_V_DIM, MLSTM_QK_DIM
    assert seq % MLSTM_CHUNK == 0 and dv == LANES
    bf16 = jnp.bfloat16
    pad = ((0, 0), (0, 0), (0, LANES - dk))
    wq = jnp.pad(w_qk[:, :, :dk], pad).astype(bf16)
    wk = jnp.pad(w_qk[:, :, dk:], pad).astype(bf16)
    rpad = ((0, 0), (0, LANES - dk), (0, 0))
    wiq = jnp.pad(w_if[:H * dk].reshape(H, dk, 2 * H), rpad).astype(bf16)
    wik = jnp.pad(w_if[H * dk:2 * H * dk].reshape(H, dk, 2 * H), rpad).astype(bf16)
    wix = w_if[2 * H * dk:].astype(bf16)
    wiqt, wikt, wixt = wiq.transpose(0, 2, 1), wik.transpose(0, 2, 1), wix.T
    full = lambda a: pl.BlockSpec(a.shape, lambda b: (0,) * a.ndim)
    consts = [conv_w, conv_b.reshape(1, -1), wq, wk, wiq, wik, wix, wiqt, wikt, wixt,
              b_if.reshape(1, -1), b_if.reshape(-1, 1), g.reshape(1, -1)]
    return pl.pallas_call(
        functools.partial(_mlstm_kernel, seq=seq),
        grid=(batch,),
        in_specs=[pl.BlockSpec((seq, MLSTM_WIDTH), lambda b: (b, 0)),
                  pl.BlockSpec((seq, MLSTM_WIDTH), lambda b: (b, 1))] + [full(a) for a in consts],
        out_specs=pl.BlockSpec((seq, MLSTM_WIDTH), lambda b: (b, 0)),
        out_shape=jax.ShapeDtypeStruct((n, MLSTM_WIDTH), bf16),
        scratch_shapes=[pltpu.VMEM((H, seq, LANES), bf16), pltpu.VMEM((H, seq, LANES), bf16),
                        pltpu.VMEM((seq, 2 * H), jnp.float32),
                        pltpu.VMEM((seq // MLSTM_CHUNK, 2 * H, MLSTM_CHUNK), jnp.float32),
                        pltpu.VMEM((H, LANES, LANES), jnp.float32),
                        pltpu.VMEM((H, 1, LANES), jnp.float32),
                        pltpu.VMEM((H, 1, LANES), jnp.float32)],
        compiler_params=_params("parallel"),
        name="mlstm",
    )(xo, xo, *consts)


def _store_rows_as_tiles(dst_ref, val):
    rows = val.shape[0]
    for c in range(ROW_CHUNKS):
        dst_ref[pl.ds(c, rows, stride=ROW_CHUNKS), :] = val[:, c * LANES:(c + 1) * LANES]


def _load_tiles_as_rows(src_ref, rows, lead=()):
    return jnp.concatenate(
        [src_ref[lead + (pl.ds(c, rows, stride=ROW_CHUNKS), slice(None))] for c in range(ROW_CHUNKS)],
        axis=1)


def _merge_route_body(x_ref, ya_ref, ym_ref, ga_ref, gm_ref, wb0_ref, wb1_ref, wo_ref, g_ref,
                      wr_ref, br_ref, x1_ref, h2_ref, gate_ref, lpos_ref, cnt_ref):
    f32, bf16 = jnp.float32, jnp.bfloat16
    tm = x_ref.shape[0]

    u = jax.nn.sigmoid(ga_ref[...].astype(f32)) * jnp.dot(ya_ref[...], wb0_ref[...], preferred_element_type=f32)
    u = u + jax.nn.sigmoid(gm_ref[...].astype(f32)) * jnp.dot(ym_ref[...], wb1_ref[...], preferred_element_type=f32)
    x1 = x_ref[...] + jnp.dot(u.astype(bf16), wo_ref[...], preferred_element_type=f32)
    x1_ref[...] = x1
    h2 = _rms(x1, g_ref[...])
    _store_rows_as_tiles(h2_ref, h2)
    (h_hi, h_lo), (w_hi, w_lo) = _bf16_parts(h2, 2), _bf16_parts(wr_ref[...], 2)
    logits = (jnp.dot(h_hi, w_hi, preferred_element_type=f32) + jnp.dot(h_lo, w_hi, preferred_element_type=f32)
              + jnp.dot(h_hi, w_lo, preferred_element_type=f32)) + br_ref[...]
    logits = jnp.transpose(logits)[:N_EXPERTS]
    e_id = lax.broadcasted_iota(jnp.int32, logits.shape, 0).astype(f32)
    chosen = jnp.zeros(logits.shape, f32)
    vals, ids = [], []
    for _ in range(TOP_K):
        top = jnp.max(logits, axis=0, keepdims=True)
        first = jnp.min(jnp.where(logits == top, e_id, float(N_EXPERTS)), axis=0, keepdims=True)
        hit = e_id == first
        chosen = jnp.where(hit, 1.0, chosen)
        logits = jnp.where(hit, -jnp.inf, logits)
        vals.append(top)
        ids.append(first)
    ex = [jnp.exp(v - vals[0]) for v in vals]
    total = ex[0] + ex[1] + ex[2] + ex[3]

    chosen_b = chosen.astype(bf16)
    t_r = lax.broadcasted_iota(jnp.int32, (tm, tm), 0)
    t_c = lax.broadcasted_iota(jnp.int32, (tm, tm), 1)
    earlier = jnp.where(t_r < t_c, 1.0, 0.0).astype(bf16)
    before = jnp.dot(chosen_b, earlier, preferred_element_type=f32)
    e_r = lax.broadcasted_iota(jnp.int32, (N_EXPERTS, N_EXPERTS), 0)
    e_c = lax.broadcasted_iota(jnp.int32, (N_EXPERTS, N_EXPERTS), 1)
    lower = jnp.where(e_c < e_r, 1.0, 0.0).astype(bf16)
    seg_off = jnp.sum(jnp.dot(lower, chosen_b, preferred_element_type=f32), axis=1, keepdims=True)
    pos = before + seg_off
    for kk in range(TOP_K):
        gate_ref[kk:kk + 1, :] = ex[kk] / total
        lpos_ref[kk:kk + 1, :] = jnp.sum(jnp.where(e_id == ids[kk], pos, 0.0), axis=0,
                                         keepdims=True).astype(jnp.int32)
    counts = jnp.sum(chosen, axis=1, keepdims=True)
    cnt_ref[0] = jnp.broadcast_to(counts, (N_EXPERTS, LANES)).astype(jnp.int32)


def merge_route(x2, ya, ym, gates, wb0, wb1, wo, g, w_router, b_router, *, tm):
    n, d = x2.shape
    e = w_router.shape[1]
    assert n % tm == 0 and d == ROW_CHUNKS * LANES and e == N_EXPERTS
    full = lambda a: pl.BlockSpec(a.shape, lambda i: (0,) * a.ndim)
    lane_pad = ((0, 0), (0, LANES - e))
    consts = [wb0, wb1, wo, g.reshape(1, d), jnp.pad(w_router, lane_pad),
              jnp.pad(b_router.reshape(1, e), lane_pad)]
    tok = lambda dt: jax.ShapeDtypeStruct((TOP_K, n), dt)
    return pl.pallas_call(
        _merge_route_body,
        grid=(n // tm,),
        in_specs=[pl.BlockSpec((tm, d), lambda i: (i, 0)),
                  pl.BlockSpec((tm, ATT_WIDTH), lambda i: (i, 0)),
                  pl.BlockSpec((tm, MLSTM_WIDTH), lambda i: (i, 0)),
                  pl.BlockSpec((tm, d), lambda i: (i, 0)),
                  pl.BlockSpec((tm, d), lambda i: (i, 1))] + [full(a) for a in consts],
        out_specs=[pl.BlockSpec((tm, d), lambda i: (i, 0)),
                   pl.BlockSpec((tm * ROW_CHUNKS, LANES), lambda i: (i, 0)),
                   pl.BlockSpec((TOP_K, tm), lambda i: (0, i)),
                   pl.BlockSpec((TOP_K, tm), lambda i: (0, i)),
                   pl.BlockSpec((1, e, LANES), lambda i: (i, 0, 0))],
        out_shape=[jax.ShapeDtypeStruct((n, d), jnp.float32),
                   jax.ShapeDtypeStruct((n * ROW_CHUNKS, LANES), jnp.float32),
                   tok(jnp.float32), tok(jnp.int32),
                   jax.ShapeDtypeStruct((n // tm, e, LANES), jnp.int32)],
        compiler_params=_params("parallel"),
        name="merge_route",
    )(x2, ya, ym, gates, gates, *consts)


def _segment_pieces(cnt, max_rows, fn):
    top = 1 << (max_rows.bit_length() - 1)
    sizes = [top >> i for i in range(top.bit_length())]
    rare = [s for s in sizes if s * 8 > top]

    def pieces(group, done):
        for size in group:
            bit = cnt & size

            def piece(done=done, size=size):
                fn(done, size)
            pl.when(bit != 0)(piece)
            done = done + bit

    pl.when(cnt >= rare[-1])(lambda: pieces(rare, 0))
    pieces([s for s in sizes if s not in rare], cnt & -rare[-1])


def _for_tile_segments(meta, tile_rows, make_copy, action):
    cnt_ref, off_ref, dst_ref = meta

    def per_expert(e, carry):
        off, dst = off_ref[0, 0, e], dst_ref[0, 0, e]
        _segment_pieces(cnt_ref[0, 0, e], tile_rows,
                        lambda done, size: getattr(make_copy(off + done, dst + done, size), action)())
        return carry
    lax.fori_loop(0, N_EXPERTS, per_expert, 0)


def _dispatch_body(lpos_ref, cnt_ref, off_ref, dst_ref, pad_len_ref, pad_row_ref, x_ref, out_ref,
                   cbuf, zbuf, sem, zsem, *, tc, rb):
    step = pl.program_id(0)
    last = pl.num_programs(0) - 1
    slot = step % 2

    def compact(t, carry):
        row = x_ref[t]
        for kk in range(TOP_K):
            cbuf[slot, lpos_ref[0, 0, t * TOP_K + kk]] = row
        return carry
    lax.fori_loop(0, tc, compact, 0, unroll=8)

    def seg_copy(sl):
        return lambda local, glob, size: pltpu.make_async_copy(
            cbuf.at[sl, pl.ds(local, size)], out_ref.at[pl.ds(glob, size)], sem.at[sl])

    def wait_tile(sl):
        pltpu.make_async_copy(cbuf.at[sl], out_ref.at[pl.ds(0, TOP_K * tc)], sem.at[sl]).wait()

    _for_tile_segments((cnt_ref, off_ref, dst_ref), tc, seg_copy(slot), "start")

    @pl.when(step > 0)
    def _():
        wait_tile(1 - slot)

    @pl.when(step == last)
    def _():
        wait_tile(slot)
        zbuf[...] = jnp.zeros_like(zbuf)

        def pads(action):
            def per_pad(j, carry):
                row = pad_row_ref[0, j]
                _segment_pieces(pad_len_ref[0, j], rb, lambda done, size: getattr(
                    pltpu.make_async_copy(zbuf.at[pl.ds(0, size)], out_ref.at[pl.ds(row + done, size)], zsem),
                    action)())
                return carry
            lax.fori_loop(0, pad_len_ref.shape[1], per_pad, 0)
        pads("start")
        pads("wait")


def dispatch(lpos, seg_cnt, seg_off, seg_dst, pad_len, pad_row, h2_tiles, n_rows, *, tc, rb):
    n = h2_tiles.shape[0]
    steps = n // tc
    assert n % tc == 0
    smem = pltpu.SMEM
    seg = pl.BlockSpec((1, 1, N_EXPERTS), lambda i: (i, 0, 0), memory_space=smem)
    whole = lambda a: pl.BlockSpec(a.shape, lambda i: (0,) * a.ndim, memory_space=smem)
    return pl.pallas_call(
        functools.partial(_dispatch_body, tc=tc, rb=rb),
        grid=(steps,),
        in_specs=[pl.BlockSpec((1, 1, TOP_K * tc), lambda i: (i, 0, 0), memory_space=smem),
                  seg, seg, seg, whole(pad_len), whole(pad_row),
                  pl.BlockSpec((tc, ROW_CHUNKS, LANES), lambda i: (i, 0, 0))],
        out_specs=pl.BlockSpec(memory_space=pl.ANY),
        out_shape=jax.ShapeDtypeStruct((n_rows, ROW_CHUNKS, LANES), h2_tiles.dtype),
        scratch_shapes=[pltpu.VMEM((2, TOP_K * tc, ROW_CHUNKS, LANES), h2_tiles.dtype),
                        pltpu.VMEM((rb, ROW_CHUNKS, LANES), h2_tiles.dtype),
                        pltpu.SemaphoreType.DMA((2,)), pltpu.SemaphoreType.DMA(())],
        compiler_params=_params("arbitrary"),
        name="dispatch",
    )(lpos, seg_cnt, seg_off, seg_dst, pad_len, pad_row, h2_tiles)


def _experts_body(blk_e_ref, nxt_e_ref, valid_ref, used_ref, x_ref, wgu_hbm, bg_ref, bu_ref, wd_hbm, bd_ref,
                  perm_ref, y_ref, wgu_st, wd_st, wg_sc, wu_sc, wd_sc, sem, *, halves):
    f32, bf16 = jnp.float32, jnp.bfloat16
    step = pl.program_id(0)
    live = step < used_ref[0]
    rows = x_ref.shape[0] // ROW_CHUNKS
    expert = blk_e_ref[step]

    def fetch(e, action):
        getattr(pltpu.make_async_copy(wgu_hbm.at[e], wgu_st, sem.at[0]), action)()
        getattr(pltpu.make_async_copy(wd_hbm.at[e], wd_st, sem.at[1]), action)()

    @pl.when(live & ((step == 0) | (expert != blk_e_ref[jnp.maximum(step - 1, 0)])))
    def _():
        @pl.when(step == 0)
        def _():
            fetch(expert, "start")
        fetch(expert, "wait")
        tile = perm_ref.shape[0]
        half = tile // 2
        for t in range(wgu_st.shape[1] // tile):
            w = wgu_st[:, t * tile:(t + 1) * tile].astype(bf16)
            o = jnp.dot(w, perm_ref[...], preferred_element_type=f32)
            wg_sc[:, t * half:(t + 1) * half] = o[:, :half].astype(bf16)
            wu_sc[:, t * half:(t + 1) * half] = o[:, half:].astype(bf16)
        wd_sc[...] = wd_st[...].astype(bf16)
        nxt = nxt_e_ref[step]

        @pl.when(nxt >= 0)
        def _():
            fetch(nxt, "start")

    part = rows // halves

    def ffn(groups):
        for h in range(groups):
            xb = jnp.concatenate(
                [x_ref[pl.ds(h * part * ROW_CHUNKS + c, part, stride=ROW_CHUNKS), :]
                 for c in range(ROW_CHUNKS)], axis=1).astype(bf16)
            gate = jnp.dot(xb, wg_sc[...], preferred_element_type=f32) + bg_ref[0]
            up = jnp.dot(xb, wu_sc[...], preferred_element_type=f32) + bu_ref[0]
            gate = jnp.minimum(gate, SWIGLU_LIMIT)
            up = jnp.clip(up, -SWIGLU_LIMIT, SWIGLU_LIMIT)
            act = (up + 1.0) * (gate * jax.nn.sigmoid(gate * SWIGLU_ALPHA))
            y = jnp.dot(act.astype(bf16), wd_sc[...], preferred_element_type=f32) + bd_ref[0]
            for c in range(ROW_CHUNKS):
                y_ref[pl.ds(h * part * ROW_CHUNKS + c, part, stride=ROW_CHUNKS), :] = \
                    y[:, c * LANES:(c + 1) * LANES]
        if groups < halves:
            y_ref[groups * part * ROW_CHUNKS:, :] = jnp.zeros(((halves - groups) * part * ROW_CHUNKS, LANES), f32)

    n_valid = valid_ref[step]
    for groups in range(1, halves + 1):
        wanted = live & (n_valid > (groups - 1) * part)
        if groups < halves:
            wanted = wanted & (n_valid <= groups * part)
        pl.when(wanted)(functools.partial(ffn, groups))

    @pl.when(jnp.logical_not(live))
    def _():
        y_ref[...] = jnp.zeros_like(y_ref)


def experts(blk_expert, nxt_expert, blk_valid, n_used, xs_tiles, w_gu, b_g, b_u, w_d, b_d, *, halves=2):
    n_blocks = blk_expert.shape[0]
    rb = MOE_ROW_BLOCK
    d, ff = w_d.shape[2], w_d.shape[1]
    tile = 2 * LANES
    perm = np.zeros((tile, tile), np.float32)
    perm[2 * np.arange(LANES), np.arange(LANES)] = 1.0
    perm[2 * np.arange(LANES) + 1, LANES + np.arange(LANES)] = 1.0
    e_map3 = lambda i, be, nx, bv, nu: (be[i], 0, 0)
    rows_map = lambda i, be, nx, bv, nu: (i, 0)
    grid_spec = pltpu.PrefetchScalarGridSpec(
        num_scalar_prefetch=4,
        grid=(n_blocks,),
        in_specs=[pl.BlockSpec((rb * ROW_CHUNKS, LANES), rows_map),
                  pl.BlockSpec(memory_space=pl.ANY),
                  pl.BlockSpec((1, 1, ff), e_map3), pl.BlockSpec((1, 1, ff), e_map3),
                  pl.BlockSpec(memory_space=pl.ANY), pl.BlockSpec((1, 1, d), e_map3),
                  pl.BlockSpec((tile, tile), lambda i, be, nx, bv, nu: (0, 0))],
        out_specs=pl.BlockSpec((rb * ROW_CHUNKS, LANES), rows_map),
        scratch_shapes=[pltpu.VMEM((d, 2 * ff), w_gu.dtype), pltpu.VMEM((ff, d), w_d.dtype),
                        pltpu.VMEM((d, ff), jnp.bfloat16), pltpu.VMEM((d, ff), jnp.bfloat16),
                        pltpu.VMEM((ff, d), jnp.bfloat16), pltpu.SemaphoreType.DMA((2,))],
    )
    return pl.pallas_call(
        functools.partial(_experts_body, halves=halves),
        grid_spec=grid_spec,
        out_shape=jax.ShapeDtypeStruct(xs_tiles.shape, jnp.float32),
        compiler_params=_params("arbitrary"),
        name="experts",
    )(blk_expert, nxt_expert, blk_valid, n_used, xs_tiles, w_gu, b_g, b_u, w_d, b_d,
      jnp.asarray(perm, jnp.bfloat16))


def _combine_ple_body(lpos_ref, gw_ref, cnt_ref, off_ref, dst_ref, ncnt_ref, noff_ref, ndst_ref,
                      x1_ref, ys_ref, p_ref, gp_ref, wpg_ref, wpp_ref, *rest):
    cbuf, msum, sem = rest[-3:]
    rest = rest[:-3]
    gf_ref, o_ref = rest if len(rest) == 2 else (None, rest[0])
    f32, bf16 = jnp.float32, jnp.bfloat16
    tm = x1_ref.shape[0]
    step = pl.program_id(0)
    slot = step % 2

    def seg_copy(sl):
        return lambda local, glob, size: pltpu.make_async_copy(
            ys_ref.at[pl.ds(glob, size)], cbuf.at[sl, pl.ds(local, size)], sem.at[sl])

    cur, nxt = (cnt_ref, off_ref, dst_ref), (ncnt_ref, noff_ref, ndst_ref)

    @pl.when(step == 0)
    def _():
        _for_tile_segments(cur, tm, seg_copy(0), "start")

    @pl.when(step + 1 < pl.num_programs(0))
    def _():
        _for_tile_segments(nxt, tm, seg_copy(1 - slot), "start")

    pltpu.make_async_copy(ys_ref.at[pl.ds(0, TOP_K * tm)], cbuf.at[slot], sem.at[slot]).wait()

    def weighted_sum(t, carry):
        acc = gw_ref[0, 0, t * TOP_K] * cbuf[slot, lpos_ref[0, 0, t * TOP_K]]
        for kk in range(1, TOP_K):
            acc = acc + gw_ref[0, 0, t * TOP_K + kk] * cbuf[slot, lpos_ref[0, 0, t * TOP_K + kk]]
        msum[t] = acc
        return carry
    lax.fori_loop(0, tm, weighted_sum, 0, unroll=8)

    moe = jnp.concatenate([msum[:, c, :] for c in range(ROW_CHUNKS)], axis=1)
    x2 = x1_ref[...] + moe
    r = _rms(x2, gp_ref[...]).astype(bf16)
    gate = jax.nn.sigmoid(jnp.dot(r, wpg_ref[...], preferred_element_type=f32))
    emb = jnp.dot(p_ref[...].astype(bf16), wpp_ref[...], preferred_element_type=f32)
    x3 = x2 + gate * emb
    o_ref[...] = x3 if gf_ref is None else _rms(x3, gf_ref[...])


def combine_ple(lpos, gate_w, seg_cnt, seg_off, seg_dst, x1, ys_tiles, p2, g_ple, w_pg, w_pp, g_final,
                *, tm):
    n, d = x1.shape
    steps = n // tm
    smem = pltpu.SMEM
    full = lambda a: pl.BlockSpec(a.shape, lambda i: (0,) * a.ndim)
    seg = lambda shift: pl.BlockSpec((1, 1, N_EXPERTS),
                                     lambda i: (jnp.minimum(i + shift, steps - 1), 0, 0), memory_space=smem)
    tok = pl.BlockSpec((1, 1, TOP_K * tm), lambda i: (i, 0, 0), memory_space=smem)
    consts = [g_ple.reshape(1, d), w_pg, w_pp] + ([] if g_final is None else [g_final.reshape(1, d)])
    return pl.pallas_call(
        _combine_ple_body,
        grid=(steps,),
        in_specs=[tok, tok, seg(0), seg(0), seg(0), seg(1), seg(1), seg(1),
                  pl.BlockSpec((tm, d), lambda i: (i, 0)),
                  pl.BlockSpec(memory_space=pl.ANY),
                  pl.BlockSpec((tm, p2.shape[1]), lambda i: (i, 0))] + [full(a) for a in consts],
        out_specs=pl.BlockSpec((tm, d), lambda i: (i, 0)),
        out_shape=jax.ShapeDtypeStruct((n, d), jnp.float32),
        scratch_shapes=[pltpu.VMEM((2, TOP_K * tm, ROW_CHUNKS, LANES), ys_tiles.dtype),
                        pltpu.VMEM((tm, ROW_CHUNKS, LANES), jnp.float32),
                        pltpu.SemaphoreType.DMA((2,))],
        compiler_params=_params("arbitrary"),
        name="combine_ple",
    )(lpos, gate_w, seg_cnt, seg_off, seg_dst, seg_cnt, seg_off, seg_dst, x1, ys_tiles, p2, *consts)


def _moe(h2_tiles, lpos, tile_cnt, w_gu, b_gu, w_d, b_d, *, tm):
    n = h2_tiles.shape[0] // ROW_CHUNKS
    rb = MOE_ROW_BLOCK
    i32 = jnp.int32
    n_blocks = -(-n * TOP_K // rb) + N_EXPERTS
    n_rows = n_blocks * rb
    counts = jnp.sum(tile_cnt, axis=0)
    padded = (counts + rb - 1) // rb * rb
    pend = jnp.cumsum(padded)
    pstart = pend - padded
    blk_start = jnp.arange(n_blocks, dtype=i32) * rb
    blk_expert = jnp.minimum(jnp.sum((pend[None, :] <= blk_start[:, None]).astype(i32), axis=1),
                             N_EXPERTS - 1)
    n_used = pend[-1:] // rb
    seg_cnt = tile_cnt[:, None, :]
    seg_off = (jnp.cumsum(tile_cnt, axis=1) - tile_cnt)[:, None, :]
    seg_dst = (pstart[None, :] + jnp.cumsum(tile_cnt, axis=0) - tile_cnt)[:, None, :]
    tail = n_used + jnp.arange(N_EXPERTS, dtype=i32)
    pad_len = jnp.concatenate([padded - counts, jnp.where(tail < n_blocks, rb, 0)])[None, :]
    pad_row = jnp.concatenate([pstart + counts, tail * rb])[None, :]
    b_g, b_u = b_gu[:, None, 0::2], b_gu[:, None, 1::2]
    xs = dispatch(lpos, seg_cnt, seg_off, seg_dst, pad_len.astype(i32), pad_row.astype(i32),
                  h2_tiles.reshape(n, ROW_CHUNKS, LANES), n_rows, tc=tm, rb=rb)
    e_ids = jnp.arange(N_EXPERTS, dtype=i32)
    later = (padded > 0)[None, :] & (e_ids[None, :] > e_ids[:, None])
    nxt_tab = jnp.min(jnp.where(later, e_ids[None, :], N_EXPERTS), axis=1)
    nxt_tab = jnp.where(nxt_tab == N_EXPERTS, -1, nxt_tab)
    nxt_expert = jnp.sum(jnp.where(blk_expert[:, None] == e_ids[None, :], nxt_tab[None, :], 0), axis=1)
    hit = blk_expert[:, None] == e_ids[None, :]
    blk_valid = jnp.clip(jnp.sum(jnp.where(hit, (pstart + counts)[None, :], 0), axis=1) - blk_start, 0, rb)
    ys = experts(blk_expert, nxt_expert.astype(i32), blk_valid.astype(i32), n_used.astype(i32),
                 xs.reshape(n_rows * ROW_CHUNKS, LANES), w_gu, b_g, b_u, w_d, b_d[:, None, :])
    return (seg_cnt, seg_off, seg_dst), ys.reshape(n_rows, ROW_CHUNKS, LANES)


def kernel(x, p, norm_mix_g, w_in, conv_w, conv_b, w_qk_m, w_if, b_if, mnorm_g, w_branch, w_out, norm_ffn_g, w_router, b_router, w_gate_up, b_gate_up, w_down, b_down, norm_ple_g, w_ple_gate, w_ple_proj, final_norm_g):
    B, S, D = x.shape
    depth = w_in.shape[0]
    bf16 = jnp.bfloat16
    x2 = x.reshape(B * S, D)
    for i in range(depth):
        qkv, xo, gates = in_proj(x2, norm_mix_g[i], w_in[i].astype(bf16))
        y_a = moba(qkv, B, S)
        y_m = mlstm(xo, conv_w[i], conv_b[i], w_qk_m[i], w_if[i], b_if[i], mnorm_g[i], B, S)
        x1, h2_tiles, gate_w, lpos, tile_cnt = merge_route(
            x2, y_a, y_m, gates, w_branch[i, 0].astype(bf16), w_branch[i, 1].astype(bf16),
            w_out[i].astype(bf16), norm_ffn_g[i], w_router[i], b_router[i], tm=MOE_TOKEN_TILE)
        by_tile = lambda a: a.T.reshape(-1, 1, MOE_TOKEN_TILE * TOP_K)
        lpos, gate_w = by_tile(lpos), by_tile(gate_w)
        segs, ys = _moe(h2_tiles, lpos, tile_cnt[:, :, 0], w_gate_up[i], b_gate_up[i],
                        w_down[i], b_down[i], tm=MOE_TOKEN_TILE)
        x2 = combine_ple(lpos, gate_w, *segs, x1, ys, p[i].reshape(B * S, -1), norm_ple_g[i],
                         w_ple_gate[i].astype(bf16), w_ple_proj[i].astype(bf16),
                         final_norm_g if i == depth - 1 else None, tm=MOE_TOKEN_TILE)
    return x2.reshape(B, S, D)
```

```python
import functools

import jax
import jax.numpy as jnp
import numpy as np
from jax import lax
from jax.experimental import pallas as pl
from jax.experimental.pallas import tpu as pltpu

RMS_EPS = 1e-6
LANES = 128
ROW_CHUNKS = 8
VMEM_LIMIT = 56 * 1024 * 1024

N_ATT_HEADS = 8
ATT_HEAD_DIM = 64
ATT_WIDTH = N_ATT_HEADS * ATT_HEAD_DIM
MOBA_BLOCK = 256
MOBA_TOPK = 3

N_MLSTM_HEADS = 4
MLSTM_WIDTH = 512
MLSTM_V_DIM = 128
MLSTM_QK_DIM = 64
MLSTM_CONV = 4
MLSTM_CHUNK = 256

N_EXPERTS = 32
TOP_K = 4
SWIGLU_ALPHA = 1.702
SWIGLU_LIMIT = 7.0
MOE_ROW_BLOCK = 512
MOE_TOKEN_TILE = 512

NEG_BIG = -1e30

_NT = (((1,), (1,)), ((), ()))
_TN = (((0,), (0,)), ((), ()))


def _params(*sem):
    return pltpu.CompilerParams(dimension_semantics=sem, vmem_limit_bytes=VMEM_LIMIT)


def _rms(x, g):
    return x * lax.rsqrt(jnp.mean(x * x, axis=-1, keepdims=True) + RMS_EPS) * g


def _bf16_parts(x, n):
    parts = []
    for _ in range(n):
        p = x.astype(jnp.bfloat16)
        parts.append(p)
        x = x - p.astype(jnp.float32)
    return parts


def _in_proj_body(x_ref, g_ref, w_ref, qkv_ref, xo_ref, gate_ref, *, col_chunk):
    h = _rms(x_ref[...], g_ref[...]).astype(jnp.bfloat16)
    col = 0
    for out_ref in (qkv_ref, xo_ref, gate_ref):
        for c in range(0, out_ref.shape[1], col_chunk):
            out_ref[:, c:c + col_chunk] = jnp.dot(
                h, w_ref[:, col + c:col + c + col_chunk],
                preferred_element_type=jnp.float32).astype(out_ref.dtype)
        col += out_ref.shape[1]


def in_proj(x2, g, w_bf16, *, tm=512, col_chunk=512):
    n, d = x2.shape
    widths = (3 * ATT_WIDTH, 2 * MLSTM_WIDTH, 2 * d)
    assert sum(widths) == w_bf16.shape[1] and n % tm == 0
    return pl.pallas_call(
        functools.partial(_in_proj_body, col_chunk=col_chunk),
        grid=(n // tm,),
        in_specs=[pl.BlockSpec((tm, d), lambda i: (i, 0)),
                  pl.BlockSpec((1, d), lambda i: (0, 0)),
                  pl.BlockSpec(w_bf16.shape, lambda i: (0, 0))],
        out_specs=[pl.BlockSpec((tm, w), lambda i: (i, 0)) for w in widths],
        out_shape=[jax.ShapeDtypeStruct((n, w), jnp.bfloat16) for w in widths],
        compiler_params=_params("parallel"),
        name="in_proj",
    )(x2, g.reshape(1, d), w_bf16)


def _moba_body(q_ref, k_ref, v_ref, o_ref, *, pair, seq):
    blk = MOBA_BLOCK
    n_blk = seq // blk
    f32, bf16 = jnp.float32, jnp.bfloat16
    lane = lax.broadcasted_iota(jnp.int32, (1, LANES), 1)
    first_head = lane < ATT_HEAD_DIM
    row = lax.broadcasted_iota(jnp.int32, (2 * blk, blk), 0)
    col = lax.broadcasted_iota(jnp.int32, (2 * blk, blk), 1)
    t_in_blk = jnp.where(row >= blk, row - blk, row)
    rel = (t_in_blk - col).astype(f32)
    causal = t_in_blk >= col
    log2e = float(np.log2(np.e))
    scale2 = ATT_HEAD_DIM ** -0.5 * log2e
    head = (2 * pair + (row[:, 0:1] >= blk).astype(jnp.int32)).astype(f32)
    slope2 = jnp.exp2(-8.0 * (head + 1.0) / N_ATT_HEADS) * log2e
    bias = [slope2 * (rel + float(d * blk)) for d in range(n_blk)]
    lane_row = lax.broadcasted_iota(jnp.int32, (LANES, LANES), 0)
    spread = [jnp.where(lane_row == j, 1.0, 0.0).astype(bf16) for j in range(n_blk)]

    k_all = k_ref[...]
    v_ones = jnp.concatenate([v_ref[...], jnp.ones((seq, LANES), bf16)], axis=1)
    k_mean = jnp.concatenate(
        [jnp.mean(k_all[j * blk:(j + 1) * blk].astype(f32), axis=0, keepdims=True)
         for j in range(n_blk)] + [jnp.zeros((LANES - n_blk, LANES), f32)], axis=0)
    k_mean_parts = _bf16_parts(k_mean, 3)

    for qi in range(n_blk):
        q_blk = q_ref[qi * blk:(qi + 1) * blk, :]
        zero = jnp.zeros_like(q_blk)
        q2 = jnp.concatenate([jnp.where(first_head, q_blk, zero), jnp.where(first_head, zero, q_blk)], axis=0)
        sel = None
        if qi > MOBA_TOPK:
            g = sum(lax.dot_general(q2, part, _NT, preferred_element_type=f32) for part in k_mean_parts)
            g = jnp.where(lane < qi, g, -jnp.inf)
            ahead = jnp.zeros((2 * blk, LANES), f32)
            for d in range(1, qi):
                ahead = ahead + jnp.where(pltpu.roll(g, d, axis=1) >= g, 1.0, 0.0)
                ahead = ahead + jnp.where(pltpu.roll(g, LANES - d, axis=1) > g, 1.0, 0.0)
            keep = jnp.where(ahead < MOBA_TOPK, 1.0, 0.0).astype(bf16)
            sel = [jnp.dot(keep, spread[j], preferred_element_type=f32) > 0.5 for j in range(qi)]
        logits = []
        for j in range(qi + 1):
            s = lax.dot_general(q2, k_all[j * blk:(j + 1) * blk], _NT, preferred_element_type=f32)
            s = s * scale2 - bias[qi - j]
            if j == qi:
                s = jnp.where(causal, s, NEG_BIG)
            elif sel is not None:
                s = jnp.where(jnp.concatenate([sel[j]] * (blk // LANES), axis=1), s, NEG_BIG)
            logits.append(s)
        s_all = jnp.concatenate(logits, axis=1)
        m = jnp.max(s_all, axis=1, keepdims=True)
        p = jnp.exp2(s_all - m)
        o2 = jnp.dot(p.astype(bf16), v_ones[:(qi + 1) * blk], preferred_element_type=f32)
        o2 = o2[:, :LANES] / o2[:, LANES:LANES + 1]
        o = jnp.where(first_head, o2[:blk], o2[blk:])
        o_ref[qi * blk:(qi + 1) * blk, :] = o.astype(o_ref.dtype)


def _moba_kernel(q_ref, k_ref, v_ref, o_ref, *, seq):
    _moba_body(q_ref, k_ref, v_ref, o_ref, pair=pl.program_id(1), seq=seq)


def moba(qkv, batch, seq):
    n = batch * seq
    n_pairs = ATT_WIDTH // LANES
    assert seq % MOBA_BLOCK == 0
    spec = lambda off: pl.BlockSpec((seq, LANES), lambda b, p: (b, off + p))
    return pl.pallas_call(
        functools.partial(_moba_kernel, seq=seq),
        grid=(batch, n_pairs),
        in_specs=[spec(0), spec(n_pairs), spec(2 * n_pairs)],
        out_specs=pl.BlockSpec((seq, LANES), lambda b, p: (b, p)),
        out_shape=jax.ShapeDtypeStruct((n, ATT_WIDTH), jnp.bfloat16),
        compiler_params=_params("parallel", "parallel"),
        name="moba",
    )(qkv, qkv, qkv)


def _log_sigmoid(z):
    return jnp.minimum(z, 0.0) - jnp.log(1.0 + jnp.exp(-jnp.abs(z)))


def _mlstm_body(xm_ref, om_ref, cw_ref, cb_ref, wq_ref, wk_ref, wiq_ref, wik_ref, wix_ref,
                wiqt_ref, wikt_ref, wixt_ref, bi_ref, bit_ref, g_ref, y_ref,
                q_sc, k_sc, gcol_sc, grow_sc, ct_sc, n_sc, m_sc, *, seq):
    L = MLSTM_CHUNK
    H = N_MLSTM_HEADS
    n_chunks = seq // L
    f32, bf16 = jnp.float32, jnp.bfloat16

    x = xm_ref[...].astype(f32)
    t_idx = lax.broadcasted_iota(jnp.int32, (seq, 1), 0)
    acc = x * cw_ref[MLSTM_CONV - 1:MLSTM_CONV, :] + cb_ref[...]
    for d in range(1, MLSTM_CONV):
        shifted = jnp.where(t_idx >= d, pltpu.roll(x, d, axis=0), 0.0)
        acc = acc + shifted * cw_ref[MLSTM_CONV - 1 - d:MLSTM_CONV - d, :]
    xc = (acc * jax.nn.sigmoid(acc)).astype(bf16)

    xm = xm_ref[...]
    if_col = jnp.dot(xm, wix_ref[...], preferred_element_type=f32) + bi_ref[...]
    if_row = lax.dot_general(wixt_ref[...], xm, _NT, preferred_element_type=f32) + bit_ref[...]
    for h in range(H):
        xch = xc[:, h * LANES:(h + 1) * LANES]
        q = jnp.dot(xch, wq_ref[h], preferred_element_type=f32).astype(bf16)
        k = jnp.dot(xch, wk_ref[h], preferred_element_type=f32).astype(bf16)
        if_col = if_col + jnp.dot(q, wiq_ref[h], preferred_element_type=f32)
        if_col = if_col + jnp.dot(k, wik_ref[h], preferred_element_type=f32)
        if_row = if_row + lax.dot_general(wiqt_ref[h], q, _NT, preferred_element_type=f32)
        if_row = if_row + lax.dot_general(wikt_ref[h], k, _NT, preferred_element_type=f32)
        q_sc[h] = q
        k_sc[h] = (k.astype(f32) * (MLSTM_QK_DIM ** -0.5)).astype(bf16)
    is_f_col = lax.broadcasted_iota(jnp.int32, (1, 2 * H), 1) >= H
    is_f_row = lax.broadcasted_iota(jnp.int32, (2 * H, 1), 0) >= H
    gcol_sc[...] = jnp.where(is_f_col, _log_sigmoid(if_col), if_col)
    g_row = jnp.where(is_f_row, _log_sigmoid(if_row), if_row)
    for c in range(n_chunks):
        grow_sc[c] = g_row[:, c * L:(c + 1) * L]

    ct_sc[...] = jnp.zeros_like(ct_sc)
    n_sc[...] = jnp.zeros_like(n_sc)
    m_sc[...] = jnp.zeros_like(m_sc)

    r_i = lax.broadcasted_iota(jnp.int32, (L, L), 0)
    c_i = lax.broadcasted_iota(jnp.int32, (L, L), 1)
    causal = r_i >= c_i
    tri_lo = jnp.where(causal, 1.0, 0.0).astype(f32)
    tri_up = jnp.where(c_i >= r_i, 1.0, 0.0).astype(f32)

    def chunk(c, carry):
        r0 = pl.multiple_of(c * L, L)
        gc = gcol_sc[pl.ds(r0, L), :]
        gr = grow_sc[c]
        b_cols = jnp.dot(tri_lo, gc, precision=lax.Precision.HIGHEST, preferred_element_type=f32)
        b_rows = jnp.dot(gr, tri_up, precision=lax.Precision.HIGHEST, preferred_element_type=f32)
        for h in range(H):
            hs = slice(h * LANES, (h + 1) * LANES)
            b_col = b_cols[:, H + h:H + h + 1]
            i_col = gc[:, h:h + 1]
            b_row = b_rows[H + h:H + h + 1, :]
            i_row = gr[h:h + 1, :]
            b_last = b_col[L - 1:L, :]
            m_prev = m_sc[h][:, 0:1]
            q = q_sc[h, pl.ds(r0, L), :]
            k = k_sc[h, pl.ds(r0, L), :]
            v = xm_ref[pl.ds(r0, L), hs]

            log_d = jnp.where(causal, b_col + (i_row - b_row), NEG_BIG)
            inter = b_col + m_prev
            m_t = jnp.maximum(inter, jnp.max(log_d, axis=1, keepdims=True))
            w_inter = jnp.exp(inter - m_t)
            s = lax.dot_general(q, k, _NT, preferred_element_type=f32) * jnp.exp(log_d - m_t)
            num = jnp.dot(s.astype(bf16), v, preferred_element_type=f32)
            num = num + w_inter * jnp.dot(q, ct_sc[h].astype(bf16), preferred_element_type=f32)
            den = jnp.sum(s, axis=1, keepdims=True)
            den = den + w_inter * jnp.sum(q.astype(f32) * n_sc[h], axis=1, keepdims=True)
            hv = num / jnp.maximum(jnp.abs(den), jnp.exp(-m_t))
            hv = hv * lax.rsqrt(jnp.mean(hv * hv, axis=1, keepdims=True) + RMS_EPS) * g_ref[:, hs]
            o_gate = jax.nn.sigmoid(om_ref[pl.ds(r0, L), hs].astype(f32))
            y_ref[pl.ds(r0, L), hs] = (o_gate * hv).astype(y_ref.dtype)

            g_col = b_last - b_col + i_col
            m_new = jnp.maximum(b_last + m_prev, jnp.max(g_col, axis=0, keepdims=True))
            wg = jnp.exp(g_col - m_new)
            decay = jnp.exp(b_last + m_prev - m_new)
            wv = (wg * v.astype(f32)).astype(bf16)
            ct_sc[h] = decay * ct_sc[h] + lax.dot_general(k, wv, _TN, preferred_element_type=f32)
            n_sc[h] = decay * n_sc[h] + jnp.sum(wg * k.astype(f32), axis=0, keepdims=True)
            m_sc[h] = jnp.broadcast_to(m_new, (1, LANES))
        return carry

    lax.fori_loop(0, n_chunks, chunk, 0)


def _mlstm_kernel(*refs, seq):
    _mlstm_body(*refs, seq=seq)


def mlstm(xo, conv_w, conv_b, w_qk, w_if, b_if, g, batch, seq):
    n = batch * seq
    H, dv, dk = N_MLSTM_HEADS, MLSTM_V_DIM, MLSTM_QK_DIM
    assert seq % MLSTM_CHUNK == 0 and dv == LANES
    bf16 = jnp.bfloat16
    pad = ((0, 0), (0, 0), (0, LANES - dk))
    wq = jnp.pad(w_qk[:, :, :dk], pad).astype(bf16)
    wk = jnp.pad(w_qk[:, :, dk:], pad).astype(bf16)
    rpad = ((0, 0), (0, LANES - dk), (0, 0))
    wiq = jnp.pad(w_if[:H * dk].reshape(H, dk, 2 * H), rpad).astype(bf16)
    wik = jnp.pad(w_if[H * dk:2 * H * dk].reshape(H, dk, 2 * H), rpad).astype(bf16)
    wix = w_if[2 * H * dk:].astype(bf16)
    wiqt, wikt, wixt = wiq.transpose(0, 2, 1), wik.transpose(0, 2, 1), wix.T
    full = lambda a: pl.BlockSpec(a.shape, lambda b: (0,) * a.ndim)
    consts = [conv_w, conv_b.reshape(1, -1), wq, wk, wiq, wik, wix, wiqt, wikt, wixt,
              b_if.reshape(1, -1), b_if.reshape(-1, 1), g.reshape(1, -1)]
    return pl.pallas_call(
        functools.partial(_mlstm_kernel, seq=seq),
        grid=(batch,),
        in_specs=[pl.BlockSpec((seq, MLSTM_WIDTH), lambda b: (b, 0)),
                  pl.BlockSpec((seq, MLSTM_WIDTH), lambda b: (b, 1))] + [full(a) for a in consts],
        out_specs=pl.BlockSpec((seq, MLSTM_WIDTH), lambda b: (b, 0)),
        out_shape=jax.ShapeDtypeStruct((n, MLSTM_WIDTH), bf16),
        scratch_shapes=[pltpu.VMEM((H, seq, LANES), bf16), pltpu.VMEM((H, seq, LANES), bf16),
                        pltpu.VMEM((seq, 2 * H), jnp.float32),
                        pltpu.VMEM((seq // MLSTM_CHUNK, 2 * H, MLSTM_CHUNK), jnp.float32),
                        pltpu.VMEM((H, LANES, LANES), jnp.float32),
                        pltpu.VMEM((H, 1, LANES), jnp.float32),
                        pltpu.VMEM((H, 1, LANES), jnp.float32)],
        compiler_params=_params("parallel"),
        name="mlstm",
    )(xo, xo, *consts)


def _store_rows_as_tiles(dst_ref, val):
    rows = val.shape[0]
    for c in range(ROW_CHUNKS):
        dst_ref[pl.ds(c, rows, stride=ROW_CHUNKS), :] = val[:, c * LANES:(c + 1) * LANES]


def _load_tiles_as_rows(src_ref, rows, lead=()):
    return jnp.concatenate(
        [src_ref[lead + (pl.ds(c, rows, stride=ROW_CHUNKS), slice(None))] for c in range(ROW_CHUNKS)],
        axis=1)


def _merge_route_body(x_ref, ya_ref, ym_ref, ga_ref, gm_ref, wb0_ref, wb1_ref, wo_ref, g_ref,
                      wr_ref, br_ref, x1_ref, h2_ref, gate_ref, lpos_ref, cnt_ref):
    f32, bf16 = jnp.float32, jnp.bfloat16
    tm = x_ref.shape[0]

    u = jax.nn.sigmoid(ga_ref[...].astype(f32)) * jnp.dot(ya_ref[...], wb0_ref[...], preferred_element_type=f32)
    u = u + jax.nn.sigmoid(gm_ref[...].astype(f32)) * jnp.dot(ym_ref[...], wb1_ref[...], preferred_element_type=f32)
    x1 = x_ref[...] + jnp.dot(u.astype(bf16), wo_ref[...], preferred_element_type=f32)
    x1_ref[...] = x1
    h2 = _rms(x1, g_ref[...])
    _store_rows_as_tiles(h2_ref, h2)
    (h_hi, h_lo), (w_hi, w_lo) = _bf16_parts(h2, 2), _bf16_parts(wr_ref[...], 2)
    logits = (jnp.dot(h_hi, w_hi, preferred_element_type=f32) + jnp.dot(h_lo, w_hi, preferred_element_type=f32)
              + jnp.dot(h_hi, w_lo, preferred_element_type=f32)) + br_ref[...]
    logits = jnp.transpose(logits)[:N_EXPERTS]
    e_id = lax.broadcasted_iota(jnp.int32, logits.shape, 0).astype(f32)
    chosen = jnp.zeros(logits.shape, f32)
    vals, ids = [], []
    for _ in range(TOP_K):
        top = jnp.max(logits, axis=0, keepdims=True)
        first = jnp.min(jnp.where(logits == top, e_id, float(N_EXPERTS)), axis=0, keepdims=True)
        hit = e_id == first
        chosen = jnp.where(hit, 1.0, chosen)
        logits = jnp.where(hit, -jnp.inf, logits)
        vals.append(top)
        ids.append(first)
    ex = [jnp.exp(v - vals[0]) for v in vals]
    total = ex[0] + ex[1] + ex[2] + ex[3]

    chosen_b = chosen.astype(bf16)
    t_r = lax.broadcasted_iota(jnp.int32, (tm, tm), 0)
    t_c = lax.broadcasted_iota(jnp.int32, (tm, tm), 1)
    earlier = jnp.where(t_r < t_c, 1.0, 0.0).astype(bf16)
    before = jnp.dot(chosen_b, earlier, preferred_element_type=f32)
    e_r = lax.broadcasted_iota(jnp.int32, (N_EXPERTS, N_EXPERTS), 0)
    e_c = lax.broadcasted_iota(jnp.int32, (N_EXPERTS, N_EXPERTS), 1)
    lower = jnp.where(e_c < e_r, 1.0, 0.0).astype(bf16)
    seg_off = jnp.sum(jnp.dot(lower, chosen_b, preferred_element_type=f32), axis=1, keepdims=True)
    pos = before + seg_off
    for kk in range(TOP_K):
        gate_ref[0, :, kk * tm:(kk + 1) * tm] = ex[kk] / total
        lpos_ref[0, :, kk * tm:(kk + 1) * tm] = jnp.sum(jnp.where(e_id == ids[kk], pos, 0.0), axis=0,
                                                        keepdims=True).astype(jnp.int32)
    counts = jnp.sum(chosen, axis=1, keepdims=True)
    cnt_ref[0] = jnp.broadcast_to(counts, (N_EXPERTS, LANES)).astype(jnp.int32)


def merge_route(x2, ya, ym, gates, wb0, wb1, wo, g, w_router, b_router, *, tm):
    n, d = x2.shape
    e = w_router.shape[1]
    assert n % tm == 0 and d == ROW_CHUNKS * LANES and e == N_EXPERTS
    full = lambda a: pl.BlockSpec(a.shape, lambda i: (0,) * a.ndim)
    lane_pad = ((0, 0), (0, LANES - e))
    consts = [wb0, wb1, wo, g.reshape(1, d), jnp.pad(w_router, lane_pad),
              jnp.pad(b_router.reshape(1, e), lane_pad)]
    tok = lambda dt: jax.ShapeDtypeStruct((n // tm, 1, TOP_K * tm), dt)
    return pl.pallas_call(
        _merge_route_body,
        grid=(n // tm,),
        in_specs=[pl.BlockSpec((tm, d), lambda i: (i, 0)),
                  pl.BlockSpec((tm, ATT_WIDTH), lambda i: (i, 0)),
                  pl.BlockSpec((tm, MLSTM_WIDTH), lambda i: (i, 0)),
                  pl.BlockSpec((tm, d), lambda i: (i, 0)),
                  pl.BlockSpec((tm, d), lambda i: (i, 1))] + [full(a) for a in consts],
        out_specs=[pl.BlockSpec((tm, d), lambda i: (i, 0)),
                   pl.BlockSpec((tm * ROW_CHUNKS, LANES), lambda i: (i, 0)),
                   pl.BlockSpec((1, 1, TOP_K * tm), lambda i: (i, 0, 0)),
                   pl.BlockSpec((1, 1, TOP_K * tm), lambda i: (i, 0, 0)),
                   pl.BlockSpec((1, e, LANES), lambda i: (i, 0, 0))],
        out_shape=[jax.ShapeDtypeStruct((n, d), jnp.float32),
                   jax.ShapeDtypeStruct((n * ROW_CHUNKS, LANES), jnp.float32),
                   tok(jnp.float32), tok(jnp.int32),
                   jax.ShapeDtypeStruct((n // tm, e, LANES), jnp.int32)],
        compiler_params=_params("parallel"),
        name="merge_route",
    )(x2, ya, ym, gates, gates, *consts)


def _segment_pieces(cnt, max_rows, fn):
    top = 1 << (max_rows.bit_length() - 1)
    sizes = [top >> i for i in range(top.bit_length())]
    rare = [s for s in sizes if s * 8 > top]

    def pieces(group, done):
        for size in group:
            bit = cnt & size

            def piece(done=done, size=size):
                fn(done, size)
            pl.when(bit != 0)(piece)
            done = done + bit

    pl.when(cnt >= rare[-1])(lambda: pieces(rare, 0))
    pieces([s for s in sizes if s not in rare], cnt & -rare[-1])


def _for_tile_segments(meta, tile_rows, make_copy, action):
    cnt_ref, off_ref, dst_ref = meta

    def per_expert(e, carry):
        off, dst = off_ref[0, 0, e], dst_ref[0, 0, e]
        _segment_pieces(cnt_ref[0, 0, e], tile_rows,
                        lambda done, size: getattr(make_copy(off + done, dst + done, size), action)())
        return carry
    lax.fori_loop(0, N_EXPERTS, per_expert, 0)


def _dispatch_body(lpos_ref, cnt_ref, off_ref, dst_ref, pad_len_ref, pad_row_ref, x_ref, out_ref,
                   cbuf, zbuf, sem, zsem, *, tc, rb):
    step = pl.program_id(0)
    last = pl.num_programs(0) - 1
    slot = step % 2

    def compact(t, carry):
        row = x_ref[t]
        for kk in range(TOP_K):
            cbuf[slot, lpos_ref[0, 0, kk * tc + t]] = row
        return carry
    lax.fori_loop(0, tc, compact, 0, unroll=8)

    def seg_copy(sl):
        return lambda local, glob, size: pltpu.make_async_copy(
            cbuf.at[sl, pl.ds(local, size)], out_ref.at[pl.ds(glob, size)], sem.at[sl])

    def wait_tile(sl):
        pltpu.make_async_copy(cbuf.at[sl], out_ref.at[pl.ds(0, TOP_K * tc)], sem.at[sl]).wait()

    _for_tile_segments((cnt_ref, off_ref, dst_ref), tc, seg_copy(slot), "start")

    @pl.when(step > 0)
    def _():
        wait_tile(1 - slot)

    @pl.when(step == last)
    def _():
        wait_tile(slot)
        zbuf[...] = jnp.zeros_like(zbuf)

        def pads(action):
            def per_pad(j, carry):
                row = pad_row_ref[0, j]
                _segment_pieces(pad_len_ref[0, j], rb, lambda done, size: getattr(
                    pltpu.make_async_copy(zbuf.at[pl.ds(0, size)], out_ref.at[pl.ds(row + done, size)], zsem),
                    action)())
                return carry
            lax.fori_loop(0, pad_len_ref.shape[1], per_pad, 0)
        pads("start")
        pads("wait")


def dispatch(lpos, seg_cnt, seg_off, seg_dst, pad_len, pad_row, h2_tiles, n_rows, *, tc, rb):
    n = h2_tiles.shape[0]
    steps = n // tc
    assert n % tc == 0
    smem = pltpu.SMEM
    seg = pl.BlockSpec((1, 1, N_EXPERTS), lambda i: (i, 0, 0), memory_space=smem)
    whole = lambda a: pl.BlockSpec(a.shape, lambda i: (0,) * a.ndim, memory_space=smem)
    return pl.pallas_call(
        functools.partial(_dispatch_body, tc=tc, rb=rb),
        grid=(steps,),
        in_specs=[pl.BlockSpec((1, 1, TOP_K * tc), lambda i: (i, 0, 0), memory_space=smem),
                  seg, seg, seg, whole(pad_len), whole(pad_row),
                  pl.BlockSpec((tc, ROW_CHUNKS, LANES), lambda i: (i, 0, 0))],
        out_specs=pl.BlockSpec(memory_space=pl.ANY),
        out_shape=jax.ShapeDtypeStruct((n_rows, ROW_CHUNKS, LANES), h2_tiles.dtype),
        scratch_shapes=[pltpu.VMEM((2, TOP_K * tc, ROW_CHUNKS, LANES), h2_tiles.dtype),
                        pltpu.VMEM((rb, ROW_CHUNKS, LANES), h2_tiles.dtype),
                        pltpu.SemaphoreType.DMA((2,)), pltpu.SemaphoreType.DMA(())],
        compiler_params=_params("arbitrary"),
        name="dispatch",
    )(lpos, seg_cnt, seg_off, seg_dst, pad_len, pad_row, h2_tiles)


def _experts_body(blk_e_ref, nxt_e_ref, valid_ref, used_ref, x_ref, wgu_hbm, bg_ref, bu_ref, wd_hbm, bd_ref,
                  perm_ref, y_ref, wgu_st, wd_st, wg_sc, wu_sc, wd_sc, sem, *, halves):
    f32, bf16 = jnp.float32, jnp.bfloat16
    step = pl.program_id(0)
    live = step < used_ref[0]
    rows = x_ref.shape[0] // ROW_CHUNKS
    expert = blk_e_ref[step]

    def fetch(e, action):
        getattr(pltpu.make_async_copy(wgu_hbm.at[e], wgu_st, sem.at[0]), action)()
        getattr(pltpu.make_async_copy(wd_hbm.at[e], wd_st, sem.at[1]), action)()

    @pl.when(live & ((step == 0) | (expert != blk_e_ref[jnp.maximum(step - 1, 0)])))
    def _():
        @pl.when(step == 0)
        def _():
            fetch(expert, "start")
        fetch(expert, "wait")
        tile = perm_ref.shape[0]
        half = tile // 2
        for t in range(wgu_st.shape[1] // tile):
            w = wgu_st[:, t * tile:(t + 1) * tile].astype(bf16)
            o = jnp.dot(w, perm_ref[...], preferred_element_type=f32)
            wg_sc[:, t * half:(t + 1) * half] = o[:, :half].astype(bf16)
            wu_sc[:, t * half:(t + 1) * half] = o[:, half:].astype(bf16)
        wd_sc[...] = wd_st[...].astype(bf16)
        nxt = nxt_e_ref[step]

        @pl.when(nxt >= 0)
        def _():
            fetch(nxt, "start")

    part = rows // halves

    def ffn(groups):
        for h in range(groups):
            xb = jnp.concatenate(
                [x_ref[pl.ds(h * part * ROW_CHUNKS + c, part, stride=ROW_CHUNKS), :]
                 for c in range(ROW_CHUNKS)], axis=1).astype(bf16)
            gate = jnp.dot(xb, wg_sc[...], preferred_element_type=f32) + bg_ref[0]
            up = jnp.dot(xb, wu_sc[...], preferred_element_type=f32) + bu_ref[0]
            gate = jnp.minimum(gate, SWIGLU_LIMIT)
            up = jnp.clip(up, -SWIGLU_LIMIT, SWIGLU_LIMIT)
            act = (up + 1.0) * (gate * jax.nn.sigmoid(gate * SWIGLU_ALPHA))
            y = jnp.dot(act.astype(bf16), wd_sc[...], preferred_element_type=f32) + bd_ref[0]
            for c in range(ROW_CHUNKS):
                y_ref[pl.ds(h * part * ROW_CHUNKS + c, part, stride=ROW_CHUNKS), :] = \
                    y[:, c * LANES:(c + 1) * LANES]
        if groups < halves:
            y_ref[groups * part * ROW_CHUNKS:, :] = jnp.zeros(((halves - groups) * part * ROW_CHUNKS, LANES), f32)

    n_valid = valid_ref[step]
    for groups in range(1, halves + 1):
        wanted = live & (n_valid > (groups - 1) * part)
        if groups < halves:
            wanted = wanted & (n_valid <= groups * part)
        pl.when(wanted)(functools.partial(ffn, groups))

    @pl.when(jnp.logical_not(live))
    def _():
        y_ref[...] = jnp.zeros_like(y_ref)


def experts(blk_expert, nxt_expert, blk_valid, n_used, xs_tiles, w_gu, b_g, b_u, w_d, b_d, *, halves=2):
    n_blocks = blk_expert.shape[0]
    rb = MOE_ROW_BLOCK
    d, ff = w_d.shape[2], w_d.shape[1]
    tile = 2 * LANES
    perm = np.zeros((tile, tile), np.float32)
    perm[2 * np.arange(LANES), np.arange(LANES)] = 1.0
    perm[2 * np.arange(LANES) + 1, LANES + np.arange(LANES)] = 1.0
    e_map3 = lambda i, be, nx, bv, nu: (be[i], 0, 0)
    rows_map = lambda i, be, nx, bv, nu: (i, 0)
    grid_spec = pltpu.PrefetchScalarGridSpec(
        num_scalar_prefetch=4,
        grid=(n_blocks,),
        in_specs=[pl.BlockSpec((rb * ROW_CHUNKS, LANES), rows_map),
                  pl.BlockSpec(memory_space=pl.ANY),
                  pl.BlockSpec((1, 1, ff), e_map3), pl.BlockSpec((1, 1, ff), e_map3),
                  pl.BlockSpec(memory_space=pl.ANY), pl.BlockSpec((1, 1, d), e_map3),
                  pl.BlockSpec((tile, tile), lambda i, be, nx, bv, nu: (0, 0))],
        out_specs=pl.BlockSpec((rb * ROW_CHUNKS, LANES), rows_map),
        scratch_shapes=[pltpu.VMEM((d, 2 * ff), w_gu.dtype), pltpu.VMEM((ff, d), w_d.dtype),
                        pltpu.VMEM((d, ff), jnp.bfloat16), pltpu.VMEM((d, ff), jnp.bfloat16),
                        pltpu.VMEM((ff, d), jnp.bfloat16), pltpu.SemaphoreType.DMA((2,))],
    )
    return pl.pallas_call(
        functools.partial(_experts_body, halves=halves),
        grid_spec=grid_spec,
        out_shape=jax.ShapeDtypeStruct(xs_tiles.shape, jnp.float32),
        compiler_params=_params("arbitrary"),
        name="experts",
    )(blk_expert, nxt_expert, blk_valid, n_used, xs_tiles, w_gu, b_g, b_u, w_d, b_d,
      jnp.asarray(perm, jnp.bfloat16))


def _combine_ple_body(lpos_ref, gw_ref, cnt_ref, off_ref, dst_ref, ncnt_ref, noff_ref, ndst_ref,
                      x1_ref, ys_ref, p_ref, gp_ref, wpg_ref, wpp_ref, *rest):
    cbuf, msum, sem = rest[-3:]
    rest = rest[:-3]
    gf_ref, o_ref = rest if len(rest) == 2 else (None, rest[0])
    f32, bf16 = jnp.float32, jnp.bfloat16
    tm = x1_ref.shape[0]
    step = pl.program_id(0)
    slot = step % 2

    def seg_copy(sl):
        return lambda local, glob, size: pltpu.make_async_copy(
            ys_ref.at[pl.ds(glob, size)], cbuf.at[sl, pl.ds(local, size)], sem.at[sl])

    cur, nxt = (cnt_ref, off_ref, dst_ref), (ncnt_ref, noff_ref, ndst_ref)

    @pl.when(step == 0)
    def _():
        _for_tile_segments(cur, tm, seg_copy(0), "start")

    @pl.when(step + 1 < pl.num_programs(0))
    def _():
        _for_tile_segments(nxt, tm, seg_copy(1 - slot), "start")

    pltpu.make_async_copy(ys_ref.at[pl.ds(0, TOP_K * tm)], cbuf.at[slot], sem.at[slot]).wait()

    def weighted_sum(t, carry):
        acc = gw_ref[0, 0, t] * cbuf[slot, lpos_ref[0, 0, t]]
        for kk in range(1, TOP_K):
            acc = acc + gw_ref[0, 0, kk * tm + t] * cbuf[slot, lpos_ref[0, 0, kk * tm + t]]
        msum[t] = acc
        return carry
    lax.fori_loop(0, tm, weighted_sum, 0, unroll=8)

    moe = jnp.concatenate([msum[:, c, :] for c in range(ROW_CHUNKS)], axis=1)
    x2 = x1_ref[...] + moe
    r = _rms(x2, gp_ref[...]).astype(bf16)
    gate = jax.nn.sigmoid(jnp.dot(r, wpg_ref[...], preferred_element_type=f32))
    emb = jnp.dot(p_ref[...].astype(bf16), wpp_ref[...], preferred_element_type=f32)
    x3 = x2 + gate * emb
    o_ref[...] = x3 if gf_ref is None else _rms(x3, gf_ref[...])


def combine_ple(lpos, gate_w, seg_cnt, seg_off, seg_dst, x1, ys_tiles, p2, g_ple, w_pg, w_pp, g_final,
                *, tm):
    n, d = x1.shape
    steps = n // tm
    smem = pltpu.SMEM
    full = lambda a: pl.BlockSpec(a.shape, lambda i: (0,) * a.ndim)
    seg = lambda shift: pl.BlockSpec((1, 1, N_EXPERTS),
                                     lambda i: (jnp.minimum(i + shift, steps - 1), 0, 0), memory_space=smem)
    tok = pl.BlockSpec((1, 1, TOP_K * tm), lambda i: (i, 0, 0), memory_space=smem)
    consts = [g_ple.reshape(1, d), w_pg, w_pp] + ([] if g_final is None else [g_final.reshape(1, d)])
    return pl.pallas_call(
        _combine_ple_body,
        grid=(steps,),
        in_specs=[tok, tok, seg(0), seg(0), seg(0), seg(1), seg(1), seg(1),
                  pl.BlockSpec((tm, d), lambda i: (i, 0)),
                  pl.BlockSpec(memory_space=pl.ANY),
                  pl.BlockSpec((tm, p2.shape[1]), lambda i: (i, 0))] + [full(a) for a in consts],
        out_specs=pl.BlockSpec((tm, d), lambda i: (i, 0)),
        out_shape=jax.ShapeDtypeStruct((n, d), jnp.float32),
        scratch_shapes=[pltpu.VMEM((2, TOP_K * tm, ROW_CHUNKS, LANES), ys_tiles.dtype),
                        pltpu.VMEM((tm, ROW_CHUNKS, LANES), jnp.float32),
                        pltpu.SemaphoreType.DMA((2,))],
        compiler_params=_params("arbitrary"),
        name="combine_ple",
    )(lpos, gate_w, seg_cnt, seg_off, seg_dst, seg_cnt, seg_off, seg_dst, x1, ys_tiles, p2, *consts)


def _moe(h2_tiles, lpos, tile_cnt, w_gu, b_gu, w_d, b_d, *, tm):
    n = h2_tiles.shape[0] // ROW_CHUNKS
    rb = MOE_ROW_BLOCK
    i32 = jnp.int32
    n_blocks = -(-n * TOP_K // rb) + N_EXPERTS
    n_rows = n_blocks * rb
    counts = jnp.sum(tile_cnt, axis=0)
    padded = (counts + rb - 1) // rb * rb
    pend = jnp.cumsum(padded)
    pstart = pend - padded
    blk_start = jnp.arange(n_blocks, dtype=i32) * rb
    blk_expert = jnp.minimum(jnp.sum((pend[None, :] <= blk_start[:, None]).astype(i32), axis=1),
                             N_EXPERTS - 1)
    n_used = pend[-1:] // rb
    seg_cnt = tile_cnt[:, None, :]
    seg_off = (jnp.cumsum(tile_cnt, axis=1) - tile_cnt)[:, None, :]
    seg_dst = (pstart[None, :] + jnp.cumsum(tile_cnt, axis=0) - tile_cnt)[:, None, :]
    tail = n_used + jnp.arange(N_EXPERTS, dtype=i32)
    pad_len = jnp.concatenate([padded - counts, jnp.where(tail < n_blocks, rb, 0)])[None, :]
    pad_row = jnp.concatenate([pstart + counts, tail * rb])[None, :]
    b_g, b_u = b_gu[:, None, 0::2], b_gu[:, None, 1::2]
    xs = dispatch(lpos, seg_cnt, seg_off, seg_dst, pad_len.astype(i32), pad_row.astype(i32),
                  h2_tiles.reshape(n, ROW_CHUNKS, LANES), n_rows, tc=tm, rb=rb)
    e_ids = jnp.arange(N_EXPERTS, dtype=i32)
    later = (padded > 0)[None, :] & (e_ids[None, :] > e_ids[:, None])
    nxt_tab = jnp.min(jnp.where(later, e_ids[None, :], N_EXPERTS), axis=1)
    nxt_tab = jnp.where(nxt_tab == N_EXPERTS, -1, nxt_tab)
    nxt_expert = jnp.sum(jnp.where(blk_expert[:, None] == e_ids[None, :], nxt_tab[None, :], 0), axis=1)
    hit = blk_expert[:, None] == e_ids[None, :]
    blk_valid = jnp.clip(jnp.sum(jnp.where(hit, (pstart + counts)[None, :], 0), axis=1) - blk_start, 0, rb)
    ys = experts(blk_expert, nxt_expert.astype(i32), blk_valid.astype(i32), n_used.astype(i32),
                 xs.reshape(n_rows * ROW_CHUNKS, LANES), w_gu, b_g, b_u, w_d, b_d[:, None, :])
    return (seg_cnt, seg_off, seg_dst), ys.reshape(n_rows, ROW_CHUNKS, LANES)


def kernel(x, p, norm_mix_g, w_in, conv_w, conv_b, w_qk_m, w_if, b_if, mnorm_g, w_branch, w_out, norm_ffn_g, w_router, b_router, w_gate_up, b_gate_up, w_down, b_down, norm_ple_g, w_ple_gate, w_ple_proj, final_norm_g):
    B, S, D = x.shape
    depth = w_in.shape[0]
    bf16 = jnp.bfloat16
    x2 = x.reshape(B * S, D)
    for i in range(depth):
        qkv, xo, gates = in_proj(x2, norm_mix_g[i], w_in[i].astype(bf16))
        y_a = moba(qkv, B, S)
        y_m = mlstm(xo, conv_w[i], conv_b[i], w_qk_m[i], w_if[i], b_if[i], mnorm_g[i], B, S)
        x1, h2_tiles, gate_w, lpos, tile_cnt = merge_route(
            x2, y_a, y_m, gates, w_branch[i, 0].astype(bf16), w_branch[i, 1].astype(bf16),
            w_out[i].astype(bf16), norm_ffn_g[i], w_router[i], b_router[i], tm=MOE_TOKEN_TILE)
        segs, ys = _moe(h2_tiles, lpos, tile_cnt[:, :, 0], w_gate_up[i], b_gate_up[i],
                        w_down[i], b_down[i], tm=MOE_TOKEN_TILE)
        x2 = combine_ple(lpos, gate_w, *segs, x1, ys, p[i].reshape(B * S, -1), norm_ple_g[i],
                         w_ple_gate[i].astype(bf16), w_ple_proj[i].astype(bf16),
                         final_norm_g if i == depth - 1 else None, tm=MOE_TOKEN_TILE)
    return x2.reshape(B, S, D)
```

```python
import functools

import jax
import jax.numpy as jnp
import numpy as np
from jax import lax
from jax.experimental import pallas as pl
from jax.experimental.pallas import tpu as pltpu

RMS_EPS = 1e-6
LANES = 128
ROW_CHUNKS = 8
VMEM_LIMIT = 56 * 1024 * 1024

N_ATT_HEADS = 8
ATT_HEAD_DIM = 64
ATT_WIDTH = N_ATT_HEADS * ATT_HEAD_DIM
MOBA_BLOCK = 256
MOBA_TOPK = 3

N_MLSTM_HEADS = 4
MLSTM_WIDTH = 512
MLSTM_V_DIM = 128
MLSTM_QK_DIM = 64
MLSTM_CONV = 4
MLSTM_CHUNK = 256

N_EXPERTS = 32
TOP_K = 4
SWIGLU_ALPHA = 1.702
SWIGLU_LIMIT = 7.0
MOE_ROW_BLOCK = 512
MOE_TOKEN_TILE = 512

NEG_BIG = -1e30

_NT = (((1,), (1,)), ((), ()))
_TN = (((0,), (0,)), ((), ()))


def _params(*sem):
    return pltpu.CompilerParams(dimension_semantics=sem, vmem_limit_bytes=VMEM_LIMIT)


def _rms(x, g):
    return x * lax.rsqrt(jnp.mean(x * x, axis=-1, keepdims=True) + RMS_EPS) * g


def _bf16_parts(x, n):
    parts = []
    for _ in range(n):
        p = x.astype(jnp.bfloat16)
        parts.append(p)
        x = x - p.astype(jnp.float32)
    return parts


def _in_proj_body(x_ref, g_ref, w_ref, qkv_ref, xo_ref, gate_ref, *, col_chunk):
    h = _rms(x_ref[...], g_ref[...]).astype(jnp.bfloat16)
    col = 0
    for out_ref in (qkv_ref, xo_ref, gate_ref):
        for c in range(0, out_ref.shape[1], col_chunk):
            out_ref[:, c:c + col_chunk] = jnp.dot(
                h, w_ref[:, col + c:col + c + col_chunk],
                preferred_element_type=jnp.float32).astype(out_ref.dtype)
        col += out_ref.shape[1]


def in_proj(x2, g, w_bf16, *, tm=512, col_chunk=512):
    n, d = x2.shape
    widths = (3 * ATT_WIDTH, 2 * MLSTM_WIDTH, 2 * d)
    assert sum(widths) == w_bf16.shape[1] and n % tm == 0
    return pl.pallas_call(
        functools.partial(_in_proj_body, col_chunk=col_chunk),
        grid=(n // tm,),
        in_specs=[pl.BlockSpec((tm, d), lambda i: (i, 0)),
                  pl.BlockSpec((1, d), lambda i: (0, 0)),
                  pl.BlockSpec(w_bf16.shape, lambda i: (0, 0))],
        out_specs=[pl.BlockSpec((tm, w), lambda i: (i, 0)) for w in widths],
        out_shape=[jax.ShapeDtypeStruct((n, w), jnp.bfloat16) for w in widths],
        compiler_params=_params("parallel"),
        name="in_proj",
    )(x2, g.reshape(1, d), w_bf16)


def _moba_body(q_ref, k_ref, v_ref, o_ref, *, pair, seq):
    blk = MOBA_BLOCK
    n_blk = seq // blk
    f32, bf16 = jnp.float32, jnp.bfloat16
    lane = lax.broadcasted_iota(jnp.int32, (1, LANES), 1)
    first_head = lane < ATT_HEAD_DIM
    row = lax.broadcasted_iota(jnp.int32, (2 * blk, blk), 0)
    col = lax.broadcasted_iota(jnp.int32, (2 * blk, blk), 1)
    t_in_blk = jnp.where(row >= blk, row - blk, row)
    rel = (t_in_blk - col).astype(f32)
    causal = t_in_blk >= col
    log2e = float(np.log2(np.e))
    scale2 = ATT_HEAD_DIM ** -0.5 * log2e
    head = (2 * pair + (row[:, 0:1] >= blk).astype(jnp.int32)).astype(f32)
    slope2 = jnp.exp2(-8.0 * (head + 1.0) / N_ATT_HEADS) * log2e
    bias = [slope2 * (rel + float(d * blk)) for d in range(n_blk)]
    lane_row = lax.broadcasted_iota(jnp.int32, (LANES, LANES), 0)
    spread = [jnp.where(lane_row == j, 1.0, 0.0).astype(bf16) for j in range(n_blk)]

    k_all = k_ref[...]
    v_ones = jnp.concatenate([v_ref[...], jnp.ones((seq, LANES), bf16)], axis=1)
    k_mean = jnp.concatenate(
        [jnp.mean(k_all[j * blk:(j + 1) * blk].astype(f32), axis=0, keepdims=True)
         for j in range(n_blk)] + [jnp.zeros((LANES - n_blk, LANES), f32)], axis=0)
    k_mean_parts = _bf16_parts(k_mean, 3)

    for qi in range(n_blk):
        q_blk = q_ref[qi * blk:(qi + 1) * blk, :]
        zero = jnp.zeros_like(q_blk)
        q2 = jnp.concatenate([jnp.where(first_head, q_blk, zero), jnp.where(first_head, zero, q_blk)], axis=0)
        sel = None
        if qi > MOBA_TOPK:
            g = sum(lax.dot_general(q2, part, _NT, preferred_element_type=f32) for part in k_mean_parts)
            g = jnp.where(lane < qi, g, -jnp.inf)
            ahead = jnp.zeros((2 * blk, LANES), f32)
            for d in range(1, qi):
                ahead = ahead + jnp.where(pltpu.roll(g, d, axis=1) >= g, 1.0, 0.0)
                ahead = ahead + jnp.where(pltpu.roll(g, LANES - d, axis=1) > g, 1.0, 0.0)
            keep = jnp.where(ahead < MOBA_TOPK, 1.0, 0.0).astype(bf16)
            sel = [jnp.dot(keep, spread[j], preferred_element_type=f32) > 0.5 for j in range(qi)]
        logits = []
        for j in range(qi + 1):
            s = lax.dot_general(q2, k_all[j * blk:(j + 1) * blk], _NT, preferred_element_type=f32)
            s = s * scale2 - bias[qi - j]
            if j == qi:
                s = jnp.where(causal, s, NEG_BIG)
            elif sel is not None:
                s = jnp.where(jnp.concatenate([sel[j]] * (blk // LANES), axis=1), s, NEG_BIG)
            logits.append(s)
        s_all = jnp.concatenate(logits, axis=1)
        m = jnp.max(s_all, axis=1, keepdims=True)
        p = jnp.exp2(s_all - m)
        o2 = jnp.dot(p.astype(bf16), v_ones[:(qi + 1) * blk], preferred_element_type=f32)
        o2 = o2[:, :LANES] / o2[:, LANES:LANES + 1]
        o = jnp.where(first_head, o2[:blk], o2[blk:])
        o_ref[qi * blk:(qi + 1) * blk, :] = o.astype(o_ref.dtype)


def _moba_kernel(q_ref, k_ref, v_ref, o_ref, *, seq):
    _moba_body(q_ref, k_ref, v_ref, o_ref, pair=pl.program_id(1), seq=seq)


def moba(qkv, batch, seq):
    n = batch * seq
    n_pairs = ATT_WIDTH // LANES
    assert seq % MOBA_BLOCK == 0
    spec = lambda off: pl.BlockSpec((seq, LANES), lambda b, p: (b, off + p))
    return pl.pallas_call(
        functools.partial(_moba_kernel, seq=seq),
        grid=(batch, n_pairs),
        in_specs=[spec(0), spec(n_pairs), spec(2 * n_pairs)],
        out_specs=pl.BlockSpec((seq, LANES), lambda b, p: (b, p)),
        out_shape=jax.ShapeDtypeStruct((n, ATT_WIDTH), jnp.bfloat16),
        compiler_params=_params("parallel", "parallel"),
        name="moba",
    )(qkv, qkv, qkv)


def _log_sigmoid(z):
    return jnp.minimum(z, 0.0) - jnp.log(1.0 + jnp.exp(-jnp.abs(z)))


def _mlstm_body(xm_ref, om_ref, cw_ref, cb_ref, wq_ref, wk_ref, wiq_ref, wik_ref, wix_ref,
                wiqt_ref, wikt_ref, wixt_ref, bi_ref, bit_ref, g_ref, y_ref,
                q_sc, k_sc, gcol_sc, grow_sc, ct_sc, n_sc, m_sc, *, seq):
    L = MLSTM_CHUNK
    H = N_MLSTM_HEADS
    n_chunks = seq // L
    f32, bf16 = jnp.float32, jnp.bfloat16

    x = xm_ref[...].astype(f32)
    t_idx = lax.broadcasted_iota(jnp.int32, (seq, 1), 0)
    acc = x * cw_ref[MLSTM_CONV - 1:MLSTM_CONV, :] + cb_ref[...]
    for d in range(1, MLSTM_CONV):
        shifted = jnp.where(t_idx >= d, pltpu.roll(x, d, axis=0), 0.0)
        acc = acc + shifted * cw_ref[MLSTM_CONV - 1 - d:MLSTM_CONV - d, :]
    xc = (acc * jax.nn.sigmoid(acc)).astype(bf16)

    xm = xm_ref[...]
    if_col = jnp.dot(xm, wix_ref[...], preferred_element_type=f32) + bi_ref[...]
    if_row = lax.dot_general(wixt_ref[...], xm, _NT, preferred_element_type=f32) + bit_ref[...]
    for h in range(H):
        xch = xc[:, h * LANES:(h + 1) * LANES]
        q = jnp.dot(xch, wq_ref[h], preferred_element_type=f32).astype(bf16)
        k = jnp.dot(xch, wk_ref[h], preferred_element_type=f32).astype(bf16)
        if_col = if_col + jnp.dot(q, wiq_ref[h], preferred_element_type=f32)
        if_col = if_col + jnp.dot(k, wik_ref[h], preferred_element_type=f32)
        if_row = if_row + lax.dot_general(wiqt_ref[h], q, _NT, preferred_element_type=f32)
        if_row = if_row + lax.dot_general(wikt_ref[h], k, _NT, preferred_element_type=f32)
        q_sc[h] = q
        k_sc[h] = (k.astype(f32) * (MLSTM_QK_DIM ** -0.5)).astype(bf16)
    is_f_col = lax.broadcasted_iota(jnp.int32, (1, 2 * H), 1) >= H
    is_f_row = lax.broadcasted_iota(jnp.int32, (2 * H, 1), 0) >= H
    gcol_sc[...] = jnp.where(is_f_col, _log_sigmoid(if_col), if_col)
    g_row = jnp.where(is_f_row, _log_sigmoid(if_row), if_row)
    for c in range(n_chunks):
        grow_sc[c] = g_row[:, c * L:(c + 1) * L]

    ct_sc[...] = jnp.zeros_like(ct_sc)
    n_sc[...] = jnp.zeros_like(n_sc)
    m_sc[...] = jnp.zeros_like(m_sc)

    r_i = lax.broadcasted_iota(jnp.int32, (L, L), 0)
    c_i = lax.broadcasted_iota(jnp.int32, (L, L), 1)
    causal = r_i >= c_i
    tri_lo = jnp.where(causal, 1.0, 0.0).astype(f32)
    tri_up = jnp.where(c_i >= r_i, 1.0, 0.0).astype(f32)

    def chunk(c, carry):
        r0 = pl.multiple_of(c * L, L)
        gc = gcol_sc[pl.ds(r0, L), :]
        gr = grow_sc[c]
        b_cols = jnp.dot(tri_lo, gc, precision=lax.Precision.HIGHEST, preferred_element_type=f32)
        b_rows = jnp.dot(gr, tri_up, precision=lax.Precision.HIGHEST, preferred_element_type=f32)
        for h in range(H):
            hs = slice(h * LANES, (h + 1) * LANES)
            b_col = b_cols[:, H + h:H + h + 1]
            i_col = gc[:, h:h + 1]
            b_row = b_rows[H + h:H + h + 1, :]
            i_row = gr[h:h + 1, :]
            b_last = b_col[L - 1:L, :]
            m_prev = m_sc[h][:, 0:1]
            q = q_sc[h, pl.ds(r0, L), :]
            k = k_sc[h, pl.ds(r0, L), :]
            v = xm_ref[pl.ds(r0, L), hs]

            log_d = jnp.where(causal, b_col + (i_row - b_row), NEG_BIG)
            inter = b_col + m_prev
            m_t = jnp.maximum(inter, jnp.max(log_d, axis=1, keepdims=True))
            w_inter = jnp.exp(inter - m_t)
            s = lax.dot_general(q, k, _NT, preferred_element_type=f32) * jnp.exp(log_d - m_t)
            num = jnp.dot(s.astype(bf16), v, preferred_element_type=f32)
            num = num + w_inter * jnp.dot(q, ct_sc[h].astype(bf16), preferred_element_type=f32)
            den = jnp.sum(s, axis=1, keepdims=True)
            den = den + w_inter * jnp.sum(q.astype(f32) * n_sc[h], axis=1, keepdims=True)
            hv = num / jnp.maximum(jnp.abs(den), jnp.exp(-m_t))
            hv = hv * lax.rsqrt(jnp.mean(hv * hv, axis=1, keepdims=True) + RMS_EPS) * g_ref[:, hs]
            o_gate = jax.nn.sigmoid(om_ref[pl.ds(r0, L), hs].astype(f32))
            y_ref[pl.ds(r0, L), hs] = (o_gate * hv).astype(y_ref.dtype)

            g_col = b_last - b_col + i_col
            m_new = jnp.maximum(b_last + m_prev, jnp.max(g_col, axis=0, keepdims=True))
            wg = jnp.exp(g_col - m_new)
            decay = jnp.exp(b_last + m_prev - m_new)
            wv = (wg * v.astype(f32)).astype(bf16)
            ct_sc[h] = decay * ct_sc[h] + lax.dot_general(k, wv, _TN, preferred_element_type=f32)
            n_sc[h] = decay * n_sc[h] + jnp.sum(wg * k.astype(f32), axis=0, keepdims=True)
            m_sc[h] = jnp.broadcast_to(m_new, (1, LANES))
        return carry

    lax.fori_loop(0, n_chunks, chunk, 0)


def _mlstm_kernel(*refs, seq):
    _mlstm_body(*refs, seq=seq)


def mlstm(xo, conv_w, conv_b, w_qk, w_if, b_if, g, batch, seq):
    n = batch * seq
    H, dv, dk = N_MLSTM_HEADS, MLSTM_V_DIM, MLSTM_QK_DIM
    assert seq % MLSTM_CHUNK == 0 and dv == LANES
    bf16 = jnp.bfloat16
    pad = ((0, 0), (0, 0), (0, LANES - dk))
    wq = jnp.pad(w_qk[:, :, :dk], pad).astype(bf16)
    wk = jnp.pad(w_qk[:, :, dk:], pad).astype(bf16)
    rpad = ((0, 0), (0, LANES - dk), (0, 0))
    wiq = jnp.pad(w_if[:H * dk].reshape(H, dk, 2 * H), rpad).astype(bf16)
    wik = jnp.pad(w_if[H * dk:2 * H * dk].reshape(H, dk, 2 * H), rpad).astype(bf16)
    wix = w_if[2 * H * dk:].astype(bf16)
    wiqt, wikt, wixt = wiq.transpose(0, 2, 1), wik.transpose(0, 2, 1), wix.T
    full = lambda a: pl.BlockSpec(a.shape, lambda b: (0,) * a.ndim)
    consts = [conv_w, conv_b.reshape(1, -1), wq, wk, wiq, wik, wix, wiqt, wikt, wixt,
              b_if.reshape(1, -1), b_if.reshape(-1, 1), g.reshape(1, -1)]
    return pl.pallas_call(
        functools.partial(_mlstm_kernel, seq=seq),
        grid=(batch,),
        in_specs=[pl.BlockSpec((seq, MLSTM_WIDTH), lambda b: (b, 0)),
                  pl.BlockSpec((seq, MLSTM_WIDTH), lambda b: (b, 1))] + [full(a) for a in consts],
        out_specs=pl.BlockSpec((seq, MLSTM_WIDTH), lambda b: (b, 0)),
        out_shape=jax.ShapeDtypeStruct((n, MLSTM_WIDTH), bf16),
        scratch_shapes=[pltpu.VMEM((H, seq, LANES), bf16), pltpu.VMEM((H, seq, LANES), bf16),
                        pltpu.VMEM((seq, 2 * H), jnp.float32),
                        pltpu.VMEM((seq // MLSTM_CHUNK, 2 * H, MLSTM_CHUNK), jnp.float32),
                        pltpu.VMEM((H, LANES, LANES), jnp.float32),
                        pltpu.VMEM((H, 1, LANES), jnp.float32),
                        pltpu.VMEM((H, 1, LANES), jnp.float32)],
        compiler_params=_params("parallel"),
        name="mlstm",
    )(xo, xo, *consts)


def _store_rows_as_tiles(dst_ref, val):
    rows = val.shape[0]
    for c in range(ROW_CHUNKS):
        dst_ref[pl.ds(c, rows, stride=ROW_CHUNKS), :] = val[:, c * LANES:(c + 1) * LANES]


def _load_tiles_as_rows(src_ref, rows, lead=()):
    return jnp.concatenate(
        [src_ref[lead + (pl.ds(c, rows, stride=ROW_CHUNKS), slice(None))] for c in range(ROW_CHUNKS)],
        axis=1)


def _merge_route_body(x_ref, ya_ref, ym_ref, ga_ref, gm_ref, wb0_ref, wb1_ref, wo_ref, g_ref,
                      wr_ref, br_ref, x1_ref, h2_ref, gate_ref, lpos_ref, cnt_ref):
    f32, bf16 = jnp.float32, jnp.bfloat16
    tm = x_ref.shape[0]

    u = jax.nn.sigmoid(ga_ref[...].astype(f32)) * jnp.dot(ya_ref[...], wb0_ref[...], preferred_element_type=f32)
    u = u + jax.nn.sigmoid(gm_ref[...].astype(f32)) * jnp.dot(ym_ref[...], wb1_ref[...], preferred_element_type=f32)
    x1 = x_ref[...] + jnp.dot(u.astype(bf16), wo_ref[...], preferred_element_type=f32)
    x1_ref[...] = x1
    h2 = _rms(x1, g_ref[...])
    _store_rows_as_tiles(h2_ref, h2)
    (h_hi, h_lo), (w_hi, w_lo) = _bf16_parts(h2, 2), _bf16_parts(wr_ref[...], 2)
    logits = (jnp.dot(h_hi, w_hi, preferred_element_type=f32) + jnp.dot(h_lo, w_hi, preferred_element_type=f32)
              + jnp.dot(h_hi, w_lo, preferred_element_type=f32)) + br_ref[...]
    logits = jnp.transpose(logits)[:N_EXPERTS]
    e_id = lax.broadcasted_iota(jnp.int32, logits.shape, 0).astype(f32)
    chosen = jnp.zeros(logits.shape, f32)
    vals, ids = [], []
    for _ in range(TOP_K):
        top = jnp.max(logits, axis=0, keepdims=True)
        first = jnp.min(jnp.where(logits == top, e_id, float(N_EXPERTS)), axis=0, keepdims=True)
        hit = e_id == first
        chosen = jnp.where(hit, 1.0, chosen)
        logits = jnp.where(hit, -jnp.inf, logits)
        vals.append(top)
        ids.append(first)
    ex = [jnp.exp(v - vals[0]) for v in vals]
    total = ex[0] + ex[1] + ex[2] + ex[3]

    chosen_b = chosen.astype(bf16)
    t_r = lax.broadcasted_iota(jnp.int32, (tm, tm), 0)
    t_c = lax.broadcasted_iota(jnp.int32, (tm, tm), 1)
    earlier = jnp.where(t_r < t_c, 1.0, 0.0).astype(bf16)
    before = jnp.dot(chosen_b, earlier, preferred_element_type=f32)
    e_r = lax.broadcasted_iota(jnp.int32, (N_EXPERTS, N_EXPERTS), 0)
    e_c = lax.broadcasted_iota(jnp.int32, (N_EXPERTS, N_EXPERTS), 1)
    lower = jnp.where(e_c < e_r, 1.0, 0.0).astype(bf16)
    seg_off = jnp.sum(jnp.dot(lower, chosen_b, preferred_element_type=f32), axis=1, keepdims=True)
    pos = before + seg_off
    for kk in range(TOP_K):
        gate_ref[0, :, kk * tm:(kk + 1) * tm] = ex[kk] / total
        lpos_ref[0, :, kk * tm:(kk + 1) * tm] = jnp.sum(jnp.where(e_id == ids[kk], pos, 0.0), axis=0,
                                                        keepdims=True).astype(jnp.int32)
    counts = jnp.sum(chosen, axis=1, keepdims=True)
    cnt_ref[0] = jnp.broadcast_to(counts, (N_EXPERTS, LANES)).astype(jnp.int32)


def merge_route(x2, ya, ym, gates, wb0, wb1, wo, g, w_router, b_router, *, tm):
    n, d = x2.shape
    e = w_router.shape[1]
    assert n % tm == 0 and d == ROW_CHUNKS * LANES and e == N_EXPERTS
    full = lambda a: pl.BlockSpec(a.shape, lambda i: (0,) * a.ndim)
    lane_pad = ((0, 0), (0, LANES - e))
    consts = [wb0, wb1, wo, g.reshape(1, d), jnp.pad(w_router, lane_pad),
              jnp.pad(b_router.reshape(1, e), lane_pad)]
    tok = lambda dt: jax.ShapeDtypeStruct((n // tm, 1, TOP_K * tm), dt)
    return pl.pallas_call(
        _merge_route_body,
        grid=(n // tm,),
        in_specs=[pl.BlockSpec((tm, d), lambda i: (i, 0)),
                  pl.BlockSpec((tm, ATT_WIDTH), lambda i: (i, 0)),
                  pl.BlockSpec((tm, MLSTM_WIDTH), lambda i: (i, 0)),
                  pl.BlockSpec((tm, d), lambda i: (i, 0)),
                  pl.BlockSpec((tm, d), lambda i: (i, 1))] + [full(a) for a in consts],
        out_specs=[pl.BlockSpec((tm, d), lambda i: (i, 0)),
                   pl.BlockSpec((tm * ROW_CHUNKS, LANES), lambda i: (i, 0)),
                   pl.BlockSpec((1, 1, TOP_K * tm), lambda i: (i, 0, 0)),
                   pl.BlockSpec((1, 1, TOP_K * tm), lambda i: (i, 0, 0)),
                   pl.BlockSpec((1, e, LANES), lambda i: (i, 0, 0))],
        out_shape=[jax.ShapeDtypeStruct((n, d), jnp.float32),
                   jax.ShapeDtypeStruct((n * ROW_CHUNKS, LANES), jnp.float32),
                   tok(jnp.float32), tok(jnp.int32),
                   jax.ShapeDtypeStruct((n // tm, e, LANES), jnp.int32)],
        compiler_params=_params("parallel"),
        name="merge_route",
    )(x2, ya, ym, gates, gates, *consts)


def _segment_pieces(cnt, max_rows, fn):
    top = 1 << (max_rows.bit_length() - 1)
    sizes = [top >> i for i in range(top.bit_length())]
    rare = [s for s in sizes if s * 8 > top]

    def pieces(group, done):
        for size in group:
            bit = cnt & size

            def piece(done=done, size=size):
                fn(done, size)
            pl.when(bit != 0)(piece)
            done = done + bit

    pl.when(cnt >= rare[-1])(lambda: pieces(rare, 0))
    pieces([s for s in sizes if s not in rare], cnt & -rare[-1])


def _for_tile_segments(meta, tile_rows, make_copy, action):
    cnt_ref, off_ref, dst_ref = meta

    def per_expert(e, carry):
        off, dst = off_ref[0, 0, e], dst_ref[0, 0, e]
        _segment_pieces(cnt_ref[0, 0, e], tile_rows,
                        lambda done, size: getattr(make_copy(off + done, dst + done, size), action)())
        return carry
    lax.fori_loop(0, N_EXPERTS, per_expert, 0)


def _dispatch_body(lpos_ref, cnt_ref, off_ref, dst_ref, pad_len_ref, pad_row_ref, x_ref, out_ref,
                   cbuf, zbuf, sem, zsem, *, tc, rb):
    step = pl.program_id(0)
    last = pl.num_programs(0) - 1
    slot = step % 2

    def compact(t, carry):
        row = x_ref[t]
        for kk in range(TOP_K):
            cbuf[slot, lpos_ref[0, 0, kk * tc + t]] = row
        return carry
    lax.fori_loop(0, tc, compact, 0, unroll=8)

    def seg_copy(sl):
        return lambda local, glob, size: pltpu.make_async_copy(
            cbuf.at[sl, pl.ds(local, size)], out_ref.at[pl.ds(glob, size)], sem.at[sl])

    def wait_tile(sl):
        pltpu.make_async_copy(cbuf.at[sl], out_ref.at[pl.ds(0, TOP_K * tc)], sem.at[sl]).wait()

    _for_tile_segments((cnt_ref, off_ref, dst_ref), tc, seg_copy(slot), "start")

    @pl.when(step > 0)
    def _():
        wait_tile(1 - slot)

    @pl.when(step == last)
    def _():
        wait_tile(slot)
        zbuf[...] = jnp.zeros_like(zbuf)

        def pads(action):
            def per_pad(j, carry):
                row = pad_row_ref[0, j]
                _segment_pieces(pad_len_ref[0, j], rb, lambda done, size: getattr(
                    pltpu.make_async_copy(zbuf.at[pl.ds(0, size)], out_ref.at[pl.ds(row + done, size)], zsem),
                    action)())
                return carry
            lax.fori_loop(0, pad_len_ref.shape[1], per_pad, 0)
        pads("start")
        pads("wait")


def dispatch(lpos, seg_cnt, seg_off, seg_dst, pad_len, pad_row, h2_tiles, n_rows, *, tc, rb):
    n = h2_tiles.shape[0]
    steps = n // tc
    assert n % tc == 0
    smem = pltpu.SMEM
    seg = pl.BlockSpec((1, 1, N_EXPERTS), lambda i: (i, 0, 0), memory_space=smem)
    whole = lambda a: pl.BlockSpec(a.shape, lambda i: (0,) * a.ndim, memory_space=smem)
    return pl.pallas_call(
        functools.partial(_dispatch_body, tc=tc, rb=rb),
        grid=(steps,),
        in_specs=[pl.BlockSpec((1, 1, TOP_K * tc), lambda i: (i, 0, 0), memory_space=smem),
                  seg, seg, seg, whole(pad_len), whole(pad_row),
                  pl.BlockSpec((tc, ROW_CHUNKS, LANES), lambda i: (i, 0, 0))],
        out_specs=pl.BlockSpec(memory_space=pl.ANY),
        out_shape=jax.ShapeDtypeStruct((n_rows, ROW_CHUNKS, LANES), h2_tiles.dtype),
        scratch_shapes=[pltpu.VMEM((2, TOP_K * tc, ROW_CHUNKS, LANES), h2_tiles.dtype),
                        pltpu.VMEM((rb, ROW_CHUNKS, LANES), h2_tiles.dtype),
                        pltpu.SemaphoreType.DMA((2,)), pltpu.SemaphoreType.DMA(())],
        compiler_params=_params("arbitrary"),
        name="dispatch",
    )(lpos, seg_cnt, seg_off, seg_dst, pad_len, pad_row, h2_tiles)


def _experts_body(blk_e_ref, nxt_e_ref, valid_ref, used_ref, x_ref, wgu_hbm, bg_ref, bu_ref, wd_hbm, bd_ref,
                  perm_ref, y_ref, wgu_st, wd_st, wg_sc, wu_sc, wd_sc, sem, *, halves):
    f32, bf16 = jnp.float32, jnp.bfloat16
    step = pl.program_id(0)
    live = step < used_ref[0]
    rows = x_ref.shape[0] // ROW_CHUNKS
    expert = blk_e_ref[step]

    def fetch(e, action):
        getattr(pltpu.make_async_copy(wgu_hbm.at[e], wgu_st, sem.at[0]), action)()
        getattr(pltpu.make_async_copy(wd_hbm.at[e], wd_st, sem.at[1]), action)()

    @pl.when(live & ((step == 0) | (expert != blk_e_ref[jnp.maximum(step - 1, 0)])))
    def _():
        @pl.when(step == 0)
        def _():
            fetch(expert, "start")
        fetch(expert, "wait")
        tile = perm_ref.shape[0]
        half = tile // 2
        for t in range(wgu_st.shape[1] // tile):
            w = wgu_st[:, t * tile:(t + 1) * tile].astype(bf16)
            o = jnp.dot(w, perm_ref[...], preferred_element_type=f32)
            wg_sc[:, t * half:(t + 1) * half] = o[:, :half].astype(bf16)
            wu_sc[:, t * half:(t + 1) * half] = o[:, half:].astype(bf16)
        wd_sc[...] = wd_st[...].astype(bf16)
        nxt = nxt_e_ref[step]

        @pl.when(nxt >= 0)
        def _():
            fetch(nxt, "start")

    part = rows // halves

    def ffn(groups):
        for h in range(groups):
            xb = jnp.concatenate(
                [x_ref[pl.ds(h * part * ROW_CHUNKS + c, part, stride=ROW_CHUNKS), :]
                 for c in range(ROW_CHUNKS)], axis=1).astype(bf16)
            gate = jnp.dot(xb, wg_sc[...], preferred_element_type=f32) + bg_ref[0]
            up = jnp.dot(xb, wu_sc[...], preferred_element_type=f32) + bu_ref[0]
            gate = jnp.minimum(gate, SWIGLU_LIMIT)
            up = jnp.clip(up, -SWIGLU_LIMIT, SWIGLU_LIMIT)
            act = (up + 1.0) * (gate * jax.nn.sigmoid(gate * SWIGLU_ALPHA))
            y = jnp.dot(act.astype(bf16), wd_sc[...], preferred_element_type=f32) + bd_ref[0]
            for c in range(ROW_CHUNKS):
                y_ref[pl.ds(h * part * ROW_CHUNKS + c, part, stride=ROW_CHUNKS), :] = \
                    y[:, c * LANES:(c + 1) * LANES]
        if groups < halves:
            y_ref[groups * part * ROW_CHUNKS:, :] = jnp.zeros(((halves - groups) * part * ROW_CHUNKS, LANES), f32)

    n_valid = valid_ref[step]
    for groups in range(1, halves + 1):
        wanted = live & (n_valid > (groups - 1) * part)
        if groups < halves:
            wanted = wanted & (n_valid <= groups * part)
        pl.when(wanted)(functools.partial(ffn, groups))

    @pl.when(jnp.logical_not(live))
    def _():
        y_ref[...] = jnp.zeros_like(y_ref)


def experts(blk_expert, nxt_expert, blk_valid, n_used, xs_tiles, w_gu, b_g, b_u, w_d, b_d, *, halves=2):
    n_blocks = blk_expert.shape[0]
    rb = MOE_ROW_BLOCK
    d, ff = w_d.shape[2], w_d.shape[1]
    tile = 2 * LANES
    perm = np.zeros((tile, tile), np.float32)
    perm[2 * np.arange(LANES), np.arange(LANES)] = 1.0
    perm[2 * np.arange(LANES) + 1, LANES + np.arange(LANES)] = 1.0
    e_map3 = lambda i, be, nx, bv, nu: (be[i], 0, 0)
    rows_map = lambda i, be, nx, bv, nu: (i, 0)
    grid_spec = pltpu.PrefetchScalarGridSpec(
        num_scalar_prefetch=4,
        grid=(n_blocks,),
        in_specs=[pl.BlockSpec((rb * ROW_CHUNKS, LANES), rows_map),
                  pl.BlockSpec(memory_space=pl.ANY),
                  pl.BlockSpec((1, 1, ff), e_map3), pl.BlockSpec((1, 1, ff), e_map3),
                  pl.BlockSpec(memory_space=pl.ANY), pl.BlockSpec((1, 1, d), e_map3),
                  pl.BlockSpec((tile, tile), lambda i, be, nx, bv, nu: (0, 0))],
        out_specs=pl.BlockSpec((rb * ROW_CHUNKS, LANES), rows_map),
        scratch_shapes=[pltpu.VMEM((d, 2 * ff), w_gu.dtype), pltpu.VMEM((ff, d), w_d.dtype),
                        pltpu.VMEM((d, ff), jnp.bfloat16), pltpu.VMEM((d, ff), jnp.bfloat16),
                        pltpu.VMEM((ff, d), jnp.bfloat16), pltpu.SemaphoreType.DMA((2,))],
    )
    return pl.pallas_call(
        functools.partial(_experts_body, halves=halves),
        grid_spec=grid_spec,
        out_shape=jax.ShapeDtypeStruct(xs_tiles.shape, jnp.float32),
        compiler_params=_params("arbitrary"),
        name="experts",
    )(blk_expert, nxt_expert, blk_valid, n_used, xs_tiles, w_gu, b_g, b_u, w_d, b_d,
      jnp.asarray(perm, jnp.bfloat16))


def _combine_ple_body(lpos_ref, gw_ref, cnt_ref, off_ref, dst_ref, ncnt_ref, noff_ref, ndst_ref,
                      x1_ref, ys_ref, p_ref, gp_ref, wpg_ref, wpp_ref, *rest):
    cbuf, msum_a, msum_b, sem = rest[-4:]
    rest = rest[:-4]
    gf_ref, o_ref = rest if len(rest) == 2 else (None, rest[0])
    f32, bf16 = jnp.float32, jnp.bfloat16
    tm = x1_ref.shape[0]
    step = pl.program_id(0)
    n_tiles = pl.num_programs(0) - 1
    slot = step % 2

    def seg_copy(sl):
        return lambda local, glob, size: pltpu.make_async_copy(
            ys_ref.at[pl.ds(glob, size)], cbuf.at[sl, pl.ds(local, size)], sem.at[sl])

    cur, nxt = (cnt_ref, off_ref, dst_ref), (ncnt_ref, noff_ref, ndst_ref)

    @pl.when(step == 0)
    def _():
        _for_tile_segments(cur, tm, seg_copy(0), "start")
        msum_b[...] = jnp.zeros_like(msum_b)

    @pl.when(step + 1 < n_tiles)
    def _():
        _for_tile_segments(nxt, tm, seg_copy(1 - slot), "start")

    @pl.when(step < n_tiles)
    def _():
        pltpu.make_async_copy(ys_ref.at[pl.ds(0, TOP_K * tm)], cbuf.at[slot], sem.at[slot]).wait()

    def gather_and_dense(msum_cur, msum_prev):
        for t in range(tm):
            acc = gw_ref[0, 0, t] * cbuf[slot, lpos_ref[0, 0, t]]
            for kk in range(1, TOP_K):
                acc = acc + gw_ref[0, 0, kk * tm + t] * cbuf[slot, lpos_ref[0, 0, kk * tm + t]]
            msum_cur[t] = acc
        moe = jnp.concatenate([msum_prev[:, c, :] for c in range(ROW_CHUNKS)], axis=1)
        x2 = x1_ref[...] + moe
        r = _rms(x2, gp_ref[...]).astype(bf16)
        gate = jax.nn.sigmoid(jnp.dot(r, wpg_ref[...], preferred_element_type=f32))
        emb = jnp.dot(p_ref[...].astype(bf16), wpp_ref[...], preferred_element_type=f32)
        x3 = x2 + gate * emb
        o_ref[...] = x3 if gf_ref is None else _rms(x3, gf_ref[...])

    pl.when(slot == 0)(functools.partial(gather_and_dense, msum_a, msum_b))
    pl.when(slot == 1)(functools.partial(gather_and_dense, msum_b, msum_a))


def combine_ple(lpos, gate_w, seg_cnt, seg_off, seg_dst, x1, ys_tiles, p2, g_ple, w_pg, w_pp, g_final,
                *, tm):
    n, d = x1.shape
    tiles = n // tm
    assert n % tm == 0 and tiles >= 2
    smem = pltpu.SMEM
    full = lambda a: pl.BlockSpec(a.shape, lambda i: (0,) * a.ndim)
    gathered = lambda i, shift=0: (jnp.minimum(i + shift, tiles - 1), 0, 0)
    finished = lambda i: (jnp.maximum(i - 1, 0), 0)
    seg = lambda shift: pl.BlockSpec((1, 1, N_EXPERTS), functools.partial(gathered, shift=shift),
                                     memory_space=smem)
    tok = pl.BlockSpec((1, 1, TOP_K * tm), gathered, memory_space=smem)
    consts = [g_ple.reshape(1, d), w_pg, w_pp] + ([] if g_final is None else [g_final.reshape(1, d)])
    return pl.pallas_call(
        _combine_ple_body,
        grid=(tiles + 1,),
        in_specs=[tok, tok, seg(0), seg(0), seg(0), seg(1), seg(1), seg(1),
                  pl.BlockSpec((tm, d), finished),
                  pl.BlockSpec(memory_space=pl.ANY),
                  pl.BlockSpec((tm, p2.shape[1]), finished)] + [full(a) for a in consts],
        out_specs=pl.BlockSpec((tm, d), finished),
        out_shape=jax.ShapeDtypeStruct((n, d), jnp.float32),
        scratch_shapes=[pltpu.VMEM((2, TOP_K * tm, ROW_CHUNKS, LANES), ys_tiles.dtype),
                        pltpu.VMEM((tm, ROW_CHUNKS, LANES), jnp.float32),
                        pltpu.VMEM((tm, ROW_CHUNKS, LANES), jnp.float32),
                        pltpu.SemaphoreType.DMA((2,))],
        compiler_params=_params("arbitrary"),
        name="combine_ple",
    )(lpos, gate_w, seg_cnt, seg_off, seg_dst, seg_cnt, seg_off, seg_dst, x1, ys_tiles, p2, *consts)


def _moe(h2_tiles, lpos, tile_cnt, w_gu, b_gu, w_d, b_d, *, tm):
    n = h2_tiles.shape[0] // ROW_CHUNKS
    rb = MOE_ROW_BLOCK
    i32 = jnp.int32
    n_blocks = -(-n * TOP_K // rb) + N_EXPERTS
    n_rows = n_blocks * rb
    counts = jnp.sum(tile_cnt, axis=0)
    padded = (counts + rb - 1) // rb * rb
    pend = jnp.cumsum(padded)
    pstart = pend - padded
    blk_start = jnp.arange(n_blocks, dtype=i32) * rb
    blk_expert = jnp.minimum(jnp.sum((pend[None, :] <= blk_start[:, None]).astype(i32), axis=1),
                             N_EXPERTS - 1)
    n_used = pend[-1:] // rb
    seg_cnt = tile_cnt[:, None, :]
    seg_off = (jnp.cumsum(tile_cnt, axis=1) - tile_cnt)[:, None, :]
    seg_dst = (pstart[None, :] + jnp.cumsum(tile_cnt, axis=0) - tile_cnt)[:, None, :]
    tail = n_used + jnp.arange(N_EXPERTS, dtype=i32)
    pad_len = jnp.concatenate([padded - counts, jnp.where(tail < n_blocks, rb, 0)])[None, :]
    pad_row = jnp.concatenate([pstart + counts, tail * rb])[None, :]
    b_g, b_u = b_gu[:, None, 0::2], b_gu[:, None, 1::2]
    xs = dispatch(lpos, seg_cnt, seg_off, seg_dst, pad_len.astype(i32), pad_row.astype(i32),
                  h2_tiles.reshape(n, ROW_CHUNKS, LANES), n_rows, tc=tm, rb=rb)
    e_ids = jnp.arange(N_EXPERTS, dtype=i32)
    later = (padded > 0)[None, :] & (e_ids[None, :] > e_ids[:, None])
    nxt_tab = jnp.min(jnp.where(later, e_ids[None, :], N_EXPERTS), axis=1)
    nxt_tab = jnp.where(nxt_tab == N_EXPERTS, -1, nxt_tab)
    nxt_expert = jnp.sum(jnp.where(blk_expert[:, None] == e_ids[None, :], nxt_tab[None, :], 0), axis=1)
    hit = blk_expert[:, None] == e_ids[None, :]
    blk_valid = jnp.clip(jnp.sum(jnp.where(hit, (pstart + counts)[None, :], 0), axis=1) - blk_start, 0, rb)
    ys = experts(blk_expert, nxt_expert.astype(i32), blk_valid.astype(i32), n_used.astype(i32),
                 xs.reshape(n_rows * ROW_CHUNKS, LANES), w_gu, b_g, b_u, w_d, b_d[:, None, :])
    return (seg_cnt, seg_off, seg_dst), ys.reshape(n_rows, ROW_CHUNKS, LANES)


def kernel(x, p, norm_mix_g, w_in, conv_w, conv_b, w_qk_m, w_if, b_if, mnorm_g, w_branch, w_out, norm_ffn_g, w_router, b_router, w_gate_up, b_gate_up, w_down, b_down, norm_ple_g, w_ple_gate, w_ple_proj, final_norm_g):
    B, S, D = x.shape
    depth = w_in.shape[0]
    bf16 = jnp.bfloat16
    x2 = x.reshape(B * S, D)
    for i in range(depth):
        qkv, xo, gates = in_proj(x2, norm_mix_g[i], w_in[i].astype(bf16))
        y_a = moba(qkv, B, S)
        y_m = mlstm(xo, conv_w[i], conv_b[i], w_qk_m[i], w_if[i], b_if[i], mnorm_g[i], B, S)
        x1, h2_tiles, gate_w, lpos, tile_cnt = merge_route(
            x2, y_a, y_m, gates, w_branch[i, 0].astype(bf16), w_branch[i, 1].astype(bf16),
            w_out[i].astype(bf16), norm_ffn_g[i], w_router[i], b_router[i], tm=MOE_TOKEN_TILE)
        segs, ys = _moe(h2_tiles, lpos, tile_cnt[:, :, 0], w_gate_up[i], b_gate_up[i],
                        w_down[i], b_down[i], tm=MOE_TOKEN_TILE)
        x2 = combine_ple(lpos, gate_w, *segs, x1, ys, p[i].reshape(B * S, -1), norm_ple_g[i],
                         w_ple_gate[i].astype(bf16), w_ple_proj[i].astype(bf16),
                         final_norm_g if i == depth - 1 else None, tm=MOE_TOKEN_TILE)
    return x2.reshape(B, S, D)
```

```python
import functools

import jax
import jax.numpy as jnp
import numpy as np
from jax import lax
from jax.experimental import pallas as pl
from jax.experimental.pallas import tpu as pltpu

RMS_EPS = 1e-6
LANES = 128
ROW_CHUNKS = 8
VMEM_LIMIT = 56 * 1024 * 1024

N_ATT_HEADS = 8
ATT_HEAD_DIM = 64
ATT_WIDTH = N_ATT_HEADS * ATT_HEAD_DIM
MOBA_BLOCK = 256
MOBA_TOPK = 3

N_MLSTM_HEADS = 4
MLSTM_WIDTH = 512
MLSTM_V_DIM = 128
MLSTM_QK_DIM = 64
MLSTM_CONV = 4
MLSTM_SEQS_PER_STEP = 1
MLSTM_CHUNK = 256

N_EXPERTS = 32
TOP_K = 4
SWIGLU_ALPHA = 1.702
SWIGLU_LIMIT = 7.0
MOE_ROW_BLOCK = 512
MOE_TOKEN_TILE = 512

NEG_BIG = -1e30

_NT = (((1,), (1,)), ((), ()))
_TN = (((0,), (0,)), ((), ()))


def _params(*sem):
    return pltpu.CompilerParams(dimension_semantics=sem, vmem_limit_bytes=VMEM_LIMIT)


def _rms(x, g):
    return x * lax.rsqrt(jnp.mean(x * x, axis=-1, keepdims=True) + RMS_EPS) * g


def _bf16_parts(x, n):
    parts = []
    for _ in range(n):
        p = x.astype(jnp.bfloat16)
        parts.append(p)
        x = x - p.astype(jnp.float32)
    return parts


def _in_proj_body(x_ref, g_ref, w_ref, qkv_ref, xo_ref, gate_ref, *, col_chunk):
    h = _rms(x_ref[...], g_ref[...]).astype(jnp.bfloat16)
    col = 0
    for out_ref in (qkv_ref, xo_ref, gate_ref):
        for c in range(0, out_ref.shape[1], col_chunk):
            out_ref[:, c:c + col_chunk] = jnp.dot(
                h, w_ref[:, col + c:col + c + col_chunk],
                preferred_element_type=jnp.float32).astype(out_ref.dtype)
        col += out_ref.shape[1]


def in_proj(x2, g, w_bf16, *, tm=512, col_chunk=512):
    n, d = x2.shape
    widths = (3 * ATT_WIDTH, 2 * MLSTM_WIDTH, 2 * d)
    assert sum(widths) == w_bf16.shape[1] and n % tm == 0
    return pl.pallas_call(
        functools.partial(_in_proj_body, col_chunk=col_chunk),
        grid=(n // tm,),
        in_specs=[pl.BlockSpec((tm, d), lambda i: (i, 0)),
                  pl.BlockSpec((1, d), lambda i: (0, 0)),
                  pl.BlockSpec(w_bf16.shape, lambda i: (0, 0))],
        out_specs=[pl.BlockSpec((tm, w), lambda i: (i, 0)) for w in widths],
        out_shape=[jax.ShapeDtypeStruct((n, w), jnp.bfloat16) for w in widths],
        compiler_params=_params("parallel"),
        name="in_proj",
    )(x2, g.reshape(1, d), w_bf16)


def _moba_body(q_ref, k_ref, v_ref, o_ref, *, pair, seq):
    blk = MOBA_BLOCK
    n_blk = seq // blk
    f32, bf16 = jnp.float32, jnp.bfloat16
    lane = lax.broadcasted_iota(jnp.int32, (1, LANES), 1)
    first_head = lane < ATT_HEAD_DIM
    row = lax.broadcasted_iota(jnp.int32, (2 * blk, blk), 0)
    col = lax.broadcasted_iota(jnp.int32, (2 * blk, blk), 1)
    t_in_blk = jnp.where(row >= blk, row - blk, row)
    causal = t_in_blk >= col
    log2e = float(np.log2(np.e))
    scale = ATT_HEAD_DIM ** -0.5
    scale2 = scale * log2e
    lane_row = lax.broadcasted_iota(jnp.int32, (LANES, LANES), 0)
    spread = [jnp.where(lane_row == j, 1.0, 0.0).astype(bf16) for j in range(n_blk)]

    assert 8 % N_ATT_HEADS == 0 and float(np.log2(scale)).is_integer()
    head = 2 * pair + (row[:, 0:1] >= blk).astype(jnp.int32)
    c_h = sum(jnp.where(head == h, -(2.0 ** (-8.0 * (h + 1) / N_ATT_HEADS)) / scale, 0.0)
              for h in range(N_ATT_HEADS))
    c_t = c_h * t_in_blk[:, 0:1].astype(f32)
    s_row = lax.broadcasted_iota(jnp.int32, (seq, LANES), 0)
    s_lane = lax.broadcasted_iota(jnp.int32, (seq, LANES), 1)
    k_extra = jnp.where(s_lane < 2, 1.0, jnp.where(s_lane == 2, (s_row % blk).astype(f32),
                        jnp.where(s_lane == 3, (s_row // blk * blk).astype(f32), 0.0)))

    def q_extra(qi):
        return jnp.where(lane == 0, c_t, jnp.where(lane == 1, c_h * float(blk * qi),
                         jnp.where((lane == 2) | (lane == 3), -c_h, 0.0))).astype(bf16)

    k_all = jnp.concatenate([k_ref[...], k_extra.astype(bf16)], axis=1)
    v_ones = jnp.concatenate([v_ref[...], jnp.ones((seq, LANES), bf16)], axis=1)
    k_mean = jnp.concatenate(
        [jnp.mean(k_ref[j * blk:(j + 1) * blk, :].astype(f32), axis=0, keepdims=True)
         for j in range(n_blk)] + [jnp.zeros((LANES - n_blk, LANES), f32)], axis=0)
    k_mean_parts = _bf16_parts(k_mean, 3)

    for qi in range(n_blk):
        q_blk = q_ref[qi * blk:(qi + 1) * blk, :]
        zero = jnp.zeros_like(q_blk)
        q2 = jnp.concatenate([jnp.where(first_head, q_blk, zero), jnp.where(first_head, zero, q_blk)], axis=0)
        sel = None
        if qi > MOBA_TOPK:
            g = sum(lax.dot_general(q2, part, _NT, preferred_element_type=f32) for part in k_mean_parts)
            g = jnp.where(lane < qi, g, -jnp.inf)
            ahead = jnp.zeros((2 * blk, LANES), f32)
            for d in range(1, qi):
                ahead = ahead + jnp.where(pltpu.roll(g, d, axis=1) >= g, 1.0, 0.0)
                ahead = ahead + jnp.where(pltpu.roll(g, LANES - d, axis=1) > g, 1.0, 0.0)
            keep = jnp.where(ahead < MOBA_TOPK, 1.0, 0.0).astype(bf16)
            sel = [jnp.dot(keep, spread[j], preferred_element_type=f32) > 0.5 for j in range(qi)]
        q_aug = jnp.concatenate([q2, q_extra(qi)], axis=1)
        logits = []
        for j in range(qi + 1):
            s = lax.dot_general(q_aug, k_all[j * blk:(j + 1) * blk], _NT, preferred_element_type=f32) * scale2
            if j == qi:
                s = jnp.where(causal, s, NEG_BIG)
            elif sel is not None:
                s = jnp.where(jnp.concatenate([sel[j]] * (blk // LANES), axis=1), s, NEG_BIG)
            logits.append(s)
        s_all = jnp.concatenate(logits, axis=1)
        m = jnp.max(s_all, axis=1, keepdims=True)
        p = jnp.exp2(s_all - m)
        o2 = jnp.dot(p.astype(bf16), v_ones[:(qi + 1) * blk], preferred_element_type=f32)
        o2 = o2[:, :LANES] / o2[:, LANES:LANES + 1]
        o = jnp.where(first_head, o2[:blk], o2[blk:])
        o_ref[qi * blk:(qi + 1) * blk, :] = o.astype(o_ref.dtype)


def _moba_kernel(q_ref, k_ref, v_ref, o_ref, *, seq):
    _moba_body(q_ref, k_ref, v_ref, o_ref, pair=pl.program_id(1), seq=seq)


def moba(qkv, batch, seq):
    n = batch * seq
    n_pairs = ATT_WIDTH // LANES
    assert seq % MOBA_BLOCK == 0
    spec = lambda off: pl.BlockSpec((seq, LANES), lambda b, p: (b, off + p))
    return pl.pallas_call(
        functools.partial(_moba_kernel, seq=seq),
        grid=(batch, n_pairs),
        in_specs=[spec(0), spec(n_pairs), spec(2 * n_pairs)],
        out_specs=pl.BlockSpec((seq, LANES), lambda b, p: (b, p)),
        out_shape=jax.ShapeDtypeStruct((n, ATT_WIDTH), jnp.bfloat16),
        compiler_params=_params("parallel", "parallel"),
        name="moba",
    )(qkv, qkv, qkv)


def _log_sigmoid(z):
    return jnp.minimum(z, 0.0) - jnp.log(1.0 + jnp.exp(-jnp.abs(z)))


def _mlstm_body(xm_ref, om_ref, cw_ref, cb_ref, wq_ref, wk_ref, wiq_ref, wik_ref, wix_ref,
                wiqt_ref, wikt_ref, wixt_ref, bi_ref, bit_ref, g_ref, y_ref,
                q_sc, k_sc, gcol_sc, grow_sc, ct_sc, n_sc, m_sc, *, seq):
    L = MLSTM_CHUNK
    H = N_MLSTM_HEADS
    n_chunks = seq // L
    n_seq = xm_ref.shape[0] // seq
    f32, bf16 = jnp.float32, jnp.bfloat16

    t_idx = lax.broadcasted_iota(jnp.int32, (seq, 1), 0)
    is_f_col = lax.broadcasted_iota(jnp.int32, (1, 2 * H), 1) >= H
    is_f_row = lax.broadcasted_iota(jnp.int32, (2 * H, 1), 0) >= H
    for b in range(n_seq):
        xm = xm_ref[b * seq:(b + 1) * seq, :]
        x = xm.astype(f32)
        acc = x * cw_ref[MLSTM_CONV - 1:MLSTM_CONV, :] + cb_ref[...]
        for d in range(1, MLSTM_CONV):
            shifted = jnp.where(t_idx >= d, pltpu.roll(x, d, axis=0), 0.0)
            acc = acc + shifted * cw_ref[MLSTM_CONV - 1 - d:MLSTM_CONV - d, :]
        xc = (acc * jax.nn.sigmoid(acc)).astype(bf16)

        if_col = jnp.dot(xm, wix_ref[...], preferred_element_type=f32) + bi_ref[...]
        if_row = lax.dot_general(wixt_ref[...], xm, _NT, preferred_element_type=f32) + bit_ref[...]
        for h in range(H):
            xch = xc[:, h * LANES:(h + 1) * LANES]
            q = jnp.dot(xch, wq_ref[h], preferred_element_type=f32).astype(bf16)
            k = jnp.dot(xch, wk_ref[h], preferred_element_type=f32).astype(bf16)
            if_col = if_col + jnp.dot(q, wiq_ref[h], preferred_element_type=f32)
            if_col = if_col + jnp.dot(k, wik_ref[h], preferred_element_type=f32)
            if_row = if_row + lax.dot_general(wiqt_ref[h], q, _NT, preferred_element_type=f32)
            if_row = if_row + lax.dot_general(wikt_ref[h], k, _NT, preferred_element_type=f32)
            q_sc[b * H + h] = q
            k_sc[b * H + h] = (k.astype(f32) * (MLSTM_QK_DIM ** -0.5)).astype(bf16)
        gcol_sc[b] = jnp.where(is_f_col, _log_sigmoid(if_col), if_col)
        g_row = jnp.where(is_f_row, _log_sigmoid(if_row), if_row)
        for c in range(n_chunks):
            grow_sc[b * n_chunks + c] = g_row[:, c * L:(c + 1) * L]

    ct_sc[...] = jnp.zeros_like(ct_sc)
    n_sc[...] = jnp.zeros_like(n_sc)
    m_sc[...] = jnp.zeros_like(m_sc)

    r_i = lax.broadcasted_iota(jnp.int32, (L, L), 0)
    c_i = lax.broadcasted_iota(jnp.int32, (L, L), 1)
    causal = r_i >= c_i
    tri_lo = jnp.where(causal, 1.0, 0.0).astype(f32)
    tri_up = jnp.where(c_i >= r_i, 1.0, 0.0).astype(f32)

    def chunk(c, carry):
      r0 = pl.multiple_of(c * L, L)
      for b in range(n_seq):
        rows = pl.ds(pl.multiple_of(b * seq + c * L, L), L)
        gc = gcol_sc[b, pl.ds(r0, L), :]
        gr = grow_sc[b * n_chunks + c]
        b_cols = jnp.dot(tri_lo, gc, precision=lax.Precision.HIGHEST, preferred_element_type=f32)
        b_rows = jnp.dot(gr, tri_up, precision=lax.Precision.HIGHEST, preferred_element_type=f32)
        for h in range(H):
            st = b * H + h
            hs = slice(h * LANES, (h + 1) * LANES)
            b_col = b_cols[:, H + h:H + h + 1]
            i_col = gc[:, h:h + 1]
            b_row = b_rows[H + h:H + h + 1, :]
            i_row = gr[h:h + 1, :]
            b_last = b_col[L - 1:L, :]
            m_prev = m_sc[st][:, 0:1]
            q = q_sc[st, pl.ds(r0, L), :]
            k = k_sc[st, pl.ds(r0, L), :]
            v = xm_ref[rows, hs]

            log_d = jnp.where(causal, b_col + (i_row - b_row), NEG_BIG)
            inter = b_col + m_prev
            m_t = jnp.maximum(inter, jnp.max(log_d, axis=1, keepdims=True))
            w_inter = jnp.exp(inter - m_t)
            s = lax.dot_general(q, k, _NT, preferred_element_type=f32) * jnp.exp(log_d - m_t)
            num = jnp.dot(s.astype(bf16), v, preferred_element_type=f32)
            num = num + w_inter * jnp.dot(q, ct_sc[st].astype(bf16), preferred_element_type=f32)
            den = jnp.sum(s, axis=1, keepdims=True)
            den = den + w_inter * jnp.sum(q.astype(f32) * n_sc[st], axis=1, keepdims=True)
            hv = num / jnp.maximum(jnp.abs(den), jnp.exp(-m_t))
            hv = hv * lax.rsqrt(jnp.mean(hv * hv, axis=1, keepdims=True) + RMS_EPS) * g_ref[:, hs]
            o_gate = jax.nn.sigmoid(om_ref[rows, hs].astype(f32))
            y_ref[rows, hs] = (o_gate * hv).astype(y_ref.dtype)

            g_col = b_last - b_col + i_col
            m_new = jnp.maximum(b_last + m_prev, jnp.max(g_col, axis=0, keepdims=True))
            wg = jnp.exp(g_col - m_new)
            decay = jnp.exp(b_last + m_prev - m_new)
            wv = (wg * v.astype(f32)).astype(bf16)
            ct_sc[st] = decay * ct_sc[st] + lax.dot_general(k, wv, _TN, preferred_element_type=f32)
            n_sc[st] = decay * n_sc[st] + jnp.sum(wg * k.astype(f32), axis=0, keepdims=True)
            m_sc[st] = jnp.broadcast_to(m_new, (1, LANES))
      return carry

    lax.fori_loop(0, n_chunks, chunk, 0)


def _mlstm_kernel(*refs, seq):
    _mlstm_body(*refs, seq=seq)


def mlstm(xo, conv_w, conv_b, w_qk, w_if, b_if, g, batch, seq):
    n = batch * seq
    H, dv, dk = N_MLSTM_HEADS, MLSTM_V_DIM, MLSTM_QK_DIM
    assert seq % MLSTM_CHUNK == 0 and dv == LANES
    bf16 = jnp.bfloat16
    pad = ((0, 0), (0, 0), (0, LANES - dk))
    wq = jnp.pad(w_qk[:, :, :dk], pad).astype(bf16)
    wk = jnp.pad(w_qk[:, :, dk:], pad).astype(bf16)
    rpad = ((0, 0), (0, LANES - dk), (0, 0))
    wiq = jnp.pad(w_if[:H * dk].reshape(H, dk, 2 * H), rpad).astype(bf16)
    wik = jnp.pad(w_if[H * dk:2 * H * dk].reshape(H, dk, 2 * H), rpad).astype(bf16)
    wix = w_if[2 * H * dk:].astype(bf16)
    wiqt, wikt, wixt = wiq.transpose(0, 2, 1), wik.transpose(0, 2, 1), wix.T
    full = lambda a: pl.BlockSpec(a.shape, lambda b: (0,) * a.ndim)
    consts = [conv_w, conv_b.reshape(1, -1), wq, wk, wiq, wik, wix, wiqt, wikt, wixt,
              b_if.reshape(1, -1), b_if.reshape(-1, 1), g.reshape(1, -1)]
    ns = MLSTM_SEQS_PER_STEP if batch % MLSTM_SEQS_PER_STEP == 0 else 1
    return pl.pallas_call(
        functools.partial(_mlstm_kernel, seq=seq),
        grid=(batch // ns,),
        in_specs=[pl.BlockSpec((ns * seq, MLSTM_WIDTH), lambda b: (b, 0)),
                  pl.BlockSpec((ns * seq, MLSTM_WIDTH), lambda b: (b, 1))] + [full(a) for a in consts],
        out_specs=pl.BlockSpec((ns * seq, MLSTM_WIDTH), lambda b: (b, 0)),
        out_shape=jax.ShapeDtypeStruct((n, MLSTM_WIDTH), bf16),
        scratch_shapes=[pltpu.VMEM((ns * H, seq, LANES), bf16), pltpu.VMEM((ns * H, seq, LANES), bf16),
                        pltpu.VMEM((ns, seq, 2 * H), jnp.float32),
                        pltpu.VMEM((ns * (seq // MLSTM_CHUNK), 2 * H, MLSTM_CHUNK), jnp.float32),
                        pltpu.VMEM((ns * H, LANES, LANES), jnp.float32),
                        pltpu.VMEM((ns * H, 1, LANES), jnp.float32),
                        pltpu.VMEM((ns * H, 1, LANES), jnp.float32)],
        compiler_params=_params("parallel"),
        name="mlstm",
    )(xo, xo, *consts)


def _store_rows_as_tiles(dst_ref, val):
    rows = val.shape[0]
    for c in range(ROW_CHUNKS):
        dst_ref[pl.ds(c, rows, stride=ROW_CHUNKS), :] = val[:, c * LANES:(c + 1) * LANES]


def _load_tiles_as_rows(src_ref, rows, lead=()):
    return jnp.concatenate(
        [src_ref[lead + (pl.ds(c, rows, stride=ROW_CHUNKS), slice(None))] for c in range(ROW_CHUNKS)],
        axis=1)


def _merge_route_body(x_ref, ya_ref, ym_ref, ga_ref, gm_ref, wb0_ref, wb1_ref, wo_ref, g_ref,
                      wr_ref, br_ref, x1_ref, h2_ref, gate_ref, lpos_ref, cnt_ref):
    f32, bf16 = jnp.float32, jnp.bfloat16
    tm = x_ref.shape[0]

    u = jax.nn.sigmoid(ga_ref[...].astype(f32)) * jnp.dot(ya_ref[...], wb0_ref[...], preferred_element_type=f32)
    u = u + jax.nn.sigmoid(gm_ref[...].astype(f32)) * jnp.dot(ym_ref[...], wb1_ref[...], preferred_element_type=f32)
    x1 = x_ref[...] + jnp.dot(u.astype(bf16), wo_ref[...], preferred_element_type=f32)
    x1_ref[...] = x1
    h2 = _rms(x1, g_ref[...])
    _store_rows_as_tiles(h2_ref, h2)
    (h_hi, h_lo), (w_hi, w_lo) = _bf16_parts(h2, 2), _bf16_parts(wr_ref[...], 2)
    logits = (jnp.dot(h_hi, w_hi, preferred_element_type=f32) + jnp.dot(h_lo, w_hi, preferred_element_type=f32)
              + jnp.dot(h_hi, w_lo, preferred_element_type=f32)) + br_ref[...]
    logits = jnp.transpose(logits)[:N_EXPERTS]
    e_id = lax.broadcasted_iota(jnp.int32, logits.shape, 0).astype(f32)
    chosen = jnp.zeros(logits.shape, f32)
    vals, ids = [], []
    for _ in range(TOP_K):
        top = jnp.max(logits, axis=0, keepdims=True)
        first = jnp.min(jnp.where(logits == top, e_id, float(N_EXPERTS)), axis=0, keepdims=True)
        hit = e_id == first
        chosen = jnp.where(hit, 1.0, chosen)
        logits = jnp.where(hit, -jnp.inf, logits)
        vals.append(top)
        ids.append(first)
    ex = [jnp.exp(v - vals[0]) for v in vals]
    total = ex[0] + ex[1] + ex[2] + ex[3]

    chosen_b = chosen.astype(bf16)
    t_r = lax.broadcasted_iota(jnp.int32, (tm, tm), 0)
    t_c = lax.broadcasted_iota(jnp.int32, (tm, tm), 1)
    earlier = jnp.where(t_r < t_c, 1.0, 0.0).astype(bf16)
    before = jnp.dot(chosen_b, earlier, preferred_element_type=f32)
    e_r = lax.broadcasted_iota(jnp.int32, (N_EXPERTS, N_EXPERTS), 0)
    e_c = lax.broadcasted_iota(jnp.int32, (N_EXPERTS, N_EXPERTS), 1)
    lower = jnp.where(e_c < e_r, 1.0, 0.0).astype(bf16)
    seg_off = jnp.sum(jnp.dot(lower, chosen_b, preferred_element_type=f32), axis=1, keepdims=True)
    pos = before + seg_off
    for kk in range(TOP_K):
        gate_ref[0, :, kk * tm:(kk + 1) * tm] = ex[kk] / total
        lpos_ref[0, :, kk * tm:(kk + 1) * tm] = jnp.sum(jnp.where(e_id == ids[kk], pos, 0.0), axis=0,
                                                        keepdims=True).astype(jnp.int32)
    counts = jnp.sum(chosen, axis=1, keepdims=True)
    cnt_ref[0] = jnp.broadcast_to(counts, (N_EXPERTS, LANES)).astype(jnp.int32)


def merge_route(x2, ya, ym, gates, wb0, wb1, wo, g, w_router, b_router, *, tm):
    n, d = x2.shape
    e = w_router.shape[1]
    assert n % tm == 0 and d == ROW_CHUNKS * LANES and e == N_EXPERTS
    full = lambda a: pl.BlockSpec(a.shape, lambda i: (0,) * a.ndim)
    lane_pad = ((0, 0), (0, LANES - e))
    consts = [wb0, wb1, wo, g.reshape(1, d), jnp.pad(w_router, lane_pad),
              jnp.pad(b_router.reshape(1, e), lane_pad)]
    tok = lambda dt: jax.ShapeDtypeStruct((n // tm, 1, TOP_K * tm), dt)
    return pl.pallas_call(
        _merge_route_body,
        grid=(n // tm,),
        in_specs=[pl.BlockSpec((tm, d), lambda i: (i, 0)),
                  pl.BlockSpec((tm, ATT_WIDTH), lambda i: (i, 0)),
                  pl.BlockSpec((tm, MLSTM_WIDTH), lambda i: (i, 0)),
                  pl.BlockSpec((tm, d), lambda i: (i, 0)),
                  pl.BlockSpec((tm, d), lambda i: (i, 1))] + [full(a) for a in consts],
        out_specs=[pl.BlockSpec((tm, d), lambda i: (i, 0)),
                   pl.BlockSpec((tm * ROW_CHUNKS, LANES), lambda i: (i, 0)),
                   pl.BlockSpec((1, 1, TOP_K * tm), lambda i: (i, 0, 0)),
                   pl.BlockSpec((1, 1, TOP_K * tm), lambda i: (i, 0, 0)),
                   pl.BlockSpec((1, e, LANES), lambda i: (i, 0, 0))],
        out_shape=[jax.ShapeDtypeStruct((n, d), jnp.float32),
                   jax.ShapeDtypeStruct((n * ROW_CHUNKS, LANES), jnp.float32),
                   tok(jnp.float32), tok(jnp.int32),
                   jax.ShapeDtypeStruct((n // tm, e, LANES), jnp.int32)],
        compiler_params=_params("parallel"),
        name="merge_route",
    )(x2, ya, ym, gates, gates, *consts)


def _segment_pieces(cnt, max_rows, fn):
    top = 1 << (max_rows.bit_length() - 1)
    sizes = [top >> i for i in range(top.bit_length())]
    rare = [s for s in sizes if s * 8 > top]

    def pieces(group, done):
        for size in group:
            bit = cnt & size

            def piece(done=done, size=size):
                fn(done, size)
            pl.when(bit != 0)(piece)
            done = done + bit

    pl.when(cnt >= rare[-1])(lambda: pieces(rare, 0))
    pieces([s for s in sizes if s not in rare], cnt & -rare[-1])


def _for_tile_segments(meta, tile_rows, make_copy, action):
    cnt_ref, off_ref, dst_ref = meta

    def per_expert(e, carry):
        off, dst = off_ref[0, 0, e], dst_ref[0, 0, e]
        _segment_pieces(cnt_ref[0, 0, e], tile_rows,
                        lambda done, size: getattr(make_copy(off + done, dst + done, size), action)())
        return carry
    lax.fori_loop(0, N_EXPERTS, per_expert, 0)


def _dispatch_body(lpos_ref, cnt_ref, off_ref, dst_ref, pad_len_ref, pad_row_ref, x_ref, out_ref,
                   cbuf, zbuf, sem, zsem, *, tc, rb):
    step = pl.program_id(0)
    last = pl.num_programs(0) - 1
    slot = step % 2

    def compact(t, carry):
        row = x_ref[t]
        for kk in range(TOP_K):
            cbuf[slot, lpos_ref[0, 0, kk * tc + t]] = row
        return carry
    lax.fori_loop(0, tc, compact, 0, unroll=8)

    def seg_copy(sl):
        return lambda local, glob, size: pltpu.make_async_copy(
            cbuf.at[sl, pl.ds(local, size)], out_ref.at[pl.ds(glob, size)], sem.at[sl])

    def wait_tile(sl):
        pltpu.make_async_copy(cbuf.at[sl], out_ref.at[pl.ds(0, TOP_K * tc)], sem.at[sl]).wait()

    _for_tile_segments((cnt_ref, off_ref, dst_ref), tc, seg_copy(slot), "start")

    @pl.when(step > 0)
    def _():
        wait_tile(1 - slot)

    @pl.when(step == last)
    def _():
        wait_tile(slot)
        zbuf[...] = jnp.zeros_like(zbuf)

        def pads(action):
            def per_pad(j, carry):
                row = pad_row_ref[0, j]
                _segment_pieces(pad_len_ref[0, j], rb, lambda done, size: getattr(
                    pltpu.make_async_copy(zbuf.at[pl.ds(0, size)], out_ref.at[pl.ds(row + done, size)], zsem),
                    action)())
                return carry
            lax.fori_loop(0, pad_len_ref.shape[1], per_pad, 0)
        pads("start")
        pads("wait")


def dispatch(lpos, seg_cnt, seg_off, seg_dst, pad_len, pad_row, h2_tiles, n_rows, *, tc, rb):
    n = h2_tiles.shape[0]
    steps = n // tc
    assert n % tc == 0
    smem = pltpu.SMEM
    seg = pl.BlockSpec((1, 1, N_EXPERTS), lambda i: (i, 0, 0), memory_space=smem)
    whole = lambda a: pl.BlockSpec(a.shape, lambda i: (0,) * a.ndim, memory_space=smem)
    return pl.pallas_call(
        functools.partial(_dispatch_body, tc=tc, rb=rb),
        grid=(steps,),
        in_specs=[pl.BlockSpec((1, 1, TOP_K * tc), lambda i: (i, 0, 0), memory_space=smem),
                  seg, seg, seg, whole(pad_len), whole(pad_row),
                  pl.BlockSpec((tc, ROW_CHUNKS, LANES), lambda i: (i, 0, 0))],
        out_specs=pl.BlockSpec(memory_space=pl.ANY),
        out_shape=jax.ShapeDtypeStruct((n_rows, ROW_CHUNKS, LANES), h2_tiles.dtype),
        scratch_shapes=[pltpu.VMEM((2, TOP_K * tc, ROW_CHUNKS, LANES), h2_tiles.dtype),
                        pltpu.VMEM((rb, ROW_CHUNKS, LANES), h2_tiles.dtype),
                        pltpu.SemaphoreType.DMA((2,)), pltpu.SemaphoreType.DMA(())],
        compiler_params=_params("arbitrary"),
        name="dispatch",
    )(lpos, seg_cnt, seg_off, seg_dst, pad_len, pad_row, h2_tiles)


def _experts_body(blk_e_ref, nxt_e_ref, valid_ref, used_ref, x_ref, wgu_hbm, bg_ref, bu_ref, wd_hbm, bd_ref,
                  perm_ref, y_ref, wgu_st, wd_st, wg_sc, wu_sc, wd_sc, sem, *, halves):
    f32, bf16 = jnp.float32, jnp.bfloat16
    step = pl.program_id(0)
    live = step < used_ref[0]
    rows = x_ref.shape[0] // ROW_CHUNKS
    expert = blk_e_ref[step]

    def fetch(e, action):
        getattr(pltpu.make_async_copy(wgu_hbm.at[e], wgu_st, sem.at[0]), action)()
        getattr(pltpu.make_async_copy(wd_hbm.at[e], wd_st, sem.at[1]), action)()

    @pl.when(live & ((step == 0) | (expert != blk_e_ref[jnp.maximum(step - 1, 0)])))
    def _():
        @pl.when(step == 0)
        def _():
            fetch(expert, "start")
        fetch(expert, "wait")
        tile = perm_ref.shape[0]
        half = tile // 2
        for t in range(wgu_st.shape[1] // tile):
            w = wgu_st[:, t * tile:(t + 1) * tile].astype(bf16)
            o = jnp.dot(w, perm_ref[...], preferred_element_type=f32)
            wg_sc[:, t * half:(t + 1) * half] = o[:, :half].astype(bf16)
            wu_sc[:, t * half:(t + 1) * half] = o[:, half:].astype(bf16)
        wd_sc[...] = wd_st[...].astype(bf16)
        nxt = nxt_e_ref[step]

        @pl.when(nxt >= 0)
        def _():
            fetch(nxt, "start")

    part = rows // halves

    def ffn(groups):
        for h in range(groups):
            xb = jnp.concatenate(
                [x_ref[pl.ds(h * part * ROW_CHUNKS + c, part, stride=ROW_CHUNKS), :]
                 for c in range(ROW_CHUNKS)], axis=1).astype(bf16)
            gate = jnp.dot(xb, wg_sc[...], preferred_element_type=f32) + bg_ref[0]
            up = jnp.dot(xb, wu_sc[...], preferred_element_type=f32) + bu_ref[0]
            gate = jnp.minimum(gate, SWIGLU_LIMIT)
            up = jnp.clip(up, -SWIGLU_LIMIT, SWIGLU_LIMIT)
            act = (up + 1.0) * (gate * jax.nn.sigmoid(gate * SWIGLU_ALPHA))
            y = jnp.dot(act.astype(bf16), wd_sc[...], preferred_element_type=f32) + bd_ref[0]
            for c in range(ROW_CHUNKS):
                y_ref[pl.ds(h * part * ROW_CHUNKS + c, part, stride=ROW_CHUNKS), :] = \
                    y[:, c * LANES:(c + 1) * LANES]
        if groups < halves:
            y_ref[groups * part * ROW_CHUNKS:, :] = jnp.zeros(((halves - groups) * part * ROW_CHUNKS, LANES), f32)

    n_valid = valid_ref[step]
    for groups in range(1, halves + 1):
        wanted = live & (n_valid > (groups - 1) * part)
        if groups < halves:
            wanted = wanted & (n_valid <= groups * part)
        pl.when(wanted)(functools.partial(ffn, groups))

    @pl.when(jnp.logical_not(live))
    def _():
        y_ref[...] = jnp.zeros_like(y_ref)


def experts(blk_expert, nxt_expert, blk_valid, n_used, xs_tiles, w_gu, b_g, b_u, w_d, b_d, *, halves=2):
    n_blocks = blk_expert.shape[0]
    rb = MOE_ROW_BLOCK
    d, ff = w_d.shape[2], w_d.shape[1]
    tile = 2 * LANES
    perm = np.zeros((tile, tile), np.float32)
    perm[2 * np.arange(LANES), np.arange(LANES)] = 1.0
    perm[2 * np.arange(LANES) + 1, LANES + np.arange(LANES)] = 1.0
    e_map3 = lambda i, be, nx, bv, nu: (be[i], 0, 0)
    rows_map = lambda i, be, nx, bv, nu: (i, 0)
    grid_spec = pltpu.PrefetchScalarGridSpec(
        num_scalar_prefetch=4,
        grid=(n_blocks,),
        in_specs=[pl.BlockSpec((rb * ROW_CHUNKS, LANES), rows_map),
                  pl.BlockSpec(memory_space=pl.ANY),
                  pl.BlockSpec((1, 1, ff), e_map3), pl.BlockSpec((1, 1, ff), e_map3),
                  pl.BlockSpec(memory_space=pl.ANY), pl.BlockSpec((1, 1, d), e_map3),
                  pl.BlockSpec((tile, tile), lambda i, be, nx, bv, nu: (0, 0))],
        out_specs=pl.BlockSpec((rb * ROW_CHUNKS, LANES), rows_map),
        scratch_shapes=[pltpu.VMEM((d, 2 * ff), w_gu.dtype), pltpu.VMEM((ff, d), w_d.dtype),
                        pltpu.VMEM((d, ff), jnp.bfloat16), pltpu.VMEM((d, ff), jnp.bfloat16),
                        pltpu.VMEM((ff, d), jnp.bfloat16), pltpu.SemaphoreType.DMA((2,))],
    )
    return pl.pallas_call(
        functools.partial(_experts_body, halves=halves),
        grid_spec=grid_spec,
        out_shape=jax.ShapeDtypeStruct(xs_tiles.shape, jnp.float32),
        compiler_params=_params("arbitrary"),
        name="experts",
    )(blk_expert, nxt_expert, blk_valid, n_used, xs_tiles, w_gu, b_g, b_u, w_d, b_d,
      jnp.asarray(perm, jnp.bfloat16))


def _combine_ple_body(lpos_ref, gw_ref, cnt_ref, off_ref, dst_ref, ncnt_ref, noff_ref, ndst_ref,
                      x1_ref, ys_ref, p_ref, gp_ref, wpg_ref, wpp_ref, *rest):
    cbuf, msum_a, msum_b, sem = rest[-4:]
    rest = rest[:-4]
    gf_ref, o_ref = rest if len(rest) == 2 else (None, rest[0])
    f32, bf16 = jnp.float32, jnp.bfloat16
    tm = x1_ref.shape[0]
    step = pl.program_id(0)
    n_tiles = pl.num_programs(0) - 1
    slot = step % 2

    def seg_copy(sl):
        return lambda local, glob, size: pltpu.make_async_copy(
            ys_ref.at[pl.ds(glob, size)], cbuf.at[sl, pl.ds(local, size)], sem.at[sl])

    cur, nxt = (cnt_ref, off_ref, dst_ref), (ncnt_ref, noff_ref, ndst_ref)

    @pl.when(step == 0)
    def _():
        _for_tile_segments(cur, tm, seg_copy(0), "start")
        msum_b[...] = jnp.zeros_like(msum_b)

    @pl.when(step + 1 < n_tiles)
    def _():
        _for_tile_segments(nxt, tm, seg_copy(1 - slot), "start")

    @pl.when(step < n_tiles)
    def _():
        pltpu.make_async_copy(ys_ref.at[pl.ds(0, TOP_K * tm)], cbuf.at[slot], sem.at[slot]).wait()

    def gather_and_dense(msum_cur, msum_prev):
        for t in range(tm):
            acc = gw_ref[0, 0, t] * cbuf[slot, lpos_ref[0, 0, t]]
            for kk in range(1, TOP_K):
                acc = acc + gw_ref[0, 0, kk * tm + t] * cbuf[slot, lpos_ref[0, 0, kk * tm + t]]
            msum_cur[t] = acc
        moe = jnp.concatenate([msum_prev[:, c, :] for c in range(ROW_CHUNKS)], axis=1)
        x2 = x1_ref[...] + moe
        r = _rms(x2, gp_ref[...]).astype(bf16)
        gate = jax.nn.sigmoid(jnp.dot(r, wpg_ref[...], preferred_element_type=f32))
        emb = jnp.dot(p_ref[...].astype(bf16), wpp_ref[...], preferred_element_type=f32)
        x3 = x2 + gate * emb
        o_ref[...] = x3 if gf_ref is None else _rms(x3, gf_ref[...])

    pl.when(slot == 0)(functools.partial(gather_and_dense, msum_a, msum_b))
    pl.when(slot == 1)(functools.partial(gather_and_dense, msum_b, msum_a))


def combine_ple(lpos, gate_w, seg_cnt, seg_off, seg_dst, x1, ys_tiles, p2, g_ple, w_pg, w_pp, g_final,
                *, tm):
    n, d = x1.shape
    tiles = n // tm
    assert n % tm == 0 and tiles >= 2
    smem = pltpu.SMEM
    full = lambda a: pl.BlockSpec(a.shape, lambda i: (0,) * a.ndim)
    gathered = lambda i, shift=0: (jnp.minimum(i + shift, tiles - 1), 0, 0)
    finished = lambda i: (jnp.maximum(i - 1, 0), 0)
    seg = lambda shift: pl.BlockSpec((1, 1, N_EXPERTS), functools.partial(gathered, shift=shift),
                                     memory_space=smem)
    tok = pl.BlockSpec((1, 1, TOP_K * tm), gathered, memory_space=smem)
    consts = [g_ple.reshape(1, d), w_pg, w_pp] + ([] if g_final is None else [g_final.reshape(1, d)])
    return pl.pallas_call(
        _combine_ple_body,
        grid=(tiles + 1,),
        in_specs=[tok, tok, seg(0), seg(0), seg(0), seg(1), seg(1), seg(1),
                  pl.BlockSpec((tm, d), finished),
                  pl.BlockSpec(memory_space=pl.ANY),
                  pl.BlockSpec((tm, p2.shape[1]), finished)] + [full(a) for a in consts],
        out_specs=pl.BlockSpec((tm, d), finished),
        out_shape=jax.ShapeDtypeStruct((n, d), jnp.float32),
        scratch_shapes=[pltpu.VMEM((2, TOP_K * tm, ROW_CHUNKS, LANES), ys_tiles.dtype),
                        pltpu.VMEM((tm, ROW_CHUNKS, LANES), jnp.float32),
                        pltpu.VMEM((tm, ROW_CHUNKS, LANES), jnp.float32),
                        pltpu.SemaphoreType.DMA((2,))],
        compiler_params=_params("arbitrary"),
        name="combine_ple",
    )(lpos, gate_w, seg_cnt, seg_off, seg_dst, seg_cnt, seg_off, seg_dst, x1, ys_tiles, p2, *consts)


def _moe(h2_tiles, lpos, tile_cnt, w_gu, b_gu, w_d, b_d, *, tm):
    n = h2_tiles.shape[0] // ROW_CHUNKS
    rb = MOE_ROW_BLOCK
    i32 = jnp.int32
    n_blocks = -(-n * TOP_K // rb) + N_EXPERTS
    n_rows = n_blocks * rb
    counts = jnp.sum(tile_cnt, axis=0)
    padded = (counts + rb - 1) // rb * rb
    pend = jnp.cumsum(padded)
    pstart = pend - padded
    blk_start = jnp.arange(n_blocks, dtype=i32) * rb
    blk_expert = jnp.minimum(jnp.sum((pend[None, :] <= blk_start[:, None]).astype(i32), axis=1),
                             N_EXPERTS - 1)
    n_used = pend[-1:] // rb
    seg_cnt = tile_cnt[:, None, :]
    seg_off = (jnp.cumsum(tile_cnt, axis=1) - tile_cnt)[:, None, :]
    seg_dst = (pstart[None, :] + jnp.cumsum(tile_cnt, axis=0) - tile_cnt)[:, None, :]
    tail = n_used + jnp.arange(N_EXPERTS, dtype=i32)
    pad_len = jnp.concatenate([padded - counts, jnp.where(tail < n_blocks, rb, 0)])[None, :]
    pad_row = jnp.concatenate([pstart + counts, tail * rb])[None, :]
    b_g, b_u = b_gu[:, None, 0::2], b_gu[:, None, 1::2]
    xs = dispatch(lpos, seg_cnt, seg_off, seg_dst, pad_len.astype(i32), pad_row.astype(i32),
                  h2_tiles.reshape(n, ROW_CHUNKS, LANES), n_rows, tc=tm, rb=rb)
    e_ids = jnp.arange(N_EXPERTS, dtype=i32)
    later = (padded > 0)[None, :] & (e_ids[None, :] > e_ids[:, None])
    nxt_tab = jnp.min(jnp.where(later, e_ids[None, :], N_EXPERTS), axis=1)
    nxt_tab = jnp.where(nxt_tab == N_EXPERTS, -1, nxt_tab)
    nxt_expert = jnp.sum(jnp.where(blk_expert[:, None] == e_ids[None, :], nxt_tab[None, :], 0), axis=1)
    hit = blk_expert[:, None] == e_ids[None, :]
    blk_valid = jnp.clip(jnp.sum(jnp.where(hit, (pstart + counts)[None, :], 0), axis=1) - blk_start, 0, rb)
    ys = experts(blk_expert, nxt_expert.astype(i32), blk_valid.astype(i32), n_used.astype(i32),
                 xs.reshape(n_rows * ROW_CHUNKS, LANES), w_gu, b_g, b_u, w_d, b_d[:, None, :])
    return (seg_cnt, seg_off, seg_dst), ys.reshape(n_rows, ROW_CHUNKS, LANES)


def kernel(x, p, norm_mix_g, w_in, conv_w, conv_b, w_qk_m, w_if, b_if, mnorm_g, w_branch, w_out, norm_ffn_g, w_router, b_router, w_gate_up, b_gate_up, w_down, b_down, norm_ple_g, w_ple_gate, w_ple_proj, final_norm_g):
    B, S, D = x.shape
    depth = w_in.shape[0]
    bf16 = jnp.bfloat16
    x2 = x.reshape(B * S, D)
    for i in range(depth):
        qkv, xo, gates = in_proj(x2, norm_mix_g[i], w_in[i].astype(bf16))
        y_a = moba(qkv, B, S)
        y_m = mlstm(xo, conv_w[i], conv_b[i], w_qk_m[i], w_if[i], b_if[i], mnorm_g[i], B, S)
        x1, h2_tiles, gate_w, lpos, tile_cnt = merge_route(
            x2, y_a, y_m, gates, w_branch[i, 0].astype(bf16), w_branch[i, 1].astype(bf16),
            w_out[i].astype(bf16), norm_ffn_g[i], w_router[i], b_router[i], tm=MOE_TOKEN_TILE)
        segs, ys = _moe(h2_tiles, lpos, tile_cnt[:, :, 0], w_gate_up[i], b_gate_up[i],
                        w_down[i], b_down[i], tm=MOE_TOKEN_TILE)
        x2 = combine_ple(lpos, gate_w, *segs, x1, ys, p[i].reshape(B * S, -1), norm_ple_g[i],
                         w_ple_gate[i].astype(bf16), w_ple_proj[i].astype(bf16),
                         final_norm_g if i == depth - 1 else None, tm=MOE_TOKEN_TILE)
    return x2.reshape(B, S, D)
```

```python
import functools

import jax
import jax.numpy as jnp
import numpy as np
from jax import lax
from jax.experimental import pallas as pl
from jax.experimental.pallas import tpu as pltpu

RMS_EPS = 1e-6
LANES = 128
ROW_CHUNKS = 8
VMEM_LIMIT = 56 * 1024 * 1024

N_ATT_HEADS = 8
ATT_HEAD_DIM = 64
ATT_WIDTH = N_ATT_HEADS * ATT_HEAD_DIM
MOBA_BLOCK = 256
MOBA_TOPK = 3

N_MLSTM_HEADS = 4
MLSTM_WIDTH = 512
MLSTM_V_DIM = 128
MLSTM_QK_DIM = 64
MLSTM_CONV = 4
MLSTM_CHUNK = 256

N_EXPERTS = 32
TOP_K = 4
SWIGLU_ALPHA = 1.702
SWIGLU_LIMIT = 7.0
MOE_ROW_BLOCK = 512
MOE_TOKEN_TILE = 512

NEG_BIG = -1e30

_NT = (((1,), (1,)), ((), ()))
_TN = (((0,), (0,)), ((), ()))


def _params(*sem):
    return pltpu.CompilerParams(dimension_semantics=sem, vmem_limit_bytes=VMEM_LIMIT)


def _rms(x, g):
    return x * lax.rsqrt(jnp.mean(x * x, axis=-1, keepdims=True) + RMS_EPS) * g


def _bf16_parts(x, n):
    parts = []
    for _ in range(n):
        p = x.astype(jnp.bfloat16)
        parts.append(p)
        x = x - p.astype(jnp.float32)
    return parts


def _in_proj_body(x_ref, g_ref, w_ref, qkv_ref, xo_ref, gate_ref, *, col_chunk):
    h = _rms(x_ref[...], g_ref[...]).astype(jnp.bfloat16)
    col = 0
    for out_ref in (qkv_ref, xo_ref, gate_ref):
        for c in range(0, out_ref.shape[1], col_chunk):
            out_ref[:, c:c + col_chunk] = jnp.dot(
                h, w_ref[:, col + c:col + c + col_chunk],
                preferred_element_type=jnp.float32).astype(out_ref.dtype)
        col += out_ref.shape[1]


def in_proj(x2, g, w_bf16, *, tm=512, col_chunk=512):
    n, d = x2.shape
    widths = (3 * ATT_WIDTH, 2 * MLSTM_WIDTH, 2 * d)
    assert sum(widths) == w_bf16.shape[1] and n % tm == 0
    return pl.pallas_call(
        functools.partial(_in_proj_body, col_chunk=col_chunk),
        grid=(n // tm,),
        in_specs=[pl.BlockSpec((tm, d), lambda i: (i, 0)),
                  pl.BlockSpec((1, d), lambda i: (0, 0)),
                  pl.BlockSpec(w_bf16.shape, lambda i: (0, 0))],
        out_specs=[pl.BlockSpec((tm, w), lambda i: (i, 0)) for w in widths],
        out_shape=[jax.ShapeDtypeStruct((n, w), jnp.bfloat16) for w in widths],
        compiler_params=_params("parallel"),
        name="in_proj",
    )(x2, g.reshape(1, d), w_bf16)


def _moba_body(q_ref, k_ref, v_ref, o_ref, *, pair, seq):
    blk = MOBA_BLOCK
    n_blk = seq // blk
    f32, bf16 = jnp.float32, jnp.bfloat16
    lane = lax.broadcasted_iota(jnp.int32, (1, LANES), 1)
    first_head = lane < ATT_HEAD_DIM
    row = lax.broadcasted_iota(jnp.int32, (2 * blk, blk), 0)
    col = lax.broadcasted_iota(jnp.int32, (2 * blk, blk), 1)
    t_in_blk = jnp.where(row >= blk, row - blk, row)
    causal = t_in_blk >= col
    log2e = float(np.log2(np.e))
    scale = ATT_HEAD_DIM ** -0.5
    scale2 = scale * log2e
    lane_row = lax.broadcasted_iota(jnp.int32, (LANES, LANES), 0)
    spread = [jnp.where(lane_row == j, 1.0, 0.0).astype(bf16) for j in range(n_blk)]

    assert 8 % N_ATT_HEADS == 0 and float(np.log2(scale)).is_integer()
    head = 2 * pair + (row[:, 0:1] >= blk).astype(jnp.int32)
    c_h = sum(jnp.where(head == h, -(2.0 ** (-8.0 * (h + 1) / N_ATT_HEADS)) / scale, 0.0)
              for h in range(N_ATT_HEADS))
    c_t = c_h * t_in_blk[:, 0:1].astype(f32)
    s_row = lax.broadcasted_iota(jnp.int32, (seq, LANES), 0)
    s_lane = lax.broadcasted_iota(jnp.int32, (seq, LANES), 1)
    k_extra = jnp.where(s_lane < 2, 1.0, jnp.where(s_lane == 2, (s_row % blk).astype(f32),
                        jnp.where(s_lane == 3, (s_row // blk * blk).astype(f32), 0.0)))

    def q_extra(qi):
        return jnp.where(lane == 0, c_t, jnp.where(lane == 1, c_h * float(blk * qi),
                         jnp.where((lane == 2) | (lane == 3), -c_h, 0.0))).astype(bf16)

    k_all = jnp.concatenate([k_ref[...], k_extra.astype(bf16)], axis=1)
    v_ones = jnp.concatenate([v_ref[...], jnp.ones((seq, LANES), bf16)], axis=1)
    k_mean = jnp.concatenate(
        [jnp.mean(k_ref[j * blk:(j + 1) * blk, :].astype(f32), axis=0, keepdims=True)
         for j in range(n_blk)] + [jnp.zeros((LANES - n_blk, LANES), f32)], axis=0)
    k_mean_parts = _bf16_parts(k_mean, 3)

    for qi in range(n_blk):
        q_blk = q_ref[qi * blk:(qi + 1) * blk, :]
        zero = jnp.zeros_like(q_blk)
        q2 = jnp.concatenate([jnp.where(first_head, q_blk, zero), jnp.where(first_head, zero, q_blk)], axis=0)
        sel = None
        if qi > MOBA_TOPK:
            g = sum(lax.dot_general(q2, part, _NT, preferred_element_type=f32) for part in k_mean_parts)
            g = jnp.where(lane < qi, g, -jnp.inf)
            ahead = jnp.zeros((2 * blk, LANES), f32)
            for d in range(1, qi):
                ahead = ahead + jnp.where(pltpu.roll(g, d, axis=1) >= g, 1.0, 0.0)
                ahead = ahead + jnp.where(pltpu.roll(g, LANES - d, axis=1) > g, 1.0, 0.0)
            keep = jnp.where(ahead < MOBA_TOPK, 1.0, 0.0).astype(bf16)
            sel = [jnp.dot(keep, spread[j], preferred_element_type=f32) > 0.5 for j in range(qi)]
        q_aug = jnp.concatenate([q2, q_extra(qi)], axis=1)
        logits = []
        for j in range(qi + 1):
            s = lax.dot_general(q_aug, k_all[j * blk:(j + 1) * blk], _NT, preferred_element_type=f32) * scale2
            if j == qi:
                s = jnp.where(causal, s, NEG_BIG)
            elif sel is not None:
                s = jnp.where(jnp.concatenate([sel[j]] * (blk // LANES), axis=1), s, NEG_BIG)
            logits.append(s)
        s_all = jnp.concatenate(logits, axis=1)
        m = jnp.max(s_all, axis=1, keepdims=True)
        p = jnp.exp2(s_all - m)
        o2 = jnp.dot(p.astype(bf16), v_ones[:(qi + 1) * blk], preferred_element_type=f32)
        o2 = o2[:, :LANES] / o2[:, LANES:LANES + 1]
        o = jnp.where(first_head, o2[:blk], o2[blk:])
        o_ref[qi * blk:(qi + 1) * blk, :] = o.astype(o_ref.dtype)


def _moba_kernel(q_ref, k_ref, v_ref, o_ref, *, seq):
    _moba_body(q_ref, k_ref, v_ref, o_ref, pair=pl.program_id(1), seq=seq)


def moba(qkv, batch, seq):
    n = batch * seq
    n_pairs = ATT_WIDTH // LANES
    assert seq % MOBA_BLOCK == 0
    spec = lambda off: pl.BlockSpec((seq, LANES), lambda b, p: (b, off + p))
    return pl.pallas_call(
        functools.partial(_moba_kernel, seq=seq),
        grid=(batch, n_pairs),
        in_specs=[spec(0), spec(n_pairs), spec(2 * n_pairs)],
        out_specs=pl.BlockSpec((seq, LANES), lambda b, p: (b, p)),
        out_shape=jax.ShapeDtypeStruct((n, ATT_WIDTH), jnp.bfloat16),
        compiler_params=_params("parallel", "parallel"),
        name="moba",
    )(qkv, qkv, qkv)


def _log_sigmoid(z):
    return jnp.minimum(z, 0.0) - jnp.log(1.0 + jnp.exp(-jnp.abs(z)))


def _mlstm_body(xm_ref, om_ref, cw_ref, cb_ref, wq_ref, wk_ref, wiq_ref, wik_ref, wix_ref,
                wiqt_ref, wikt_ref, wixt_ref, bi_ref, bit_ref, g_ref, y_ref,
                q_sc, k_sc, gcol_sc, grow_sc, ct_sc, n_sc, m_sc, *, seq):
    L = MLSTM_CHUNK
    H = N_MLSTM_HEADS
    n_chunks = seq // L
    n_seq = xm_ref.shape[0] // seq
    f32, bf16 = jnp.float32, jnp.bfloat16

    t_idx = lax.broadcasted_iota(jnp.int32, (seq, 1), 0)
    is_f_col = lax.broadcasted_iota(jnp.int32, (1, 2 * H), 1) >= H
    is_f_row = lax.broadcasted_iota(jnp.int32, (2 * H, 1), 0) >= H
    for b in range(n_seq):
        xm = xm_ref[b * seq:(b + 1) * seq, :]
        x = xm.astype(f32)
        acc = x * cw_ref[MLSTM_CONV - 1:MLSTM_CONV, :] + cb_ref[...]
        for d in range(1, MLSTM_CONV):
            shifted = jnp.where(t_idx >= d, pltpu.roll(x, d, axis=0), 0.0)
            acc = acc + shifted * cw_ref[MLSTM_CONV - 1 - d:MLSTM_CONV - d, :]
        xc = (acc * jax.nn.sigmoid(acc)).astype(bf16)

        if_col = jnp.dot(xm, wix_ref[...], preferred_element_type=f32) + bi_ref[...]
        if_row = lax.dot_general(wixt_ref[...], xm, _NT, preferred_element_type=f32) + bit_ref[...]
        for h in range(H):
            xch = xc[:, h * LANES:(h + 1) * LANES]
            q = jnp.dot(xch, wq_ref[h], preferred_element_type=f32).astype(bf16)
            k = jnp.dot(xch, wk_ref[h], preferred_element_type=f32).astype(bf16)
            if_col = if_col + jnp.dot(q, wiq_ref[h], preferred_element_type=f32)
            if_col = if_col + jnp.dot(k, wik_ref[h], preferred_element_type=f32)
            if_row = if_row + lax.dot_general(wiqt_ref[h], q, _NT, preferred_element_type=f32)
            if_row = if_row + lax.dot_general(wikt_ref[h], k, _NT, preferred_element_type=f32)
            q_sc[b * H + h] = q
            k_sc[b * H + h] = (k.astype(f32) * (MLSTM_QK_DIM ** -0.5)).astype(bf16)
        gcol_sc[b] = jnp.where(is_f_col, _log_sigmoid(if_col), if_col)
        g_row = jnp.where(is_f_row, _log_sigmoid(if_row), if_row)
        for c in range(n_chunks):
            grow_sc[b * n_chunks + c] = g_row[:, c * L:(c + 1) * L]

    ct_sc[...] = jnp.zeros_like(ct_sc)
    n_sc[...] = jnp.zeros_like(n_sc)
    m_sc[...] = jnp.zeros_like(m_sc)

    r_i = lax.broadcasted_iota(jnp.int32, (L, L), 0)
    c_i = lax.broadcasted_iota(jnp.int32, (L, L), 1)
    causal = r_i >= c_i
    tri_lo = jnp.where(causal, 1.0, 0.0).astype(f32)
    tri_up = jnp.where(c_i >= r_i, 1.0, 0.0).astype(f32)

    def chunk(c, carry):
      r0 = pl.multiple_of(c * L, L)
      for b in range(n_seq):
        rows = pl.ds(pl.multiple_of(b * seq + c * L, L), L)
        gc = gcol_sc[b, pl.ds(r0, L), :]
        gr = grow_sc[b * n_chunks + c]
        b_cols = jnp.dot(tri_lo, gc, precision=lax.Precision.HIGHEST, preferred_element_type=f32)
        b_rows = jnp.dot(gr, tri_up, precision=lax.Precision.HIGHEST, preferred_element_type=f32)
        for h in range(H):
            st = b * H + h
            hs = slice(h * LANES, (h + 1) * LANES)
            b_col = b_cols[:, H + h:H + h + 1]
            i_col = gc[:, h:h + 1]
            b_row = b_rows[H + h:H + h + 1, :]
            i_row = gr[h:h + 1, :]
            b_last = b_col[L - 1:L, :]
            m_prev = m_sc[st][:, 0:1]
            q = q_sc[st, pl.ds(r0, L), :]
            k = k_sc[st, pl.ds(r0, L), :]
            v = xm_ref[rows, hs]

            log_d = jnp.where(causal, b_col + (i_row - b_row), NEG_BIG)
            inter = b_col + m_prev
            m_t = jnp.maximum(inter, jnp.max(log_d, axis=1, keepdims=True))
            w_inter = jnp.exp(inter - m_t)
            s = lax.dot_general(q, k, _NT, preferred_element_type=f32) * jnp.exp(log_d - m_t)
            num = jnp.dot(s.astype(bf16), v, preferred_element_type=f32)
            num = num + w_inter * jnp.dot(q, ct_sc[st].astype(bf16), preferred_element_type=f32)
            den = jnp.sum(s, axis=1, keepdims=True)
            den = den + w_inter * jnp.sum(q.astype(f32) * n_sc[st], axis=1, keepdims=True)
            hv = num / jnp.maximum(jnp.abs(den), jnp.exp(-m_t))
            hv = hv * lax.rsqrt(jnp.mean(hv * hv, axis=1, keepdims=True) + RMS_EPS) * g_ref[:, hs]
            o_gate = jax.nn.sigmoid(om_ref[rows, hs].astype(f32))
            y_ref[rows, hs] = (o_gate * hv).astype(y_ref.dtype)

            g_col = b_last - b_col + i_col
            m_new = jnp.maximum(b_last + m_prev, jnp.max(g_col, axis=0, keepdims=True))
            wg = jnp.exp(g_col - m_new)
            decay = jnp.exp(b_last + m_prev - m_new)
            wv = (wg * v.astype(f32)).astype(bf16)
            ct_sc[st] = decay * ct_sc[st] + lax.dot_general(k, wv, _TN, preferred_element_type=f32)
            n_sc[st] = decay * n_sc[st] + jnp.sum(wg * k.astype(f32), axis=0, keepdims=True)
            m_sc[st] = jnp.broadcast_to(m_new, (1, LANES))
      return carry

    lax.fori_loop(0, n_chunks, chunk, 0)


def _mlstm_kernel(*refs, seq):
    _mlstm_body(*refs, seq=seq)


def mlstm(xo, conv_w, conv_b, w_qk, w_if, b_if, g, batch, seq):
    n = batch * seq
    H, dv, dk = N_MLSTM_HEADS, MLSTM_V_DIM, MLSTM_QK_DIM
    assert seq % MLSTM_CHUNK == 0 and dv == LANES
    bf16 = jnp.bfloat16
    pad = ((0, 0), (0, 0), (0, LANES - dk))
    wq = jnp.pad(w_qk[:, :, :dk], pad).astype(bf16)
    wk = jnp.pad(w_qk[:, :, dk:], pad).astype(bf16)
    rpad = ((0, 0), (0, LANES - dk), (0, 0))
    wiq = jnp.pad(w_if[:H * dk].reshape(H, dk, 2 * H), rpad).astype(bf16)
    wik = jnp.pad(w_if[H * dk:2 * H * dk].reshape(H, dk, 2 * H), rpad).astype(bf16)
    wix = w_if[2 * H * dk:].astype(bf16)
    wiqt, wikt, wixt = wiq.transpose(0, 2, 1), wik.transpose(0, 2, 1), wix.T
    full = lambda a: pl.BlockSpec(a.shape, lambda b: (0,) * a.ndim)
    consts = [conv_w, conv_b.reshape(1, -1), wq, wk, wiq, wik, wix, wiqt, wikt, wixt,
              b_if.reshape(1, -1), b_if.reshape(-1, 1), g.reshape(1, -1)]
    ns = 1
    return pl.pallas_call(
        functools.partial(_mlstm_kernel, seq=seq),
        grid=(batch // ns,),
        in_specs=[pl.BlockSpec((ns * seq, MLSTM_WIDTH), lambda b: (b, 0)),
                  pl.BlockSpec((ns * seq, MLSTM_WIDTH), lambda b: (b, 1))] + [full(a) for a in consts],
        out_specs=pl.BlockSpec((ns * seq, MLSTM_WIDTH), lambda b: (b, 0)),
        out_shape=jax.ShapeDtypeStruct((n, MLSTM_WIDTH), bf16),
        scratch_shapes=[pltpu.VMEM((ns * H, seq, LANES), bf16), pltpu.VMEM((ns * H, seq, LANES), bf16),
                        pltpu.VMEM((ns, seq, 2 * H), jnp.float32),
                        pltpu.VMEM((ns * (seq // MLSTM_CHUNK), 2 * H, MLSTM_CHUNK), jnp.float32),
                        pltpu.VMEM((ns * H, LANES, LANES), jnp.float32),
                        pltpu.VMEM((ns * H, 1, LANES), jnp.float32),
                        pltpu.VMEM((ns * H, 1, LANES), jnp.float32)],
        compiler_params=_params("parallel"),
        name="mlstm",
    )(xo, xo, *consts)


def _store_rows_as_tiles(dst_ref, val):
    rows = val.shape[0]
    for c in range(ROW_CHUNKS):
        dst_ref[pl.ds(c, rows, stride=ROW_CHUNKS), :] = val[:, c * LANES:(c + 1) * LANES]


def _merge_route_body(x_ref, ya_ref, ym_ref, ga_ref, gm_ref, wb0_ref, wb1_ref, wo_ref, g_ref,
                      wr_ref, br_ref, x1_ref, h2_ref, gate_ref, lpos_ref, cnt_ref):
    f32, bf16 = jnp.float32, jnp.bfloat16
    tm = x_ref.shape[0]

    u = jax.nn.sigmoid(ga_ref[...].astype(f32)) * jnp.dot(ya_ref[...], wb0_ref[...], preferred_element_type=f32)
    u = u + jax.nn.sigmoid(gm_ref[...].astype(f32)) * jnp.dot(ym_ref[...], wb1_ref[...], preferred_element_type=f32)
    x1 = x_ref[...] + jnp.dot(u.astype(bf16), wo_ref[...], preferred_element_type=f32)
    x1_ref[...] = x1
    h2 = _rms(x1, g_ref[...])
    _store_rows_as_tiles(h2_ref, h2)
    (h_hi, h_lo), (w_hi, w_lo) = _bf16_parts(h2, 2), _bf16_parts(wr_ref[...], 2)
    logits = (jnp.dot(h_hi, w_hi, preferred_element_type=f32) + jnp.dot(h_lo, w_hi, preferred_element_type=f32)
              + jnp.dot(h_hi, w_lo, preferred_element_type=f32)) + br_ref[...]
    logits = jnp.transpose(logits)[:N_EXPERTS]
    e_id = lax.broadcasted_iota(jnp.int32, logits.shape, 0).astype(f32)
    chosen = jnp.zeros(logits.shape, f32)
    vals, ids = [], []
    for _ in range(TOP_K):
        top = jnp.max(logits, axis=0, keepdims=True)
        first = jnp.min(jnp.where(logits == top, e_id, float(N_EXPERTS)), axis=0, keepdims=True)
        hit = e_id == first
        chosen = jnp.where(hit, 1.0, chosen)
        logits = jnp.where(hit, -jnp.inf, logits)
        vals.append(top)
        ids.append(first)
    ex = [jnp.exp(v - vals[0]) for v in vals]
    total = ex[0] + ex[1] + ex[2] + ex[3]

    chosen_b = chosen.astype(bf16)
    t_r = lax.broadcasted_iota(jnp.int32, (tm, tm), 0)
    t_c = lax.broadcasted_iota(jnp.int32, (tm, tm), 1)
    earlier = jnp.where(t_r < t_c, 1.0, 0.0).astype(bf16)
    before = jnp.dot(chosen_b, earlier, preferred_element_type=f32)
    e_r = lax.broadcasted_iota(jnp.int32, (N_EXPERTS, N_EXPERTS), 0)
    e_c = lax.broadcasted_iota(jnp.int32, (N_EXPERTS, N_EXPERTS), 1)
    lower = jnp.where(e_c < e_r, 1.0, 0.0).astype(bf16)
    seg_off = jnp.sum(jnp.dot(lower, chosen_b, preferred_element_type=f32), axis=1, keepdims=True)
    pos = before + seg_off
    for kk in range(TOP_K):
        gate_ref[0, :, kk * tm:(kk + 1) * tm] = ex[kk] / total
        lpos_ref[0, :, kk * tm:(kk + 1) * tm] = jnp.sum(jnp.where(e_id == ids[kk], pos, 0.0), axis=0,
                                                        keepdims=True).astype(jnp.int32)
    counts = jnp.sum(chosen, axis=1, keepdims=True)
    cnt_ref[0] = jnp.broadcast_to(counts, (N_EXPERTS, LANES)).astype(jnp.int32)


def merge_route(x2, ya, ym, gates, wb0, wb1, wo, g, w_router, b_router, *, tm):
    n, d = x2.shape
    e = w_router.shape[1]
    assert n % tm == 0 and d == ROW_CHUNKS * LANES and e == N_EXPERTS
    full = lambda a: pl.BlockSpec(a.shape, lambda i: (0,) * a.ndim)
    lane_pad = ((0, 0), (0, LANES - e))
    consts = [wb0, wb1, wo, g.reshape(1, d), jnp.pad(w_router, lane_pad),
              jnp.pad(b_router.reshape(1, e), lane_pad)]
    tok = lambda dt: jax.ShapeDtypeStruct((n // tm, 1, TOP_K * tm), dt)
    return pl.pallas_call(
        _merge_route_body,
        grid=(n // tm,),
        in_specs=[pl.BlockSpec((tm, d), lambda i: (i, 0)),
                  pl.BlockSpec((tm, ATT_WIDTH), lambda i: (i, 0)),
                  pl.BlockSpec((tm, MLSTM_WIDTH), lambda i: (i, 0)),
                  pl.BlockSpec((tm, d), lambda i: (i, 0)),
                  pl.BlockSpec((tm, d), lambda i: (i, 1))] + [full(a) for a in consts],
        out_specs=[pl.BlockSpec((tm, d), lambda i: (i, 0)),
                   pl.BlockSpec((tm * ROW_CHUNKS, LANES), lambda i: (i, 0)),
                   pl.BlockSpec((1, 1, TOP_K * tm), lambda i: (i, 0, 0)),
                   pl.BlockSpec((1, 1, TOP_K * tm), lambda i: (i, 0, 0)),
                   pl.BlockSpec((1, e, LANES), lambda i: (i, 0, 0))],
        out_shape=[jax.ShapeDtypeStruct((n, d), jnp.float32),
                   jax.ShapeDtypeStruct((n * ROW_CHUNKS, LANES), jnp.float32),
                   tok(jnp.float32), tok(jnp.int32),
                   jax.ShapeDtypeStruct((n // tm, e, LANES), jnp.int32)],
        compiler_params=_params("parallel"),
        name="merge_route",
    )(x2, ya, ym, gates, gates, *consts)


def _segment_pieces(cnt, max_rows, fn):
    done = 0
    size = 1 << (max_rows.bit_length() - 1)
    while size:
        bit = cnt & size

        def piece(done=done, size=size):
            fn(done, size)
        pl.when(bit != 0)(piece)
        done = done + bit
        size >>= 1


def _for_tile_segments(meta, tile_rows, make_copy, action):
    cnt_ref, off_ref, dst_ref = meta

    def per_expert(e, carry):
        off, dst = off_ref[0, 0, e], dst_ref[0, 0, e]
        _segment_pieces(cnt_ref[0, 0, e], tile_rows,
                        lambda done, size: getattr(make_copy(off + done, dst + done, size), action)())
        return carry
    lax.fori_loop(0, N_EXPERTS, per_expert, 0)


def _dispatch_body(lpos_ref, cnt_ref, off_ref, dst_ref, pad_len_ref, pad_row_ref, x_ref, out_ref,
                   cbuf, zbuf, sem, zsem, *, tc, rb):
    step = pl.program_id(0)
    last = pl.num_programs(0) - 1
    slot = step % 2

    def compact(t, carry):
        row = x_ref[t]
        for kk in range(TOP_K):
            cbuf[slot, lpos_ref[0, 0, kk * tc + t]] = row
        return carry
    lax.fori_loop(0, tc, compact, 0, unroll=8)

    def seg_copy(sl):
        return lambda local, glob, size: pltpu.make_async_copy(
            cbuf.at[sl, pl.ds(local, size)], out_ref.at[pl.ds(glob, size)], sem.at[sl])

    def wait_tile(sl):
        pltpu.make_async_copy(cbuf.at[sl], out_ref.at[pl.ds(0, TOP_K * tc)], sem.at[sl]).wait()

    _for_tile_segments((cnt_ref, off_ref, dst_ref), tc, seg_copy(slot), "start")

    @pl.when(step > 0)
    def _():
        wait_tile(1 - slot)

    @pl.when(step == last)
    def _():
        wait_tile(slot)
        zbuf[...] = jnp.zeros_like(zbuf)

        def pads(action):
            def per_pad(j, carry):
                row = pad_row_ref[0, j]
                _segment_pieces(pad_len_ref[0, j], rb, lambda done, size: getattr(
                    pltpu.make_async_copy(zbuf.at[pl.ds(0, size)], out_ref.at[pl.ds(row + done, size)], zsem),
                    action)())
                return carry
            lax.fori_loop(0, pad_len_ref.shape[1], per_pad, 0)
        pads("start")
        pads("wait")


def dispatch(lpos, seg_cnt, seg_off, seg_dst, pad_len, pad_row, h2_tiles, n_rows, *, tc, rb):
    n = h2_tiles.shape[0]
    steps = n // tc
    assert n % tc == 0
    smem = pltpu.SMEM
    seg = pl.BlockSpec((1, 1, N_EXPERTS), lambda i: (i, 0, 0), memory_space=smem)
    whole = lambda a: pl.BlockSpec(a.shape, lambda i: (0,) * a.ndim, memory_space=smem)
    return pl.pallas_call(
        functools.partial(_dispatch_body, tc=tc, rb=rb),
        grid=(steps,),
        in_specs=[pl.BlockSpec((1, 1, TOP_K * tc), lambda i: (i, 0, 0), memory_space=smem),
                  seg, seg, seg, whole(pad_len), whole(pad_row),
                  pl.BlockSpec((tc, ROW_CHUNKS, LANES), lambda i: (i, 0, 0))],
        out_specs=pl.BlockSpec(memory_space=pl.ANY),
        out_shape=jax.ShapeDtypeStruct((n_rows, ROW_CHUNKS, LANES), h2_tiles.dtype),
        scratch_shapes=[pltpu.VMEM((2, TOP_K * tc, ROW_CHUNKS, LANES), h2_tiles.dtype),
                        pltpu.VMEM((rb, ROW_CHUNKS, LANES), h2_tiles.dtype),
                        pltpu.SemaphoreType.DMA((2,)), pltpu.SemaphoreType.DMA(())],
        compiler_params=_params("arbitrary"),
        name="dispatch",
    )(lpos, seg_cnt, seg_off, seg_dst, pad_len, pad_row, h2_tiles)


def _experts_body(blk_e_ref, nxt_e_ref, valid_ref, used_ref, x_ref, wgu_hbm, bg_ref, bu_ref, wd_hbm, bd_ref,
                  perm_ref, y_ref, wgu_st, wd_st, wg_sc, wu_sc, wd_sc, sem, *, halves):
    f32, bf16 = jnp.float32, jnp.bfloat16
    step = pl.program_id(0)
    live = step < used_ref[0]
    rows = x_ref.shape[0] // ROW_CHUNKS
    expert = blk_e_ref[step]

    def fetch(e, action):
        getattr(pltpu.make_async_copy(wgu_hbm.at[e], wgu_st, sem.at[0]), action)()
        getattr(pltpu.make_async_copy(wd_hbm.at[e], wd_st, sem.at[1]), action)()

    @pl.when(live & ((step == 0) | (expert != blk_e_ref[jnp.maximum(step - 1, 0)])))
    def _():
        @pl.when(step == 0)
        def _():
            fetch(expert, "start")
        fetch(expert, "wait")
        tile = perm_ref.shape[0]
        half = tile // 2
        for t in range(wgu_st.shape[1] // tile):
            w = wgu_st[:, t * tile:(t + 1) * tile].astype(bf16)
            o = jnp.dot(w, perm_ref[...], preferred_element_type=f32)
            wg_sc[:, t * half:(t + 1) * half] = o[:, :half].astype(bf16)
            wu_sc[:, t * half:(t + 1) * half] = o[:, half:].astype(bf16)
        wd_sc[...] = wd_st[...].astype(bf16)
        nxt = nxt_e_ref[step]

        @pl.when(nxt >= 0)
        def _():
            fetch(nxt, "start")

    part = rows // halves

    def ffn(groups):
        for h in range(groups):
            xb = jnp.concatenate(
                [x_ref[pl.ds(h * part * ROW_CHUNKS + c, part, stride=ROW_CHUNKS), :]
                 for c in range(ROW_CHUNKS)], axis=1).astype(bf16)
            gate = jnp.dot(xb, wg_sc[...], preferred_element_type=f32) + bg_ref[0]
            up = jnp.dot(xb, wu_sc[...], preferred_element_type=f32) + bu_ref[0]
            gate = jnp.minimum(gate, SWIGLU_LIMIT)
            up = jnp.clip(up, -SWIGLU_LIMIT, SWIGLU_LIMIT)
            act = (up + 1.0) * (gate * jax.nn.sigmoid(gate * SWIGLU_ALPHA))
            y = jnp.dot(act.astype(bf16), wd_sc[...], preferred_element_type=f32) + bd_ref[0]
            for c in range(ROW_CHUNKS):
                y_ref[pl.ds(h * part * ROW_CHUNKS + c, part, stride=ROW_CHUNKS), :] = \
                    y[:, c * LANES:(c + 1) * LANES]
        if groups < halves:
            y_ref[groups * part * ROW_CHUNKS:, :] = jnp.zeros(((halves - groups) * part * ROW_CHUNKS, LANES), f32)

    n_valid = valid_ref[step]
    for groups in range(1, halves + 1):
        wanted = live & (n_valid > (groups - 1) * part)
        if groups < halves:
            wanted = wanted & (n_valid <= groups * part)
        pl.when(wanted)(functools.partial(ffn, groups))

    @pl.when(jnp.logical_not(live))
    def _():
        y_ref[...] = jnp.zeros_like(y_ref)


def experts(blk_expert, nxt_expert, blk_valid, n_used, xs_tiles, w_gu, b_g, b_u, w_d, b_d, *, halves=2):
    n_blocks = blk_expert.shape[0]
    rb = MOE_ROW_BLOCK
    d, ff = w_d.shape[2], w_d.shape[1]
    tile = 2 * LANES
    perm = np.zeros((tile, tile), np.float32)
    perm[2 * np.arange(LANES), np.arange(LANES)] = 1.0
    perm[2 * np.arange(LANES) + 1, LANES + np.arange(LANES)] = 1.0
    e_map3 = lambda i, be, nx, bv, nu: (be[i], 0, 0)
    rows_map = lambda i, be, nx, bv, nu: (i, 0)
    grid_spec = pltpu.PrefetchScalarGridSpec(
        num_scalar_prefetch=4,
        grid=(n_blocks,),
        in_specs=[pl.BlockSpec((rb * ROW_CHUNKS, LANES), rows_map),
                  pl.BlockSpec(memory_space=pl.ANY),
                  pl.BlockSpec((1, 1, ff), e_map3), pl.BlockSpec((1, 1, ff), e_map3),
                  pl.BlockSpec(memory_space=pl.ANY), pl.BlockSpec((1, 1, d), e_map3),
                  pl.BlockSpec((tile, tile), lambda i, be, nx, bv, nu: (0, 0))],
        out_specs=pl.BlockSpec((rb * ROW_CHUNKS, LANES), rows_map),
        scratch_shapes=[pltpu.VMEM((d, 2 * ff), w_gu.dtype), pltpu.VMEM((ff, d), w_d.dtype),
                        pltpu.VMEM((d, ff), jnp.bfloat16), pltpu.VMEM((d, ff), jnp.bfloat16),
                        pltpu.VMEM((ff, d), jnp.bfloat16), pltpu.SemaphoreType.DMA((2,))],
    )
    return pl.pallas_call(
        functools.partial(_experts_body, halves=halves),
        grid_spec=grid_spec,
        out_shape=jax.ShapeDtypeStruct(xs_tiles.shape, jnp.float32),
        compiler_params=_params("arbitrary"),
        name="experts",
    )(blk_expert, nxt_expert, blk_valid, n_used, xs_tiles, w_gu, b_g, b_u, w_d, b_d,
      jnp.asarray(perm, jnp.bfloat16))


def _combine_ple_body(lpos_ref, gw_ref, cnt_ref, off_ref, dst_ref, ncnt_ref, noff_ref, ndst_ref,
                      x1_ref, ys_ref, p_ref, gp_ref, wpg_ref, wpp_ref, *rest):
    cbuf, msum_a, msum_b, sem = rest[-4:]
    rest = rest[:-4]
    gf_ref, o_ref = rest if len(rest) == 2 else (None, rest[0])
    f32, bf16 = jnp.float32, jnp.bfloat16
    tm = x1_ref.shape[0]
    step = pl.program_id(0)
    n_tiles = pl.num_programs(0) - 1
    slot = step % 2

    def seg_copy(sl):
        return lambda local, glob, size: pltpu.make_async_copy(
            ys_ref.at[pl.ds(glob, size)], cbuf.at[sl, pl.ds(local, size)], sem.at[sl])

    cur, nxt = (cnt_ref, off_ref, dst_ref), (ncnt_ref, noff_ref, ndst_ref)

    @pl.when(step == 0)
    def _():
        _for_tile_segments(cur, tm, seg_copy(0), "start")
        msum_b[...] = jnp.zeros_like(msum_b)

    @pl.when(step + 1 < n_tiles)
    def _():
        _for_tile_segments(nxt, tm, seg_copy(1 - slot), "start")

    @pl.when(step < n_tiles)
    def _():
        pltpu.make_async_copy(ys_ref.at[pl.ds(0, TOP_K * tm)], cbuf.at[slot], sem.at[slot]).wait()

    def gather_and_dense(msum_cur, msum_prev):
        for t in range(tm):
            acc = gw_ref[0, 0, t] * cbuf[slot, lpos_ref[0, 0, t]]
            for kk in range(1, TOP_K):
                acc = acc + gw_ref[0, 0, kk * tm + t] * cbuf[slot, lpos_ref[0, 0, kk * tm + t]]
            msum_cur[t] = acc
        moe = jnp.concatenate([msum_prev[:, c, :] for c in range(ROW_CHUNKS)], axis=1)
        x2 = x1_ref[...] + moe
        r = _rms(x2, gp_ref[...]).astype(bf16)
        gate = jax.nn.sigmoid(jnp.dot(r, wpg_ref[...], preferred_element_type=f32))
        emb = jnp.dot(p_ref[...].astype(bf16), wpp_ref[...], preferred_element_type=f32)
        x3 = x2 + gate * emb
        o_ref[...] = x3 if gf_ref is None else _rms(x3, gf_ref[...])

    pl.when(slot == 0)(functools.partial(gather_and_dense, msum_a, msum_b))
    pl.when(slot == 1)(functools.partial(gather_and_dense, msum_b, msum_a))


def combine_ple(lpos, gate_w, seg_cnt, seg_off, seg_dst, x1, ys_tiles, p2, g_ple, w_pg, w_pp, g_final,
                *, tm):
    n, d = x1.shape
    tiles = n // tm
    assert n % tm == 0 and tiles >= 2
    smem = pltpu.SMEM
    full = lambda a: pl.BlockSpec(a.shape, lambda i: (0,) * a.ndim)
    gathered = lambda i, shift=0: (jnp.minimum(i + shift, tiles - 1), 0, 0)
    finished = lambda i: (jnp.maximum(i - 1, 0), 0)
    seg = lambda shift: pl.BlockSpec((1, 1, N_EXPERTS), functools.partial(gathered, shift=shift),
                                     memory_space=smem)
    tok = pl.BlockSpec((1, 1, TOP_K * tm), gathered, memory_space=smem)
    consts = [g_ple.reshape(1, d), w_pg, w_pp] + ([] if g_final is None else [g_final.reshape(1, d)])
    return pl.pallas_call(
        _combine_ple_body,
        grid=(tiles + 1,),
        in_specs=[tok, tok, seg(0), seg(0), seg(0), seg(1), seg(1), seg(1),
                  pl.BlockSpec((tm, d), finished),
                  pl.BlockSpec(memory_space=pl.ANY),
                  pl.BlockSpec((tm, p2.shape[1]), finished)] + [full(a) for a in consts],
        out_specs=pl.BlockSpec((tm, d), finished),
        out_shape=jax.ShapeDtypeStruct((n, d), jnp.float32),
        scratch_shapes=[pltpu.VMEM((2, TOP_K * tm, ROW_CHUNKS, LANES), ys_tiles.dtype),
                        pltpu.VMEM((tm, ROW_CHUNKS, LANES), jnp.float32),
                        pltpu.VMEM((tm, ROW_CHUNKS, LANES), jnp.float32),
                        pltpu.SemaphoreType.DMA((2,))],
        compiler_params=_params("arbitrary"),
        name="combine_ple",
    )(lpos, gate_w, seg_cnt, seg_off, seg_dst, seg_cnt, seg_off, seg_dst, x1, ys_tiles, p2, *consts)


def _moe(h2_tiles, lpos, tile_cnt, w_gu, b_gu, w_d, b_d, *, tm):
    n = h2_tiles.shape[0] // ROW_CHUNKS
    rb = MOE_ROW_BLOCK
    i32 = jnp.int32
    n_blocks = -(-n * TOP_K // rb) + N_EXPERTS
    n_rows = n_blocks * rb
    counts = jnp.sum(tile_cnt, axis=0)
    padded = (counts + rb - 1) // rb * rb
    pend = jnp.cumsum(padded)
    pstart = pend - padded
    blk_start = jnp.arange(n_blocks, dtype=i32) * rb
    blk_expert = jnp.minimum(jnp.sum((pend[None, :] <= blk_start[:, None]).astype(i32), axis=1),
                             N_EXPERTS - 1)
    n_used = pend[-1:] // rb
    seg_cnt = tile_cnt[:, None, :]
    seg_off = (jnp.cumsum(tile_cnt, axis=1) - tile_cnt)[:, None, :]
    seg_dst = (pstart[None, :] + jnp.cumsum(tile_cnt, axis=0) - tile_cnt)[:, None, :]
    tail = n_used + jnp.arange(N_EXPERTS, dtype=i32)
    pad_len = jnp.concatenate([padded - counts, jnp.where(tail < n_blocks, rb, 0)])[None, :]
    pad_row = jnp.concatenate([pstart + counts, tail * rb])[None, :]
    b_g, b_u = b_gu[:, None, 0::2], b_gu[:, None, 1::2]
    xs = dispatch(lpos, seg_cnt, seg_off, seg_dst, pad_len.astype(i32), pad_row.astype(i32),
                  h2_tiles.reshape(n, ROW_CHUNKS, LANES), n_rows, tc=tm, rb=rb)
    e_ids = jnp.arange(N_EXPERTS, dtype=i32)
    later = (padded > 0)[None, :] & (e_ids[None, :] > e_ids[:, None])
    nxt_tab = jnp.min(jnp.where(later, e_ids[None, :], N_EXPERTS), axis=1)
    nxt_tab = jnp.where(nxt_tab == N_EXPERTS, -1, nxt_tab)
    nxt_expert = jnp.sum(jnp.where(blk_expert[:, None] == e_ids[None, :], nxt_tab[None, :], 0), axis=1)
    hit = blk_expert[:, None] == e_ids[None, :]
    blk_valid = jnp.clip(jnp.sum(jnp.where(hit, (pstart + counts)[None, :], 0), axis=1) - blk_start, 0, rb)
    ys = experts(blk_expert, nxt_expert.astype(i32), blk_valid.astype(i32), n_used.astype(i32),
                 xs.reshape(n_rows * ROW_CHUNKS, LANES), w_gu, b_g, b_u, w_d, b_d[:, None, :])
    return (seg_cnt, seg_off, seg_dst), ys.reshape(n_rows, ROW_CHUNKS, LANES)


def kernel(x, p, norm_mix_g, w_in, conv_w, conv_b, w_qk_m, w_if, b_if, mnorm_g, w_branch, w_out, norm_ffn_g, w_router, b_router, w_gate_up, b_gate_up, w_down, b_down, norm_ple_g, w_ple_gate, w_ple_proj, final_norm_g):
    B, S, D = x.shape
    depth = w_in.shape[0]
    bf16 = jnp.bfloat16
    x2 = x.reshape(B * S, D)
    for i in range(depth):
        qkv, xo, gates = in_proj(x2, norm_mix_g[i], w_in[i].astype(bf16))
        y_a = moba(qkv, B, S)
        y_m = mlstm(xo, conv_w[i], conv_b[i], w_qk_m[i], w_if[i], b_if[i], mnorm_g[i], B, S)
        x1, h2_tiles, gate_w, lpos, tile_cnt = merge_route(
            x2, y_a, y_m, gates, w_branch[i, 0].astype(bf16), w_branch[i, 1].astype(bf16),
            w_out[i].astype(bf16), norm_ffn_g[i], w_router[i], b_router[i], tm=MOE_TOKEN_TILE)
        segs, ys = _moe(h2_tiles, lpos, tile_cnt[:, :, 0], w_gate_up[i], b_gate_up[i],
                        w_down[i], b_down[i], tm=MOE_TOKEN_TILE)
        x2 = combine_ple(lpos, gate_w, *segs, x1, ys, p[i].reshape(B * S, -1), norm_ple_g[i],
                         w_ple_gate[i].astype(bf16), w_ple_proj[i].astype(bf16),
                         final_norm_g if i == depth - 1 else None, tm=MOE_TOKEN_TILE)
    return x2.reshape(B, S, D)
```
